```python
import math
import jax
import jax.numpy as jnp
from jax import lax
import numpy as np

D_MODEL = 2048
BATCH = 4
SEQ = 2048
DEPTH = 2
DEC_BATCH = 16
DEC_SEQ = 32
PAST_LEN = 4096

CHUNK = 64
Q_BLOCK = 128
HEAD_DIM = 128
N_HEADS = D_MODEL // HEAD_DIM
H_A = N_HEADS // 2
KV_A = max(H_A // 4, 1)
H_B = N_HEADS - H_A
H_C = N_HEADS // 2
H_D = N_HEADS - H_C
IDX_HEADS = 8
IDX_DIM = 64
TOPK_MAX = 256
T5_BUCKETS = 32
T5_MAX_DIST = 128
D_LEFT_CHUNKS = 8
D_BAND_LEFT = D_LEFT_CHUNKS * CHUNK
D_REL_CLIP = 128
D_FF = 4 * D_MODEL
MIX_WIDTH = N_HEADS * HEAD_DIM
EVEN_SIZES = (H_A * HEAD_DIM, KV_A * HEAD_DIM, KV_A * HEAD_DIM, IDX_HEADS * IDX_DIM, IDX_DIM, IDX_HEADS,
              H_B * HEAD_DIM, H_B * HEAD_DIM, H_B * HEAD_DIM, H_B)
ODD_SIZES = (H_C * HEAD_DIM, H_C * HEAD_DIM, H_C * HEAD_DIM, H_D * HEAD_DIM, H_D * HEAD_DIM, H_D * HEAD_DIM)
EVEN_WIDTH = sum(EVEN_SIZES)
ODD_WIDTH = sum(ODD_SIZES)
N_EVEN = (DEPTH + 1) // 2
N_ODD = DEPTH // 2
EPS = 1e-6
NEG_INF = -1e30
F32 = jnp.float32

kernel_name = "hybrid_streaming_encoder_step"


def rms_norm(x, gain):
    xf = x.astype(F32)
    y = xf * lax.rsqrt(jnp.mean(xf * xf, axis=-1, keepdims=True) + EPS)
    return (y * gain.astype(F32)).astype(x.dtype)


def split_cols(h, sizes):
    return jnp.split(h, np.cumsum(sizes)[:-1].tolist(), axis=-1)


def cat_rows(past, new):
    if past is None:
        return new
    return jnp.concatenate([past.astype(new.dtype), new], axis=1)


def sweep_queries(fn, q_pos, *qs):
    T = q_pos.shape[0]
    if T <= Q_BLOCK or T % Q_BLOCK:
        return fn(q_pos, *qs)
    nb = T // Q_BLOCK
    pos_b = q_pos.reshape(nb, Q_BLOCK)
    qs_b = tuple(jnp.moveaxis(q.reshape(q.shape[0], nb, Q_BLOCK, *q.shape[2:]), 1, 0) for q in qs)
    out = lax.map(lambda args: fn(*args), (pos_b,) + qs_b)
    B = out.shape[1]
    return jnp.moveaxis(out, 0, 1).reshape(B, T, *out.shape[3:])


def t5_bucket(rel):
    nb = T5_BUCKETS // 2
    max_exact = nb // 2
    ret = jnp.where(rel > 0, nb, 0)
    n = jnp.abs(rel)
    nf = jnp.maximum(n, 1).astype(F32)
    large = max_exact + (jnp.log(nf / max_exact) / math.log(T5_MAX_DIST / max_exact)
                         * (nb - max_exact)).astype(jnp.int32)
    large = jnp.minimum(large, nb - 1)
    return ret + jnp.where(n < max_exact, n, large)


def sparse_index_attention(q, iq, iw, q_pos, k, v, ik, k_pos, t5_table):
    B, L = k.shape[0], k.shape[1]
    n_sel = min(TOPK_MAX, L // 4)
    rep = H_A // KV_A
    scale = HEAD_DIM ** -0.5
    ikf = ik.astype(F32)
    table = t5_table.astype(F32)
    gather = jax.vmap(lambda rows, idx: rows[idx])

    def block(p, qb, iqb, iwb):
        tq = p.shape[0]
        dots = jax.nn.relu(jnp.einsum('bthe,bse->bths', iqb.astype(F32), ikf))
        score = jnp.einsum('bth,bths->bts', iwb.astype(F32), dots)
        admissible = (k_pos[None, :] // CHUNK) <= (p[:, None] // CHUNK)
        score = jnp.where(admissible[None], score, NEG_INF)
        _, idx = lax.top_k(score, n_sel)
        sel_pos = k_pos[idx]
        valid = (sel_pos // CHUNK) <= (p[None, :, None] // CHUNK)
        k_sel = gather(k, idx).astype(F32)
        v_sel = gather(v, idx)
        qg = qb.reshape(B, tq, KV_A, rep, HEAD_DIM).astype(F32)
        logits = jnp.einsum('btgrd,btkgd->btgrk', qg, k_sel) * scale
        bias = table[t5_bucket(sel_pos - p[None, :, None])]
        bias = bias.reshape(B, tq, n_sel, KV_A, rep).transpose(0, 1, 3, 4, 2)
        logits = jnp.where(valid[:, :, None, None, :], logits + bias, NEG_INF)
        probs = jax.nn.softmax(logits, axis=-1).astype(v.dtype)
        out = jnp.einsum('btgrk,btkgd->btgrd', probs, v_sel)
        return out.reshape(B, tq, H_A * HEAD_DIM)

    return sweep_queries(block, q_pos, q, iq, iw)


def forgetting_attention(q, cum_q, q_pos, k, v, cum_k, k_pos):
    B, _, H, _ = q.shape
    scale = HEAD_DIM ** -0.5
    kf = k.astype(F32)
    ck = jnp.swapaxes(cum_k, 1, 2)[:, :, None, :]

    def block(p, qb, cqb):
        logits = jnp.einsum('bthd,bshd->bhts', qb.astype(F32), kf) * scale
        logits = logits + jnp.swapaxes(cqb, 1, 2)[..., None] - ck
        causal = k_pos[None, :] <= p[:, None]
        logits = jnp.where(causal, logits, NEG_INF)
        probs = jax.nn.softmax(logits, axis=-1).astype(v.dtype)
        return jnp.einsum('bhts,bshd->bthd', probs, v).reshape(B, p.shape[0], H * HEAD_DIM)

    return sweep_queries(block, q_pos, q, cum_q)


def stick_breaking_attention(q, q_pos, k, v, k_pos):
    B, _, H, _ = q.shape
    scale = HEAD_DIM ** -0.5
    kf = k.astype(F32)

    def block(p, qb):
        z = jnp.einsum('bthd,bshd->bhts', qb.astype(F32), kf) * scale
        strict = k_pos[None, :] < p[:, None]
        log_keep = jnp.where(strict, jax.nn.log_sigmoid(-z), 0.0)
        later = lax.cumsum(log_keep, axis=3, reverse=True) - log_keep
        weight = jnp.where(strict, jnp.exp(jax.nn.log_sigmoid(z) + later), 0.0)
        return jnp.einsum('bhts,bshd->bthd', weight.astype(v.dtype), v).reshape(B, p.shape[0], H * HEAD_DIM)

    return sweep_queries(block, q_pos, q)


def band_attention(q, q_pos, kb, vb, k_pos, rel_table):
    scale = HEAD_DIM ** -0.5
    logits = jnp.einsum('bgthd,bgshd->bghts', q.astype(F32), kb.astype(F32)) * scale
    rel = jnp.clip(k_pos[:, None, :] - q_pos[:, :, None], -D_REL_CLIP, D_REL_CLIP) + D_REL_CLIP
    bias = jnp.moveaxis(rel_table.astype(F32)[rel], -1, 1)
    qc = q_pos[:, :, None] // CHUNK
    kc = k_pos[:, None, :] // CHUNK
    visible = (k_pos[:, None, :] >= 0) & (kc <= qc) & (kc >= qc - D_LEFT_CHUNKS)
    logits = jnp.where(visible[None, :, None], logits + bias[None], NEG_INF)
    probs = jax.nn.softmax(logits, axis=-1).astype(vb.dtype)
    return jnp.einsum('bghts,bgshd->bgthd', probs, vb)


def chunk_band_prompt(q, k, v, rel_table):
    B, T, H, _ = q.shape
    nc = T // CHUNK
    pad = ((0, 0), (D_BAND_LEFT, 0), (0, 0), (0, 0))
    kp = jnp.pad(k, pad)
    vp = jnp.pad(v, pad)
    band = jnp.arange(nc)[:, None] * CHUNK + jnp.arange(D_BAND_LEFT + CHUNK)[None, :]
    k_pos = band - D_BAND_LEFT
    q_pos = jnp.arange(T).reshape(nc, CHUNK)
    out = band_attention(q.reshape(B, nc, CHUNK, H, HEAD_DIM), q_pos, kp[:, band], vp[:, band], k_pos, rel_table)
    return out.reshape(B, T, H * HEAD_DIM)


def chunk_band_step(q, k, v, past_k, past_v, offset, rel_table):
    B, T, H, _ = q.shape
    w = past_k.shape[1]
    kb = cat_rows(past_k, k)[:, None]
    vb = cat_rows(past_v, v)[:, None]
    k_pos = (offset - w + jnp.arange(w + T))[None]
    q_pos = (offset + jnp.arange(T))[None]
    out = band_attention(q[:, None], q_pos, kb, vb, k_pos, rel_table)
    return out.reshape(B, T, H * HEAD_DIM)


def even_mixer(h, past, w_in, w_out, a_qn, a_kn, b_qn, b_kn, f_bias, t5_table):
    B, T, _ = h.shape
    aq, ak, av, iq, ik, iw, bq, bk, bv, bf = split_cols(h @ w_in, EVEN_SIZES)
    aq = rms_norm(aq.reshape(B, T, H_A, HEAD_DIM), a_qn)
    ak = rms_norm(ak.reshape(B, T, KV_A, HEAD_DIM), a_kn)
    av = av.reshape(B, T, KV_A, HEAD_DIM)
    iq = iq.reshape(B, T, IDX_HEADS, IDX_DIM)
    bq = rms_norm(bq.reshape(B, T, H_B, HEAD_DIM), b_qn)
    bk = rms_norm(bk.reshape(B, T, H_B, HEAD_DIM), b_kn)
    bv = bv.reshape(B, T, H_B, HEAD_DIM)
    log_f = jax.nn.log_sigmoid(bf.astype(F32) + f_bias.astype(F32))
    p_ak, p_av, p_ik, p_bk, p_bv, p_lf = (None,) * 6 if past is None else past
    ak_all = cat_rows(p_ak, ak)
    av_all = cat_rows(p_av, av)
    ik_all = cat_rows(p_ik, ik)
    bk_all = cat_rows(p_bk, bk)
    bv_all = cat_rows(p_bv, bv)
    lf_all = cat_rows(p_lf, log_f)
    L = ak_all.shape[1]
    offset = L - T
    k_pos = jnp.arange(L)
    q_pos = offset + jnp.arange(T)
    cum = jnp.cumsum(lf_all, axis=1)
    out_a = sparse_index_attention(aq, iq, iw, q_pos, ak_all, av_all, ik_all, k_pos, t5_table)
    out_b = forgetting_attention(bq, cum[:, offset:], q_pos, bk_all, bv_all, cum, k_pos)
    y = jnp.concatenate([out_a, out_b], axis=-1) @ w_out
    return y, (ak, av, ik, bk, bv, log_f)


def odd_mixer(h, past, w_in, w_out, d_qn, d_kn, d_rel):
    B, T, _ = h.shape
    cq, ck, cv, dq, dk, dv = split_cols(h @ w_in, ODD_SIZES)
    cq = cq.reshape(B, T, H_C, HEAD_DIM)
    ck = ck.reshape(B, T, H_C, HEAD_DIM)
    cv = cv.reshape(B, T, H_C, HEAD_DIM)
    dq = rms_norm(dq.reshape(B, T, H_D, HEAD_DIM), d_qn)
    dk = rms_norm(dk.reshape(B, T, H_D, HEAD_DIM), d_kn)
    dv = dv.reshape(B, T, H_D, HEAD_DIM)
    if past is None:
        pos = jnp.arange(T)
        out_c = stick_breaking_attention(cq, pos, ck, cv, pos)
        out_d = chunk_band_prompt(dq, dk, dv, d_rel)
        keep = min(D_BAND_LEFT, T)
        d_rows = (dk[:, T - keep:], dv[:, T - keep:])
    else:
        p_ck, p_cv, p_dk, p_dv = past
        offset = p_ck.shape[1]
        out_c = stick_breaking_attention(cq, offset + jnp.arange(T), cat_rows(p_ck, ck), cat_rows(p_cv, cv),
                                         jnp.arange(offset + T))
        out_d = chunk_band_step(dq, dk, dv, p_dk, p_dv, offset, d_rel)
        d_rows = (dk, dv)
    y = jnp.concatenate([out_c, out_d], axis=-1) @ w_out
    return y, (ck, cv, d_rows[0], d_rows[1])


def squared_relu_mlp(h, w_up, w_down):
    return jnp.square(jax.nn.relu(h @ w_up)) @ w_down


def setup_inputs(seed: int = 0) -> dict:
    key = jax.random.key(seed)
    ks = iter(jax.random.split(key, 40))

    def nrm(shape, s=1.0):
        return s * jax.random.normal(next(ks), shape, F32)

    d_cache = min(D_BAND_LEFT, PAST_LEN)
    return {
        "x_prompt": nrm((BATCH, SEQ, D_MODEL)),
        "x_sample": nrm((DEC_BATCH, DEC_SEQ, D_MODEL)),
        "cache_a_k": nrm((N_EVEN, DEC_BATCH, PAST_LEN, KV_A, HEAD_DIM)),
        "cache_a_v": nrm((N_EVEN, DEC_BATCH, PAST_LEN, KV_A, HEAD_DIM)),
        "cache_a_kidx": nrm((N_EVEN, DEC_BATCH, PAST_LEN, IDX_DIM)),
        "cache_b_k": nrm((N_EVEN, DEC_BATCH, PAST_LEN, H_B, HEAD_DIM)),
        "cache_b_v": nrm((N_EVEN, DEC_BATCH, PAST_LEN, H_B, HEAD_DIM)),
        "cache_b_logf": jax.nn.log_sigmoid(3.0 + nrm((N_EVEN, DEC_BATCH, PAST_LEN, H_B))),
        "cache_c_k": nrm((N_ODD, DEC_BATCH, PAST_LEN, H_C, HEAD_DIM)),
        "cache_c_v": nrm((N_ODD, DEC_BATCH, PAST_LEN, H_C, HEAD_DIM)),
        "cache_d_k": nrm((N_ODD, DEC_BATCH, d_cache, H_D, HEAD_DIM)),
        "cache_d_v": nrm((N_ODD, DEC_BATCH, d_cache, H_D, HEAD_DIM)),
        "attn_norm": 1.0 + nrm((DEPTH, D_MODEL), 0.02),
        "mlp_norm": 1.0 + nrm((DEPTH, D_MODEL), 0.02),
        "w_in_even": nrm((N_EVEN, D_MODEL, EVEN_WIDTH), D_MODEL ** -0.5),
        "w_out_even": nrm((N_EVEN, MIX_WIDTH, D_MODEL), MIX_WIDTH ** -0.5),
        "w_in_odd": nrm((N_ODD, D_MODEL, ODD_WIDTH), D_MODEL ** -0.5),
        "w_out_odd": nrm((N_ODD, MIX_WIDTH, D_MODEL), MIX_WIDTH ** -0.5),
        "a_q_norm": 1.0 + nrm((N_EVEN, HEAD_DIM), 0.02),
        "a_k_norm": 1.0 + nrm((N_EVEN, HEAD_DIM), 0.02),
        "b_q_norm": 1.0 + nrm((N_EVEN, HEAD_DIM), 0.02),
        "b_k_norm": 1.0 + nrm((N_EVEN, HEAD_DIM), 0.02),
        "forget_bias": 3.0 + nrm((N_EVEN, H_B), 0.5),
        "t5_bias": nrm((T5_BUCKETS, H_A), 0.5),
        "d_q_norm": 1.0 + nrm((N_ODD, HEAD_DIM), 0.02),
        "d_k_norm": 1.0 + nrm((N_ODD, HEAD_DIM), 0.02),
        "d_rel_bias": nrm((N_ODD, 2 * D_REL_CLIP + 1, H_D), 0.5),
        "w_up": nrm((DEPTH, D_MODEL, D_FF), D_MODEL ** -0.5),
        "w_down": nrm((DEPTH, D_FF, D_MODEL), D_FF ** -0.5),
    }


def reference(x_prompt, x_sample, cache_a_k, cache_a_v, cache_a_kidx, cache_b_k, cache_b_v, cache_b_logf,
              cache_c_k, cache_c_v, cache_d_k, cache_d_v, attn_norm, mlp_norm, w_in_even, w_out_even,
              w_in_odd, w_out_odd, a_q_norm, a_k_norm, b_q_norm, b_k_norm, forget_bias, t5_bias,
              d_q_norm, d_k_norm, d_rel_bias, w_up, w_down):
    def trunk(x, caches_even, caches_odd):
        new_even, new_odd = [], []
        for i in range(DEPTH):
            h = rms_norm(x, attn_norm[i])
            j = i // 2
            if i % 2 == 0:
                past = None if caches_even is None else tuple(c[j] for c in caches_even)
                y, st = even_mixer(h, past, w_in_even[j], w_out_even[j], a_q_norm[j], a_k_norm[j],
                                   b_q_norm[j], b_k_norm[j], forget_bias[j], t5_bias)
                new_even.append(st)
            else:
                past = None if caches_odd is None else tuple(c[j] for c in caches_odd)
                y, st = odd_mixer(h, past, w_in_odd[j], w_out_odd[j], d_q_norm[j], d_k_norm[j], d_rel_bias[j])
                new_odd.append(st)
            x = x + y
            x = x + squared_relu_mlp(rms_norm(x, mlp_norm[i]), w_up[i], w_down[i])
        st_even = tuple(jnp.stack(s) for s in zip(*new_even))
        st_odd = tuple(jnp.stack(s) for s in zip(*new_odd))
        return x, st_even, st_odd

    y_prompt, (pa_k, pa_v, pa_kidx, pb_k, pb_v, pb_logf), (pc_k, pc_v, pd_k, pd_v) = trunk(x_prompt, None, None)
    y_sample, (sa_k, sa_v, sa_kidx, sb_k, sb_v, sb_logf), (sc_k, sc_v, sd_k, sd_v) = trunk(
        x_sample,
        (cache_a_k, cache_a_v, cache_a_kidx, cache_b_k, cache_b_v, cache_b_logf),
        (cache_c_k, cache_c_v, cache_d_k, cache_d_v))
    return (y_prompt, y_sample,
            pa_k, pa_v, pa_kidx, pb_k, pb_v, pb_logf, pc_k, pc_v, pd_k, pd_v,
            sa_k, sa_v, sa_kidx, sb_k, sb_v, sb_logf, sc_k, sc_v, sd_k, sd_v)
```

```python
import functools

import jax
import jax.numpy as jnp
from jax import lax
from jax.experimental import pallas as pl
from jax.experimental.pallas import tpu as pltpu

F32 = jnp.float32
BF16 = jnp.bfloat16
I32 = jnp.int32

HEAD_DIM = 128
CHUNK = 64
CHUNK_SHIFT = 6
H_A = 8
KV_A = 2
REP_A = H_A // KV_A
H_B = 8
H_C = 8
H_D = 8
IDX_HEADS = 8
IDX_DIM = 64
TOPK_MAX = 256
T5_BUCKETS = 32
D_LEFT_CHUNKS = 8
D_BAND_LEFT = D_LEFT_CHUNKS * CHUNK
D_REL_CLIP = 128
EPS = 1e-6
NEG_INF = -1e30
SCALE = HEAD_DIM ** -0.5
LANE = 128
INT32_MIN = -(2 ** 31)

VMEM_LIMIT_BYTES = 56 * 1024 * 1024

EV_AQ, EV_BQ, EV_BK, EV_BV, EV_AK, EV_AV, EV_IQ, EV_MISC = 0, 1024, 2048, 3072, 4096, 4352, 4608, 5120
EV_WIDTH = 5248
EV_PAD = 5632
MISC_IK, MISC_IW, MISC_BF = 0, 64, 72
OD_DQ, OD_DK, OD_CQ, OD_CK, OD_CV, OD_DV = 0, 1024, 2048, 3072, 4096, 5120
OD_WIDTH = 6144


def _cparams(*sem):
    return pltpu.CompilerParams(dimension_semantics=sem, vmem_limit_bytes=VMEM_LIMIT_BYTES)


def _dot_nt(a, b):
    return lax.dot_general(a, b, (((1,), (1,)), ((), ())), preferred_element_type=F32)


def _dot(a, b):
    return jnp.dot(a, b, preferred_element_type=F32)


def _rmsnorm_kernel(x_ref, g_ref, o_ref):
    x = x_ref[...]
    ms = jnp.mean(x * x, axis=-1, keepdims=True)
    o_ref[...] = (x * lax.rsqrt(ms + EPS) * g_ref[...]).astype(o_ref.dtype)


def _rmsnorm(x2, gain):
    m, d = x2.shape
    tm = min(512, m)
    return pl.pallas_call(
        _rmsnorm_kernel,
        grid=(m // tm,),
        in_specs=[pl.BlockSpec((tm, d), lambda i: (i, 0)), pl.BlockSpec((1, d), lambda i: (0, 0))],
        out_specs=pl.BlockSpec((tm, d), lambda i: (i, 0)),
        out_shape=jax.ShapeDtypeStruct((m, d), BF16),
        compiler_params=_cparams("parallel"),
        name="rmsnorm",
    )(x2, gain.reshape(1, d))


def _mm_kernel(*refs, nk, mode, n_norm_tiles):
    a_ref, b_ref = refs[0], refs[1]
    if mode == "headnorm":
        gain_ref, flag_ref, o_ref = refs[2], refs[3], refs[4]
        rest = refs[5:]
    elif mode == "residual":
        res_ref, o_ref = refs[2], refs[3]
        rest = refs[4:]
    else:
        o_ref = refs[2]
        rest = refs[3:]

    def epilogue(acc):
        if mode == "headnorm":
            j = pl.program_id(1)

            @pl.when(j < n_norm_tiles)
            def _():
                tn = acc.shape[1]
                for g in range(tn // HEAD_DIM):
                    sl = slice(g * HEAD_DIM, (g + 1) * HEAD_DIM)
                    blk = acc[:, sl]
                    ms = jnp.mean(blk * blk, axis=-1, keepdims=True)
                    normed = blk * lax.rsqrt(ms + EPS) * gain_ref[:, sl]
                    o_ref[:, sl] = jnp.where(flag_ref[:, sl] > 0.0, normed, blk)

            @pl.when(j >= n_norm_tiles)
            def _():
                o_ref[...] = acc
        elif mode == "relu2":
            r = jnp.maximum(acc, 0.0)
            o_ref[...] = (r * r).astype(o_ref.dtype)
        elif mode == "residual":
            o_ref[...] = res_ref[...] + acc
        else:
            o_ref[...] = acc.astype(o_ref.dtype)

    if nk == 1:
        epilogue(_dot(a_ref[...], b_ref[...]))
    else:
        acc_ref = rest[0]
        k = pl.program_id(2)

        @pl.when(k == 0)
        def _():
            acc_ref[...] = jnp.zeros_like(acc_ref)

        acc_ref[...] += _dot(a_ref[...], b_ref[...])

        @pl.when(k == nk - 1)
        def _():
            epilogue(acc_ref[...])


def _matmul(a, b, *, mode="plain", out_dtype=F32, gain=None, flag=None, n_norm_tiles=0, residual=None,
            tn=512, tk=2048, name="matmul"):
    m, kdim = a.shape
    _, n = b.shape
    tm = 1024 if m % 1024 == 0 else min(512, m)
    nk = kdim // tk
    in_specs = [pl.BlockSpec((tm, tk), lambda i, j, k: (i, k)), pl.BlockSpec((tk, tn), lambda i, j, k: (k, j))]
    args = [a, b]
    if mode == "headnorm":
        in_specs += [pl.BlockSpec((1, tn), lambda i, j, k: (0, j)), pl.BlockSpec((1, tn), lambda i, j, k: (0, j))]
        args += [gain, flag]
    elif mode == "residual":
        in_specs += [pl.BlockSpec((tm, tn), lambda i, j, k: (i, j))]
        args += [residual]
    scratch = [pltpu.VMEM((tm, tn), F32)] if nk > 1 else []
    return pl.pallas_call(
        functools.partial(_mm_kernel, nk=nk, mode=mode, n_norm_tiles=n_norm_tiles),
        grid=(m // tm, n // tn, nk),
        in_specs=in_specs,
        out_specs=pl.BlockSpec((tm, tn), lambda i, j, k: (i, j)),
        out_shape=jax.ShapeDtypeStruct((m, n), out_dtype),
        scratch_shapes=scratch,
        compiler_params=_cparams("parallel", "parallel", "arbitrary"),
        name=name,
    )(*args)


def _cumsum_block(x, carry):
    n = x.shape[1]
    r = lax.broadcasted_iota(I32, (n, n), 0)
    c = lax.broadcasted_iota(I32, (n, n), 1)
    tri = jnp.where(r <= c, 1.0, 0.0).astype(BF16)
    hi = x.astype(BF16)
    r1 = x - hi.astype(F32)
    mid = r1.astype(BF16)
    lo = (r1 - mid.astype(F32)).astype(BF16)
    cs = _dot(hi, tri) + _dot(mid, tri) + _dot(lo, tri) + carry
    return cs, cs[:, n - 1:n]


def _logf_cum_kernel(*refs, p_len, t_len, blk):
    if p_len:
        raw_ref, fb_ref, past_ref, lf_ref, cum_ref = refs
    else:
        raw_ref, fb_ref, lf_ref, cum_ref = refs
    x = raw_ref[0] + fb_ref[...]
    lf = jnp.minimum(x, 0.0) - jnp.log1p(jnp.exp(-jnp.abs(x)))
    lf_ref[0] = lf
    carry = jnp.zeros((H_B, 1), F32)
    for s in range(0, p_len, blk):
        cs, carry = _cumsum_block(past_ref[0, :, s:s + blk], carry)
        cum_ref[0, :, s:s + blk] = cs
    nb = min(blk, t_len)
    for s in range(0, t_len, nb):
        cs, carry = _cumsum_block(lf[:, s:s + nb], carry)
        cum_ref[0, :, p_len + s:p_len + s + nb] = cs


def _logf_cum(raw_h, fbias, past_h):
    b, h, t = raw_h.shape
    p = 0 if past_h is None else past_h.shape[2]
    in_specs = [pl.BlockSpec((1, h, t), lambda i: (i, 0, 0)), pl.BlockSpec((h, 1), lambda i: (0, 0))]
    args = [raw_h, fbias.reshape(h, 1)]
    if p:
        in_specs.append(pl.BlockSpec((1, h, p), lambda i: (i, 0, 0)))
        args.append(past_h)
    return pl.pallas_call(
        functools.partial(_logf_cum_kernel, p_len=p, t_len=t, blk=512),
        grid=(b,),
        in_specs=in_specs,
        out_specs=[pl.BlockSpec((1, h, t), lambda i: (i, 0, 0)), pl.BlockSpec((1, h, p + t), lambda i: (i, 0, 0))],
        out_shape=[jax.ShapeDtypeStruct((b, h, t), F32), jax.ShapeDtypeStruct((b, h, p + t), F32)],
        compiler_params=_cparams("parallel"),
        name="logf_cum",
    )(*args)


def _toeplitz(row_vals, rows, cols):
    w = row_vals.shape[1]
    full = jnp.broadcast_to(row_vals[0:1, :], (rows, w))
    return pltpu.roll(full, 0, 1, stride=1, stride_axis=0)[:, :cols]


def _t5_strip_kernel(tab_ref, o_ref, *, rows, cols, width):
    h = pl.program_id(0)
    m = lax.broadcasted_iota(I32, (8, width), 1)
    d = jnp.where(m < width // 2, m, m - width)
    rel = d - LANE
    n = jnp.abs(rel)
    large = jnp.full_like(n, 8)
    for thr in (12, 16, 23, 32, 46, 64, 91):
        large = large + jnp.where(n >= thr, 1, 0)
    bucket = jnp.where(rel > 0, T5_BUCKETS // 2, 0) + jnp.where(n < 8, n, large)
    vals = jnp.zeros((8, width), F32)
    for bkt in range(T5_BUCKETS):
        vals = jnp.where(bucket == bkt, tab_ref[bkt, h], vals)
    vals = vals - tab_ref[T5_BUCKETS // 2 - 1, h]
    o_ref[0] = _toeplitz(vals, rows, cols)


def _t5_strip(t5_table):
    rows, cols, width = LANE, 2 * LANE, 4 * LANE
    return pl.pallas_call(
        functools.partial(_t5_strip_kernel, rows=rows, cols=cols, width=width),
        grid=(H_A,),
        in_specs=[pl.BlockSpec(memory_space=pltpu.SMEM)],
        out_specs=pl.BlockSpec((1, rows, cols), lambda h: (h, 0, 0)),
        out_shape=jax.ShapeDtypeStruct((H_A, rows, cols), F32),
        compiler_params=_cparams("parallel"),
        name="t5_strip",
    )(t5_table)


def _band_bias_kernel(tab_ref, o_ref, *, rows, cols, width):
    h = pl.program_id(0)
    m = lax.broadcasted_iota(I32, (8, width), 1)
    d = jnp.where(m < width // 2, m, m - width)
    idx = jnp.clip(d - D_BAND_LEFT, -D_REL_CLIP, D_REL_CLIP) + D_REL_CLIP
    vals = jnp.zeros((8, width), F32)
    for r in range(2 * D_REL_CLIP + 1):
        vals = jnp.where(idx == r, tab_ref[r, h], vals)
    vals = vals - tab_ref[0, h]
    o_ref[0] = _toeplitz(vals, rows, cols)


def _band_bias(rel_table):
    rows, cols, width = D_BAND_LEFT, 2 * D_BAND_LEFT, 4 * D_BAND_LEFT
    return pl.pallas_call(
        functools.partial(_band_bias_kernel, rows=rows, cols=cols, width=width),
        grid=(H_D,),
        in_specs=[pl.BlockSpec(memory_space=pltpu.SMEM)],
        out_specs=pl.BlockSpec((1, rows, cols), lambda h: (h, 0, 0)),
        out_shape=jax.ShapeDtypeStruct((H_D, rows, cols), F32),
        compiler_params=_cparams("parallel"),
        name="band_bias",
    )(rel_table)


def _select_kernel(iq_ref, misc_ref, kk_ref, o_ref, key_ref, *, tq, lp, l_valid, q_base, n_sel):
    qi = pl.program_id(1)
    iq = iq_ref[0]
    iw = misc_ref[0][:, MISC_IW:MISC_IW + IDX_HEADS]
    ik = kk_ref[0][:, :IDX_DIM].astype(BF16)
    score = jnp.zeros((tq, lp), F32)
    for h in range(IDX_HEADS):
        dots = _dot_nt(iq[:, h * IDX_DIM:(h + 1) * IDX_DIM].astype(BF16), ik)
        score = score + iw[:, h:h + 1] * jnp.maximum(dots, 0.0)
    q_pos = q_base + qi * tq + lax.broadcasted_iota(I32, (tq, 1), 0)
    k_pos = lax.broadcasted_iota(I32, (1, lp), 1)
    adm = ((k_pos >> CHUNK_SHIFT) <= (q_pos >> CHUNK_SHIFT)) & (k_pos < l_valid)
    score = jnp.where(adm, score, NEG_INF)
    bits = lax.bitcast_convert_type(score, I32)
    key_ref[...] = jnp.where(bits < 0, bits ^ jnp.int32(0x7FFFFFFF), bits)

    def count(pred):
        return jnp.sum(jnp.where(pred, 1.0, 0.0), axis=-1, keepdims=True)

    kf = float(n_sel)
    t0 = jnp.where(count(key_ref[...] >= 0) >= kf, jnp.int32(0), jnp.int32(INT32_MIN))

    def thr_body(i, t):
        cand = t + (jnp.int32(1) << (30 - i))
        return jnp.where(count(key_ref[...] >= cand) >= kf, cand, t)

    thr = lax.fori_loop(0, 31, thr_body, t0)
    need = kf - count(key_ref[...] > thr)

    nbits = max(lp - 1, 1).bit_length()

    def tie_body(i, jb):
        cand = jb + (jnp.int32(1) << (nbits - 1 - i))
        taken = count((key_ref[...] == thr) & (k_pos < cand))
        return jnp.where(taken <= need, cand, jb)

    jbound = lax.fori_loop(0, nbits, tie_body, jnp.zeros((tq, 1), I32))
    key = key_ref[...]
    sel = ((key > thr) | ((key == thr) & (k_pos < jbound))) & adm
    madd = jnp.where(sel, 0.0, NEG_INF)
    for kt in range(lp // LANE):
        o_ref[0, 0, kt] = madd[:, kt * LANE:(kt + 1) * LANE]


def _select_mask(iq_arr, iq_col, misc_arr, misc_col, kk_arr, kk_col, *, t, tq, lp, l_valid, q_base, n_sel):
    b = iq_arr.shape[0]
    nq = t // tq
    wk = kk_arr.shape[2] if kk_col is None else LANE
    kcol = 0 if kk_col is None else kk_col
    return pl.pallas_call(
        functools.partial(_select_kernel, tq=tq, lp=lp, l_valid=l_valid, q_base=q_base, n_sel=n_sel),
        grid=(b, nq),
        in_specs=[
            pl.BlockSpec((1, tq, IDX_HEADS * IDX_DIM), lambda i, j: (i, j, iq_col)),
            pl.BlockSpec((1, tq, LANE), lambda i, j: (i, j, misc_col)),
            pl.BlockSpec((1, lp, wk), lambda i, j: (i, 0, kcol)),
        ],
        out_specs=pl.BlockSpec((1, 1, lp // LANE, tq, LANE), lambda i, j: (i, j, 0, 0, 0)),
        out_shape=jax.ShapeDtypeStruct((b, nq, lp // LANE, tq, LANE), F32),
        scratch_shapes=[pltpu.VMEM((tq, lp), I32)],
        compiler_params=_cparams("parallel", "parallel"),
        name="a_select",
    )(iq_arr, misc_arr, kk_arr)


def _attn_a_kernel(q_ref, k_ref, v_ref, mask_ref, tz_ref, o_ref, m_s, l_s, acc_s, *, tq, q_base):
    qi = pl.program_id(2)
    q = q_ref[0] * SCALE
    qs = jnp.concatenate([q[:, r * HEAD_DIM:(r + 1) * HEAD_DIM] for r in range(REP_A)], axis=0).astype(BF16)
    m_s[...] = jnp.full_like(m_s, NEG_INF)
    l_s[...] = jnp.zeros_like(l_s)
    acc_s[...] = jnp.zeros_like(acc_s)

    def tile(kt, bias):
        off = pl.multiple_of(kt * LANE, LANE)
        k = k_ref[0, pl.ds(off, LANE), :].astype(BF16)
        v = v_ref[0, pl.ds(off, LANE), :].astype(BF16)
        s = _dot_nt(qs, k).reshape(REP_A, tq, LANE) + mask_ref[0, 0, kt][None]
        if bias is not None:
            s = s + bias
        m_prev = m_s[...]
        m_new = jnp.maximum(m_prev, jnp.max(s, axis=-1, keepdims=True))
        alpha = jnp.exp(m_prev - m_new)
        p = jnp.exp(s - m_new)
        l_s[...] = alpha * l_s[...] + jnp.sum(p, axis=-1, keepdims=True)
        pv = _dot(p.reshape(REP_A * tq, LANE).astype(BF16), v).reshape(REP_A, tq, HEAD_DIM)
        acc_s[...] = alpha * acc_s[...] + pv
        m_s[...] = m_new

    t_diag = (q_base + qi * tq) // LANE

    def plain_body(kt, carry):
        tile(kt, None)
        return carry

    lax.fori_loop(0, jnp.maximum(t_diag - 1, 0), plain_body, 0)

    @pl.when(t_diag >= 1)
    def _():
        tile(t_diag - 1, tz_ref[:, :, 0:LANE])

    tile(t_diag, tz_ref[:, :, LANE:2 * LANE])
    out = acc_s[...] / l_s[...]
    for r in range(REP_A):
        o_ref[0, :, r * HEAD_DIM:(r + 1) * HEAD_DIM] = out[r].astype(o_ref.dtype)


def _attn_a(q_arr, q_col, k_arr, k_col, v_arr, v_col, mask, tz, *, t, tq, lp, q_base):
    b = q_arr.shape[0]
    nq = t // tq
    nkt = lp // LANE
    return pl.pallas_call(
        functools.partial(_attn_a_kernel, tq=tq, q_base=q_base),
        grid=(b, KV_A, nq),
        in_specs=[
            pl.BlockSpec((1, tq, REP_A * HEAD_DIM), lambda i, g, j: (i, j, q_col + g)),
            pl.BlockSpec((1, lp, HEAD_DIM), lambda i, g, j: (i, 0, k_col + g)),
            pl.BlockSpec((1, lp, HEAD_DIM), lambda i, g, j: (i, 0, v_col + g)),
            pl.BlockSpec((1, 1, nkt, tq, LANE), lambda i, g, j: (i, j, 0, 0, 0)),
            pl.BlockSpec((REP_A, tq, 2 * LANE), lambda i, g, j: (g, 0, 0)),
        ],
        out_specs=pl.BlockSpec((1, tq, REP_A * HEAD_DIM), lambda i, g, j: (i, j, g)),
        out_shape=jax.ShapeDtypeStruct((b, t, H_A * HEAD_DIM), BF16),
        scratch_shapes=[pltpu.VMEM((REP_A, tq, 1), F32), pltpu.VMEM((REP_A, tq, 1), F32),
                        pltpu.VMEM((REP_A, tq, HEAD_DIM), F32)],
        compiler_params=_cparams("parallel", "parallel", "parallel"),
        name="attn_a",
    )(q_arr, k_arr, v_arr, mask, tz)


def _softmax_update(s, v_bf16, m_s, l_s, acc_s):
    m_prev = m_s[...]
    m_new = jnp.maximum(m_prev, jnp.max(s, axis=-1, keepdims=True))
    alpha = jnp.exp(m_prev - m_new)
    p = jnp.exp(s - m_new)
    l_s[...] = alpha * l_s[...] + jnp.sum(p, axis=-1, keepdims=True)
    acc_s[...] = alpha * acc_s[...] + _dot(p.astype(BF16), v_bf16)
    m_s[...] = m_new


def _attn_b_prompt_kernel(q_ref, k_ref, v_ref, ck_ref, o_ref, m_s, l_s, acc_s, *, tq):
    qi = pl.program_id(2)
    q = (q_ref[0] * SCALE).astype(BF16)
    m_s[...] = jnp.full_like(m_s, NEG_INF)
    l_s[...] = jnp.zeros_like(l_s)
    acc_s[...] = jnp.zeros_like(acc_s)

    def tile(kt, diag):
        off = pl.multiple_of(kt * tq, tq)
        k = k_ref[0, pl.ds(off, tq), :].astype(BF16)
        v = v_ref[0, pl.ds(off, tq), :].astype(BF16)
        s = _dot_nt(q, k) - ck_ref[0, 0, kt]
        if diag:
            r = lax.broadcasted_iota(I32, (tq, tq), 0)
            c = lax.broadcasted_iota(I32, (tq, tq), 1)
            s = jnp.where(c <= r, s, NEG_INF)
        _softmax_update(s, v, m_s, l_s, acc_s)

    def body(kt, carry):
        tile(kt, False)
        return carry

    lax.fori_loop(0, qi, body, 0)
    tile(qi, True)
    o_ref[0] = (acc_s[...] / l_s[...]).astype(o_ref.dtype)


def _attn_b_prompt(proj3, cum_h, *, tq=256):
    b, t, _ = proj3.shape
    nq = t // tq
    ck = cum_h.reshape(b, H_B, nq, 1, tq)
    qc, kc, vc = EV_BQ // HEAD_DIM, EV_BK // HEAD_DIM, EV_BV // HEAD_DIM
    return pl.pallas_call(
        functools.partial(_attn_b_prompt_kernel, tq=tq),
        grid=(b, H_B, nq),
        in_specs=[
            pl.BlockSpec((1, tq, HEAD_DIM), lambda i, h, j: (i, j, qc + h)),
            pl.BlockSpec((1, t, HEAD_DIM), lambda i, h, j: (i, 0, kc + h)),
            pl.BlockSpec((1, t, HEAD_DIM), lambda i, h, j: (i, 0, vc + h)),
            pl.BlockSpec((1, 1, nq, 1, tq), lambda i, h, j: (i, h, 0, 0, 0)),
        ],
        out_specs=pl.BlockSpec((1, tq, HEAD_DIM), lambda i, h, j: (i, j, h)),
        out_shape=jax.ShapeDtypeStruct((b, t, H_B * HEAD_DIM), BF16),
        scratch_shapes=[pltpu.VMEM((tq, 1), F32), pltpu.VMEM((tq, 1), F32), pltpu.VMEM((tq, HEAD_DIM), F32)],
        compiler_params=_cparams("parallel", "parallel", "parallel"),
        name="attn_b_prompt",
    )(proj3, proj3, proj3, ck)


def _attn_b_step_kernel(q_ref, kc_ref, vc_ref, kn_ref, vn_ref, ckc_ref, ckn_ref, o_ref, m_s, l_s, acc_s,
                        *, nkc, t):
    kb = pl.program_id(1)

    @pl.when(kb == 0)
    def _():
        m_s[...] = jnp.full_like(m_s, NEG_INF)
        l_s[...] = jnp.zeros_like(l_s)
        acc_s[...] = jnp.zeros_like(acc_s)

    def heads(k_ref, v_ref, ck, causal):
        for h in range(H_B):
            sl = slice(h * HEAD_DIM, (h + 1) * HEAD_DIM)
            q = (q_ref[0, :, sl] * SCALE).astype(BF16)
            s = _dot_nt(q, k_ref[0, :, sl].astype(BF16)) - ck[h:h + 1, :]
            if causal:
                r = lax.broadcasted_iota(I32, s.shape, 0)
                c = lax.broadcasted_iota(I32, s.shape, 1)
                s = jnp.where(c <= r, s, NEG_INF)
            _softmax_update(s, v_ref[0, :, sl].astype(BF16), m_s.at[h], l_s.at[h], acc_s.at[h])

    @pl.when(kb < nkc)
    def _():
        heads(kc_ref, vc_ref, ckc_ref[0, 0], False)

    @pl.when(kb == nkc)
    def _():
        heads(kn_ref, vn_ref, ckn_ref[0], True)
        for h in range(H_B):
            o_ref[0, :, h * HEAD_DIM:(h + 1) * HEAD_DIM] = (acc_s[h] / l_s[h]).astype(o_ref.dtype)


def _attn_b_step(proj3, cache_k, cache_v, cum_h, *, tkc=1024):
    b, t, _ = proj3.shape
    p = cache_k.shape[1]
    nkc = p // tkc
    w = H_B * HEAD_DIM
    ck_cache = cum_h[:, :, :p].reshape(b, H_B, nkc, tkc).transpose(0, 2, 1, 3)
    ck_new = cum_h[:, :, p:]
    last = nkc - 1
    return pl.pallas_call(
        functools.partial(_attn_b_step_kernel, nkc=nkc, t=t),
        grid=(b, nkc + 1),
        in_specs=[
            pl.BlockSpec((1, t, w), lambda i, j: (i, 0, EV_BQ // w)),
            pl.BlockSpec((1, tkc, w), lambda i, j: (i, jnp.minimum(j, last), 0)),
            pl.BlockSpec((1, tkc, w), lambda i, j: (i, jnp.minimum(j, last), 0)),
            pl.BlockSpec((1, t, w), lambda i, j: (i, 0, EV_BK // w)),
            pl.BlockSpec((1, t, w), lambda i, j: (i, 0, EV_BV // w)),
            pl.BlockSpec((1, 1, H_B, tkc), lambda i, j: (i, jnp.minimum(j, last), 0, 0)),
            pl.BlockSpec((1, H_B, t), lambda i, j: (i, 0, 0)),
        ],
        out_specs=pl.BlockSpec((1, t, w), lambda i, j: (i, 0, 0)),
        out_shape=jax.ShapeDtypeStruct((b, t, w), BF16),
        scratch_shapes=[pltpu.VMEM((H_B, t, 1), F32), pltpu.VMEM((H_B, t, 1), F32),
                        pltpu.VMEM((H_B, t, HEAD_DIM), F32)],
        compiler_params=_cparams("parallel", "arbitrary"),
        name="attn_b_step",
    )(proj3, cache_k, cache_v, proj3, proj3, ck_cache, ck_new)


def _stick_tile(q_bf16, k_bf16, v_bf16, run, strict):
    tq, tk = q_bf16.shape[0], k_bf16.shape[0]
    z = _dot_nt(q_bf16, k_bf16)
    tail = jnp.log1p(jnp.exp(-jnp.abs(z)))
    log_beta = jnp.minimum(z, 0.0) - tail
    log_keep = -jnp.maximum(z, 0.0) - tail
    if strict is not None:
        log_keep = jnp.where(strict, log_keep, 0.0)
    r = lax.broadcasted_iota(I32, (tk, tk), 0)
    c = lax.broadcasted_iota(I32, (tk, tk), 1)
    tri = jnp.where(r > c, 1.0, 0.0).astype(BF16)
    hi = log_keep.astype(BF16)
    lo = (log_keep - hi.astype(F32)).astype(BF16)
    both = _dot(jnp.concatenate([hi, lo], axis=0), tri)
    later = both[:tq] + both[tq:] + run
    w = jnp.exp(log_beta + later)
    if strict is not None:
        w = jnp.where(strict, w, 0.0)
    new_run = run + jnp.sum(log_keep, axis=-1, keepdims=True)
    return _dot(w.astype(BF16), v_bf16), new_run


def _attn_c_prompt_kernel(q_ref, k_ref, v_ref, o_ref, run_s, acc_s, *, tq):
    qi = pl.program_id(2)
    q = (q_ref[0] * SCALE).astype(BF16)

    def load(kt):
        off = pl.multiple_of(kt * tq, tq)
        return k_ref[0, pl.ds(off, tq), :].astype(BF16), v_ref[0, pl.ds(off, tq), :].astype(BF16)

    r = lax.broadcasted_iota(I32, (tq, tq), 0)
    c = lax.broadcasted_iota(I32, (tq, tq), 1)
    k, v = load(qi)
    out, run = _stick_tile(q, k, v, jnp.zeros((tq, 1), F32), c < r)
    acc_s[...] = out
    run_s[...] = run

    def body(i, carry):
        k, v = load(qi - 1 - i)
        out, run = _stick_tile(q, k, v, run_s[...], None)
        acc_s[...] += out
        run_s[...] = run
        return carry

    lax.fori_loop(0, qi, body, 0)
    o_ref[0] = acc_s[...].astype(o_ref.dtype)


def _attn_c_prompt(proj3, *, tq=256):
    b, t, _ = proj3.shape
    nq = t // tq
    qc, kc, vc = OD_CQ // HEAD_DIM, OD_CK // HEAD_DIM, OD_CV // HEAD_DIM
    return pl.pallas_call(
        functools.partial(_attn_c_prompt_kernel, tq=tq),
        grid=(b, H_C, nq),
        in_specs=[
            pl.BlockSpec((1, tq, HEAD_DIM), lambda i, h, j: (i, j, qc + h)),
            pl.BlockSpec((1, t, HEAD_DIM), lambda i, h, j: (i, 0, kc + h)),
            pl.BlockSpec((1, t, HEAD_DIM), lambda i, h, j: (i, 0, vc + h)),
        ],
        out_specs=pl.BlockSpec((1, tq, HEAD_DIM), lambda i, h, j: (i, j, h)),
        out_shape=jax.ShapeDtypeStruct((b, t, H_C * HEAD_DIM), BF16),
        scratch_shapes=[pltpu.VMEM((tq, 1), F32), pltpu.VMEM((tq, HEAD_DIM), F32)],
        compiler_params=_cparams("parallel", "parallel", "parallel"),
        name="attn_c_prompt",
    )(proj3, proj3, proj3)


def _attn_c_step_kernel(q_ref, kc_ref, vc_ref, kn_ref, vn_ref, o_ref, run_s, acc_s, *, nkc, t, tkc, sub):
    kb = pl.program_id(1)

    @pl.when(kb == 0)
    def _():
        r = lax.broadcasted_iota(I32, (t, t), 0)
        c = lax.broadcasted_iota(I32, (t, t), 1)
        for h in range(H_C):
            sl = slice(h * HEAD_DIM, (h + 1) * HEAD_DIM)
            q = (q_ref[0, :, sl] * SCALE).astype(BF16)
            out, run = _stick_tile(q, kn_ref[0, :, sl].astype(BF16), vn_ref[0, :, sl].astype(BF16),
                                   jnp.zeros((t, 1), F32), c < r)
            acc_s[h] = out
            run_s[h] = run

    @pl.when(kb > 0)
    def _():
        def body(i, carry):
            off = pl.multiple_of(tkc - sub - i * sub, sub)
            for h in range(H_C):
                sl = slice(h * HEAD_DIM, (h + 1) * HEAD_DIM)
                q = (q_ref[0, :, sl] * SCALE).astype(BF16)
                k = kc_ref[0, pl.ds(off, sub), sl].astype(BF16)
                v = vc_ref[0, pl.ds(off, sub), sl].astype(BF16)
                out, run = _stick_tile(q, k, v, run_s[h], None)
                acc_s[h] += out
                run_s[h] = run
            return carry

        lax.fori_loop(0, tkc // sub, body, 0)

    @pl.when(kb == nkc)
    def _():
        for h in range(H_C):
            o_ref[0, :, h * HEAD_DIM:(h + 1) * HEAD_DIM] = acc_s[h].astype(o_ref.dtype)


def _attn_c_step(proj3, cache_k, cache_v, *, tkc=1024, sub=256):
    b, t, _ = proj3.shape
    p = cache_k.shape[1]
    nkc = p // tkc
    w = H_C * HEAD_DIM

    def cache_idx(i, j):
        return (i, jnp.clip(nkc - j, 0, nkc - 1), 0)

    return pl.pallas_call(
        functools.partial(_attn_c_step_kernel, nkc=nkc, t=t, tkc=tkc, sub=sub),
        grid=(b, nkc + 1),
        in_specs=[
            pl.BlockSpec((1, t, w), lambda i, j: (i, 0, OD_CQ // w)),
            pl.BlockSpec((1, tkc, w), cache_idx),
            pl.BlockSpec((1, tkc, w), cache_idx),
            pl.BlockSpec((1, t, w), lambda i, j: (i, 0, OD_CK // w)),
            pl.BlockSpec((1, t, w), lambda i, j: (i, 0, OD_CV // w)),
        ],
        out_specs=pl.BlockSpec((1, t, w), lambda i, j: (i, 0, 0)),
        out_shape=jax.ShapeDtypeStruct((b, t, w), BF16),
        scratch_shapes=[pltpu.VMEM((H_C, t, 1), F32), pltpu.VMEM((H_C, t, HEAD_DIM), F32)],
        compiler_params=_cparams("parallel", "arbitrary"),
        name="attn_c_step",
    )(proj3, cache_k, cache_v, proj3, proj3)


def _attn_d_kernel(q_ref, kp_ref, vp_ref, ko_ref, vo_ref, bias_ref, o_ref, *, tq, q_base):
    qi = pl.program_id(2)
    q = (q_ref[0] * SCALE).astype(BF16)
    q0 = q_base + qi * tq
    q_chunk = (q0 + lax.broadcasted_iota(I32, (tq, 1), 0)) >> CHUNK_SHIFT
    kp_pos = q0 - D_BAND_LEFT + lax.broadcasted_iota(I32, (1, D_BAND_LEFT), 1)
    ko_pos = q0 + lax.broadcasted_iota(I32, (1, tq), 1)
    vis_p = (kp_pos >= 0) & ((kp_pos >> CHUNK_SHIFT) >= q_chunk - D_LEFT_CHUNKS)
    vis_o = (ko_pos >> CHUNK_SHIFT) <= q_chunk
    s_p = _dot_nt(q, kp_ref[0].astype(BF16)) + bias_ref[0, :, 0:D_BAND_LEFT]
    s_o = _dot_nt(q, ko_ref[0].astype(BF16)) + bias_ref[0, :, D_BAND_LEFT:D_BAND_LEFT + tq]
    s_p = jnp.where(vis_p, s_p, NEG_INF)
    s_o = jnp.where(vis_o, s_o, NEG_INF)
    m = jnp.maximum(jnp.max(s_p, axis=-1, keepdims=True), jnp.max(s_o, axis=-1, keepdims=True))
    p_p = jnp.exp(s_p - m)
    p_o = jnp.exp(s_o - m)
    l = jnp.sum(p_p, axis=-1, keepdims=True) + jnp.sum(p_o, axis=-1, keepdims=True)
    out = _dot(p_p.astype(BF16), vp_ref[0].astype(BF16)) + _dot(p_o.astype(BF16), vo_ref[0].astype(BF16))
    o_ref[0] = (out / l).astype(o_ref.dtype)


def _attn_d(q_arr, q_col, kprev, kprev_col, vprev, vprev_col, kown, kown_col, vown, vown_col, bias,
            *, t, tq, q_base, prev_follows_own):
    b = q_arr.shape[0]
    nq = t // tq
    if prev_follows_own:
        def prev_idx(col):
            return lambda i, h, j: (i, jnp.maximum(j - 1, 0), col + h)
    else:
        def prev_idx(col):
            return lambda i, h, j: (i, 0, col + h)
    return pl.pallas_call(
        functools.partial(_attn_d_kernel, tq=tq, q_base=q_base),
        grid=(b, H_D, nq),
        in_specs=[
            pl.BlockSpec((1, tq, HEAD_DIM), lambda i, h, j: (i, j, q_col + h)),
            pl.BlockSpec((1, D_BAND_LEFT, HEAD_DIM), prev_idx(kprev_col)),
            pl.BlockSpec((1, D_BAND_LEFT, HEAD_DIM), prev_idx(vprev_col)),
            pl.BlockSpec((1, tq, HEAD_DIM), lambda i, h, j: (i, j, kown_col + h)),
            pl.BlockSpec((1, tq, HEAD_DIM), lambda i, h, j: (i, j, vown_col + h)),
            pl.BlockSpec((1, tq, D_BAND_LEFT + tq), lambda i, h, j: (h, 0, 0)),
        ],
        out_specs=pl.BlockSpec((1, tq, HEAD_DIM), lambda i, h, j: (i, j, h)),
        out_shape=jax.ShapeDtypeStruct((b, t, H_D * HEAD_DIM), BF16),
        compiler_params=_cparams("parallel", "parallel", "parallel"),
        name="attn_d",
    )(q_arr, kprev, vprev, kown, vown, bias)


def _pad_rows(x, rows):
    return jnp.pad(x, ((0, 0), (0, rows - x.shape[1]), (0, 0)))


def _even_mixer(h2, b, t, past, w, tz):
    proj = _matmul(h2, w["w_in"], mode="headnorm", gain=w["gain"], flag=w["flag"], n_norm_tiles=9,
                   name="proj_even")
    proj3 = proj.reshape(b, t, EV_PAD)
    ak = proj3[:, :, EV_AK:EV_AK + KV_A * HEAD_DIM]
    av = proj3[:, :, EV_AV:EV_AV + KV_A * HEAD_DIM]
    ik = proj3[:, :, EV_MISC + MISC_IK:EV_MISC + MISC_IK + IDX_DIM]
    bk = proj3[:, :, EV_BK:EV_BK + H_B * HEAD_DIM]
    bv = proj3[:, :, EV_BV:EV_BV + H_B * HEAD_DIM]
    raw_h = proj3[:, :, EV_MISC + MISC_BF:EV_MISC + MISC_BF + H_B].transpose(0, 2, 1)
    iq_col = EV_IQ // (IDX_HEADS * IDX_DIM)
    misc_col = EV_MISC // LANE
    aq_col = EV_AQ // (REP_A * HEAD_DIM)
    if past is None:
        logf_h, cum_h = _logf_cum(raw_h, w["f_bias"], None)
        n_sel = min(TOPK_MAX, t // 4)
        tq = LANE
        mask = _select_mask(proj3, iq_col, proj3, misc_col, proj3, misc_col,
                            t=t, tq=tq, lp=t, l_valid=t, q_base=0, n_sel=n_sel)
        out_a = _attn_a(proj3, aq_col, proj3, EV_AK // HEAD_DIM, proj3, EV_AV // HEAD_DIM, mask, tz,
                        t=t, tq=tq, lp=t, q_base=0)
        out_b = _attn_b_prompt(proj3, cum_h)
    else:
        p_ak, p_av, p_ik, p_bk, p_bv, p_lf = past
        p = p_ak.shape[1]
        l_valid = p + t
        lp = -(-l_valid // LANE) * LANE
        logf_h, cum_h = _logf_cum(raw_h, w["f_bias"], p_lf.transpose(0, 2, 1))
        n_sel = min(TOPK_MAX, l_valid // 4)
        ik_all = _pad_rows(jnp.concatenate([p_ik, ik], axis=1), lp)
        ak_all = _pad_rows(jnp.concatenate([p_ak.reshape(b, p, -1), ak], axis=1), lp)
        av_all = _pad_rows(jnp.concatenate([p_av.reshape(b, p, -1), av], axis=1), lp)
        mask = _select_mask(proj3, iq_col, proj3, misc_col, ik_all, None,
                            t=t, tq=t, lp=lp, l_valid=l_valid, q_base=p, n_sel=n_sel)
        out_a = _attn_a(proj3, aq_col, ak_all, 0, av_all, 0, mask, tz[:, :t], t=t, tq=t, lp=lp, q_base=p)
        out_b = _attn_b_step(proj3, p_bk.reshape(b, p, -1), p_bv.reshape(b, p, -1), cum_h)
    mixed = jnp.concatenate([out_a, out_b], axis=-1).reshape(b * t, -1)
    state = (ak.reshape(b, t, KV_A, HEAD_DIM), av.reshape(b, t, KV_A, HEAD_DIM), ik,
             bk.reshape(b, t, H_B, HEAD_DIM), bv.reshape(b, t, H_B, HEAD_DIM), logf_h.transpose(0, 2, 1))
    return mixed, state


def _odd_mixer(h2, b, t, past, w, band):
    proj = _matmul(h2, w["w_in"], mode="headnorm", gain=w["gain"], flag=w["flag"], n_norm_tiles=4,
                   name="proj_odd")
    proj3 = proj.reshape(b, t, OD_WIDTH)
    ck = proj3[:, :, OD_CK:OD_CK + H_C * HEAD_DIM]
    cv = proj3[:, :, OD_CV:OD_CV + H_C * HEAD_DIM]
    dk = proj3[:, :, OD_DK:OD_DK + H_D * HEAD_DIM]
    dv = proj3[:, :, OD_DV:OD_DV + H_D * HEAD_DIM]
    dq_col, dk_col, dv_col = OD_DQ // HEAD_DIM, OD_DK // HEAD_DIM, OD_DV // HEAD_DIM
    if past is None:
        out_c = _attn_c_prompt(proj3)
        out_d = _attn_d(proj3, dq_col, proj3, dk_col, proj3, dv_col, proj3, dk_col, proj3, dv_col, band,
                        t=t, tq=D_BAND_LEFT, q_base=0, prev_follows_own=True)
        keep = min(D_BAND_LEFT, t)
        d_rows = (dk[:, t - keep:], dv[:, t - keep:])
    else:
        p_ck, p_cv, p_dk, p_dv = past
        p = p_ck.shape[1]
        out_c = _attn_c_step(proj3, p_ck.reshape(b, p, -1), p_cv.reshape(b, p, -1))
        wd = p_dk.shape[1]
        out_d = _attn_d(proj3, dq_col, p_dk.reshape(b, wd, -1), 0, p_dv.reshape(b, wd, -1), 0,
                        proj3, dk_col, proj3, dv_col, band[:, :t, :D_BAND_LEFT + t],
                        t=t, tq=t, q_base=p, prev_follows_own=False)
        d_rows = (dk, dv)
    mixed = jnp.concatenate([out_c, out_d], axis=-1).reshape(b * t, -1)
    state = (ck.reshape(b, t, H_C, HEAD_DIM), cv.reshape(b, t, H_C, HEAD_DIM),
             d_rows[0].reshape(b, -1, H_D, HEAD_DIM), d_rows[1].reshape(b, -1, H_D, HEAD_DIM))
    return mixed, state


def _trunk(x, caches_even, caches_odd, wts):
    b, t, d = x.shape
    x2 = x.reshape(b * t, d)
    h2 = _rmsnorm(x2, wts["attn_norm"][0])
    mixed, st_even = _even_mixer(h2, b, t, caches_even, wts["even"], wts["tz"])
    x2 = _matmul(mixed, wts["even"]["w_out"], mode="residual", residual=x2, name="out_even")
    h2 = _rmsnorm(x2, wts["mlp_norm"][0])
    u = _matmul(h2, wts["w_up"][0], mode="relu2", out_dtype=BF16, name="mlp_up0")
    x2 = _matmul(u, wts["w_down"][0], mode="residual", residual=x2, name="mlp_down0")
    h2 = _rmsnorm(x2, wts["attn_norm"][1])
    mixed, st_odd = _odd_mixer(h2, b, t, caches_odd, wts["odd"], wts["band"])
    x2 = _matmul(mixed, wts["odd"]["w_out"], mode="residual", residual=x2, name="out_odd")
    h2 = _rmsnorm(x2, wts["mlp_norm"][1])
    u = _matmul(h2, wts["w_up"][1], mode="relu2", out_dtype=BF16, name="mlp_up1")
    x2 = _matmul(u, wts["w_down"][1], mode="residual", residual=x2, name="mlp_down1")
    return x2.reshape(b, t, d), tuple(s[None] for s in st_even), tuple(s[None] for s in st_odd)


def _even_weights(w_in, w_out, a_qn, a_kn, b_qn, b_kn, f_bias):
    aq, ak, av, iq, ik, iw, bq, bk, bv, bf = jnp.split(
        w_in, [1024, 1280, 1536, 2048, 2112, 2120, 3144, 4168, 5192], axis=1)
    d = w_in.shape[0]
    pad = jnp.zeros((d, EV_PAD - EV_WIDTH + LANE - (IDX_DIM + IDX_HEADS + H_B)), w_in.dtype)
    w_r = jnp.concatenate([aq, bq, bk, bv, ak, av, iq, ik, iw, bf, pad], axis=1).astype(BF16)
    ones = jnp.ones((EV_PAD,), F32)
    gain = ones.at[EV_AQ:EV_AQ + 1024].set(jnp.tile(a_qn, H_A))
    gain = gain.at[EV_BQ:EV_BQ + 1024].set(jnp.tile(b_qn, H_B))
    gain = gain.at[EV_BK:EV_BK + 1024].set(jnp.tile(b_kn, H_B))
    gain = gain.at[EV_AK:EV_AK + 256].set(jnp.tile(a_kn, KV_A))
    flag = jnp.zeros((EV_PAD,), F32).at[0:EV_BV].set(1.0).at[EV_AK:EV_AV].set(1.0)
    return {"w_in": w_r, "gain": gain.reshape(1, -1), "flag": flag.reshape(1, -1),
            "w_out": w_out.astype(BF16), "f_bias": f_bias}


def _odd_weights(w_in, w_out, d_qn, d_kn):
    cq, ck, cv, dq, dk, dv = jnp.split(w_in, 6, axis=1)
    w_r = jnp.concatenate([dq, dk, cq, ck, cv, dv], axis=1).astype(BF16)
    gain = jnp.ones((OD_WIDTH,), F32)
    gain = gain.at[OD_DQ:OD_DQ + 1024].set(jnp.tile(d_qn, H_D)).at[OD_DK:OD_DK + 1024].set(jnp.tile(d_kn, H_D))
    flag = jnp.zeros((OD_WIDTH,), F32).at[0:OD_CQ].set(1.0)
    return {"w_in": w_r, "gain": gain.reshape(1, -1), "flag": flag.reshape(1, -1), "w_out": w_out.astype(BF16)}


def kernel(x_prompt, x_sample, cache_a_k, cache_a_v, cache_a_kidx, cache_b_k, cache_b_v, cache_b_logf,
           cache_c_k, cache_c_v, cache_d_k, cache_d_v, attn_norm, mlp_norm, w_in_even, w_out_even,
           w_in_odd, w_out_odd, a_q_norm, a_k_norm, b_q_norm, b_k_norm, forget_bias, t5_bias,
           d_q_norm, d_k_norm, d_rel_bias, w_up, w_down):
    wts = {
        "attn_norm": attn_norm,
        "mlp_norm": mlp_norm,
        "even": _even_weights(w_in_even[0], w_out_even[0], a_q_norm[0], a_k_norm[0], b_q_norm[0], b_k_norm[0],
                              forget_bias[0]),
        "odd": _odd_weights(w_in_odd[0], w_out_odd[0], d_q_norm[0], d_k_norm[0]),
        "w_up": w_up.astype(BF16),
        "w_down": w_down.astype(BF16),
        "tz": _t5_strip(t5_bias),
        "band": _band_bias(d_rel_bias[0]),
    }
    y_p, ev_p, od_p = _trunk(x_prompt, None, None, wts)
    caches_even = (cache_a_k[0], cache_a_v[0], cache_a_kidx[0], cache_b_k[0], cache_b_v[0], cache_b_logf[0])
    caches_odd = (cache_c_k[0], cache_c_v[0], cache_d_k[0], cache_d_v[0])
    y_s, ev_s, od_s = _trunk(x_sample, caches_even, caches_odd, wts)
    return (y_p, y_s) + ev_p + od_p + ev_s + od_s
```

```python
import functools

import numpy as np
import jax
import jax.numpy as jnp
from jax import lax
from jax.experimental import pallas as pl
from jax.experimental.pallas import tpu as pltpu

F32 = jnp.float32
BF16 = jnp.bfloat16
I32 = jnp.int32

HEAD_DIM = 128
CHUNK = 64
CHUNK_SHIFT = 6
H_A = 8
KV_A = 2
REP_A = H_A // KV_A
H_B = 8
H_C = 8
H_D = 8
IDX_HEADS = 8
IDX_DIM = 64
TOPK_MAX = 256
T5_BUCKETS = 32
D_LEFT_CHUNKS = 8
D_BAND_LEFT = D_LEFT_CHUNKS * CHUNK
D_REL_CLIP = 128
EPS = 1e-6
NEG_INF = -1e30
SCALE = HEAD_DIM ** -0.5
LANE = 128
INT32_MIN = -(2 ** 31)
NEG_INF_KEY = int(np.float32(NEG_INF).view(np.int32)) ^ 0x7FFFFFFF

VMEM_LIMIT_BYTES = 56 * 1024 * 1024

EV_AQ, EV_BQ, EV_BK, EV_BV, EV_AK, EV_AV, EV_IQ, EV_MISC = 0, 1024, 2048, 3072, 4096, 4352, 4608, 5120
EV_WIDTH = 5248
EV_PAD = 5632
MISC_IK, MISC_IW, MISC_BF = 0, 64, 72
OD_DQ, OD_DK, OD_CQ, OD_CK, OD_CV, OD_DV = 0, 1024, 2048, 3072, 4096, 5120
OD_WIDTH = 6144


def _cparams(*sem):
    return pltpu.CompilerParams(dimension_semantics=sem, vmem_limit_bytes=VMEM_LIMIT_BYTES)


def _dot_nt(a, b):
    return lax.dot_general(a, b, (((1,), (1,)), ((), ())), preferred_element_type=F32)


def _dot(a, b):
    return jnp.dot(a, b, preferred_element_type=F32)


def _lane_blocks(x):
    return [x[..., j * LANE:(j + 1) * LANE] for j in range(x.shape[-1] // LANE)]


def _pad_keys(x, rows):
    return jnp.concatenate([x, jnp.zeros((rows - x.shape[0], x.shape[1]), x.dtype)], axis=0)


def _rmsnorm_kernel(x_ref, g_ref, o_ref):
    x = x_ref[...]
    ms = jnp.mean(x * x, axis=-1, keepdims=True)
    o_ref[...] = (x * lax.rsqrt(ms + EPS) * g_ref[...]).astype(o_ref.dtype)


def _rmsnorm(x2, gain):
    m, d = x2.shape
    tm = min(512, m)
    return pl.pallas_call(
        _rmsnorm_kernel,
        grid=(m // tm,),
        in_specs=[pl.BlockSpec((tm, d), lambda i: (i, 0)), pl.BlockSpec((1, d), lambda i: (0, 0))],
        out_specs=pl.BlockSpec((tm, d), lambda i: (i, 0)),
        out_shape=jax.ShapeDtypeStruct((m, d), BF16),
        compiler_params=_cparams("parallel"),
        name="rmsnorm",
    )(x2, gain.reshape(1, d))


def _mm_kernel(*refs, nk, mode, n_norm_tiles):
    a_ref, b_ref = refs[0], refs[1]
    if mode == "headnorm":
        gain_ref, flag_ref, o_ref = refs[2], refs[3], refs[4]
        rest = refs[5:]
    elif mode == "residual":
        res_ref, o_ref = refs[2], refs[3]
        rest = refs[4:]
    else:
        o_ref = refs[2]
        rest = refs[3:]

    def epilogue(acc):
        if mode == "headnorm":
            j = pl.program_id(1)

            @pl.when(j < n_norm_tiles)
            def _():
                tn = acc.shape[1]
                for g in range(tn // HEAD_DIM):
                    sl = slice(g * HEAD_DIM, (g + 1) * HEAD_DIM)
                    blk = acc[:, sl]
                    ms = jnp.mean(blk * blk, axis=-1, keepdims=True)
                    normed = blk * lax.rsqrt(ms + EPS) * gain_ref[:, sl]
                    o_ref[:, sl] = jnp.where(flag_ref[:, sl] > 0.0, normed, blk)

            @pl.when(j >= n_norm_tiles)
            def _():
                o_ref[...] = acc
        elif mode == "relu2":
            r = jnp.maximum(acc, 0.0)
            o_ref[...] = (r * r).astype(o_ref.dtype)
        elif mode == "residual":
            o_ref[...] = res_ref[...] + acc
        else:
            o_ref[...] = acc.astype(o_ref.dtype)

    if nk == 1:
        epilogue(_dot(a_ref[...], b_ref[...]))
    else:
        acc_ref = rest[0]
        k = pl.program_id(2)

        @pl.when(k == 0)
        def _():
            acc_ref[...] = jnp.zeros_like(acc_ref)

        acc_ref[...] += _dot(a_ref[...], b_ref[...])

        @pl.when(k == nk - 1)
        def _():
            epilogue(acc_ref[...])


def _matmul(a, b, *, mode="plain", out_dtype=F32, gain=None, flag=None, n_norm_tiles=0, residual=None,
            tn=512, tk=2048, name="matmul"):
    m, kdim = a.shape
    _, n = b.shape
    tm = 1024 if m % 1024 == 0 else min(512, m)
    nk = kdim // tk
    in_specs = [pl.BlockSpec((tm, tk), lambda i, j, k: (i, k)), pl.BlockSpec((tk, tn), lambda i, j, k: (k, j))]
    args = [a, b]
    if mode == "headnorm":
        in_specs += [pl.BlockSpec((1, tn), lambda i, j, k: (0, j)), pl.BlockSpec((1, tn), lambda i, j, k: (0, j))]
        args += [gain, flag]
    elif mode == "residual":
        in_specs += [pl.BlockSpec((tm, tn), lambda i, j, k: (i, j))]
        args += [residual]
    scratch = [pltpu.VMEM((tm, tn), F32)] if nk > 1 else []
    return pl.pallas_call(
        functools.partial(_mm_kernel, nk=nk, mode=mode, n_norm_tiles=n_norm_tiles),
        grid=(m // tm, n // tn, nk),
        in_specs=in_specs,
        out_specs=pl.BlockSpec((tm, tn), lambda i, j, k: (i, j)),
        out_shape=jax.ShapeDtypeStruct((m, n), out_dtype),
        scratch_shapes=scratch,
        compiler_params=_cparams("parallel", "parallel", "arbitrary"),
        name=name,
    )(*args)


def _cumsum_block(x, carry):
    n = x.shape[1]
    r = lax.broadcasted_iota(I32, (n, n), 0)
    c = lax.broadcasted_iota(I32, (n, n), 1)
    tri = jnp.where(r <= c, 1.0, 0.0).astype(BF16)
    hi = x.astype(BF16)
    r1 = x - hi.astype(F32)
    mid = r1.astype(BF16)
    lo = (r1 - mid.astype(F32)).astype(BF16)
    cs = _dot(hi, tri) + _dot(mid, tri) + _dot(lo, tri) + carry
    return cs, cs[:, n - 1:n]


def _logf_cum_kernel(*refs, p_len, t_len, blk):
    if p_len:
        raw_ref, fb_ref, past_ref, lf_ref, cum_ref = refs
    else:
        raw_ref, fb_ref, lf_ref, cum_ref = refs
    x = raw_ref[0] + fb_ref[...]
    lf = jnp.minimum(x, 0.0) - jnp.log1p(jnp.exp(-jnp.abs(x)))
    lf_ref[0] = lf
    carry = jnp.zeros((H_B, 1), F32)
    for s in range(0, p_len, blk):
        cs, carry = _cumsum_block(past_ref[0, :, s:s + blk], carry)
        cum_ref[0, :, s:s + blk] = cs
    nb = min(blk, t_len)
    for s in range(0, t_len, nb):
        cs, carry = _cumsum_block(lf[:, s:s + nb], carry)
        cum_ref[0, :, p_len + s:p_len + s + nb] = cs


def _logf_cum(raw_h, fbias, past_h):
    b, h, t = raw_h.shape
    p = 0 if past_h is None else past_h.shape[2]
    in_specs = [pl.BlockSpec((1, h, t), lambda i: (i, 0, 0)), pl.BlockSpec((h, 1), lambda i: (0, 0))]
    args = [raw_h, fbias.reshape(h, 1)]
    if p:
        in_specs.append(pl.BlockSpec((1, h, p), lambda i: (i, 0, 0)))
        args.append(past_h)
    return pl.pallas_call(
        functools.partial(_logf_cum_kernel, p_len=p, t_len=t, blk=512),
        grid=(b,),
        in_specs=in_specs,
        out_specs=[pl.BlockSpec((1, h, t), lambda i: (i, 0, 0)), pl.BlockSpec((1, h, p + t), lambda i: (i, 0, 0))],
        out_shape=[jax.ShapeDtypeStruct((b, h, t), F32), jax.ShapeDtypeStruct((b, h, p + t), F32)],
        compiler_params=_cparams("parallel"),
        name="logf_cum",
    )(*args)


def _toeplitz(row_vals, rows, cols):
    w = row_vals.shape[1]
    full = jnp.broadcast_to(row_vals[0:1, :], (rows, w))
    return pltpu.roll(full, 0, 1, stride=1, stride_axis=0)[:, :cols]


def _t5_strip_kernel(tab_ref, o_ref, *, tile):
    h = pl.program_id(0)
    width = 4 * tile
    m = lax.broadcasted_iota(I32, (8, width), 1)
    d = jnp.where(m < width // 2, m, m - width)
    rel = d - tile
    n = jnp.abs(rel)
    large = jnp.full_like(n, 8)
    for thr in (12, 16, 23, 32, 46, 64, 91):
        large = large + jnp.where(n >= thr, 1, 0)
    bucket = jnp.where(rel > 0, T5_BUCKETS // 2, 0) + jnp.where(n < 8, n, large)
    vals = jnp.zeros((8, width), F32)
    for bkt in range(T5_BUCKETS):
        vals = jnp.where(bucket == bkt, tab_ref[bkt, h], vals)
    vals = vals - tab_ref[T5_BUCKETS // 2 - 1, h]
    o_ref[0] = _toeplitz(vals, tile, 2 * tile)


def _t5_strip(t5_table, tile):
    return pl.pallas_call(
        functools.partial(_t5_strip_kernel, tile=tile),
        grid=(H_A,),
        in_specs=[pl.BlockSpec(memory_space=pltpu.SMEM)],
        out_specs=pl.BlockSpec((1, tile, 2 * tile), lambda h: (h, 0, 0)),
        out_shape=jax.ShapeDtypeStruct((H_A, tile, 2 * tile), F32),
        compiler_params=_cparams("parallel"),
        name="t5_strip",
    )(t5_table)


def _band_bias_kernel(tab_ref, o_ref, *, rows, cols, width):
    h = pl.program_id(0)
    m = lax.broadcasted_iota(I32, (8, width), 1)
    d = jnp.where(m < width // 2, m, m - width)
    idx = jnp.clip(d - D_BAND_LEFT, -D_REL_CLIP, D_REL_CLIP) + D_REL_CLIP
    vals = jnp.zeros((8, width), F32)
    for r in range(2 * D_REL_CLIP + 1):
        vals = jnp.where(idx == r, tab_ref[r, h], vals)
    vals = vals - tab_ref[0, h]
    o_ref[0] = _toeplitz(vals, rows, cols)


def _band_bias(rel_table):
    rows, cols, width = D_BAND_LEFT, 2 * D_BAND_LEFT, 4 * D_BAND_LEFT
    return pl.pallas_call(
        functools.partial(_band_bias_kernel, rows=rows, cols=cols, width=width),
        grid=(H_D,),
        in_specs=[pl.BlockSpec(memory_space=pltpu.SMEM)],
        out_specs=pl.BlockSpec((1, rows, cols), lambda h: (h, 0, 0)),
        out_shape=jax.ShapeDtypeStruct((H_D, rows, cols), F32),
        compiler_params=_cparams("parallel"),
        name="band_bias",
    )(rel_table)


def _select_kernel(iq_ref, misc_ref, kk_ref, o_ref, key_ref, *, tq, lp, l_valid, q_base, n_sel, tk_out,
                   causal_prefix):
    qi = pl.program_id(1)

    def run(pfx):
        iq = iq_ref[0]
        iw = misc_ref[0][:, MISC_IW:MISC_IW + IDX_HEADS]
        ik = kk_ref[0, 0:pfx, 0:IDX_DIM].astype(BF16)
        score = jnp.zeros((tq, pfx), F32)
        for h in range(IDX_HEADS):
            dots = _dot_nt(iq[:, h * IDX_DIM:(h + 1) * IDX_DIM].astype(BF16), ik)
            score = score + iw[:, h:h + 1] * jnp.maximum(dots, 0.0)
        q_pos = q_base + qi * tq + lax.broadcasted_iota(I32, (tq, 1), 0)
        k_pos = lax.broadcasted_iota(I32, (1, pfx), 1)
        adm = ((k_pos >> CHUNK_SHIFT) <= (q_pos >> CHUNK_SHIFT)) & (k_pos < l_valid)
        score = jnp.where(adm, score, NEG_INF)
        bits = lax.bitcast_convert_type(score, I32)
        key_ref[:, 0:pfx] = jnp.where(bits < 0, bits ^ jnp.int32(0x7FFFFFFF), bits)

        def keys():
            return key_ref[:, 0:pfx]

        def count(pred):
            return jnp.sum(jnp.where(pred, 1.0, 0.0), axis=-1, keepdims=True)

        kf = float(n_sel)
        t0 = jnp.where(count(keys() >= 0) >= kf, jnp.int32(0), jnp.int32(INT32_MIN))

        def thr_body(i, t):
            cand = t + (jnp.int32(1) << (30 - i))
            return jnp.where(count(keys() >= cand) >= kf, cand, t)

        thr = lax.fori_loop(0, 31, thr_body, t0)
        need = kf - count(keys() > thr)
        tied = (count(keys() == thr) != need) & (thr > NEG_INF_KEY)
        any_tied = jnp.max(jnp.where(tied, 1.0, 0.0)) > 0.5
        nbits = max(pfx - 1, 1).bit_length()

        def tie_search():
            def tie_body(i, jb):
                cand = jb + (jnp.int32(1) << (nbits - 1 - i))
                taken = count((keys() == thr) & (k_pos < cand))
                return jnp.where(taken <= need, cand, jb)

            return lax.fori_loop(0, nbits, tie_body, jnp.zeros((tq, 1), I32))

        jbound = lax.cond(any_tied, tie_search, lambda: jnp.full((tq, 1), pfx, I32))
        key = keys()
        sel = ((key > thr) | ((key == thr) & (k_pos < jbound))) & adm
        madd = jnp.where(sel, 0.0, NEG_INF)
        for kt in range(lp // tk_out):
            if (kt + 1) * tk_out <= pfx:
                o_ref[0, 0, kt] = madd[:, kt * tk_out:(kt + 1) * tk_out]
            else:
                o_ref[0, 0, kt] = jnp.full((tq, tk_out), NEG_INF, F32)

    if causal_prefix:
        for j in range(lp // tq):
            pl.when(qi == j)(functools.partial(run, (j + 1) * tq))
    else:
        run(lp)


def _select_mask(iq_arr, iq_col, misc_arr, misc_col, kk_arr, kk_col, *, t, tq, lp, l_valid, q_base, n_sel,
                 tk_out, causal_prefix):
    b = iq_arr.shape[0]
    nq = t // tq
    wk = kk_arr.shape[2] if kk_col is None else LANE
    kcol = 0 if kk_col is None else kk_col
    return pl.pallas_call(
        functools.partial(_select_kernel, tq=tq, lp=lp, l_valid=l_valid, q_base=q_base, n_sel=n_sel,
                          tk_out=tk_out, causal_prefix=causal_prefix),
        grid=(b, nq),
        in_specs=[
            pl.BlockSpec((1, tq, IDX_HEADS * IDX_DIM), lambda i, j: (i, j, iq_col)),
            pl.BlockSpec((1, tq, LANE), lambda i, j: (i, j, misc_col)),
            pl.BlockSpec((1, lp, wk), lambda i, j: (i, 0, kcol)),
        ],
        out_specs=pl.BlockSpec((1, 1, lp // tk_out, tq, tk_out), lambda i, j: (i, j, 0, 0, 0)),
        out_shape=jax.ShapeDtypeStruct((b, nq, lp // tk_out, tq, tk_out), F32),
        scratch_shapes=[pltpu.VMEM((tq, lp), I32)],
        compiler_params=_cparams("parallel", "parallel"),
        name="a_select",
    )(iq_arr, misc_arr, kk_arr)


def _online_update(s, v_bf16, m_s, l_s, acc_s):
    blocks = _lane_blocks(s)
    mx = blocks[0]
    for blk in blocks[1:]:
        mx = jnp.maximum(mx, blk)
    m_prev = m_s[...]
    m_new = jnp.maximum(m_prev, jnp.max(mx, axis=-1, keepdims=True))
    alpha = jnp.exp(m_prev - m_new)
    ps = [jnp.exp(blk - m_new) for blk in blocks]
    psum = ps[0]
    for p in ps[1:]:
        psum = psum + p
    l_s[...] = alpha * l_s[...] + psum
    p_all = jnp.concatenate([p.astype(BF16) for p in ps], axis=1) if len(ps) > 1 else ps[0].astype(BF16)
    acc_s[...] = alpha * acc_s[...] + _dot(p_all, v_bf16)
    m_s[...] = m_new


def _online_init(m_s, l_s, acc_s):
    m_s[...] = jnp.full_like(m_s, NEG_INF)
    l_s[...] = jnp.zeros_like(l_s)
    acc_s[...] = jnp.zeros_like(acc_s)


def _online_result(l_s, acc_s):
    return acc_s[...] / jnp.sum(l_s[...], axis=-1, keepdims=True)


def _stack_heads(q):
    return jnp.concatenate([q[:, r * HEAD_DIM:(r + 1) * HEAD_DIM] for r in range(REP_A)], axis=0).astype(BF16)


def _attn_a_prompt_kernel(q_ref, k_ref, v_ref, mask_ref, tz_ref, o_ref, m_s, l_s, acc_s, *, tq):
    qi = pl.program_id(2)
    qs = _stack_heads(q_ref[0] * SCALE)
    _online_init(m_s, l_s, acc_s)

    def tile(kt, bias):
        off = pl.multiple_of(kt * tq, tq)
        k = k_ref[0, pl.ds(off, tq), :].astype(BF16)
        v = v_ref[0, pl.ds(off, tq), :].astype(BF16)
        s = _dot_nt(qs, k).reshape(REP_A, tq, tq) + mask_ref[0, 0, kt][None]
        if bias is not None:
            s = s + bias
        _online_update(s.reshape(REP_A * tq, tq), v, m_s, l_s, acc_s)

    def plain_body(kt, carry):
        tile(kt, None)
        return carry

    lax.fori_loop(0, jnp.maximum(qi - 1, 0), plain_body, 0)

    @pl.when(qi >= 1)
    def _():
        tile(qi - 1, tz_ref[:, :, 0:tq])

    tile(qi, tz_ref[:, :, tq:2 * tq])
    out = _online_result(l_s, acc_s)
    for r in range(REP_A):
        o_ref[0, :, r * HEAD_DIM:(r + 1) * HEAD_DIM] = out[r * tq:(r + 1) * tq].astype(o_ref.dtype)


def _attn_a_prompt(proj3, mask, tz, *, tq):
    b, t, _ = proj3.shape
    nq = t // tq
    qc, kc, vc = EV_AQ // (REP_A * HEAD_DIM), EV_AK // HEAD_DIM, EV_AV // HEAD_DIM
    rows = REP_A * tq
    return pl.pallas_call(
        functools.partial(_attn_a_prompt_kernel, tq=tq),
        grid=(b, KV_A, nq),
        in_specs=[
            pl.BlockSpec((1, tq, REP_A * HEAD_DIM), lambda i, g, j: (i, j, qc + g)),
            pl.BlockSpec((1, t, HEAD_DIM), lambda i, g, j: (i, 0, kc + g)),
            pl.BlockSpec((1, t, HEAD_DIM), lambda i, g, j: (i, 0, vc + g)),
            pl.BlockSpec((1, 1, nq, tq, tq), lambda i, g, j: (i, j, 0, 0, 0)),
            pl.BlockSpec((REP_A, tq, 2 * tq), lambda i, g, j: (g, 0, 0)),
        ],
        out_specs=pl.BlockSpec((1, tq, REP_A * HEAD_DIM), lambda i, g, j: (i, j, g)),
        out_shape=jax.ShapeDtypeStruct((b, t, H_A * HEAD_DIM), BF16),
        scratch_shapes=[pltpu.VMEM((rows, LANE), F32), pltpu.VMEM((rows, LANE), F32),
                        pltpu.VMEM((rows, HEAD_DIM), F32)],
        compiler_params=_cparams("parallel", "parallel", "parallel"),
        name="attn_a_prompt",
    )(proj3, proj3, proj3, mask, tz)


def _attn_a_step_kernel(q_ref, kc_ref, vc_ref, kn_ref, vn_ref, mask_ref, tz_ref, o_ref, *, t, p):
    g = pl.program_id(1)
    qs = _stack_heads(q_ref[0] * SCALE)
    kc = kc_ref[0, pl.ds(g, p, stride=KV_A), :].astype(BF16)
    vc = vc_ref[0, pl.ds(g, p, stride=KV_A), :].astype(BF16)
    kn = _pad_keys(kn_ref[0], LANE).astype(BF16)
    vn = _pad_keys(vn_ref[0], LANE).astype(BF16)
    mask = mask_ref[0, 0, 0]
    s_c = _dot_nt(qs, kc).reshape(REP_A, t, p) + mask[None, :, 0:p]
    s_n = _dot_nt(qs, kn).reshape(REP_A, t, LANE) + mask[None, :, p:p + LANE] + tz_ref[:, :, LANE:2 * LANE]
    blocks = _lane_blocks(s_c)
    blocks[-1] = blocks[-1] + tz_ref[:, :, 0:LANE]
    blocks.append(s_n)
    mx = blocks[0]
    for blk in blocks[1:]:
        mx = jnp.maximum(mx, blk)
    m = jnp.max(mx, axis=-1, keepdims=True)
    ps = [jnp.exp(blk - m) for blk in blocks]
    psum = ps[0]
    for pb in ps[1:]:
        psum = psum + pb
    l = jnp.sum(psum, axis=-1, keepdims=True)
    p_c = jnp.concatenate([pb.astype(BF16) for pb in ps[:-1]], axis=-1).reshape(REP_A * t, p)
    out = _dot(p_c, vc) + _dot(ps[-1].astype(BF16).reshape(REP_A * t, LANE), vn)
    out = out.reshape(REP_A, t, HEAD_DIM) / l
    for r in range(REP_A):
        o_ref[0, :, r * HEAD_DIM:(r + 1) * HEAD_DIM] = out[r].astype(o_ref.dtype)


def _attn_a_step(proj3, cache_k, cache_v, mask, tz):
    b, t, _ = proj3.shape
    p = cache_k.shape[1] // KV_A
    qc, kc, vc = EV_AQ // (REP_A * HEAD_DIM), EV_AK // HEAD_DIM, EV_AV // HEAD_DIM
    return pl.pallas_call(
        functools.partial(_attn_a_step_kernel, t=t, p=p),
        grid=(b, KV_A),
        in_specs=[
            pl.BlockSpec((1, t, REP_A * HEAD_DIM), lambda i, g: (i, 0, qc + g)),
            pl.BlockSpec((1, p * KV_A, HEAD_DIM), lambda i, g: (i, 0, 0)),
            pl.BlockSpec((1, p * KV_A, HEAD_DIM), lambda i, g: (i, 0, 0)),
            pl.BlockSpec((1, t, HEAD_DIM), lambda i, g: (i, 0, kc + g)),
            pl.BlockSpec((1, t, HEAD_DIM), lambda i, g: (i, 0, vc + g)),
            pl.BlockSpec((1, 1, 1, t, p + LANE), lambda i, g: (i, 0, 0, 0, 0)),
            pl.BlockSpec((REP_A, t, 2 * LANE), lambda i, g: (g, 0, 0)),
        ],
        out_specs=pl.BlockSpec((1, t, REP_A * HEAD_DIM), lambda i, g: (i, 0, g)),
        out_shape=jax.ShapeDtypeStruct((b, t, H_A * HEAD_DIM), BF16),
        compiler_params=_cparams("parallel", "parallel"),
        name="attn_a_step",
    )(proj3, cache_k, cache_v, proj3, proj3, mask, tz)


def _causal_mask(s):
    r = lax.broadcasted_iota(I32, s.shape, 0)
    c = lax.broadcasted_iota(I32, s.shape, 1)
    return jnp.where(c <= r, s, NEG_INF)


def _attn_b_prompt_kernel(q_ref, k_ref, v_ref, ck_ref, o_ref, m_s, l_s, acc_s, *, tq):
    qi = pl.program_id(2)
    q = (q_ref[0] * SCALE).astype(BF16)
    _online_init(m_s, l_s, acc_s)

    def tile(kt, diag):
        off = pl.multiple_of(kt * tq, tq)
        k = k_ref[0, pl.ds(off, tq), :].astype(BF16)
        v = v_ref[0, pl.ds(off, tq), :].astype(BF16)
        s = _dot_nt(q, k) - ck_ref[0, 0, kt]
        if diag:
            s = _causal_mask(s)
        _online_update(s, v, m_s, l_s, acc_s)

    def body(kt, carry):
        tile(kt, False)
        return carry

    lax.fori_loop(0, qi, body, 0)
    tile(qi, True)
    o_ref[0] = _online_result(l_s, acc_s).astype(o_ref.dtype)


def _attn_b_prompt(proj3, cum_h, *, tq=512):
    b, t, _ = proj3.shape
    nq = t // tq
    ck = cum_h.reshape(b, H_B, nq, 1, tq)
    qc, kc, vc = EV_BQ // HEAD_DIM, EV_BK // HEAD_DIM, EV_BV // HEAD_DIM
    return pl.pallas_call(
        functools.partial(_attn_b_prompt_kernel, tq=tq),
        grid=(b, H_B, nq),
        in_specs=[
            pl.BlockSpec((1, tq, HEAD_DIM), lambda i, h, j: (i, j, qc + h)),
            pl.BlockSpec((1, t, HEAD_DIM), lambda i, h, j: (i, 0, kc + h)),
            pl.BlockSpec((1, t, HEAD_DIM), lambda i, h, j: (i, 0, vc + h)),
            pl.BlockSpec((1, 1, nq, 1, tq), lambda i, h, j: (i, h, 0, 0, 0)),
        ],
        out_specs=pl.BlockSpec((1, tq, HEAD_DIM), lambda i, h, j: (i, j, h)),
        out_shape=jax.ShapeDtypeStruct((b, t, H_B * HEAD_DIM), BF16),
        scratch_shapes=[pltpu.VMEM((tq, LANE), F32), pltpu.VMEM((tq, LANE), F32), pltpu.VMEM((tq, HEAD_DIM), F32)],
        compiler_params=_cparams("parallel", "parallel", "parallel"),
        name="attn_b_prompt",
    )(proj3, proj3, proj3, ck)


def _attn_b_step_kernel(q_ref, kc_ref, vc_ref, kn_ref, vn_ref, ckc_ref, ckn_ref, o_ref, m_s, l_s, acc_s,
                        *, nkc, t, tkc):
    kb = pl.program_id(1)

    @pl.when(kb == 0)
    def _():
        _online_init(m_s, l_s, acc_s)

    def q_head(h):
        return (q_ref[0, :, h * HEAD_DIM:(h + 1) * HEAD_DIM] * SCALE).astype(BF16)

    @pl.when(kb < nkc)
    def _():
        ck = ckc_ref[0, 0]
        for h in range(H_B):
            k = kc_ref[0, pl.ds(h, tkc, stride=H_B), :].astype(BF16)
            v = vc_ref[0, pl.ds(h, tkc, stride=H_B), :].astype(BF16)
            s = _dot_nt(q_head(h), k) - ck[h:h + 1, :]
            _online_update(s, v, m_s.at[h], l_s.at[h], acc_s.at[h])

    @pl.when(kb == nkc)
    def _():
        ckn = ckn_ref[0]
        for h in range(H_B):
            sl = slice(h * HEAD_DIM, (h + 1) * HEAD_DIM)
            k = _pad_keys(kn_ref[0, :, sl], LANE).astype(BF16)
            v = _pad_keys(vn_ref[0, :, sl], LANE).astype(BF16)
            s = _causal_mask(_dot_nt(q_head(h), k) - ckn[h:h + 1, :])
            _online_update(s, v, m_s.at[h], l_s.at[h], acc_s.at[h])
            o_ref[0, :, sl] = _online_result(l_s.at[h], acc_s.at[h]).astype(o_ref.dtype)


def _attn_b_step(proj3, cache_k, cache_v, cum_h, *, tkc=1024):
    b, t, _ = proj3.shape
    p = cache_k.shape[1] // H_B
    nkc = p // tkc
    w = H_B * HEAD_DIM
    ck_cache = cum_h[:, :, :p].reshape(b, H_B, nkc, tkc).transpose(0, 2, 1, 3)
    ck_new = jnp.pad(cum_h[:, :, p:], ((0, 0), (0, 0), (0, LANE - t)))
    last = nkc - 1
    return pl.pallas_call(
        functools.partial(_attn_b_step_kernel, nkc=nkc, t=t, tkc=tkc),
        grid=(b, nkc + 1),
        in_specs=[
            pl.BlockSpec((1, t, w), lambda i, j: (i, 0, EV_BQ // w)),
            pl.BlockSpec((1, tkc * H_B, HEAD_DIM), lambda i, j: (i, jnp.minimum(j, last), 0)),
            pl.BlockSpec((1, tkc * H_B, HEAD_DIM), lambda i, j: (i, jnp.minimum(j, last), 0)),
            pl.BlockSpec((1, t, w), lambda i, j: (i, 0, EV_BK // w)),
            pl.BlockSpec((1, t, w), lambda i, j: (i, 0, EV_BV // w)),
            pl.BlockSpec((1, 1, H_B, tkc), lambda i, j: (i, jnp.minimum(j, last), 0, 0)),
            pl.BlockSpec((1, H_B, LANE), lambda i, j: (i, 0, 0)),
        ],
        out_specs=pl.BlockSpec((1, t, w), lambda i, j: (i, 0, 0)),
        out_shape=jax.ShapeDtypeStruct((b, t, w), BF16),
        scratch_shapes=[pltpu.VMEM((H_B, t, LANE), F32), pltpu.VMEM((H_B, t, LANE), F32),
                        pltpu.VMEM((H_B, t, HEAD_DIM), F32)],
        compiler_params=_cparams("parallel", "arbitrary"),
        name="attn_b_step",
    )(proj3, cache_k, cache_v, proj3, proj3, ck_cache, ck_new)


def _tri_ones():
    r = lax.broadcasted_iota(I32, (LANE, 2 * LANE), 0)
    c = lax.broadcasted_iota(I32, (LANE, 2 * LANE), 1)
    return jnp.where((r > c) | (c >= LANE), 1.0, 0.0).astype(BF16)


def _stick_tile(q_bf16, k_bf16, v_bf16, run, tri_ones, row_minus_col):
    rows = q_bf16.shape[0]
    z_blocks = _lane_blocks(_dot_nt(q_bf16, k_bf16))
    rc_blocks = None if row_minus_col is None else _lane_blocks(row_minus_col)
    ws = [None] * len(z_blocks)
    for j in reversed(range(len(z_blocks))):
        z = z_blocks[j]
        tail = jnp.log1p(jnp.exp(-jnp.abs(z)))
        log_beta = jnp.minimum(z, 0.0) - tail
        log_keep = -jnp.maximum(z, 0.0) - tail
        if rc_blocks is not None:
            strict = rc_blocks[j] > 0
            log_keep = jnp.where(strict, log_keep, 0.0)
        hi = log_keep.astype(BF16)
        lo = (log_keep - hi.astype(F32)).astype(BF16)
        both = _dot(jnp.concatenate([hi, lo], axis=0), tri_ones)
        both = both[:rows] + both[rows:]
        w = jnp.exp(log_beta + both[:, :LANE] + run)
        if rc_blocks is not None:
            w = jnp.where(strict, w, 0.0)
        ws[j] = w.astype(BF16)
        run = run + both[:, LANE:]
    w_all = jnp.concatenate(ws, axis=1) if len(ws) > 1 else ws[0]
    return _dot(w_all, v_bf16), run


def _attn_c_prompt_kernel(q_ref, k_ref, v_ref, o_ref, run_s, acc_s, *, tq):
    qi = pl.program_id(2)
    q = (q_ref[0] * SCALE).astype(BF16)
    tri_ones = _tri_ones()

    def load(kt):
        off = pl.multiple_of(kt * tq, tq)
        return k_ref[0, pl.ds(off, tq), :].astype(BF16), v_ref[0, pl.ds(off, tq), :].astype(BF16)

    rc = lax.broadcasted_iota(I32, (tq, tq), 0) - lax.broadcasted_iota(I32, (tq, tq), 1)
    k, v = load(qi)
    out, run = _stick_tile(q, k, v, jnp.zeros((tq, LANE), F32), tri_ones, rc)
    acc_s[...] = out
    run_s[...] = run

    def body(i, carry):
        k, v = load(qi - 1 - i)
        out, run = _stick_tile(q, k, v, run_s[...], tri_ones, None)
        acc_s[...] += out
        run_s[...] = run
        return carry

    lax.fori_loop(0, qi, body, 0)
    o_ref[0] = acc_s[...].astype(o_ref.dtype)


def _attn_c_prompt(proj3, *, tq=512):
    b, t, _ = proj3.shape
    nq = t // tq
    qc, kc, vc = OD_CQ // HEAD_DIM, OD_CK // HEAD_DIM, OD_CV // HEAD_DIM
    return pl.pallas_call(
        functools.partial(_attn_c_prompt_kernel, tq=tq),
        grid=(b, H_C, nq),
        in_specs=[
            pl.BlockSpec((1, tq, HEAD_DIM), lambda i, h, j: (i, j, qc + h)),
            pl.BlockSpec((1, t, HEAD_DIM), lambda i, h, j: (i, 0, kc + h)),
            pl.BlockSpec((1, t, HEAD_DIM), lambda i, h, j: (i, 0, vc + h)),
        ],
        out_specs=pl.BlockSpec((1, tq, HEAD_DIM), lambda i, h, j: (i, j, h)),
        out_shape=jax.ShapeDtypeStruct((b, t, H_C * HEAD_DIM), BF16),
        scratch_shapes=[pltpu.VMEM((tq, LANE), F32), pltpu.VMEM((tq, HEAD_DIM), F32)],
        compiler_params=_cparams("parallel", "parallel", "parallel"),
        name="attn_c_prompt",
    )(proj3, proj3, proj3)


def _attn_c_step_kernel(q_ref, kc_ref, vc_ref, kn_ref, vn_ref, o_ref, run_s, acc_s, *, nkc, t, tkc, sub):
    kb = pl.program_id(1)
    tri_ones = _tri_ones()

    def q_head(h):
        return (q_ref[0, :, h * HEAD_DIM:(h + 1) * HEAD_DIM] * SCALE).astype(BF16)

    @pl.when(kb == 0)
    def _():
        rc = lax.broadcasted_iota(I32, (t, LANE), 0) - lax.broadcasted_iota(I32, (t, LANE), 1)
        for h in range(H_C):
            sl = slice(h * HEAD_DIM, (h + 1) * HEAD_DIM)
            k = _pad_keys(kn_ref[0, :, sl], LANE).astype(BF16)
            v = _pad_keys(vn_ref[0, :, sl], LANE).astype(BF16)
            out, run = _stick_tile(q_head(h), k, v, jnp.zeros((t, LANE), F32), tri_ones, rc)
            acc_s[h] = out
            run_s[h] = run

    @pl.when(kb > 0)
    def _():
        def body(i, carry):
            row0 = (tkc - sub - i * sub) * H_C
            for h in range(H_C):
                k = kc_ref[0, pl.ds(row0 + h, sub, stride=H_C), :].astype(BF16)
                v = vc_ref[0, pl.ds(row0 + h, sub, stride=H_C), :].astype(BF16)
                out, run = _stick_tile(q_head(h), k, v, run_s[h], tri_ones, None)
                acc_s[h] += out
                run_s[h] = run
            return carry

        lax.fori_loop(0, tkc // sub, body, 0)

    @pl.when(kb == nkc)
    def _():
        for h in range(H_C):
            o_ref[0, :, h * HEAD_DIM:(h + 1) * HEAD_DIM] = acc_s[h].astype(o_ref.dtype)


def _attn_c_step(proj3, cache_k, cache_v, *, tkc=1024, sub=256):
    b, t, _ = proj3.shape
    p = cache_k.shape[1] // H_C
    nkc = p // tkc
    w = H_C * HEAD_DIM

    def cache_idx(i, j):
        return (i, jnp.clip(nkc - j, 0, nkc - 1), 0)

    return pl.pallas_call(
        functools.partial(_attn_c_step_kernel, nkc=nkc, t=t, tkc=tkc, sub=sub),
        grid=(b, nkc + 1),
        in_specs=[
            pl.BlockSpec((1, t, w), lambda i, j: (i, 0, OD_CQ // w)),
            pl.BlockSpec((1, tkc * H_C, HEAD_DIM), cache_idx),
            pl.BlockSpec((1, tkc * H_C, HEAD_DIM), cache_idx),
            pl.BlockSpec((1, t, w), lambda i, j: (i, 0, OD_CK // w)),
            pl.BlockSpec((1, t, w), lambda i, j: (i, 0, OD_CV // w)),
        ],
        out_specs=pl.BlockSpec((1, t, w), lambda i, j: (i, 0, 0)),
        out_shape=jax.ShapeDtypeStruct((b, t, w), BF16),
        scratch_shapes=[pltpu.VMEM((H_C, t, LANE), F32), pltpu.VMEM((H_C, t, HEAD_DIM), F32)],
        compiler_params=_cparams("parallel", "arbitrary"),
        name="attn_c_step",
    )(proj3, cache_k, cache_v, proj3, proj3)


def _attn_d_kernel(q_ref, kp_ref, vp_ref, ko_ref, vo_ref, bias_ref, o_ref, *, tq, q_base):
    qi = pl.program_id(2)
    q = (q_ref[0] * SCALE).astype(BF16)
    q0 = q_base + qi * tq
    q_chunk = (q0 + lax.broadcasted_iota(I32, (tq, 1), 0)) >> CHUNK_SHIFT
    kp_pos = q0 - D_BAND_LEFT + lax.broadcasted_iota(I32, (1, D_BAND_LEFT), 1)
    ko_pos = q0 + lax.broadcasted_iota(I32, (1, tq), 1)
    vis_p = (kp_pos >= 0) & ((kp_pos >> CHUNK_SHIFT) >= q_chunk - D_LEFT_CHUNKS)
    vis_o = (ko_pos >> CHUNK_SHIFT) <= q_chunk
    s_p = _dot_nt(q, kp_ref[0].astype(BF16)) + bias_ref[0, :, 0:D_BAND_LEFT]
    s_o = _dot_nt(q, ko_ref[0].astype(BF16)) + bias_ref[0, :, D_BAND_LEFT:D_BAND_LEFT + tq]
    s_p = jnp.where(vis_p, s_p, NEG_INF)
    s_o = jnp.where(vis_o, s_o, NEG_INF)
    m = jnp.maximum(jnp.max(s_p, axis=-1, keepdims=True), jnp.max(s_o, axis=-1, keepdims=True))
    p_p = jnp.exp(s_p - m)
    p_o = jnp.exp(s_o - m)
    l = jnp.sum(p_p, axis=-1, keepdims=True) + jnp.sum(p_o, axis=-1, keepdims=True)
    out = _dot(p_p.astype(BF16), vp_ref[0].astype(BF16)) + _dot(p_o.astype(BF16), vo_ref[0].astype(BF16))
    o_ref[0] = (out / l).astype(o_ref.dtype)


def _attn_d(q_arr, q_col, kprev, kprev_col, vprev, vprev_col, kown, kown_col, vown, vown_col, bias,
            *, t, tq, q_base, prev_follows_own):
    b = q_arr.shape[0]
    nq = t // tq
    if prev_follows_own:
        def prev_idx(col):
            return lambda i, h, j: (i, jnp.maximum(j - 1, 0), col + h)
    else:
        def prev_idx(col):
            return lambda i, h, j: (i, 0, col + h)
    return pl.pallas_call(
        functools.partial(_attn_d_kernel, tq=tq, q_base=q_base),
        grid=(b, H_D, nq),
        in_specs=[
            pl.BlockSpec((1, tq, HEAD_DIM), lambda i, h, j: (i, j, q_col + h)),
            pl.BlockSpec((1, D_BAND_LEFT, HEAD_DIM), prev_idx(kprev_col)),
            pl.BlockSpec((1, D_BAND_LEFT, HEAD_DIM), prev_idx(vprev_col)),
            pl.BlockSpec((1, tq, HEAD_DIM), lambda i, h, j: (i, j, kown_col + h)),
            pl.BlockSpec((1, tq, HEAD_DIM), lambda i, h, j: (i, j, vown_col + h)),
            pl.BlockSpec((1, tq, D_BAND_LEFT + tq), lambda i, h, j: (h, 0, 0)),
        ],
        out_specs=pl.BlockSpec((1, tq, HEAD_DIM), lambda i, h, j: (i, j, h)),
        out_shape=jax.ShapeDtypeStruct((b, t, H_D * HEAD_DIM), BF16),
        compiler_params=_cparams("parallel", "parallel", "parallel"),
        name="attn_d",
    )(q_arr, kprev, vprev, kown, vown, bias)


def _pad_rows(x, rows):
    return jnp.pad(x, ((0, 0), (0, rows - x.shape[1]), (0, 0)))


def _even_mixer(h2, b, t, past, w, tz_prompt, tz_step, tq_a):
    proj = _matmul(h2, w["w_in"], mode="headnorm", gain=w["gain"], flag=w["flag"], n_norm_tiles=9,
                   name="proj_even")
    proj3 = proj.reshape(b, t, EV_PAD)
    ak = proj3[:, :, EV_AK:EV_AK + KV_A * HEAD_DIM]
    av = proj3[:, :, EV_AV:EV_AV + KV_A * HEAD_DIM]
    ik = proj3[:, :, EV_MISC + MISC_IK:EV_MISC + MISC_IK + IDX_DIM]
    bk = proj3[:, :, EV_BK:EV_BK + H_B * HEAD_DIM]
    bv = proj3[:, :, EV_BV:EV_BV + H_B * HEAD_DIM]
    raw_h = proj3[:, :, EV_MISC + MISC_BF:EV_MISC + MISC_BF + H_B].transpose(0, 2, 1)
    iq_col = EV_IQ // (IDX_HEADS * IDX_DIM)
    misc_col = EV_MISC // LANE
    if past is None:
        logf_h, cum_h = _logf_cum(raw_h, w["f_bias"], None)
        mask = _select_mask(proj3, iq_col, proj3, misc_col, proj3, misc_col, t=t, tq=tq_a, lp=t, l_valid=t,
                            q_base=0, n_sel=min(TOPK_MAX, t // 4), tk_out=tq_a, causal_prefix=True)
        out_a = _attn_a_prompt(proj3, mask, tz_prompt, tq=tq_a)
        out_b = _attn_b_prompt(proj3, cum_h, tq=min(512, t))
    else:
        p_ak, p_av, p_ik, p_bk, p_bv, p_lf = past
        p = p_ak.shape[1]
        l_valid = p + t
        lp = p + LANE
        logf_h, cum_h = _logf_cum(raw_h, w["f_bias"], p_lf.transpose(0, 2, 1))
        ik_all = _pad_rows(jnp.concatenate([p_ik, ik], axis=1), lp)
        mask = _select_mask(proj3, iq_col, proj3, misc_col, ik_all, None, t=t, tq=t, lp=lp, l_valid=l_valid,
                            q_base=p, n_sel=min(TOPK_MAX, l_valid // 4), tk_out=lp, causal_prefix=False)
        out_a = _attn_a_step(proj3, p_ak.reshape(b, p * KV_A, HEAD_DIM), p_av.reshape(b, p * KV_A, HEAD_DIM),
                             mask, tz_step[:, :t])
        out_b = _attn_b_step(proj3, p_bk.reshape(b, p * H_B, HEAD_DIM), p_bv.reshape(b, p * H_B, HEAD_DIM), cum_h)
    mixed = jnp.concatenate([out_a, out_b], axis=-1).reshape(b * t, -1)
    state = (ak.reshape(b, t, KV_A, HEAD_DIM), av.reshape(b, t, KV_A, HEAD_DIM), ik,
             bk.reshape(b, t, H_B, HEAD_DIM), bv.reshape(b, t, H_B, HEAD_DIM), logf_h.transpose(0, 2, 1))
    return mixed, state


def _odd_mixer(h2, b, t, past, w, band):
    proj = _matmul(h2, w["w_in"], mode="headnorm", gain=w["gain"], flag=w["flag"], n_norm_tiles=4,
                   name="proj_odd")
    proj3 = proj.reshape(b, t, OD_WIDTH)
    ck = proj3[:, :, OD_CK:OD_CK + H_C * HEAD_DIM]
    cv = proj3[:, :, OD_CV:OD_CV + H_C * HEAD_DIM]
    dk = proj3[:, :, OD_DK:OD_DK + H_D * HEAD_DIM]
    dv = proj3[:, :, OD_DV:OD_DV + H_D * HEAD_DIM]
    dq_col, dk_col, dv_col = OD_DQ // HEAD_DIM, OD_DK // HEAD_DIM, OD_DV // HEAD_DIM
    if past is None:
        out_c = _attn_c_prompt(proj3, tq=min(512, t))
        out_d = _attn_d(proj3, dq_col, proj3, dk_col, proj3, dv_col, proj3, dk_col, proj3, dv_col, band,
                        t=t, tq=D_BAND_LEFT, q_base=0, prev_follows_own=True)
        keep = min(D_BAND_LEFT, t)
        d_rows = (dk[:, t - keep:], dv[:, t - keep:])
    else:
        p_ck, p_cv, p_dk, p_dv = past
        p = p_ck.shape[1]
        out_c = _attn_c_step(proj3, p_ck.reshape(b, p * H_C, HEAD_DIM), p_cv.reshape(b, p * H_C, HEAD_DIM))
        wd = p_dk.shape[1]
        out_d = _attn_d(proj3, dq_col, p_dk.reshape(b, wd, -1), 0, p_dv.reshape(b, wd, -1), 0,
                        proj3, dk_col, proj3, dv_col, band[:, :t, :D_BAND_LEFT + t],
                        t=t, tq=t, q_base=p, prev_follows_own=False)
        d_rows = (dk, dv)
    mixed = jnp.concatenate([out_c, out_d], axis=-1).reshape(b * t, -1)
    state = (ck.reshape(b, t, H_C, HEAD_DIM), cv.reshape(b, t, H_C, HEAD_DIM),
             d_rows[0].reshape(b, -1, H_D, HEAD_DIM), d_rows[1].reshape(b, -1, H_D, HEAD_DIM))
    return mixed, state


def _trunk(x, caches_even, caches_odd, wts):
    b, t, d = x.shape
    x2 = x.reshape(b * t, d)
    h2 = _rmsnorm(x2, wts["attn_norm"][0])
    mixed, st_even = _even_mixer(h2, b, t, caches_even, wts["even"], wts["tz_prompt"], wts["tz_step"],
                                 wts["tq_a"])
    x2 = _matmul(mixed, wts["even"]["w_out"], mode="residual", residual=x2, name="out_even")
    h2 = _rmsnorm(x2, wts["mlp_norm"][0])
    u = _matmul(h2, wts["w_up"][0], mode="relu2", out_dtype=BF16, name="mlp_up0")
    x2 = _matmul(u, wts["w_down"][0], mode="residual", residual=x2, name="mlp_down0")
    h2 = _rmsnorm(x2, wts["attn_norm"][1])
    mixed, st_odd = _odd_mixer(h2, b, t, caches_odd, wts["odd"], wts["band"])
    x2 = _matmul(mixed, wts["odd"]["w_out"], mode="residual", residual=x2, name="out_odd")
    h2 = _rmsnorm(x2, wts["mlp_norm"][1])
    u = _matmul(h2, wts["w_up"][1], mode="relu2", out_dtype=BF16, name="mlp_up1")
    x2 = _matmul(u, wts["w_down"][1], mode="residual", residual=x2, name="mlp_down1")
    return x2.reshape(b, t, d), tuple(s[None] for s in st_even), tuple(s[None] for s in st_odd)


def _even_weights(w_in, w_out, a_qn, a_kn, b_qn, b_kn, f_bias):
    aq, ak, av, iq, ik, iw, bq, bk, bv, bf = jnp.split(
        w_in, [1024, 1280, 1536, 2048, 2112, 2120, 3144, 4168, 5192], axis=1)
    d = w_in.shape[0]
    pad = jnp.zeros((d, EV_PAD - EV_WIDTH + LANE - (IDX_DIM + IDX_HEADS + H_B)), w_in.dtype)
    w_r = jnp.concatenate([aq, bq, bk, bv, ak, av, iq, ik, iw, bf, pad], axis=1).astype(BF16)
    ones = jnp.ones((EV_PAD,), F32)
    gain = ones.at[EV_AQ:EV_AQ + 1024].set(jnp.tile(a_qn, H_A))
    gain = gain.at[EV_BQ:EV_BQ + 1024].set(jnp.tile(b_qn, H_B))
    gain = gain.at[EV_BK:EV_BK + 1024].set(jnp.tile(b_kn, H_B))
    gain = gain.at[EV_AK:EV_AK + 256].set(jnp.tile(a_kn, KV_A))
    flag = jnp.zeros((EV_PAD,), F32).at[0:EV_BV].set(1.0).at[EV_AK:EV_AV].set(1.0)
    return {"w_in": w_r, "gain": gain.reshape(1, -1), "flag": flag.reshape(1, -1),
            "w_out": w_out.astype(BF16), "f_bias": f_bias}


def _odd_weights(w_in, w_out, d_qn, d_kn):
    cq, ck, cv, dq, dk, dv = jnp.split(w_in, 6, axis=1)
    w_r = jnp.concatenate([dq, dk, cq, ck, cv, dv], axis=1).astype(BF16)
    gain = jnp.ones((OD_WIDTH,), F32)
    gain = gain.at[OD_DQ:OD_DQ + 1024].set(jnp.tile(d_qn, H_D)).at[OD_DK:OD_DK + 1024].set(jnp.tile(d_kn, H_D))
    flag = jnp.zeros((OD_WIDTH,), F32).at[0:OD_CQ].set(1.0)
    return {"w_in": w_r, "gain": gain.reshape(1, -1), "flag": flag.reshape(1, -1), "w_out": w_out.astype(BF16)}


def kernel(x_prompt, x_sample, cache_a_k, cache_a_v, cache_a_kidx, cache_b_k, cache_b_v, cache_b_logf,
           cache_c_k, cache_c_v, cache_d_k, cache_d_v, attn_norm, mlp_norm, w_in_even, w_out_even,
           w_in_odd, w_out_odd, a_q_norm, a_k_norm, b_q_norm, b_k_norm, forget_bias, t5_bias,
           d_q_norm, d_k_norm, d_rel_bias, w_up, w_down):
    tq_a = min(256, x_prompt.shape[1])
    wts = {
        "attn_norm": attn_norm,
        "mlp_norm": mlp_norm,
        "even": _even_weights(w_in_even[0], w_out_even[0], a_q_norm[0], a_k_norm[0], b_q_norm[0], b_k_norm[0],
                              forget_bias[0]),
        "odd": _odd_weights(w_in_odd[0], w_out_odd[0], d_q_norm[0], d_k_norm[0]),
        "w_up": w_up.astype(BF16),
        "w_down": w_down.astype(BF16),
        "tq_a": tq_a,
        "tz_prompt": _t5_strip(t5_bias, tq_a),
        "tz_step": _t5_strip(t5_bias, LANE),
        "band": _band_bias(d_rel_bias[0]),
    }
    y_p, ev_p, od_p = _trunk(x_prompt, None, None, wts)
    caches_even = (cache_a_k[0], cache_a_v[0], cache_a_kidx[0], cache_b_k[0], cache_b_v[0], cache_b_logf[0])
    caches_odd = (cache_c_k[0], cache_c_v[0], cache_d_k[0], cache_d_v[0])
    y_s, ev_s, od_s = _trunk(x_sample, caches_even, caches_odd, wts)
    return (y_p, y_s) + ev_p + od_p + ev_s + od_s
```

```python
import functools

import numpy as np
import jax
import jax.numpy as jnp
from jax import lax
from jax.experimental import pallas as pl
from jax.experimental.pallas import tpu as pltpu

F32 = jnp.float32
BF16 = jnp.bfloat16
I32 = jnp.int32

HEAD_DIM = 128
CHUNK = 64
CHUNK_SHIFT = 6
H_A = 8
KV_A = 2
REP_A = H_A // KV_A
H_B = 8
H_C = 8
H_D = 8
IDX_HEADS = 8
IDX_DIM = 64
TOPK_MAX = 256
T5_BUCKETS = 32
D_LEFT_CHUNKS = 8
D_BAND_LEFT = D_LEFT_CHUNKS * CHUNK
D_REL_CLIP = 128
EPS = 1e-6
NEG_INF = -1e30
SCALE = HEAD_DIM ** -0.5
LANE = 128
INT32_MIN = -(2 ** 31)
NEG_INF_KEY = int(np.float32(NEG_INF).view(np.int32)) ^ 0x7FFFFFFF

VMEM_LIMIT_BYTES = 56 * 1024 * 1024

EV_AQ, EV_BQ, EV_BK, EV_BV, EV_AK, EV_AV, EV_IQ, EV_MISC = 0, 1024, 2048, 3072, 4096, 4352, 4608, 5120
EV_WIDTH = 5248
EV_PAD = 5632
MISC_IK, MISC_IW, MISC_BF = 0, 64, 72
OD_DQ, OD_DK, OD_CQ, OD_CK, OD_CV, OD_DV = 0, 1024, 2048, 3072, 4096, 5120
OD_WIDTH = 6144


def _cparams(*sem):
    return pltpu.CompilerParams(dimension_semantics=sem, vmem_limit_bytes=VMEM_LIMIT_BYTES)


def _dot_nt(a, b):
    return lax.dot_general(a, b, (((1,), (1,)), ((), ())), preferred_element_type=F32)


def _dot(a, b):
    return jnp.dot(a, b, preferred_element_type=F32)


def _lane_blocks(x):
    return [x[..., j * LANE:(j + 1) * LANE] for j in range(x.shape[-1] // LANE)]


def _pad_keys(x, rows):
    return jnp.concatenate([x, jnp.zeros((rows - x.shape[0], x.shape[1]), x.dtype)], axis=0)


def _rmsnorm_kernel(x_ref, g_ref, o_ref):
    x = x_ref[...]
    ms = jnp.mean(x * x, axis=-1, keepdims=True)
    o_ref[...] = (x * lax.rsqrt(ms + EPS) * g_ref[...]).astype(o_ref.dtype)


def _rmsnorm(x2, gain):
    m, d = x2.shape
    tm = min(512, m)
    return pl.pallas_call(
        _rmsnorm_kernel,
        grid=(m // tm,),
        in_specs=[pl.BlockSpec((tm, d), lambda i: (i, 0)), pl.BlockSpec((1, d), lambda i: (0, 0))],
        out_specs=pl.BlockSpec((tm, d), lambda i: (i, 0)),
        out_shape=jax.ShapeDtypeStruct((m, d), BF16),
        compiler_params=_cparams("parallel"),
        name="rmsnorm",
    )(x2, gain.reshape(1, d))


def _mm_kernel(*refs, nk, mode, n_norm_tiles):
    a_ref, b_ref = refs[0], refs[1]
    if mode == "headnorm":
        gain_ref, flag_ref, o_ref = refs[2], refs[3], refs[4]
        rest = refs[5:]
    elif mode == "residual":
        res_ref, o_ref = refs[2], refs[3]
        rest = refs[4:]
    else:
        o_ref = refs[2]
        rest = refs[3:]

    def epilogue(acc):
        if mode == "headnorm":
            j = pl.program_id(1)

            @pl.when(j < n_norm_tiles)
            def _():
                tn = acc.shape[1]
                for g in range(tn // HEAD_DIM):
                    sl = slice(g * HEAD_DIM, (g + 1) * HEAD_DIM)
                    blk = acc[:, sl]
                    ms = jnp.mean(blk * blk, axis=-1, keepdims=True)
                    normed = blk * lax.rsqrt(ms + EPS) * gain_ref[:, sl]
                    o_ref[:, sl] = jnp.where(flag_ref[:, sl] > 0.0, normed, blk)

            @pl.when(j >= n_norm_tiles)
            def _():
                o_ref[...] = acc
        elif mode == "relu2":
            r = jnp.maximum(acc, 0.0)
            o_ref[...] = (r * r).astype(o_ref.dtype)
        elif mode == "residual":
            o_ref[...] = res_ref[...] + acc
        else:
            o_ref[...] = acc.astype(o_ref.dtype)

    if nk == 1:
        epilogue(_dot(a_ref[...], b_ref[...]))
    else:
        acc_ref = rest[0]
        k = pl.program_id(2)

        @pl.when(k == 0)
        def _():
            acc_ref[...] = jnp.zeros_like(acc_ref)

        acc_ref[...] += _dot(a_ref[...], b_ref[...])

        @pl.when(k == nk - 1)
        def _():
            epilogue(acc_ref[...])


def _matmul(a, b, *, mode="plain", out_dtype=F32, gain=None, flag=None, n_norm_tiles=0, residual=None,
            tm=1024, tn=512, tk=2048, name="matmul"):
    m, kdim = a.shape
    _, n = b.shape
    tm = tm if m % tm == 0 else min(512, m)
    nk = kdim // tk
    in_specs = [pl.BlockSpec((tm, tk), lambda i, j, k: (i, k)), pl.BlockSpec((tk, tn), lambda i, j, k: (k, j))]
    args = [a, b]
    if mode == "headnorm":
        in_specs += [pl.BlockSpec((1, tn), lambda i, j, k: (0, j)), pl.BlockSpec((1, tn), lambda i, j, k: (0, j))]
        args += [gain, flag]
    elif mode == "residual":
        in_specs += [pl.BlockSpec((tm, tn), lambda i, j, k: (i, j))]
        args += [residual]
    scratch = [pltpu.VMEM((tm, tn), F32)] if nk > 1 else []
    return pl.pallas_call(
        functools.partial(_mm_kernel, nk=nk, mode=mode, n_norm_tiles=n_norm_tiles),
        grid=(m // tm, n // tn, nk),
        in_specs=in_specs,
        out_specs=pl.BlockSpec((tm, tn), lambda i, j, k: (i, j)),
        out_shape=jax.ShapeDtypeStruct((m, n), out_dtype),
        scratch_shapes=scratch,
        compiler_params=_cparams("parallel", "parallel", "arbitrary"),
        name=name,
    )(*args)


def _cumsum_block(x, carry):
    n = x.shape[1]
    r = lax.broadcasted_iota(I32, (n, n), 0)
    c = lax.broadcasted_iota(I32, (n, n), 1)
    tri = jnp.where(r <= c, 1.0, 0.0).astype(BF16)
    hi = x.astype(BF16)
    r1 = x - hi.astype(F32)
    mid = r1.astype(BF16)
    lo = (r1 - mid.astype(F32)).astype(BF16)
    cs = _dot(hi, tri) + _dot(mid, tri) + _dot(lo, tri) + carry
    return cs, cs[:, n - 1:n]


def _logf_cum_kernel(*refs, p_len, t_len, blk):
    if p_len:
        raw_ref, fb_ref, past_ref, lf_ref, cum_ref = refs
    else:
        raw_ref, fb_ref, lf_ref, cum_ref = refs
    x = raw_ref[0] + fb_ref[...]
    lf = jnp.minimum(x, 0.0) - jnp.log1p(jnp.exp(-jnp.abs(x)))
    lf_ref[0] = lf
    carry = jnp.zeros((H_B, 1), F32)
    for s in range(0, p_len, blk):
        cs, carry = _cumsum_block(past_ref[0, :, s:s + blk], carry)
        cum_ref[0, :, s:s + blk] = cs
    nb = min(blk, t_len)
    for s in range(0, t_len, nb):
        cs, carry = _cumsum_block(lf[:, s:s + nb], carry)
        cum_ref[0, :, p_len + s:p_len + s + nb] = cs


def _logf_cum(raw_h, fbias, past_h):
    b, h, t = raw_h.shape
    p = 0 if past_h is None else past_h.shape[2]
    in_specs = [pl.BlockSpec((1, h, t), lambda i: (i, 0, 0)), pl.BlockSpec((h, 1), lambda i: (0, 0))]
    args = [raw_h, fbias.reshape(h, 1)]
    if p:
        in_specs.append(pl.BlockSpec((1, h, p), lambda i: (i, 0, 0)))
        args.append(past_h)
    return pl.pallas_call(
        functools.partial(_logf_cum_kernel, p_len=p, t_len=t, blk=512),
        grid=(b,),
        in_specs=in_specs,
        out_specs=[pl.BlockSpec((1, h, t), lambda i: (i, 0, 0)), pl.BlockSpec((1, h, p + t), lambda i: (i, 0, 0))],
        out_shape=[jax.ShapeDtypeStruct((b, h, t), F32), jax.ShapeDtypeStruct((b, h, p + t), F32)],
        compiler_params=_cparams("parallel"),
        name="logf_cum",
    )(*args)


def _toeplitz(row_vals, rows, cols):
    w = row_vals.shape[1]
    full = jnp.broadcast_to(row_vals[0:1, :], (rows, w))
    return pltpu.roll(full, 0, 1, stride=1, stride_axis=0)[:, :cols]


def _t5_strip_kernel(tab_ref, o_ref, *, tile):
    h = pl.program_id(0)
    width = 4 * tile
    m = lax.broadcasted_iota(I32, (8, width), 1)
    d = jnp.where(m < width // 2, m, m - width)
    rel = d - tile
    n = jnp.abs(rel)
    large = jnp.full_like(n, 8)
    for thr in (12, 16, 23, 32, 46, 64, 91):
        large = large + jnp.where(n >= thr, 1, 0)
    bucket = jnp.where(rel > 0, T5_BUCKETS // 2, 0) + jnp.where(n < 8, n, large)
    vals = jnp.zeros((8, width), F32)
    for bkt in range(T5_BUCKETS):
        vals = jnp.where(bucket == bkt, tab_ref[bkt, h], vals)
    vals = vals - tab_ref[T5_BUCKETS // 2 - 1, h]
    o_ref[0] = _toeplitz(vals, tile, 2 * tile)


def _t5_strip(t5_table, tile):
    return pl.pallas_call(
        functools.partial(_t5_strip_kernel, tile=tile),
        grid=(H_A,),
        in_specs=[pl.BlockSpec(memory_space=pltpu.SMEM)],
        out_specs=pl.BlockSpec((1, tile, 2 * tile), lambda h: (h, 0, 0)),
        out_shape=jax.ShapeDtypeStruct((H_A, tile, 2 * tile), F32),
        compiler_params=_cparams("parallel"),
        name="t5_strip",
    )(t5_table)


def _band_bias_kernel(tab_ref, o_ref, *, rows, cols, width):
    h = pl.program_id(0)
    m = lax.broadcasted_iota(I32, (8, width), 1)
    d = jnp.where(m < width // 2, m, m - width)
    idx = jnp.clip(d - D_BAND_LEFT, -D_REL_CLIP, D_REL_CLIP) + D_REL_CLIP
    vals = jnp.zeros((8, width), F32)
    for r in range(2 * D_REL_CLIP + 1):
        vals = jnp.where(idx == r, tab_ref[r, h], vals)
    vals = vals - tab_ref[0, h]
    i_chunk = lax.broadcasted_iota(I32, (rows, cols), 0) >> CHUNK_SHIFT
    c_chunk = lax.broadcasted_iota(I32, (rows, cols), 1) >> CHUNK_SHIFT
    visible = (c_chunk >= i_chunk) & (c_chunk <= i_chunk + D_LEFT_CHUNKS)
    o_ref[0] = jnp.where(visible, _toeplitz(vals, rows, cols), NEG_INF)


def _band_bias(rel_table):
    rows, cols, width = D_BAND_LEFT, 2 * D_BAND_LEFT, 4 * D_BAND_LEFT
    return pl.pallas_call(
        functools.partial(_band_bias_kernel, rows=rows, cols=cols, width=width),
        grid=(H_D,),
        in_specs=[pl.BlockSpec(memory_space=pltpu.SMEM)],
        out_specs=pl.BlockSpec((1, rows, cols), lambda h: (h, 0, 0)),
        out_shape=jax.ShapeDtypeStruct((H_D, rows, cols), F32),
        compiler_params=_cparams("parallel"),
        name="band_bias",
    )(rel_table)


def _select_kernel(iq_ref, misc_ref, kk_ref, o_ref, key_ref, *, tq, lp, l_valid, q_base, n_sel, tk_out,
                   causal_prefix):
    qi = pl.program_id(1)

    def run(pfx):
        iq = iq_ref[0]
        iw = misc_ref[0][:, MISC_IW:MISC_IW + IDX_HEADS]
        ik = kk_ref[0, 0:pfx, 0:IDX_DIM].astype(BF16)
        score = jnp.zeros((tq, pfx), F32)
        for h in range(IDX_HEADS):
            dots = _dot_nt(iq[:, h * IDX_DIM:(h + 1) * IDX_DIM].astype(BF16), ik)
            score = score + iw[:, h:h + 1] * jnp.maximum(dots, 0.0)
        q_pos = q_base + qi * tq + lax.broadcasted_iota(I32, (tq, 1), 0)
        k_pos = lax.broadcasted_iota(I32, (1, pfx), 1)
        adm = ((k_pos >> CHUNK_SHIFT) <= (q_pos >> CHUNK_SHIFT)) & (k_pos < l_valid)
        score = jnp.where(adm, score, NEG_INF)
        bits = lax.bitcast_convert_type(score, I32)
        key_ref[:, 0:pfx] = jnp.where(bits < 0, bits ^ jnp.int32(0x7FFFFFFF), bits)

        def keys():
            return key_ref[:, 0:pfx]

        def count(pred):
            return jnp.sum(jnp.where(pred, 1.0, 0.0), axis=-1, keepdims=True)

        kf = float(n_sel)
        t0 = jnp.where(count(keys() >= 0) >= kf, jnp.int32(0), jnp.int32(INT32_MIN))

        def thr_body(i, t):
            cand = t + (jnp.int32(1) << (30 - i))
            return jnp.where(count(keys() >= cand) >= kf, cand, t)

        thr = lax.fori_loop(0, 31, thr_body, t0)
        need = kf - count(keys() > thr)
        tied = (count(keys() == thr) != need) & (thr > NEG_INF_KEY)
        any_tied = jnp.max(jnp.where(tied, 1.0, 0.0)) > 0.5
        nbits = max(pfx - 1, 1).bit_length()

        def tie_search():
            def tie_body(i, jb):
                cand = jb + (jnp.int32(1) << (nbits - 1 - i))
                taken = count((keys() == thr) & (k_pos < cand))
                return jnp.where(taken <= need, cand, jb)

            return lax.fori_loop(0, nbits, tie_body, jnp.zeros((tq, 1), I32))

        jbound = lax.cond(any_tied, tie_search, lambda: jnp.full((tq, 1), pfx, I32))
        key = keys()
        sel = ((key > thr) | ((key == thr) & (k_pos < jbound))) & adm
        madd = jnp.where(sel, 0.0, NEG_INF)
        for kt in range(lp // tk_out):
            if (kt + 1) * tk_out <= pfx:
                o_ref[0, 0, kt] = madd[:, kt * tk_out:(kt + 1) * tk_out]
            else:
                o_ref[0, 0, kt] = jnp.full((tq, tk_out), NEG_INF, F32)

    if causal_prefix:
        for j in range(lp // tq):
            pl.when(qi == j)(functools.partial(run, (j + 1) * tq))
    else:
        run(lp)


def _select_mask(iq_arr, iq_col, misc_arr, misc_col, kk_arr, kk_col, *, t, tq, lp, l_valid, q_base, n_sel,
                 tk_out, causal_prefix):
    b = iq_arr.shape[0]
    nq = t // tq
    wk = kk_arr.shape[2] if kk_col is None else LANE
    kcol = 0 if kk_col is None else kk_col
    return pl.pallas_call(
        functools.partial(_select_kernel, tq=tq, lp=lp, l_valid=l_valid, q_base=q_base, n_sel=n_sel,
                          tk_out=tk_out, causal_prefix=causal_prefix),
        grid=(b, nq),
        in_specs=[
            pl.BlockSpec((1, tq, IDX_HEADS * IDX_DIM), lambda i, j: (i, j, iq_col)),
            pl.BlockSpec((1, tq, LANE), lambda i, j: (i, j, misc_col)),
            pl.BlockSpec((1, lp, wk), lambda i, j: (i, 0, kcol)),
        ],
        out_specs=pl.BlockSpec((1, 1, lp // tk_out, tq, tk_out), lambda i, j: (i, j, 0, 0, 0)),
        out_shape=jax.ShapeDtypeStruct((b, nq, lp // tk_out, tq, tk_out), F32),
        scratch_shapes=[pltpu.VMEM((tq, lp), I32)],
        compiler_params=_cparams("parallel", "parallel"),
        name="a_select",
    )(iq_arr, misc_arr, kk_arr)


def _online_update(s, v_bf16, m_s, l_s, acc_s):
    blocks = _lane_blocks(s)
    mx = blocks[0]
    for blk in blocks[1:]:
        mx = jnp.maximum(mx, blk)
    m_prev = m_s[...]
    m_new = jnp.maximum(m_prev, jnp.max(mx, axis=-1, keepdims=True))
    alpha = jnp.exp(m_prev - m_new)
    ps = [jnp.exp(blk - m_new) for blk in blocks]
    psum = ps[0]
    for p in ps[1:]:
        psum = psum + p
    l_s[...] = alpha * l_s[...] + psum
    p_all = jnp.concatenate([p.astype(BF16) for p in ps], axis=1) if len(ps) > 1 else ps[0].astype(BF16)
    acc_s[...] = alpha * acc_s[...] + _dot(p_all, v_bf16)
    m_s[...] = m_new


def _online_init(m_s, l_s, acc_s):
    m_s[...] = jnp.full_like(m_s, NEG_INF)
    l_s[...] = jnp.zeros_like(l_s)
    acc_s[...] = jnp.zeros_like(acc_s)


def _online_result(l_s, acc_s):
    return acc_s[...] / jnp.sum(l_s[...], axis=-1, keepdims=True)


def _stack_heads(q):
    return jnp.concatenate([q[:, r * HEAD_DIM:(r + 1) * HEAD_DIM] for r in range(REP_A)], axis=0).astype(BF16)


def _attn_a_prompt_kernel(q_ref, k_ref, v_ref, mask_ref, tz_ref, o_ref, m_s, l_s, acc_s, *, tq):
    qi = pl.program_id(2)
    qs = _stack_heads(q_ref[0] * SCALE)
    _online_init(m_s, l_s, acc_s)

    def tile(kt, bias):
        off = pl.multiple_of(kt * tq, tq)
        k = k_ref[0, pl.ds(off, tq), :].astype(BF16)
        v = v_ref[0, pl.ds(off, tq), :].astype(BF16)
        s = _dot_nt(qs, k).reshape(REP_A, tq, tq) + mask_ref[0, 0, kt][None]
        if bias is not None:
            s = s + bias
        _online_update(s.reshape(REP_A * tq, tq), v, m_s, l_s, acc_s)

    def plain_body(kt, carry):
        tile(kt, None)
        return carry

    lax.fori_loop(0, jnp.maximum(qi - 1, 0), plain_body, 0)

    @pl.when(qi >= 1)
    def _():
        tile(qi - 1, tz_ref[:, :, 0:tq])

    tile(qi, tz_ref[:, :, tq:2 * tq])
    out = _online_result(l_s, acc_s)
    for r in range(REP_A):
        o_ref[0, :, r * HEAD_DIM:(r + 1) * HEAD_DIM] = out[r * tq:(r + 1) * tq].astype(o_ref.dtype)


def _attn_a_prompt(proj3, mask, tz, *, tq):
    b, t, _ = proj3.shape
    nq = t // tq
    qc, kc, vc = EV_AQ // (REP_A * HEAD_DIM), EV_AK // HEAD_DIM, EV_AV // HEAD_DIM
    rows = REP_A * tq
    return pl.pallas_call(
        functools.partial(_attn_a_prompt_kernel, tq=tq),
        grid=(b, KV_A, nq),
        in_specs=[
            pl.BlockSpec((1, tq, REP_A * HEAD_DIM), lambda i, g, j: (i, j, qc + g)),
            pl.BlockSpec((1, t, HEAD_DIM), lambda i, g, j: (i, 0, kc + g)),
            pl.BlockSpec((1, t, HEAD_DIM), lambda i, g, j: (i, 0, vc + g)),
            pl.BlockSpec((1, 1, nq, tq, tq), lambda i, g, j: (i, j, 0, 0, 0)),
            pl.BlockSpec((REP_A, tq, 2 * tq), lambda i, g, j: (g, 0, 0)),
        ],
        out_specs=pl.BlockSpec((1, tq, REP_A * HEAD_DIM), lambda i, g, j: (i, j, g)),
        out_shape=jax.ShapeDtypeStruct((b, t, H_A * HEAD_DIM), BF16),
        scratch_shapes=[pltpu.VMEM((rows, LANE), F32), pltpu.VMEM((rows, LANE), F32),
                        pltpu.VMEM((rows, HEAD_DIM), F32)],
        compiler_params=_cparams("parallel", "parallel", "parallel"),
        name="attn_a_prompt",
    )(proj3, proj3, proj3, mask, tz)


def _attn_a_step_kernel(q_ref, kc_ref, vc_ref, kn_ref, vn_ref, mask_ref, tz_ref, o_ref, *, t, p):
    g = pl.program_id(1)
    qs = _stack_heads(q_ref[0] * SCALE)
    kc = kc_ref[0, pl.ds(g, p, stride=KV_A), :].astype(BF16)
    vc = vc_ref[0, pl.ds(g, p, stride=KV_A), :].astype(BF16)
    kn = _pad_keys(kn_ref[0], LANE).astype(BF16)
    vn = _pad_keys(vn_ref[0], LANE).astype(BF16)
    mask = mask_ref[0, 0, 0]
    s_c = _dot_nt(qs, kc).reshape(REP_A, t, p) + mask[None, :, 0:p]
    s_n = _dot_nt(qs, kn).reshape(REP_A, t, LANE) + mask[None, :, p:p + LANE] + tz_ref[:, :, LANE:2 * LANE]
    blocks = _lane_blocks(s_c)
    blocks[-1] = blocks[-1] + tz_ref[:, :, 0:LANE]
    blocks.append(s_n)
    mx = blocks[0]
    for blk in blocks[1:]:
        mx = jnp.maximum(mx, blk)
    m = jnp.max(mx, axis=-1, keepdims=True)
    ps = [jnp.exp(blk - m) for blk in blocks]
    psum = ps[0]
    for pb in ps[1:]:
        psum = psum + pb
    l = jnp.sum(psum, axis=-1, keepdims=True)
    p_c = jnp.concatenate([pb.astype(BF16) for pb in ps[:-1]], axis=-1).reshape(REP_A * t, p)
    out = _dot(p_c, vc) + _dot(ps[-1].astype(BF16).reshape(REP_A * t, LANE), vn)
    out = out.reshape(REP_A, t, HEAD_DIM) / l
    for r in range(REP_A):
        o_ref[0, :, r * HEAD_DIM:(r + 1) * HEAD_DIM] = out[r].astype(o_ref.dtype)


def _attn_a_step(proj3, cache_k, cache_v, mask, tz):
    b, t, _ = proj3.shape
    p = cache_k.shape[1] // KV_A
    qc, kc, vc = EV_AQ // (REP_A * HEAD_DIM), EV_AK // HEAD_DIM, EV_AV // HEAD_DIM
    return pl.pallas_call(
        functools.partial(_attn_a_step_kernel, t=t, p=p),
        grid=(b, KV_A),
        in_specs=[
            pl.BlockSpec((1, t, REP_A * HEAD_DIM), lambda i, g: (i, 0, qc + g)),
            pl.BlockSpec((1, p * KV_A, HEAD_DIM), lambda i, g: (i, 0, 0)),
            pl.BlockSpec((1, p * KV_A, HEAD_DIM), lambda i, g: (i, 0, 0)),
            pl.BlockSpec((1, t, HEAD_DIM), lambda i, g: (i, 0, kc + g)),
            pl.BlockSpec((1, t, HEAD_DIM), lambda i, g: (i, 0, vc + g)),
            pl.BlockSpec((1, 1, 1, t, p + LANE), lambda i, g: (i, 0, 0, 0, 0)),
            pl.BlockSpec((REP_A, t, 2 * LANE), lambda i, g: (g, 0, 0)),
        ],
        out_specs=pl.BlockSpec((1, t, REP_A * HEAD_DIM), lambda i, g: (i, 0, g)),
        out_shape=jax.ShapeDtypeStruct((b, t, H_A * HEAD_DIM), BF16),
        compiler_params=_cparams("parallel", "parallel"),
        name="attn_a_step",
    )(proj3, cache_k, cache_v, proj3, proj3, mask, tz)


def _causal_mask(s):
    r = lax.broadcasted_iota(I32, s.shape, 0)
    c = lax.broadcasted_iota(I32, s.shape, 1)
    return jnp.where(c <= r, s, NEG_INF)


def _attn_b_prompt_kernel(q_ref, k_ref, v_ref, ck_ref, o_ref, m_s, l_s, acc_s, *, tq):
    qi = pl.program_id(2)
    q = (q_ref[0] * SCALE).astype(BF16)
    _online_init(m_s, l_s, acc_s)

    def tile(kt, diag):
        off = pl.multiple_of(kt * tq, tq)
        k = k_ref[0, pl.ds(off, tq), :].astype(BF16)
        v = v_ref[0, pl.ds(off, tq), :].astype(BF16)
        s = _dot_nt(q, k) - ck_ref[0, 0, kt]
        if diag:
            s = _causal_mask(s)
        _online_update(s, v, m_s, l_s, acc_s)

    def body(kt, carry):
        tile(kt, False)
        return carry

    lax.fori_loop(0, qi, body, 0)
    tile(qi, True)
    o_ref[0] = _online_result(l_s, acc_s).astype(o_ref.dtype)


def _attn_b_prompt(proj3, cum_h, *, tq=512):
    b, t, _ = proj3.shape
    nq = t // tq
    ck = cum_h.reshape(b, H_B, nq, 1, tq)
    qc, kc, vc = EV_BQ // HEAD_DIM, EV_BK // HEAD_DIM, EV_BV // HEAD_DIM
    return pl.pallas_call(
        functools.partial(_attn_b_prompt_kernel, tq=tq),
        grid=(b, H_B, nq),
        in_specs=[
            pl.BlockSpec((1, tq, HEAD_DIM), lambda i, h, j: (i, j, qc + h)),
            pl.BlockSpec((1, t, HEAD_DIM), lambda i, h, j: (i, 0, kc + h)),
            pl.BlockSpec((1, t, HEAD_DIM), lambda i, h, j: (i, 0, vc + h)),
            pl.BlockSpec((1, 1, nq, 1, tq), lambda i, h, j: (i, h, 0, 0, 0)),
        ],
        out_specs=pl.BlockSpec((1, tq, HEAD_DIM), lambda i, h, j: (i, j, h)),
        out_shape=jax.ShapeDtypeStruct((b, t, H_B * HEAD_DIM), BF16),
        scratch_shapes=[pltpu.VMEM((tq, LANE), F32), pltpu.VMEM((tq, LANE), F32), pltpu.VMEM((tq, HEAD_DIM), F32)],
        compiler_params=_cparams("parallel", "parallel", "parallel"),
        name="attn_b_prompt",
    )(proj3, proj3, proj3, ck)


def _attn_b_step_kernel(q_ref, kc_ref, vc_ref, kn_ref, vn_ref, ckc_ref, ckn_ref, o_ref, m_s, l_s, acc_s,
                        *, nkc, t, tkc):
    kb = pl.program_id(1)

    @pl.when(kb == 0)
    def _():
        _online_init(m_s, l_s, acc_s)

    def q_head(h):
        return (q_ref[0, :, h * HEAD_DIM:(h + 1) * HEAD_DIM] * SCALE).astype(BF16)

    @pl.when(kb < nkc)
    def _():
        ck = ckc_ref[0, 0]
        for h in range(H_B):
            k = kc_ref[0, pl.ds(h, tkc, stride=H_B), :].astype(BF16)
            v = vc_ref[0, pl.ds(h, tkc, stride=H_B), :].astype(BF16)
            s = _dot_nt(q_head(h), k) - ck[h:h + 1, :]
            _online_update(s, v, m_s.at[h], l_s.at[h], acc_s.at[h])

    @pl.when(kb == nkc)
    def _():
        ckn = ckn_ref[0]
        for h in range(H_B):
            sl = slice(h * HEAD_DIM, (h + 1) * HEAD_DIM)
            k = _pad_keys(kn_ref[0, :, sl], LANE).astype(BF16)
            v = _pad_keys(vn_ref[0, :, sl], LANE).astype(BF16)
            s = _causal_mask(_dot_nt(q_head(h), k) - ckn[h:h + 1, :])
            _online_update(s, v, m_s.at[h], l_s.at[h], acc_s.at[h])
            o_ref[0, :, sl] = _online_result(l_s.at[h], acc_s.at[h]).astype(o_ref.dtype)


def _attn_b_step(proj3, cache_k, cache_v, cum_h, *, tkc=1024):
    b, t, _ = proj3.shape
    p = cache_k.shape[1] // H_B
    nkc = p // tkc
    w = H_B * HEAD_DIM
    ck_cache = cum_h[:, :, :p].reshape(b, H_B, nkc, tkc).transpose(0, 2, 1, 3)
    ck_new = jnp.pad(cum_h[:, :, p:], ((0, 0), (0, 0), (0, LANE - t)))
    last = nkc - 1
    return pl.pallas_call(
        functools.partial(_attn_b_step_kernel, nkc=nkc, t=t, tkc=tkc),
        grid=(b, nkc + 1),
        in_specs=[
            pl.BlockSpec((1, t, w), lambda i, j: (i, 0, EV_BQ // w)),
            pl.BlockSpec((1, tkc * H_B, HEAD_DIM), lambda i, j: (i, jnp.minimum(j, last), 0)),
            pl.BlockSpec((1, tkc * H_B, HEAD_DIM), lambda i, j: (i, jnp.minimum(j, last), 0)),
            pl.BlockSpec((1, t, w), lambda i, j: (i, 0, EV_BK // w)),
            pl.BlockSpec((1, t, w), lambda i, j: (i, 0, EV_BV // w)),
            pl.BlockSpec((1, 1, H_B, tkc), lambda i, j: (i, jnp.minimum(j, last), 0, 0)),
            pl.BlockSpec((1, H_B, LANE), lambda i, j: (i, 0, 0)),
        ],
        out_specs=pl.BlockSpec((1, t, w), lambda i, j: (i, 0, 0)),
        out_shape=jax.ShapeDtypeStruct((b, t, w), BF16),
        scratch_shapes=[pltpu.VMEM((H_B, t, LANE), F32), pltpu.VMEM((H_B, t, LANE), F32),
                        pltpu.VMEM((H_B, t, HEAD_DIM), F32)],
        compiler_params=_cparams("parallel", "arbitrary"),
        name="attn_b_step",
    )(proj3, cache_k, cache_v, proj3, proj3, ck_cache, ck_new)


def _tri_ones():
    r = lax.broadcasted_iota(I32, (LANE, 2 * LANE), 0)
    c = lax.broadcasted_iota(I32, (LANE, 2 * LANE), 1)
    return jnp.where((r > c) | (c >= LANE), 1.0, 0.0).astype(BF16)


def _stick_tile(q_bf16, k_bf16, v_bf16, run, tri_ones, row_minus_col):
    rows = q_bf16.shape[0]
    z_blocks = _lane_blocks(_dot_nt(q_bf16, k_bf16))
    rc_blocks = None if row_minus_col is None else _lane_blocks(row_minus_col)
    ws = [None] * len(z_blocks)
    for j in reversed(range(len(z_blocks))):
        z = z_blocks[j]
        tail = jnp.log(1.0 + jnp.exp(-jnp.abs(z)))
        log_beta = jnp.minimum(z, 0.0) - tail
        log_keep = -jnp.maximum(z, 0.0) - tail
        if rc_blocks is not None:
            strict = rc_blocks[j] > 0
            log_keep = jnp.where(strict, log_keep, 0.0)
        hi = log_keep.astype(BF16)
        lo = (log_keep - hi.astype(F32)).astype(BF16)
        both = _dot(jnp.concatenate([hi, lo], axis=0), tri_ones)
        both = both[:rows] + both[rows:]
        w = jnp.exp(log_beta + both[:, :LANE] + run)
        if rc_blocks is not None:
            w = jnp.where(strict, w, 0.0)
        ws[j] = w.astype(BF16)
        run = run + both[:, LANE:]
    w_all = jnp.concatenate(ws, axis=1) if len(ws) > 1 else ws[0]
    return _dot(w_all, v_bf16), run


def _attn_c_prompt_kernel(q_ref, k_ref, v_ref, o_ref, run_s, acc_s, *, tq):
    qi = pl.program_id(2)
    q = (q_ref[0] * SCALE).astype(BF16)
    tri_ones = _tri_ones()

    def load(kt):
        off = pl.multiple_of(kt * tq, tq)
        return k_ref[0, pl.ds(off, tq), :].astype(BF16), v_ref[0, pl.ds(off, tq), :].astype(BF16)

    rc = lax.broadcasted_iota(I32, (tq, tq), 0) - lax.broadcasted_iota(I32, (tq, tq), 1)
    k, v = load(qi)
    out, run = _stick_tile(q, k, v, jnp.zeros((tq, LANE), F32), tri_ones, rc)
    acc_s[...] = out
    run_s[...] = run

    def body(i, carry):
        k, v = load(qi - 1 - i)
        out, run = _stick_tile(q, k, v, run_s[...], tri_ones, None)
        acc_s[...] += out
        run_s[...] = run
        return carry

    lax.fori_loop(0, qi, body, 0)
    o_ref[0] = acc_s[...].astype(o_ref.dtype)


def _attn_c_prompt(proj3, *, tq=512):
    b, t, _ = proj3.shape
    nq = t // tq
    qc, kc, vc = OD_CQ // HEAD_DIM, OD_CK // HEAD_DIM, OD_CV // HEAD_DIM
    return pl.pallas_call(
        functools.partial(_attn_c_prompt_kernel, tq=tq),
        grid=(b, H_C, nq),
        in_specs=[
            pl.BlockSpec((1, tq, HEAD_DIM), lambda i, h, j: (i, j, qc + h)),
            pl.BlockSpec((1, t, HEAD_DIM), lambda i, h, j: (i, 0, kc + h)),
            pl.BlockSpec((1, t, HEAD_DIM), lambda i, h, j: (i, 0, vc + h)),
        ],
        out_specs=pl.BlockSpec((1, tq, HEAD_DIM), lambda i, h, j: (i, j, h)),
        out_shape=jax.ShapeDtypeStruct((b, t, H_C * HEAD_DIM), BF16),
        scratch_shapes=[pltpu.VMEM((tq, LANE), F32), pltpu.VMEM((tq, HEAD_DIM), F32)],
        compiler_params=_cparams("parallel", "parallel", "parallel"),
        name="attn_c_prompt",
    )(proj3, proj3, proj3)


def _attn_c_step_kernel(q_ref, kc_ref, vc_ref, kn_ref, vn_ref, o_ref, run_s, acc_s, *, nkc, t, tkc, sub):
    kb = pl.program_id(1)
    tri_ones = _tri_ones()

    def q_head(h):
        return (q_ref[0, :, h * HEAD_DIM:(h + 1) * HEAD_DIM] * SCALE).astype(BF16)

    @pl.when(kb == 0)
    def _():
        rc = lax.broadcasted_iota(I32, (t, LANE), 0) - lax.broadcasted_iota(I32, (t, LANE), 1)
        for h in range(H_C):
            sl = slice(h * HEAD_DIM, (h + 1) * HEAD_DIM)
            k = _pad_keys(kn_ref[0, :, sl], LANE).astype(BF16)
            v = _pad_keys(vn_ref[0, :, sl], LANE).astype(BF16)
            out, run = _stick_tile(q_head(h), k, v, jnp.zeros((t, LANE), F32), tri_ones, rc)
            acc_s[h] = out
            run_s[h] = run

    @pl.when(kb > 0)
    def _():
        def body(i, carry):
            row0 = (tkc - sub - i * sub) * H_C
            for h in range(H_C):
                k = kc_ref[0, pl.ds(row0 + h, sub, stride=H_C), :].astype(BF16)
                v = vc_ref[0, pl.ds(row0 + h, sub, stride=H_C), :].astype(BF16)
                out, run = _stick_tile(q_head(h), k, v, run_s[h], tri_ones, None)
                acc_s[h] += out
                run_s[h] = run
            return carry

        lax.fori_loop(0, tkc // sub, body, 0)

    @pl.when(kb == nkc)
    def _():
        for h in range(H_C):
            o_ref[0, :, h * HEAD_DIM:(h + 1) * HEAD_DIM] = acc_s[h].astype(o_ref.dtype)


def _attn_c_step(proj3, cache_k, cache_v, *, tkc=1024, sub=1024):
    b, t, _ = proj3.shape
    p = cache_k.shape[1] // H_C
    nkc = p // tkc
    w = H_C * HEAD_DIM

    def cache_idx(i, j):
        return (i, jnp.clip(nkc - j, 0, nkc - 1), 0)

    return pl.pallas_call(
        functools.partial(_attn_c_step_kernel, nkc=nkc, t=t, tkc=tkc, sub=sub),
        grid=(b, nkc + 1),
        in_specs=[
            pl.BlockSpec((1, t, w), lambda i, j: (i, 0, OD_CQ // w)),
            pl.BlockSpec((1, tkc * H_C, HEAD_DIM), cache_idx),
            pl.BlockSpec((1, tkc * H_C, HEAD_DIM), cache_idx),
            pl.BlockSpec((1, t, w), lambda i, j: (i, 0, OD_CK // w)),
            pl.BlockSpec((1, t, w), lambda i, j: (i, 0, OD_CV // w)),
        ],
        out_specs=pl.BlockSpec((1, t, w), lambda i, j: (i, 0, 0)),
        out_shape=jax.ShapeDtypeStruct((b, t, w), BF16),
        scratch_shapes=[pltpu.VMEM((H_C, t, LANE), F32), pltpu.VMEM((H_C, t, HEAD_DIM), F32)],
        compiler_params=_cparams("parallel", "arbitrary"),
        name="attn_c_step",
    )(proj3, cache_k, cache_v, proj3, proj3)


def _attn_d_kernel(q_ref, kp_ref, vp_ref, ko_ref, vo_ref, bias_ref, o_ref, *, tq, q_base):
    qi = pl.program_id(2)
    q = (q_ref[0] * SCALE).astype(BF16)
    s_o = _dot_nt(q, ko_ref[0].astype(BF16)) + bias_ref[0, :, D_BAND_LEFT:D_BAND_LEFT + tq]
    m_o = jnp.max(s_o, axis=-1, keepdims=True)

    def with_left():
        s_p = _dot_nt(q, kp_ref[0].astype(BF16)) + bias_ref[0, :, 0:D_BAND_LEFT]
        m = jnp.maximum(jnp.max(s_p, axis=-1, keepdims=True), m_o)
        p_p = jnp.exp(s_p - m)
        p_o = jnp.exp(s_o - m)
        l = jnp.sum(p_p, axis=-1, keepdims=True) + jnp.sum(p_o, axis=-1, keepdims=True)
        out = _dot(p_p.astype(BF16), vp_ref[0].astype(BF16)) + _dot(p_o.astype(BF16), vo_ref[0].astype(BF16))
        o_ref[0] = (out / l).astype(o_ref.dtype)

    def own_only():
        p_o = jnp.exp(s_o - m_o)
        l = jnp.sum(p_o, axis=-1, keepdims=True)
        o_ref[0] = (_dot(p_o.astype(BF16), vo_ref[0].astype(BF16)) / l).astype(o_ref.dtype)

    if q_base >= D_BAND_LEFT:
        with_left()
    else:
        has_left = q_base + qi * tq >= D_BAND_LEFT
        pl.when(has_left)(with_left)
        pl.when(jnp.logical_not(has_left))(own_only)


def _attn_d(q_arr, q_col, kprev, kprev_col, vprev, vprev_col, kown, kown_col, vown, vown_col, bias,
            *, t, tq, q_base, prev_follows_own):
    b = q_arr.shape[0]
    nq = t // tq
    if prev_follows_own:
        def prev_idx(col):
            return lambda i, h, j: (i, jnp.maximum(j - 1, 0), col + h)
    else:
        def prev_idx(col):
            return lambda i, h, j: (i, 0, col + h)
    return pl.pallas_call(
        functools.partial(_attn_d_kernel, tq=tq, q_base=q_base),
        grid=(b, H_D, nq),
        in_specs=[
            pl.BlockSpec((1, tq, HEAD_DIM), lambda i, h, j: (i, j, q_col + h)),
            pl.BlockSpec((1, D_BAND_LEFT, HEAD_DIM), prev_idx(kprev_col)),
            pl.BlockSpec((1, D_BAND_LEFT, HEAD_DIM), prev_idx(vprev_col)),
            pl.BlockSpec((1, tq, HEAD_DIM), lambda i, h, j: (i, j, kown_col + h)),
            pl.BlockSpec((1, tq, HEAD_DIM), lambda i, h, j: (i, j, vown_col + h)),
            pl.BlockSpec((1, tq, D_BAND_LEFT + tq), lambda i, h, j: (h, 0, 0)),
        ],
        out_specs=pl.BlockSpec((1, tq, HEAD_DIM), lambda i, h, j: (i, j, h)),
        out_shape=jax.ShapeDtypeStruct((b, t, H_D * HEAD_DIM), BF16),
        compiler_params=_cparams("parallel", "parallel", "parallel"),
        name="attn_d",
    )(q_arr, kprev, vprev, kown, vown, bias)


def _pad_rows(x, rows):
    return jnp.pad(x, ((0, 0), (0, rows - x.shape[1]), (0, 0)))


def _even_mixer(h2, b, t, past, w, tz_prompt, tz_step, tq_a):
    proj = _matmul(h2, w["w_in"], mode="headnorm", gain=w["gain"], flag=w["flag"], n_norm_tiles=9,
                   name="proj_even")
    proj3 = proj.reshape(b, t, EV_PAD)
    ak = proj3[:, :, EV_AK:EV_AK + KV_A * HEAD_DIM]
    av = proj3[:, :, EV_AV:EV_AV + KV_A * HEAD_DIM]
    ik = proj3[:, :, EV_MISC + MISC_IK:EV_MISC + MISC_IK + IDX_DIM]
    bk = proj3[:, :, EV_BK:EV_BK + H_B * HEAD_DIM]
    bv = proj3[:, :, EV_BV:EV_BV + H_B * HEAD_DIM]
    raw_h = proj3[:, :, EV_MISC + MISC_BF:EV_MISC + MISC_BF + H_B].transpose(0, 2, 1)
    iq_col = EV_IQ // (IDX_HEADS * IDX_DIM)
    misc_col = EV_MISC // LANE
    if past is None:
        logf_h, cum_h = _logf_cum(raw_h, w["f_bias"], None)
        mask = _select_mask(proj3, iq_col, proj3, misc_col, proj3, misc_col, t=t, tq=tq_a, lp=t, l_valid=t,
                            q_base=0, n_sel=min(TOPK_MAX, t // 4), tk_out=tq_a, causal_prefix=True)
        out_a = _attn_a_prompt(proj3, mask, tz_prompt, tq=tq_a)
        out_b = _attn_b_prompt(proj3, cum_h, tq=min(512, t))
    else:
        p_ak, p_av, p_ik, p_bk, p_bv, p_lf = past
        p = p_ak.shape[1]
        l_valid = p + t
        lp = p + LANE
        logf_h, cum_h = _logf_cum(raw_h, w["f_bias"], p_lf.transpose(0, 2, 1))
        ik_all = _pad_rows(jnp.concatenate([p_ik, ik], axis=1), lp)
        mask = _select_mask(proj3, iq_col, proj3, misc_col, ik_all, None, t=t, tq=t, lp=lp, l_valid=l_valid,
                            q_base=p, n_sel=min(TOPK_MAX, l_valid // 4), tk_out=lp, causal_prefix=False)
        out_a = _attn_a_step(proj3, p_ak.reshape(b, p * KV_A, HEAD_DIM), p_av.reshape(b, p * KV_A, HEAD_DIM),
                             mask, tz_step[:, :t])
        out_b = _attn_b_step(proj3, p_bk.reshape(b, p * H_B, HEAD_DIM), p_bv.reshape(b, p * H_B, HEAD_DIM), cum_h)
    mixed = jnp.concatenate([out_a, out_b], axis=-1).reshape(b * t, -1)
    state = (ak.reshape(b, t, KV_A, HEAD_DIM), av.reshape(b, t, KV_A, HEAD_DIM), ik,
             bk.reshape(b, t, H_B, HEAD_DIM), bv.reshape(b, t, H_B, HEAD_DIM), logf_h.transpose(0, 2, 1))
    return mixed, state


def _odd_mixer(h2, b, t, past, w, band):
    proj = _matmul(h2, w["w_in"], mode="headnorm", gain=w["gain"], flag=w["flag"], n_norm_tiles=4,
                   name="proj_odd")
    proj3 = proj.reshape(b, t, OD_WIDTH)
    ck = proj3[:, :, OD_CK:OD_CK + H_C * HEAD_DIM]
    cv = proj3[:, :, OD_CV:OD_CV + H_C * HEAD_DIM]
    dk = proj3[:, :, OD_DK:OD_DK + H_D * HEAD_DIM]
    dv = proj3[:, :, OD_DV:OD_DV + H_D * HEAD_DIM]
    dq_col, dk_col, dv_col = OD_DQ // HEAD_DIM, OD_DK // HEAD_DIM, OD_DV // HEAD_DIM
    if past is None:
        out_c = _attn_c_prompt(proj3, tq=min(512, t))
        out_d = _attn_d(proj3, dq_col, proj3, dk_col, proj3, dv_col, proj3, dk_col, proj3, dv_col, band,
                        t=t, tq=D_BAND_LEFT, q_base=0, prev_follows_own=True)
        keep = min(D_BAND_LEFT, t)
        d_rows = (dk[:, t - keep:], dv[:, t - keep:])
    else:
        p_ck, p_cv, p_dk, p_dv = past
        p = p_ck.shape[1]
        out_c = _attn_c_step(proj3, p_ck.reshape(b, p * H_C, HEAD_DIM), p_cv.reshape(b, p * H_C, HEAD_DIM))
        wd = p_dk.shape[1]
        out_d = _attn_d(proj3, dq_col, p_dk.reshape(b, wd, -1), 0, p_dv.reshape(b, wd, -1), 0,
                        proj3, dk_col, proj3, dv_col, band[:, :t, :D_BAND_LEFT + t],
                        t=t, tq=t, q_base=p, prev_follows_own=False)
        d_rows = (dk, dv)
    mixed = jnp.concatenate([out_c, out_d], axis=-1).reshape(b * t, -1)
    state = (ck.reshape(b, t, H_C, HEAD_DIM), cv.reshape(b, t, H_C, HEAD_DIM),
             d_rows[0].reshape(b, -1, H_D, HEAD_DIM), d_rows[1].reshape(b, -1, H_D, HEAD_DIM))
    return mixed, state


def _trunk(x, caches_even, caches_odd, wts):
    b, t, d = x.shape
    x2 = x.reshape(b * t, d)
    h2 = _rmsnorm(x2, wts["attn_norm"][0])
    mixed, st_even = _even_mixer(h2, b, t, caches_even, wts["even"], wts["tz_prompt"], wts["tz_step"],
                                 wts["tq_a"])
    x2 = _matmul(mixed, wts["even"]["w_out"], mode="residual", residual=x2, name="out_even")
    h2 = _rmsnorm(x2, wts["mlp_norm"][0])
    u = _matmul(h2, wts["w_up"][0], mode="relu2", out_dtype=BF16, tn=1024, name="mlp_up0")
    x2 = _matmul(u, wts["w_down"][0], mode="residual", residual=x2, tm=512, tk=u.shape[1], name="mlp_down0")
    h2 = _rmsnorm(x2, wts["attn_norm"][1])
    mixed, st_odd = _odd_mixer(h2, b, t, caches_odd, wts["odd"], wts["band"])
    x2 = _matmul(mixed, wts["odd"]["w_out"], mode="residual", residual=x2, name="out_odd")
    h2 = _rmsnorm(x2, wts["mlp_norm"][1])
    u = _matmul(h2, wts["w_up"][1], mode="relu2", out_dtype=BF16, tn=1024, name="mlp_up1")
    x2 = _matmul(u, wts["w_down"][1], mode="residual", residual=x2, tm=512, tk=u.shape[1], name="mlp_down1")
    return x2.reshape(b, t, d), tuple(s[None] for s in st_even), tuple(s[None] for s in st_odd)


def _even_weights(w_in, w_out, a_qn, a_kn, b_qn, b_kn, f_bias):
    aq, ak, av, iq, ik, iw, bq, bk, bv, bf = jnp.split(
        w_in, [1024, 1280, 1536, 2048, 2112, 2120, 3144, 4168, 5192], axis=1)
    d = w_in.shape[0]
    pad = jnp.zeros((d, EV_PAD - EV_WIDTH + LANE - (IDX_DIM + IDX_HEADS + H_B)), w_in.dtype)
    w_r = jnp.concatenate([aq, bq, bk, bv, ak, av, iq, ik, iw, bf, pad], axis=1).astype(BF16)
    ones = jnp.ones((EV_PAD,), F32)
    gain = ones.at[EV_AQ:EV_AQ + 1024].set(jnp.tile(a_qn, H_A))
    gain = gain.at[EV_BQ:EV_BQ + 1024].set(jnp.tile(b_qn, H_B))
    gain = gain.at[EV_BK:EV_BK + 1024].set(jnp.tile(b_kn, H_B))
    gain = gain.at[EV_AK:EV_AK + 256].set(jnp.tile(a_kn, KV_A))
    flag = jnp.zeros((EV_PAD,), F32).at[0:EV_BV].set(1.0).at[EV_AK:EV_AV].set(1.0)
    return {"w_in": w_r, "gain": gain.reshape(1, -1), "flag": flag.reshape(1, -1),
            "w_out": w_out.astype(BF16), "f_bias": f_bias}


def _odd_weights(w_in, w_out, d_qn, d_kn):
    cq, ck, cv, dq, dk, dv = jnp.split(w_in, 6, axis=1)
    w_r = jnp.concatenate([dq, dk, cq, ck, cv, dv], axis=1).astype(BF16)
    gain = jnp.ones((OD_WIDTH,), F32)
    gain = gain.at[OD_DQ:OD_DQ + 1024].set(jnp.tile(d_qn, H_D)).at[OD_DK:OD_DK + 1024].set(jnp.tile(d_kn, H_D))
    flag = jnp.zeros((OD_WIDTH,), F32).at[0:OD_CQ].set(1.0)
    return {"w_in": w_r, "gain": gain.reshape(1, -1), "flag": flag.reshape(1, -1), "w_out": w_out.astype(BF16)}


def kernel(x_prompt, x_sample, cache_a_k, cache_a_v, cache_a_kidx, cache_b_k, cache_b_v, cache_b_logf,
           cache_c_k, cache_c_v, cache_d_k, cache_d_v, attn_norm, mlp_norm, w_in_even, w_out_even,
           w_in_odd, w_out_odd, a_q_norm, a_k_norm, b_q_norm, b_k_norm, forget_bias, t5_bias,
           d_q_norm, d_k_norm, d_rel_bias, w_up, w_down):
    tq_a = min(256, x_prompt.shape[1])
    wts = {
        "attn_norm": attn_norm,
        "mlp_norm": mlp_norm,
        "even": _even_weights(w_in_even[0], w_out_even[0], a_q_norm[0], a_k_norm[0], b_q_norm[0], b_k_norm[0],
                              forget_bias[0]),
        "odd": _odd_weights(w_in_odd[0], w_out_odd[0], d_q_norm[0], d_k_norm[0]),
        "w_up": w_up.astype(BF16),
        "w_down": w_down.astype(BF16),
        "tq_a": tq_a,
        "tz_prompt": _t5_strip(t5_bias, tq_a),
        "tz_step": _t5_strip(t5_bias, LANE),
        "band": _band_bias(d_rel_bias[0]),
    }
    y_p, ev_p, od_p = _trunk(x_prompt, None, None, wts)
    caches_even = (cache_a_k[0], cache_a_v[0], cache_a_kidx[0], cache_b_k[0], cache_b_v[0], cache_b_logf[0])
    caches_odd = (cache_c_k[0], cache_c_v[0], cache_d_k[0], cache_d_v[0])
    y_s, ev_s, od_s = _trunk(x_sample, caches_even, caches_odd, wts)
    return (y_p, y_s) + ev_p + od_p + ev_s + od_s
```

```python
import functools

import numpy as np
import jax
import jax.numpy as jnp
from jax import lax
from jax.experimental import pallas as pl
from jax.experimental.pallas import tpu as pltpu

F32 = jnp.float32
BF16 = jnp.bfloat16
I32 = jnp.int32

HEAD_DIM = 128
CHUNK = 64
CHUNK_SHIFT = 6
H_A = 8
KV_A = 2
REP_A = H_A // KV_A
H_B = 8
H_C = 8
H_D = 8
IDX_HEADS = 8
IDX_DIM = 64
TOPK_MAX = 256
T5_BUCKETS = 32
D_LEFT_CHUNKS = 8
D_BAND_LEFT = D_LEFT_CHUNKS * CHUNK
D_REL_CLIP = 128
EPS = 1e-6
NEG_INF = -1e30
SCALE = HEAD_DIM ** -0.5
LANE = 128
INT32_MIN = -(2 ** 31)
NEG_INF_KEY = int(np.float32(NEG_INF).view(np.int32)) ^ 0x7FFFFFFF

VMEM_LIMIT_BYTES = 56 * 1024 * 1024

EV_AQ, EV_BQ, EV_BK, EV_BV, EV_AK, EV_AV, EV_IQ, EV_MISC = 0, 1024, 2048, 3072, 4096, 4352, 4608, 5120
EV_WIDTH = 5248
EV_PAD = 5632
MISC_IK, MISC_IW, MISC_BF = 0, 64, 72
OD_DQ, OD_DK, OD_CQ, OD_CK, OD_CV, OD_DV = 0, 1024, 2048, 3072, 4096, 5120
OD_WIDTH = 6144


def _cparams(*sem):
    return pltpu.CompilerParams(dimension_semantics=sem, vmem_limit_bytes=VMEM_LIMIT_BYTES)


def _dot_nt(a, b):
    return lax.dot_general(a, b, (((1,), (1,)), ((), ())), preferred_element_type=F32)


def _dot(a, b):
    return jnp.dot(a, b, preferred_element_type=F32)


def _lane_blocks(x):
    return [x[..., j * LANE:(j + 1) * LANE] for j in range(x.shape[-1] // LANE)]


def _pad_keys(x, rows):
    return jnp.concatenate([x, jnp.zeros((rows - x.shape[0], x.shape[1]), x.dtype)], axis=0)


def _rmsnorm_kernel(x_ref, g_ref, o_ref):
    x = x_ref[...]
    ms = jnp.mean(x * x, axis=-1, keepdims=True)
    o_ref[...] = (x * lax.rsqrt(ms + EPS) * g_ref[...]).astype(o_ref.dtype)


def _rmsnorm(x2, gain):
    m, d = x2.shape
    tm = min(512, m)
    return pl.pallas_call(
        _rmsnorm_kernel,
        grid=(m // tm,),
        in_specs=[pl.BlockSpec((tm, d), lambda i: (i, 0)), pl.BlockSpec((1, d), lambda i: (0, 0))],
        out_specs=pl.BlockSpec((tm, d), lambda i: (i, 0)),
        out_shape=jax.ShapeDtypeStruct((m, d), BF16),
        compiler_params=_cparams("parallel"),
        name="rmsnorm",
    )(x2, gain.reshape(1, d))


def _mm_kernel(*refs, n_a, nk, mode, n_norm_tiles):
    a_refs, b_ref = refs[:n_a], refs[n_a]
    refs = refs[n_a - 1:]
    if mode == "headnorm":
        gain_ref, flag_ref, o_ref = refs[2], refs[3], refs[4]
        rest = refs[5:]
    elif mode == "residual":
        res_ref, o_ref = refs[2], refs[3]
        rest = refs[4:]
    else:
        o_ref = refs[2]
        rest = refs[3:]

    def product():
        row, acc = 0, None
        for a_ref in a_refs:
            kp = a_ref.shape[1]
            part = _dot(a_ref[...], b_ref[row:row + kp, :])
            acc = part if acc is None else acc + part
            row += kp
        return acc

    def epilogue(acc):
        if mode == "headnorm":
            j = pl.program_id(1)

            @pl.when(j < n_norm_tiles)
            def _():
                tn = acc.shape[1]
                for g in range(tn // HEAD_DIM):
                    sl = slice(g * HEAD_DIM, (g + 1) * HEAD_DIM)
                    blk = acc[:, sl]
                    ms = jnp.mean(blk * blk, axis=-1, keepdims=True)
                    normed = blk * lax.rsqrt(ms + EPS) * gain_ref[:, sl]
                    o_ref[:, sl] = jnp.where(flag_ref[:, sl] > 0.0, normed, blk)

            @pl.when(j >= n_norm_tiles)
            def _():
                o_ref[...] = acc
        elif mode == "relu2":
            r = jnp.maximum(acc, 0.0)
            o_ref[...] = (r * r).astype(o_ref.dtype)
        elif mode == "residual":
            o_ref[...] = res_ref[...] + acc
        else:
            o_ref[...] = acc.astype(o_ref.dtype)

    if nk == 1:
        epilogue(product())
    else:
        acc_ref = rest[0]
        k = pl.program_id(2)

        @pl.when(k == 0)
        def _():
            acc_ref[...] = jnp.zeros_like(acc_ref)

        acc_ref[...] += product()

        @pl.when(k == nk - 1)
        def _():
            epilogue(acc_ref[...])


def _matmul(a, b, *, mode="plain", out_dtype=F32, gain=None, flag=None, n_norm_tiles=0, residual=None,
            tm=1024, tn=512, tk=2048, name="matmul"):
    a_parts = list(a) if isinstance(a, (list, tuple)) else [a]
    m = a_parts[0].shape[0]
    kdim, n = b.shape
    tm = tm if m % tm == 0 else min(512, m)
    if len(a_parts) > 1:
        tk = kdim
        in_specs = [pl.BlockSpec((tm, part.shape[1]), lambda i, j, k: (i, 0)) for part in a_parts]
    else:
        in_specs = [pl.BlockSpec((tm, tk), lambda i, j, k: (i, k))]
    nk = kdim // tk
    in_specs.append(pl.BlockSpec((tk, tn), lambda i, j, k: (k, j)))
    args = a_parts + [b]
    if mode == "headnorm":
        in_specs += [pl.BlockSpec((1, tn), lambda i, j, k: (0, j)), pl.BlockSpec((1, tn), lambda i, j, k: (0, j))]
        args += [gain, flag]
    elif mode == "residual":
        in_specs += [pl.BlockSpec((tm, tn), lambda i, j, k: (i, j))]
        args += [residual]
    scratch = [pltpu.VMEM((tm, tn), F32)] if nk > 1 else []
    return pl.pallas_call(
        functools.partial(_mm_kernel, n_a=len(a_parts), nk=nk, mode=mode, n_norm_tiles=n_norm_tiles),
        grid=(m // tm, n // tn, nk),
        in_specs=in_specs,
        out_specs=pl.BlockSpec((tm, tn), lambda i, j, k: (i, j)),
        out_shape=jax.ShapeDtypeStruct((m, n), out_dtype),
        scratch_shapes=scratch,
        compiler_params=_cparams("parallel", "parallel", "arbitrary"),
        name=name,
    )(*args)


def _cumsum_block(x, carry):
    n = x.shape[1]
    r = lax.broadcasted_iota(I32, (n, n), 0)
    c = lax.broadcasted_iota(I32, (n, n), 1)
    tri = jnp.where(r <= c, 1.0, 0.0).astype(BF16)
    hi = x.astype(BF16)
    r1 = x - hi.astype(F32)
    mid = r1.astype(BF16)
    lo = (r1 - mid.astype(F32)).astype(BF16)
    cs = _dot(hi, tri) + _dot(mid, tri) + _dot(lo, tri) + carry
    return cs, cs[:, n - 1:n]


def _logf_cum_kernel(*refs, p_len, t_len, blk):
    if p_len:
        raw_ref, fb_ref, past_ref, lf_ref, cum_ref = refs
    else:
        raw_ref, fb_ref, lf_ref, cum_ref = refs
    x = raw_ref[0] + fb_ref[...]
    lf = jnp.minimum(x, 0.0) - jnp.log1p(jnp.exp(-jnp.abs(x)))
    lf_ref[0] = lf
    carry = jnp.zeros((H_B, 1), F32)
    for s in range(0, p_len, blk):
        cs, carry = _cumsum_block(past_ref[0, :, s:s + blk], carry)
        cum_ref[0, :, s:s + blk] = cs
    nb = min(blk, t_len)
    for s in range(0, t_len, nb):
        cs, carry = _cumsum_block(lf[:, s:s + nb], carry)
        cum_ref[0, :, p_len + s:p_len + s + nb] = cs


def _logf_cum(raw_h, fbias, past_h):
    b, h, t = raw_h.shape
    p = 0 if past_h is None else past_h.shape[2]
    in_specs = [pl.BlockSpec((1, h, t), lambda i: (i, 0, 0)), pl.BlockSpec((h, 1), lambda i: (0, 0))]
    args = [raw_h, fbias.reshape(h, 1)]
    if p:
        in_specs.append(pl.BlockSpec((1, h, p), lambda i: (i, 0, 0)))
        args.append(past_h)
    return pl.pallas_call(
        functools.partial(_logf_cum_kernel, p_len=p, t_len=t, blk=512),
        grid=(b,),
        in_specs=in_specs,
        out_specs=[pl.BlockSpec((1, h, t), lambda i: (i, 0, 0)), pl.BlockSpec((1, h, p + t), lambda i: (i, 0, 0))],
        out_shape=[jax.ShapeDtypeStruct((b, h, t), F32), jax.ShapeDtypeStruct((b, h, p + t), F32)],
        compiler_params=_cparams("parallel"),
        name="logf_cum",
    )(*args)


def _toeplitz(row_vals, rows, cols):
    w = row_vals.shape[1]
    full = jnp.broadcast_to(row_vals[0:1, :], (rows, w))
    return pltpu.roll(full, 0, 1, stride=1, stride_axis=0)[:, :cols]


def _t5_strip_kernel(tab_ref, o_ref, *, tile):
    h = pl.program_id(0)
    width = 4 * tile
    m = lax.broadcasted_iota(I32, (8, width), 1)
    d = jnp.where(m < width // 2, m, m - width)
    rel = d - tile
    n = jnp.abs(rel)
    large = jnp.full_like(n, 8)
    for thr in (12, 16, 23, 32, 46, 64, 91):
        large = large + jnp.where(n >= thr, 1, 0)
    bucket = jnp.where(rel > 0, T5_BUCKETS // 2, 0) + jnp.where(n < 8, n, large)
    vals = jnp.zeros((8, width), F32)
    for bkt in range(T5_BUCKETS):
        vals = jnp.where(bucket == bkt, tab_ref[bkt, h], vals)
    vals = vals - tab_ref[T5_BUCKETS // 2 - 1, h]
    o_ref[0] = _toeplitz(vals, tile, 2 * tile)


def _t5_strip(t5_table, tile):
    return pl.pallas_call(
        functools.partial(_t5_strip_kernel, tile=tile),
        grid=(H_A,),
        in_specs=[pl.BlockSpec(memory_space=pltpu.SMEM)],
        out_specs=pl.BlockSpec((1, tile, 2 * tile), lambda h: (h, 0, 0)),
        out_shape=jax.ShapeDtypeStruct((H_A, tile, 2 * tile), F32),
        compiler_params=_cparams("parallel"),
        name="t5_strip",
    )(t5_table)


def _band_bias_kernel(tab_ref, o_ref, *, rows, cols, width):
    h = pl.program_id(0)
    m = lax.broadcasted_iota(I32, (8, width), 1)
    d = jnp.where(m < width // 2, m, m - width)
    idx = jnp.clip(d - D_BAND_LEFT, -D_REL_CLIP, D_REL_CLIP) + D_REL_CLIP
    vals = jnp.zeros((8, width), F32)
    for r in range(2 * D_REL_CLIP + 1):
        vals = jnp.where(idx == r, tab_ref[r, h], vals)
    vals = vals - tab_ref[0, h]
    i_chunk = lax.broadcasted_iota(I32, (rows, cols), 0) >> CHUNK_SHIFT
    c_chunk = lax.broadcasted_iota(I32, (rows, cols), 1) >> CHUNK_SHIFT
    visible = (c_chunk >= i_chunk) & (c_chunk <= i_chunk + D_LEFT_CHUNKS)
    o_ref[0] = jnp.where(visible, _toeplitz(vals, rows, cols), NEG_INF)


def _band_bias(rel_table):
    rows, cols, width = D_BAND_LEFT, 2 * D_BAND_LEFT, 4 * D_BAND_LEFT
    return pl.pallas_call(
        functools.partial(_band_bias_kernel, rows=rows, cols=cols, width=width),
        grid=(H_D,),
        in_specs=[pl.BlockSpec(memory_space=pltpu.SMEM)],
        out_specs=pl.BlockSpec((1, rows, cols), lambda h: (h, 0, 0)),
        out_shape=jax.ShapeDtypeStruct((H_D, rows, cols), F32),
        compiler_params=_cparams("parallel"),
        name="band_bias",
    )(rel_table)


def _select_kernel(iq_ref, misc_ref, kk_ref, o_ref, key_ref, *, sb, tq, lp, l_valid, q_base, n_sel, tk_out,
                   causal_prefix):
    qi = pl.program_id(1)
    rows = sb * tq

    def run(pfx):
        scores = []
        for s in range(sb):
            iq = iq_ref[s]
            iw = misc_ref[s][:, MISC_IW:MISC_IW + IDX_HEADS]
            ik = kk_ref[s, 0:pfx, 0:IDX_DIM].astype(BF16)
            score = jnp.zeros((tq, pfx), F32)
            for h in range(IDX_HEADS):
                dots = _dot_nt(iq[:, h * IDX_DIM:(h + 1) * IDX_DIM].astype(BF16), ik)
                score = score + iw[:, h:h + 1] * jnp.maximum(dots, 0.0)
            scores.append(score)
        score = jnp.concatenate(scores, axis=0) if sb > 1 else scores[0]
        row_in_tile = jnp.concatenate([lax.broadcasted_iota(I32, (tq, 1), 0)] * sb, axis=0)
        q_pos = q_base + qi * tq + row_in_tile
        k_pos = lax.broadcasted_iota(I32, (1, pfx), 1)
        adm = ((k_pos >> CHUNK_SHIFT) <= (q_pos >> CHUNK_SHIFT)) & (k_pos < l_valid)
        score = jnp.where(adm, score, NEG_INF)
        bits = lax.bitcast_convert_type(score, I32)
        key_ref[:, 0:pfx] = jnp.where(bits < 0, bits ^ jnp.int32(0x7FFFFFFF), bits)

        def keys():
            return key_ref[:, 0:pfx]

        def count(pred):
            return jnp.sum(jnp.where(pred, 1.0, 0.0), axis=-1, keepdims=True)

        kf = float(n_sel)
        t0 = jnp.where(count(keys() >= 0) >= kf, jnp.int32(0), jnp.int32(INT32_MIN))

        def thr_body(i, t):
            cand = t + (jnp.int32(1) << (30 - i))
            return jnp.where(count(keys() >= cand) >= kf, cand, t)

        thr = lax.fori_loop(0, 31, thr_body, t0)
        need = kf - count(keys() > thr)
        tied = (count(keys() == thr) != need) & (thr > NEG_INF_KEY)
        any_tied = jnp.max(jnp.where(tied, 1.0, 0.0)) > 0.5
        nbits = max(pfx - 1, 1).bit_length()

        def tie_search():
            def tie_body(i, jb):
                cand = jb + (jnp.int32(1) << (nbits - 1 - i))
                taken = count((keys() == thr) & (k_pos < cand))
                return jnp.where(taken <= need, cand, jb)

            return lax.fori_loop(0, nbits, tie_body, jnp.zeros((rows, 1), I32))

        jbound = lax.cond(any_tied, tie_search, lambda: jnp.full((rows, 1), pfx, I32))
        key = keys()
        sel = ((key > thr) | ((key == thr) & (k_pos < jbound))) & adm
        madd = jnp.where(sel, 0.0, NEG_INF)
        for s in range(sb):
            for kt in range(lp // tk_out):
                if (kt + 1) * tk_out <= pfx:
                    o_ref[s, 0, kt] = madd[s * tq:(s + 1) * tq, kt * tk_out:(kt + 1) * tk_out]
                else:
                    o_ref[s, 0, kt] = jnp.full((tq, tk_out), NEG_INF, F32)

    if causal_prefix:
        for j in range(lp // tq):
            pl.when(qi == j)(functools.partial(run, (j + 1) * tq))
    else:
        run(lp)


def _select_mask(iq_arr, iq_col, misc_arr, misc_col, kk_arr, kk_col, *, t, tq, lp, l_valid, q_base, n_sel,
                 tk_out, causal_prefix):
    b = iq_arr.shape[0]
    nq = t // tq
    wk = kk_arr.shape[2] if kk_col is None else LANE
    kcol = 0 if kk_col is None else kk_col
    sb = 4 if (nq == 1 and tq <= 64 and b % 4 == 0) else 1
    return pl.pallas_call(
        functools.partial(_select_kernel, sb=sb, tq=tq, lp=lp, l_valid=l_valid, q_base=q_base, n_sel=n_sel,
                          tk_out=tk_out, causal_prefix=causal_prefix),
        grid=(b // sb, nq),
        in_specs=[
            pl.BlockSpec((sb, tq, IDX_HEADS * IDX_DIM), lambda i, j: (i, j, iq_col)),
            pl.BlockSpec((sb, tq, LANE), lambda i, j: (i, j, misc_col)),
            pl.BlockSpec((sb, lp, wk), lambda i, j: (i, 0, kcol)),
        ],
        out_specs=pl.BlockSpec((sb, 1, lp // tk_out, tq, tk_out), lambda i, j: (i, j, 0, 0, 0)),
        out_shape=jax.ShapeDtypeStruct((b, nq, lp // tk_out, tq, tk_out), F32),
        scratch_shapes=[pltpu.VMEM((sb * tq, lp), I32)],
        compiler_params=_cparams("parallel", "parallel"),
        name="a_select",
    )(iq_arr, misc_arr, kk_arr)


def _online_update(s, v_bf16, m_s, l_s, acc_s):
    blocks = _lane_blocks(s)
    mx = blocks[0]
    for blk in blocks[1:]:
        mx = jnp.maximum(mx, blk)
    m_prev = m_s[...]
    m_new = jnp.maximum(m_prev, jnp.max(mx, axis=-1, keepdims=True))
    alpha = jnp.exp(m_prev - m_new)
    ps = [jnp.exp(blk - m_new) for blk in blocks]
    psum = ps[0]
    for p in ps[1:]:
        psum = psum + p
    l_s[...] = alpha * l_s[...] + psum
    p_all = jnp.concatenate([p.astype(BF16) for p in ps], axis=1) if len(ps) > 1 else ps[0].astype(BF16)
    acc_s[...] = alpha * acc_s[...] + _dot(p_all, v_bf16)
    m_s[...] = m_new


def _online_init(m_s, l_s, acc_s):
    m_s[...] = jnp.full_like(m_s, NEG_INF)
    l_s[...] = jnp.zeros_like(l_s)
    acc_s[...] = jnp.zeros_like(acc_s)


def _online_result(l_s, acc_s):
    return acc_s[...] / jnp.sum(l_s[...], axis=-1, keepdims=True)


def _stack_heads(q):
    return jnp.concatenate([q[:, r * HEAD_DIM:(r + 1) * HEAD_DIM] for r in range(REP_A)], axis=0).astype(BF16)


def _attn_a_prompt_kernel(q_ref, k_ref, v_ref, mask_ref, tz_ref, o_ref, m_s, l_s, acc_s, *, tq):
    qi = pl.program_id(2)
    qs = _stack_heads(q_ref[0] * SCALE)
    _online_init(m_s, l_s, acc_s)

    def tile(kt, bias):
        off = pl.multiple_of(kt * tq, tq)
        k = k_ref[0, pl.ds(off, tq), :].astype(BF16)
        v = v_ref[0, pl.ds(off, tq), :].astype(BF16)
        s = _dot_nt(qs, k).reshape(REP_A, tq, tq) + mask_ref[0, 0, kt][None]
        if bias is not None:
            s = s + bias
        _online_update(s.reshape(REP_A * tq, tq), v, m_s, l_s, acc_s)

    def plain_body(kt, carry):
        tile(kt, None)
        return carry

    lax.fori_loop(0, jnp.maximum(qi - 1, 0), plain_body, 0)

    @pl.when(qi >= 1)
    def _():
        tile(qi - 1, tz_ref[:, :, 0:tq])

    tile(qi, tz_ref[:, :, tq:2 * tq])
    out = _online_result(l_s, acc_s)
    for r in range(REP_A):
        o_ref[0, :, r * HEAD_DIM:(r + 1) * HEAD_DIM] = out[r * tq:(r + 1) * tq].astype(o_ref.dtype)


def _attn_a_prompt(proj3, mask, tz, *, tq):
    b, t, _ = proj3.shape
    nq = t // tq
    qc, kc, vc = EV_AQ // (REP_A * HEAD_DIM), EV_AK // HEAD_DIM, EV_AV // HEAD_DIM
    rows = REP_A * tq
    return pl.pallas_call(
        functools.partial(_attn_a_prompt_kernel, tq=tq),
        grid=(b, KV_A, nq),
        in_specs=[
            pl.BlockSpec((1, tq, REP_A * HEAD_DIM), lambda i, g, j: (i, j, qc + g)),
            pl.BlockSpec((1, t, HEAD_DIM), lambda i, g, j: (i, 0, kc + g)),
            pl.BlockSpec((1, t, HEAD_DIM), lambda i, g, j: (i, 0, vc + g)),
            pl.BlockSpec((1, 1, nq, tq, tq), lambda i, g, j: (i, j, 0, 0, 0)),
            pl.BlockSpec((REP_A, tq, 2 * tq), lambda i, g, j: (g, 0, 0)),
        ],
        out_specs=pl.BlockSpec((1, tq, REP_A * HEAD_DIM), lambda i, g, j: (i, j, g)),
        out_shape=jax.ShapeDtypeStruct((b, t, H_A * HEAD_DIM), BF16),
        scratch_shapes=[pltpu.VMEM((rows, LANE), F32), pltpu.VMEM((rows, LANE), F32),
                        pltpu.VMEM((rows, HEAD_DIM), F32)],
        compiler_params=_cparams("parallel", "parallel", "parallel"),
        name="attn_a_prompt",
    )(proj3, proj3, proj3, mask, tz)


def _attn_a_step_kernel(q_ref, kc_ref, vc_ref, kn_ref, vn_ref, mask_ref, tz_ref, o_ref, *, t, p):
    g = pl.program_id(1)
    qs = _stack_heads(q_ref[0] * SCALE)
    kc = kc_ref[0, pl.ds(g, p, stride=KV_A), :].astype(BF16)
    vc = vc_ref[0, pl.ds(g, p, stride=KV_A), :].astype(BF16)
    kn = _pad_keys(kn_ref[0], LANE).astype(BF16)
    vn = _pad_keys(vn_ref[0], LANE).astype(BF16)
    mask = mask_ref[0, 0, 0]
    s_c = _dot_nt(qs, kc).reshape(REP_A, t, p) + mask[None, :, 0:p]
    s_n = _dot_nt(qs, kn).reshape(REP_A, t, LANE) + mask[None, :, p:p + LANE] + tz_ref[:, :, LANE:2 * LANE]
    blocks = _lane_blocks(s_c)
    blocks[-1] = blocks[-1] + tz_ref[:, :, 0:LANE]
    blocks.append(s_n)
    mx = blocks[0]
    for blk in blocks[1:]:
        mx = jnp.maximum(mx, blk)
    m = jnp.max(mx, axis=-1, keepdims=True)
    ps = [jnp.exp(blk - m) for blk in blocks]
    psum = ps[0]
    for pb in ps[1:]:
        psum = psum + pb
    l = jnp.sum(psum, axis=-1, keepdims=True)
    p_c = jnp.concatenate([pb.astype(BF16) for pb in ps[:-1]], axis=-1).reshape(REP_A * t, p)
    out = _dot(p_c, vc) + _dot(ps[-1].astype(BF16).reshape(REP_A * t, LANE), vn)
    out = out.reshape(REP_A, t, HEAD_DIM) / l
    for r in range(REP_A):
        o_ref[0, :, r * HEAD_DIM:(r + 1) * HEAD_DIM] = out[r].astype(o_ref.dtype)


def _attn_a_step(proj3, cache_k, cache_v, mask, tz):
    b, t, _ = proj3.shape
    p = cache_k.shape[1] // KV_A
    qc, kc, vc = EV_AQ // (REP_A * HEAD_DIM), EV_AK // HEAD_DIM, EV_AV // HEAD_DIM
    return pl.pallas_call(
        functools.partial(_attn_a_step_kernel, t=t, p=p),
        grid=(b, KV_A),
        in_specs=[
            pl.BlockSpec((1, t, REP_A * HEAD_DIM), lambda i, g: (i, 0, qc + g)),
            pl.BlockSpec((1, p * KV_A, HEAD_DIM), lambda i, g: (i, 0, 0)),
            pl.BlockSpec((1, p * KV_A, HEAD_DIM), lambda i, g: (i, 0, 0)),
            pl.BlockSpec((1, t, HEAD_DIM), lambda i, g: (i, 0, kc + g)),
            pl.BlockSpec((1, t, HEAD_DIM), lambda i, g: (i, 0, vc + g)),
            pl.BlockSpec((1, 1, 1, t, p + LANE), lambda i, g: (i, 0, 0, 0, 0)),
            pl.BlockSpec((REP_A, t, 2 * LANE), lambda i, g: (g, 0, 0)),
        ],
        out_specs=pl.BlockSpec((1, t, REP_A * HEAD_DIM), lambda i, g: (i, 0, g)),
        out_shape=jax.ShapeDtypeStruct((b, t, H_A * HEAD_DIM), BF16),
        compiler_params=_cparams("parallel", "parallel"),
        name="attn_a_step",
    )(proj3, cache_k, cache_v, proj3, proj3, mask, tz)


def _causal_mask(s, row0=0):
    r = row0 + lax.broadcasted_iota(I32, s.shape, 0)
    c = lax.broadcasted_iota(I32, s.shape, 1)
    return jnp.where(c <= r, s, NEG_INF)


def _attn_b_prompt_kernel(q_ref, k_ref, v_ref, ck_ref, o_ref, m_s, l_s, acc_s, *, tq):
    qi = pl.program_id(2)
    q = (q_ref[0] * SCALE).astype(BF16)
    _online_init(m_s, l_s, acc_s)

    def tile(kt, diag):
        off = pl.multiple_of(kt * tq, tq)
        k = k_ref[0, pl.ds(off, tq), :].astype(BF16)
        v = v_ref[0, pl.ds(off, tq), :].astype(BF16)
        s = _dot_nt(q, k) - ck_ref[0, 0, kt]
        if diag:
            s = _causal_mask(s)
        _online_update(s, v, m_s, l_s, acc_s)

    def body(kt, carry):
        tile(kt, False)
        return carry

    lax.fori_loop(0, qi, body, 0)
    tile(qi, True)
    o_ref[0] = _online_result(l_s, acc_s).astype(o_ref.dtype)


def _attn_b_prompt(proj3, cum_h, *, tq=512):
    b, t, _ = proj3.shape
    nq = t // tq
    ck = cum_h.reshape(b, H_B, nq, 1, tq)
    qc, kc, vc = EV_BQ // HEAD_DIM, EV_BK // HEAD_DIM, EV_BV // HEAD_DIM
    return pl.pallas_call(
        functools.partial(_attn_b_prompt_kernel, tq=tq),
        grid=(b, H_B, nq),
        in_specs=[
            pl.BlockSpec((1, tq, HEAD_DIM), lambda i, h, j: (i, j, qc + h)),
            pl.BlockSpec((1, t, HEAD_DIM), lambda i, h, j: (i, 0, kc + h)),
            pl.BlockSpec((1, t, HEAD_DIM), lambda i, h, j: (i, 0, vc + h)),
            pl.BlockSpec((1, 1, nq, 1, tq), lambda i, h, j: (i, h, 0, 0, 0)),
        ],
        out_specs=pl.BlockSpec((1, tq, HEAD_DIM), lambda i, h, j: (i, j, h)),
        out_shape=jax.ShapeDtypeStruct((b, t, H_B * HEAD_DIM), BF16),
        scratch_shapes=[pltpu.VMEM((tq, LANE), F32), pltpu.VMEM((tq, LANE), F32), pltpu.VMEM((tq, HEAD_DIM), F32)],
        compiler_params=_cparams("parallel", "parallel", "parallel"),
        name="attn_b_prompt",
    )(proj3, proj3, proj3, ck)


def _attn_b_step_kernel(q_ref, kc_ref, vc_ref, kn_ref, vn_ref, ckc_ref, ckn_ref, o_ref, m_s, l_s, acc_s,
                        *, nkc, t, tkc):
    kb = pl.program_id(1)

    @pl.when(kb == 0)
    def _():
        _online_init(m_s, l_s, acc_s)

    def q_head(h):
        return (q_ref[0, :, h * HEAD_DIM:(h + 1) * HEAD_DIM] * SCALE).astype(BF16)

    @pl.when(kb < nkc)
    def _():
        ck = ckc_ref[0, 0]
        for h in range(H_B):
            k = kc_ref[0, pl.ds(h, tkc, stride=H_B), :].astype(BF16)
            v = vc_ref[0, pl.ds(h, tkc, stride=H_B), :].astype(BF16)
            s = _dot_nt(q_head(h), k) - ck[h:h + 1, :]
            _online_update(s, v, m_s.at[h], l_s.at[h], acc_s.at[h])

    @pl.when(kb == nkc)
    def _():
        ckn = ckn_ref[0]
        for h in range(H_B):
            sl = slice(h * HEAD_DIM, (h + 1) * HEAD_DIM)
            k = _pad_keys(kn_ref[0, :, sl], LANE).astype(BF16)
            v = _pad_keys(vn_ref[0, :, sl], LANE).astype(BF16)
            s = _causal_mask(_dot_nt(q_head(h), k) - ckn[h:h + 1, :])
            _online_update(s, v, m_s.at[h], l_s.at[h], acc_s.at[h])
            o_ref[0, :, sl] = _online_result(l_s.at[h], acc_s.at[h]).astype(o_ref.dtype)


def _attn_b_step(proj3, cache_k, cache_v, cum_h, *, tkc=1024):
    b, t, _ = proj3.shape
    p = cache_k.shape[1] // H_B
    nkc = p // tkc
    w = H_B * HEAD_DIM
    ck_cache = cum_h[:, :, :p].reshape(b, H_B, nkc, tkc).transpose(0, 2, 1, 3)
    ck_new = jnp.pad(cum_h[:, :, p:], ((0, 0), (0, 0), (0, LANE - t)))
    last = nkc - 1
    return pl.pallas_call(
        functools.partial(_attn_b_step_kernel, nkc=nkc, t=t, tkc=tkc),
        grid=(b, nkc + 1),
        in_specs=[
            pl.BlockSpec((1, t, w), lambda i, j: (i, 0, EV_BQ // w)),
            pl.BlockSpec((1, tkc * H_B, HEAD_DIM), lambda i, j: (i, jnp.minimum(j, last), 0)),
            pl.BlockSpec((1, tkc * H_B, HEAD_DIM), lambda i, j: (i, jnp.minimum(j, last), 0)),
            pl.BlockSpec((1, t, w), lambda i, j: (i, 0, EV_BK // w)),
            pl.BlockSpec((1, t, w), lambda i, j: (i, 0, EV_BV // w)),
            pl.BlockSpec((1, 1, H_B, tkc), lambda i, j: (i, jnp.minimum(j, last), 0, 0)),
            pl.BlockSpec((1, H_B, LANE), lambda i, j: (i, 0, 0)),
        ],
        out_specs=pl.BlockSpec((1, t, w), lambda i, j: (i, 0, 0)),
        out_shape=jax.ShapeDtypeStruct((b, t, w), BF16),
        scratch_shapes=[pltpu.VMEM((H_B, t, LANE), F32), pltpu.VMEM((H_B, t, LANE), F32),
                        pltpu.VMEM((H_B, t, HEAD_DIM), F32)],
        compiler_params=_cparams("parallel", "arbitrary"),
        name="attn_b_step",
    )(proj3, cache_k, cache_v, proj3, proj3, ck_cache, ck_new)


def _tri_ones():
    r = lax.broadcasted_iota(I32, (LANE, 2 * LANE), 0)
    c = lax.broadcasted_iota(I32, (LANE, 2 * LANE), 1)
    return jnp.where((r > c) | (c >= LANE), 1.0, 0.0).astype(BF16)


def _stick_tile(q_bf16, k_bf16, v_bf16, run, tri_ones, row_minus_col):
    rows = q_bf16.shape[0]
    z_blocks = _lane_blocks(_dot_nt(q_bf16, k_bf16))
    rc_blocks = None if row_minus_col is None else _lane_blocks(row_minus_col)
    ws = [None] * len(z_blocks)
    for j in reversed(range(len(z_blocks))):
        z = z_blocks[j]
        tail = jnp.log(1.0 + jnp.exp(-jnp.abs(z)))
        log_beta = jnp.minimum(z, 0.0) - tail
        log_keep = -jnp.maximum(z, 0.0) - tail
        if rc_blocks is not None:
            strict = rc_blocks[j] > 0
            log_keep = jnp.where(strict, log_keep, 0.0)
        hi = log_keep.astype(BF16)
        lo = (log_keep - hi.astype(F32)).astype(BF16)
        both = _dot(jnp.concatenate([hi, lo], axis=0), tri_ones)
        both = both[:rows] + both[rows:]
        w = jnp.exp(log_beta + both[:, :LANE] + run)
        if rc_blocks is not None:
            w = jnp.where(strict, w, 0.0)
        ws[j] = w.astype(BF16)
        run = run + both[:, LANE:]
    w_all = jnp.concatenate(ws, axis=1) if len(ws) > 1 else ws[0]
    return _dot(w_all, v_bf16), run


def _attn_c_prompt_kernel(q_ref, k_ref, v_ref, o_ref, run_s, acc_s, *, tq):
    qi = pl.program_id(2)
    q = (q_ref[0] * SCALE).astype(BF16)
    tri_ones = _tri_ones()

    def load(kt):
        off = pl.multiple_of(kt * tq, tq)
        return k_ref[0, pl.ds(off, tq), :].astype(BF16), v_ref[0, pl.ds(off, tq), :].astype(BF16)

    rc = lax.broadcasted_iota(I32, (tq, tq), 0) - lax.broadcasted_iota(I32, (tq, tq), 1)
    k, v = load(qi)
    out, run = _stick_tile(q, k, v, jnp.zeros((tq, LANE), F32), tri_ones, rc)
    acc_s[...] = out
    run_s[...] = run

    def body(i, carry):
        k, v = load(qi - 1 - i)
        out, run = _stick_tile(q, k, v, run_s[...], tri_ones, None)
        acc_s[...] += out
        run_s[...] = run
        return carry

    lax.fori_loop(0, qi, body, 0)
    o_ref[0] = acc_s[...].astype(o_ref.dtype)


def _attn_c_prompt(proj3, *, tq=512):
    b, t, _ = proj3.shape
    nq = t // tq
    qc, kc, vc = OD_CQ // HEAD_DIM, OD_CK // HEAD_DIM, OD_CV // HEAD_DIM
    return pl.pallas_call(
        functools.partial(_attn_c_prompt_kernel, tq=tq),
        grid=(b, H_C, nq),
        in_specs=[
            pl.BlockSpec((1, tq, HEAD_DIM), lambda i, h, j: (i, j, qc + h)),
            pl.BlockSpec((1, t, HEAD_DIM), lambda i, h, j: (i, 0, kc + h)),
            pl.BlockSpec((1, t, HEAD_DIM), lambda i, h, j: (i, 0, vc + h)),
        ],
        out_specs=pl.BlockSpec((1, tq, HEAD_DIM), lambda i, h, j: (i, j, h)),
        out_shape=jax.ShapeDtypeStruct((b, t, H_C * HEAD_DIM), BF16),
        scratch_shapes=[pltpu.VMEM((tq, LANE), F32), pltpu.VMEM((tq, HEAD_DIM), F32)],
        compiler_params=_cparams("parallel", "parallel", "parallel"),
        name="attn_c_prompt",
    )(proj3, proj3, proj3)


def _attn_c_step_kernel(q_ref, kc_ref, vc_ref, kn_ref, vn_ref, o_ref, run_s, acc_s, *, nkc, t, tkc, sub):
    kb = pl.program_id(1)
    tri_ones = _tri_ones()

    def q_head(h):
        return (q_ref[0, :, h * HEAD_DIM:(h + 1) * HEAD_DIM] * SCALE).astype(BF16)

    @pl.when(kb == 0)
    def _():
        rc = lax.broadcasted_iota(I32, (t, LANE), 0) - lax.broadcasted_iota(I32, (t, LANE), 1)
        for h in range(H_C):
            sl = slice(h * HEAD_DIM, (h + 1) * HEAD_DIM)
            k = _pad_keys(kn_ref[0, :, sl], LANE).astype(BF16)
            v = _pad_keys(vn_ref[0, :, sl], LANE).astype(BF16)
            out, run = _stick_tile(q_head(h), k, v, jnp.zeros((t, LANE), F32), tri_ones, rc)
            acc_s[h] = out
            run_s[h] = run

    @pl.when(kb > 0)
    def _():
        def body(i, carry):
            row0 = (tkc - sub - i * sub) * H_C
            for h in range(H_C):
                k = kc_ref[0, pl.ds(row0 + h, sub, stride=H_C), :].astype(BF16)
                v = vc_ref[0, pl.ds(row0 + h, sub, stride=H_C), :].astype(BF16)
                out, run = _stick_tile(q_head(h), k, v, run_s[h], tri_ones, None)
                acc_s[h] += out
                run_s[h] = run
            return carry

        lax.fori_loop(0, tkc // sub, body, 0)

    @pl.when(kb == nkc)
    def _():
        for h in range(H_C):
            o_ref[0, :, h * HEAD_DIM:(h + 1) * HEAD_DIM] = acc_s[h].astype(o_ref.dtype)


def _attn_c_step(proj3, cache_k, cache_v, *, tkc=1024, sub=1024):
    b, t, _ = proj3.shape
    p = cache_k.shape[1] // H_C
    nkc = p // tkc
    w = H_C * HEAD_DIM

    def cache_idx(i, j):
        return (i, jnp.clip(nkc - j, 0, nkc - 1), 0)

    return pl.pallas_call(
        functools.partial(_attn_c_step_kernel, nkc=nkc, t=t, tkc=tkc, sub=sub),
        grid=(b, nkc + 1),
        in_specs=[
            pl.BlockSpec((1, t, w), lambda i, j: (i, 0, OD_CQ // w)),
            pl.BlockSpec((1, tkc * H_C, HEAD_DIM), cache_idx),
            pl.BlockSpec((1, tkc * H_C, HEAD_DIM), cache_idx),
            pl.BlockSpec((1, t, w), lambda i, j: (i, 0, OD_CK // w)),
            pl.BlockSpec((1, t, w), lambda i, j: (i, 0, OD_CV // w)),
        ],
        out_specs=pl.BlockSpec((1, t, w), lambda i, j: (i, 0, 0)),
        out_shape=jax.ShapeDtypeStruct((b, t, w), BF16),
        scratch_shapes=[pltpu.VMEM((H_C, t, LANE), F32), pltpu.VMEM((H_C, t, HEAD_DIM), F32)],
        compiler_params=_cparams("parallel", "arbitrary"),
        name="attn_c_step",
    )(proj3, cache_k, cache_v, proj3, proj3)


def _attn_d_kernel(q_ref, kp_ref, vp_ref, ko_ref, vo_ref, bias_ref, o_ref, *, tq, q_base):
    qi = pl.program_id(2)
    q = (q_ref[0] * SCALE).astype(BF16)
    s_o = _dot_nt(q, ko_ref[0].astype(BF16)) + bias_ref[0, :, D_BAND_LEFT:D_BAND_LEFT + tq]
    m_o = jnp.max(s_o, axis=-1, keepdims=True)

    def with_left():
        s_p = _dot_nt(q, kp_ref[0].astype(BF16)) + bias_ref[0, :, 0:D_BAND_LEFT]
        m = jnp.maximum(jnp.max(s_p, axis=-1, keepdims=True), m_o)
        p_p = jnp.exp(s_p - m)
        p_o = jnp.exp(s_o - m)
        l = jnp.sum(p_p, axis=-1, keepdims=True) + jnp.sum(p_o, axis=-1, keepdims=True)
        out = _dot(p_p.astype(BF16), vp_ref[0].astype(BF16)) + _dot(p_o.astype(BF16), vo_ref[0].astype(BF16))
        o_ref[0] = (out / l).astype(o_ref.dtype)

    def own_only():
        p_o = jnp.exp(s_o - m_o)
        l = jnp.sum(p_o, axis=-1, keepdims=True)
        o_ref[0] = (_dot(p_o.astype(BF16), vo_ref[0].astype(BF16)) / l).astype(o_ref.dtype)

    if q_base >= D_BAND_LEFT:
        with_left()
    else:
        has_left = q_base + qi * tq >= D_BAND_LEFT
        pl.when(has_left)(with_left)
        pl.when(jnp.logical_not(has_left))(own_only)


def _attn_d(q_arr, q_col, kprev, kprev_col, vprev, vprev_col, kown, kown_col, vown, vown_col, bias,
            *, t, tq, q_base, prev_follows_own):
    b = q_arr.shape[0]
    nq = t // tq
    if prev_follows_own:
        def prev_idx(col):
            return lambda i, h, j: (i, jnp.maximum(j - 1, 0), col + h)
    else:
        def prev_idx(col):
            return lambda i, h, j: (i, 0, col + h)
    return pl.pallas_call(
        functools.partial(_attn_d_kernel, tq=tq, q_base=q_base),
        grid=(b, H_D, nq),
        in_specs=[
            pl.BlockSpec((1, tq, HEAD_DIM), lambda i, h, j: (i, j, q_col + h)),
            pl.BlockSpec((1, D_BAND_LEFT, HEAD_DIM), prev_idx(kprev_col)),
            pl.BlockSpec((1, D_BAND_LEFT, HEAD_DIM), prev_idx(vprev_col)),
            pl.BlockSpec((1, tq, HEAD_DIM), lambda i, h, j: (i, j, kown_col + h)),
            pl.BlockSpec((1, tq, HEAD_DIM), lambda i, h, j: (i, j, vown_col + h)),
            pl.BlockSpec((1, tq, D_BAND_LEFT + tq), lambda i, h, j: (h, 0, 0)),
        ],
        out_specs=pl.BlockSpec((1, tq, HEAD_DIM), lambda i, h, j: (i, j, h)),
        out_shape=jax.ShapeDtypeStruct((b, t, H_D * HEAD_DIM), BF16),
        compiler_params=_cparams("parallel", "parallel", "parallel"),
        name="attn_d",
    )(q_arr, kprev, vprev, kown, vown, bias)


def _attn_d_step_kernel(q_ref, kp_ref, vp_ref, kn_ref, vn_ref, bias_ref, o_ref, *, t):
    for h in range(H_D):
        sl = slice(h * HEAD_DIM, (h + 1) * HEAD_DIM)
        q = (q_ref[0, :, sl] * SCALE).astype(BF16)
        kp = kp_ref[0, pl.ds(h, D_BAND_LEFT, stride=H_D), :].astype(BF16)
        vp = vp_ref[0, pl.ds(h, D_BAND_LEFT, stride=H_D), :].astype(BF16)
        s_p = _dot_nt(q, kp) + bias_ref[h, :, 0:D_BAND_LEFT]
        s_o = _dot_nt(q, kn_ref[0, :, sl].astype(BF16)) + bias_ref[h, :, D_BAND_LEFT:D_BAND_LEFT + t]
        m = jnp.maximum(jnp.max(s_p, axis=-1, keepdims=True), jnp.max(s_o, axis=-1, keepdims=True))
        p_p = jnp.exp(s_p - m)
        p_o = jnp.exp(s_o - m)
        l = jnp.sum(p_p, axis=-1, keepdims=True) + jnp.sum(p_o, axis=-1, keepdims=True)
        out = _dot(p_p.astype(BF16), vp) + _dot(p_o.astype(BF16), vn_ref[0, :, sl].astype(BF16))
        o_ref[0, :, sl] = (out / l).astype(o_ref.dtype)


def _attn_d_step(proj3, cache_k, cache_v, bias):
    b, t, _ = proj3.shape
    w = H_D * HEAD_DIM
    return pl.pallas_call(
        functools.partial(_attn_d_step_kernel, t=t),
        grid=(b,),
        in_specs=[
            pl.BlockSpec((1, t, w), lambda i: (i, 0, OD_DQ // w)),
            pl.BlockSpec((1, D_BAND_LEFT * H_D, HEAD_DIM), lambda i: (i, 0, 0)),
            pl.BlockSpec((1, D_BAND_LEFT * H_D, HEAD_DIM), lambda i: (i, 0, 0)),
            pl.BlockSpec((1, t, w), lambda i: (i, 0, OD_DK // w)),
            pl.BlockSpec((1, t, w), lambda i: (i, 0, OD_DV // w)),
            pl.BlockSpec((H_D, t, D_BAND_LEFT + t), lambda i: (0, 0, 0)),
        ],
        out_specs=pl.BlockSpec((1, t, w), lambda i: (i, 0, 0)),
        out_shape=jax.ShapeDtypeStruct((b, t, w), BF16),
        compiler_params=_cparams("parallel"),
        name="attn_d_step",
    )(proj3, cache_k, cache_v, proj3, proj3, bias)


def _pad_rows(x, rows):
    return jnp.pad(x, ((0, 0), (0, rows - x.shape[1]), (0, 0)))


def _even_mixer(h2, b, t, past, w, tz_prompt, tz_step, tq_a):
    proj = _matmul(h2, w["w_in"], mode="headnorm", gain=w["gain"], flag=w["flag"], n_norm_tiles=9,
                   name="proj_even")
    proj3 = proj.reshape(b, t, EV_PAD)
    ak = proj3[:, :, EV_AK:EV_AK + KV_A * HEAD_DIM]
    av = proj3[:, :, EV_AV:EV_AV + KV_A * HEAD_DIM]
    ik = proj3[:, :, EV_MISC + MISC_IK:EV_MISC + MISC_IK + IDX_DIM]
    bk = proj3[:, :, EV_BK:EV_BK + H_B * HEAD_DIM]
    bv = proj3[:, :, EV_BV:EV_BV + H_B * HEAD_DIM]
    raw_h = proj3[:, :, EV_MISC + MISC_BF:EV_MISC + MISC_BF + H_B].transpose(0, 2, 1)
    iq_col = EV_IQ // (IDX_HEADS * IDX_DIM)
    misc_col = EV_MISC // LANE
    if past is None:
        logf_h, cum_h = _logf_cum(raw_h, w["f_bias"], None)
        mask = _select_mask(proj3, iq_col, proj3, misc_col, proj3, misc_col, t=t, tq=tq_a, lp=t, l_valid=t,
                            q_base=0, n_sel=min(TOPK_MAX, t // 4), tk_out=tq_a, causal_prefix=True)
        out_a = _attn_a_prompt(proj3, mask, tz_prompt, tq=tq_a)
        out_b = _attn_b_prompt(proj3, cum_h, tq=min(512, t))
    else:
        p_ak, p_av, p_ik, p_bk, p_bv, p_lf = past
        p = p_ak.shape[1]
        l_valid = p + t
        lp = p + LANE
        logf_h, cum_h = _logf_cum(raw_h, w["f_bias"], p_lf.transpose(0, 2, 1))
        ik_all = _pad_rows(jnp.concatenate([p_ik, ik], axis=1), lp)
        mask = _select_mask(proj3, iq_col, proj3, misc_col, ik_all, None, t=t, tq=t, lp=lp, l_valid=l_valid,
                            q_base=p, n_sel=min(TOPK_MAX, l_valid // 4), tk_out=lp, causal_prefix=False)
        out_a = _attn_a_step(proj3, p_ak.reshape(b, p * KV_A, HEAD_DIM), p_av.reshape(b, p * KV_A, HEAD_DIM),
                             mask, tz_step[:, :t])
        out_b = _attn_b_step(proj3, p_bk.reshape(b, p * H_B, HEAD_DIM), p_bv.reshape(b, p * H_B, HEAD_DIM), cum_h)
    mixed = [out_a.reshape(b * t, -1), out_b.reshape(b * t, -1)]
    state = (ak.reshape(b, t, KV_A, HEAD_DIM), av.reshape(b, t, KV_A, HEAD_DIM), ik,
             bk.reshape(b, t, H_B, HEAD_DIM), bv.reshape(b, t, H_B, HEAD_DIM), logf_h.transpose(0, 2, 1))
    return mixed, state


def _odd_mixer(h2, b, t, past, w, band):
    proj = _matmul(h2, w["w_in"], mode="headnorm", gain=w["gain"], flag=w["flag"], n_norm_tiles=4,
                   name="proj_odd")
    proj3 = proj.reshape(b, t, OD_WIDTH)
    ck = proj3[:, :, OD_CK:OD_CK + H_C * HEAD_DIM]
    cv = proj3[:, :, OD_CV:OD_CV + H_C * HEAD_DIM]
    dk = proj3[:, :, OD_DK:OD_DK + H_D * HEAD_DIM]
    dv = proj3[:, :, OD_DV:OD_DV + H_D * HEAD_DIM]
    dq_col, dk_col, dv_col = OD_DQ // HEAD_DIM, OD_DK // HEAD_DIM, OD_DV // HEAD_DIM
    if past is None:
        out_c = _attn_c_prompt(proj3, tq=min(512, t))
        out_d = _attn_d(proj3, dq_col, proj3, dk_col, proj3, dv_col, proj3, dk_col, proj3, dv_col, band,
                        t=t, tq=D_BAND_LEFT, q_base=0, prev_follows_own=True)
        keep = min(D_BAND_LEFT, t)
        d_rows = (dk[:, t - keep:], dv[:, t - keep:])
    else:
        p_ck, p_cv, p_dk, p_dv = past
        p = p_ck.shape[1]
        out_c = _attn_c_step(proj3, p_ck.reshape(b, p * H_C, HEAD_DIM), p_cv.reshape(b, p * H_C, HEAD_DIM))
        wd = p_dk.shape[1]
        out_d = _attn_d_step(proj3, p_dk.reshape(b, wd * H_D, HEAD_DIM), p_dv.reshape(b, wd * H_D, HEAD_DIM),
                             band[:, :t, :D_BAND_LEFT + t])
        d_rows = (dk, dv)
    mixed = [out_c.reshape(b * t, -1), out_d.reshape(b * t, -1)]
    state = (ck.reshape(b, t, H_C, HEAD_DIM), cv.reshape(b, t, H_C, HEAD_DIM),
             d_rows[0].reshape(b, -1, H_D, HEAD_DIM), d_rows[1].reshape(b, -1, H_D, HEAD_DIM))
    return mixed, state


def _trunk(x, caches_even, caches_odd, wts):
    b, t, d = x.shape
    x2 = x.reshape(b * t, d)
    h2 = _rmsnorm(x2, wts["attn_norm"][0])
    mixed, st_even = _even_mixer(h2, b, t, caches_even, wts["even"], wts["tz_prompt"], wts["tz_step"],
                                 wts["tq_a"])
    x2 = _matmul(mixed, wts["even"]["w_out"], mode="residual", residual=x2, name="out_even")
    h2 = _rmsnorm(x2, wts["mlp_norm"][0])
    u = _matmul(h2, wts["w_up"][0], mode="relu2", out_dtype=BF16, tn=1024, name="mlp_up0")
    x2 = _matmul(u, wts["w_down"][0], mode="residual", residual=x2, tm=512, tk=u.shape[1], name="mlp_down0")
    h2 = _rmsnorm(x2, wts["attn_norm"][1])
    mixed, st_odd = _odd_mixer(h2, b, t, caches_odd, wts["odd"], wts["band"])
    x2 = _matmul(mixed, wts["odd"]["w_out"], mode="residual", residual=x2, name="out_odd")
    h2 = _rmsnorm(x2, wts["mlp_norm"][1])
    u = _matmul(h2, wts["w_up"][1], mode="relu2", out_dtype=BF16, tn=1024, name="mlp_up1")
    x2 = _matmul(u, wts["w_down"][1], mode="residual", residual=x2, tm=512, tk=u.shape[1], name="mlp_down1")
    return x2.reshape(b, t, d), tuple(s[None] for s in st_even), tuple(s[None] for s in st_odd)


def _even_weights(w_in, w_out, a_qn, a_kn, b_qn, b_kn, f_bias):
    aq, ak, av, iq, ik, iw, bq, bk, bv, bf = jnp.split(
        w_in, [1024, 1280, 1536, 2048, 2112, 2120, 3144, 4168, 5192], axis=1)
    d = w_in.shape[0]
    pad = jnp.zeros((d, EV_PAD - EV_WIDTH + LANE - (IDX_DIM + IDX_HEADS + H_B)), w_in.dtype)
    w_r = jnp.concatenate([aq, bq, bk, bv, ak, av, iq, ik, iw, bf, pad], axis=1).astype(BF16)
    ones = jnp.ones((EV_PAD,), F32)
    gain = ones.at[EV_AQ:EV_AQ + 1024].set(jnp.tile(a_qn, H_A))
    gain = gain.at[EV_BQ:EV_BQ + 1024].set(jnp.tile(b_qn, H_B))
    gain = gain.at[EV_BK:EV_BK + 1024].set(jnp.tile(b_kn, H_B))
    gain = gain.at[EV_AK:EV_AK + 256].set(jnp.tile(a_kn, KV_A))
    flag = jnp.zeros((EV_PAD,), F32).at[0:EV_BV].set(1.0).at[EV_AK:EV_AV].set(1.0)
    return {"w_in": w_r, "gain": gain.reshape(1, -1), "flag": flag.reshape(1, -1),
            "w_out": w_out.astype(BF16), "f_bias": f_bias}


def _odd_weights(w_in, w_out, d_qn, d_kn):
    cq, ck, cv, dq, dk, dv = jnp.split(w_in, 6, axis=1)
    w_r = jnp.concatenate([dq, dk, cq, ck, cv, dv], axis=1).astype(BF16)
    gain = jnp.ones((OD_WIDTH,), F32)
    gain = gain.at[OD_DQ:OD_DQ + 1024].set(jnp.tile(d_qn, H_D)).at[OD_DK:OD_DK + 1024].set(jnp.tile(d_kn, H_D))
    flag = jnp.zeros((OD_WIDTH,), F32).at[0:OD_CQ].set(1.0)
    return {"w_in": w_r, "gain": gain.reshape(1, -1), "flag": flag.reshape(1, -1), "w_out": w_out.astype(BF16)}


def kernel(x_prompt, x_sample, cache_a_k, cache_a_v, cache_a_kidx, cache_b_k, cache_b_v, cache_b_logf,
           cache_c_k, cache_c_v, cache_d_k, cache_d_v, attn_norm, mlp_norm, w_in_even, w_out_even,
           w_in_odd, w_out_odd, a_q_norm, a_k_norm, b_q_norm, b_k_norm, forget_bias, t5_bias,
           d_q_norm, d_k_norm, d_rel_bias, w_up, w_down):
    tq_a = min(256, x_prompt.shape[1])
    wts = {
        "attn_norm": attn_norm,
        "mlp_norm": mlp_norm,
        "even": _even_weights(w_in_even[0], w_out_even[0], a_q_norm[0], a_k_norm[0], b_q_norm[0], b_k_norm[0],
                              forget_bias[0]),
        "odd": _odd_weights(w_in_odd[0], w_out_odd[0], d_q_norm[0], d_k_norm[0]),
        "w_up": w_up.astype(BF16),
        "w_down": w_down.astype(BF16),
        "tq_a": tq_a,
        "tz_prompt": _t5_strip(t5_bias, tq_a),
        "tz_step": _t5_strip(t5_bias, LANE),
        "band": _band_bias(d_rel_bias[0]),
    }
    y_p, ev_p, od_p = _trunk(x_prompt, None, None, wts)
    caches_even = (cache_a_k[0], cache_a_v[0], cache_a_kidx[0], cache_b_k[0], cache_b_v[0], cache_b_logf[0])
    caches_odd = (cache_c_k[0], cache_c_v[0], cache_d_k[0], cache_d_v[0])
    y_s, ev_s, od_s = _trunk(x_sample, caches_even, caches_odd, wts)
    return (y_p, y_s) + ev_p + od_p + ev_s + od_s
```

```python
import functools

import numpy as np
import jax
import jax.numpy as jnp
from jax import lax
from jax.experimental import pallas as pl
from jax.experimental.pallas import tpu as pltpu

F32 = jnp.float32
BF16 = jnp.bfloat16
I32 = jnp.int32

HEAD_DIM = 128
CHUNK = 64
CHUNK_SHIFT = 6
H_A = 8
KV_A = 2
REP_A = H_A // KV_A
H_B = 8
H_C = 8
H_D = 8
IDX_HEADS = 8
IDX_DIM = 64
TOPK_MAX = 256
T5_BUCKETS = 32
D_LEFT_CHUNKS = 8
D_BAND_LEFT = D_LEFT_CHUNKS * CHUNK
D_REL_CLIP = 128
EPS = 1e-6
NEG_INF = -1e30
SCALE = HEAD_DIM ** -0.5
LANE = 128
INT32_MIN = -(2 ** 31)
NEG_INF_KEY = int(np.float32(NEG_INF).view(np.int32)) ^ 0x7FFFFFFF

VMEM_LIMIT_BYTES = 56 * 1024 * 1024

EV_AQ, EV_BQ, EV_BK, EV_BV, EV_AK, EV_AV, EV_IQ, EV_MISC = 0, 1024, 2048, 3072, 4096, 4352, 4608, 5120
EV_WIDTH = 5248
EV_PAD = 5632
MISC_IK, MISC_IW, MISC_BF = 0, 64, 72
OD_DQ, OD_DK, OD_CQ, OD_CK, OD_CV, OD_DV = 0, 1024, 2048, 3072, 4096, 5120
OD_WIDTH = 6144


def _cparams(*sem):
    return pltpu.CompilerParams(dimension_semantics=sem, vmem_limit_bytes=VMEM_LIMIT_BYTES)


def _dot_nt(a, b):
    return lax.dot_general(a, b, (((1,), (1,)), ((), ())), preferred_element_type=F32)


def _dot(a, b):
    return jnp.dot(a, b, preferred_element_type=F32)


def _lane_blocks(x):
    return [x[..., j * LANE:(j + 1) * LANE] for j in range(x.shape[-1] // LANE)]


def _pad_keys(x, rows):
    return jnp.concatenate([x, jnp.zeros((rows - x.shape[0], x.shape[1]), x.dtype)], axis=0)


def _rmsnorm_kernel(x_ref, g_ref, o_ref):
    x = x_ref[...]
    ms = jnp.mean(x * x, axis=-1, keepdims=True)
    o_ref[...] = (x * lax.rsqrt(ms + EPS) * g_ref[...]).astype(o_ref.dtype)


def _rmsnorm(x2, gain):
    m, d = x2.shape
    tm = min(512, m)
    return pl.pallas_call(
        _rmsnorm_kernel,
        grid=(m // tm,),
        in_specs=[pl.BlockSpec((tm, d), lambda i: (i, 0)), pl.BlockSpec((1, d), lambda i: (0, 0))],
        out_specs=pl.BlockSpec((tm, d), lambda i: (i, 0)),
        out_shape=jax.ShapeDtypeStruct((m, d), BF16),
        compiler_params=_cparams("parallel"),
        name="rmsnorm",
    )(x2, gain.reshape(1, d))


def _mm_kernel(*refs, n_a, nk, mode, n_norm_tiles):
    a_refs, b_ref = refs[:n_a], refs[n_a]
    refs = refs[n_a - 1:]
    if mode == "headnorm":
        gain_ref, flag_ref, o_ref = refs[2], refs[3], refs[4]
        rest = refs[5:]
    elif mode == "residual":
        res_ref, o_ref = refs[2], refs[3]
        rest = refs[4:]
    else:
        o_ref = refs[2]
        rest = refs[3:]

    def product():
        row, acc = 0, None
        for a_ref in a_refs:
            kp = a_ref.shape[1]
            part = _dot(a_ref[...], b_ref[row:row + kp, :])
            acc = part if acc is None else acc + part
            row += kp
        return acc

    def epilogue(acc):
        if mode == "headnorm":
            j = pl.program_id(1)

            @pl.when(j < n_norm_tiles)
            def _():
                tn = acc.shape[1]
                for g in range(tn // HEAD_DIM):
                    sl = slice(g * HEAD_DIM, (g + 1) * HEAD_DIM)
                    blk = acc[:, sl]
                    ms = jnp.mean(blk * blk, axis=-1, keepdims=True)
                    normed = blk * lax.rsqrt(ms + EPS) * gain_ref[:, sl]
                    o_ref[:, sl] = jnp.where(flag_ref[:, sl] > 0.0, normed, blk)

            @pl.when(j >= n_norm_tiles)
            def _():
                o_ref[...] = acc
        elif mode == "relu2":
            r = jnp.maximum(acc, 0.0)
            o_ref[...] = (r * r).astype(o_ref.dtype)
        elif mode == "residual":
            o_ref[...] = res_ref[...] + acc
        else:
            o_ref[...] = acc.astype(o_ref.dtype)

    if nk == 1:
        epilogue(product())
    else:
        acc_ref = rest[0]
        k = pl.program_id(2)

        @pl.when(k == 0)
        def _():
            acc_ref[...] = jnp.zeros_like(acc_ref)

        acc_ref[...] += product()

        @pl.when(k == nk - 1)
        def _():
            epilogue(acc_ref[...])


def _matmul(a, b, *, mode="plain", out_dtype=F32, gain=None, flag=None, n_norm_tiles=0, residual=None,
            tm=1024, tn=512, tk=2048, name="matmul"):
    a_parts = list(a) if isinstance(a, (list, tuple)) else [a]
    m = a_parts[0].shape[0]
    kdim, n = b.shape
    tm = tm if m % tm == 0 else min(512, m)
    if len(a_parts) > 1:
        tk = kdim
        in_specs = [pl.BlockSpec((tm, part.shape[1]), lambda i, j, k: (i, 0)) for part in a_parts]
    else:
        in_specs = [pl.BlockSpec((tm, tk), lambda i, j, k: (i, k))]
    nk = kdim // tk
    in_specs.append(pl.BlockSpec((tk, tn), lambda i, j, k: (k, j)))
    args = a_parts + [b]
    if mode == "headnorm":
        in_specs += [pl.BlockSpec((1, tn), lambda i, j, k: (0, j)), pl.BlockSpec((1, tn), lambda i, j, k: (0, j))]
        args += [gain, flag]
    elif mode == "residual":
        in_specs += [pl.BlockSpec((tm, tn), lambda i, j, k: (i, j))]
        args += [residual]
    scratch = [pltpu.VMEM((tm, tn), F32)] if nk > 1 else []
    return pl.pallas_call(
        functools.partial(_mm_kernel, n_a=len(a_parts), nk=nk, mode=mode, n_norm_tiles=n_norm_tiles),
        grid=(m // tm, n // tn, nk),
        in_specs=in_specs,
        out_specs=pl.BlockSpec((tm, tn), lambda i, j, k: (i, j)),
        out_shape=jax.ShapeDtypeStruct((m, n), out_dtype),
        scratch_shapes=scratch,
        compiler_params=_cparams("parallel", "parallel", "arbitrary"),
        name=name,
    )(*args)


def _cumsum_block(x, carry):
    n = x.shape[1]
    r = lax.broadcasted_iota(I32, (n, n), 0)
    c = lax.broadcasted_iota(I32, (n, n), 1)
    tri = jnp.where(r <= c, 1.0, 0.0).astype(BF16)
    hi = x.astype(BF16)
    r1 = x - hi.astype(F32)
    mid = r1.astype(BF16)
    lo = (r1 - mid.astype(F32)).astype(BF16)
    cs = _dot(hi, tri) + _dot(mid, tri) + _dot(lo, tri) + carry
    return cs, cs[:, n - 1:n]


def _logf_cum_kernel(*refs, p_len, t_len, blk):
    if p_len:
        raw_ref, fb_ref, past_ref, lf_ref, cum_ref = refs
    else:
        raw_ref, fb_ref, lf_ref, cum_ref = refs
    x = raw_ref[0] + fb_ref[...]
    lf = jnp.minimum(x, 0.0) - jnp.log1p(jnp.exp(-jnp.abs(x)))
    lf_ref[0] = lf
    carry = jnp.zeros((H_B, 1), F32)
    for s in range(0, p_len, blk):
        cs, carry = _cumsum_block(past_ref[0, :, s:s + blk], carry)
        cum_ref[0, :, s:s + blk] = cs
    nb = min(blk, t_len)
    for s in range(0, t_len, nb):
        cs, carry = _cumsum_block(lf[:, s:s + nb], carry)
        cum_ref[0, :, p_len + s:p_len + s + nb] = cs


def _logf_cum(raw_h, fbias, past_h):
    b, h, t = raw_h.shape
    p = 0 if past_h is None else past_h.shape[2]
    in_specs = [pl.BlockSpec((1, h, t), lambda i: (i, 0, 0)), pl.BlockSpec((h, 1), lambda i: (0, 0))]
    args = [raw_h, fbias.reshape(h, 1)]
    if p:
        in_specs.append(pl.BlockSpec((1, h, p), lambda i: (i, 0, 0)))
        args.append(past_h)
    return pl.pallas_call(
        functools.partial(_logf_cum_kernel, p_len=p, t_len=t, blk=512),
        grid=(b,),
        in_specs=in_specs,
        out_specs=[pl.BlockSpec((1, h, t), lambda i: (i, 0, 0)), pl.BlockSpec((1, h, p + t), lambda i: (i, 0, 0))],
        out_shape=[jax.ShapeDtypeStruct((b, h, t), F32), jax.ShapeDtypeStruct((b, h, p + t), F32)],
        compiler_params=_cparams("parallel"),
        name="logf_cum",
    )(*args)


def _toeplitz(row_vals, rows, cols):
    w = row_vals.shape[1]
    full = jnp.broadcast_to(row_vals[0:1, :], (rows, w))
    return pltpu.roll(full, 0, 1, stride=1, stride_axis=0)[:, :cols]


def _t5_strip_kernel(tab_ref, o_ref, *, tile):
    h = pl.program_id(0)
    width = 4 * tile
    m = lax.broadcasted_iota(I32, (8, width), 1)
    d = jnp.where(m < width // 2, m, m - width)
    rel = d - tile
    n = jnp.abs(rel)
    large = jnp.full_like(n, 8)
    for thr in (12, 16, 23, 32, 46, 64, 91):
        large = large + jnp.where(n >= thr, 1, 0)
    bucket = jnp.where(rel > 0, T5_BUCKETS // 2, 0) + jnp.where(n < 8, n, large)
    vals = jnp.zeros((8, width), F32)
    for bkt in range(T5_BUCKETS):
        vals = jnp.where(bucket == bkt, tab_ref[bkt, h], vals)
    vals = vals - tab_ref[T5_BUCKETS // 2 - 1, h]
    o_ref[0] = _toeplitz(vals, tile, 2 * tile)


def _t5_strip(t5_table, tile):
    return pl.pallas_call(
        functools.partial(_t5_strip_kernel, tile=tile),
        grid=(H_A,),
        in_specs=[pl.BlockSpec(memory_space=pltpu.SMEM)],
        out_specs=pl.BlockSpec((1, tile, 2 * tile), lambda h: (h, 0, 0)),
        out_shape=jax.ShapeDtypeStruct((H_A, tile, 2 * tile), F32),
        compiler_params=_cparams("parallel"),
        name="t5_strip",
    )(t5_table)


def _band_bias_kernel(tab_ref, o_ref, *, rows, cols, width):
    h = pl.program_id(0)
    m = lax.broadcasted_iota(I32, (8, width), 1)
    d = jnp.where(m < width // 2, m, m - width)
    idx = jnp.clip(d - D_BAND_LEFT, -D_REL_CLIP, D_REL_CLIP) + D_REL_CLIP
    vals = jnp.zeros((8, width), F32)
    for r in range(2 * D_REL_CLIP + 1):
        vals = jnp.where(idx == r, tab_ref[r, h], vals)
    vals = vals - tab_ref[0, h]
    i_chunk = lax.broadcasted_iota(I32, (rows, cols), 0) >> CHUNK_SHIFT
    c_chunk = lax.broadcasted_iota(I32, (rows, cols), 1) >> CHUNK_SHIFT
    visible = (c_chunk >= i_chunk) & (c_chunk <= i_chunk + D_LEFT_CHUNKS)
    o_ref[0] = jnp.where(visible, _toeplitz(vals, rows, cols), NEG_INF)


def _band_bias(rel_table):
    rows, cols, width = D_BAND_LEFT, 2 * D_BAND_LEFT, 4 * D_BAND_LEFT
    return pl.pallas_call(
        functools.partial(_band_bias_kernel, rows=rows, cols=cols, width=width),
        grid=(H_D,),
        in_specs=[pl.BlockSpec(memory_space=pltpu.SMEM)],
        out_specs=pl.BlockSpec((1, rows, cols), lambda h: (h, 0, 0)),
        out_shape=jax.ShapeDtypeStruct((H_D, rows, cols), F32),
        compiler_params=_cparams("parallel"),
        name="band_bias",
    )(rel_table)


def _select_kernel(iq_ref, misc_ref, kk_ref, o_ref, key_ref, *, sb, tq, lp, l_valid, q_base, n_sel, tk_out,
                   causal_prefix):
    qi = pl.program_id(1)
    rows = sb * tq

    def run(pfx):
        scores = []
        for s in range(sb):
            iq = iq_ref[s]
            iw = misc_ref[s][:, MISC_IW:MISC_IW + IDX_HEADS]
            ik = kk_ref[s, 0:pfx, 0:IDX_DIM].astype(BF16)
            score = jnp.zeros((tq, pfx), F32)
            for h in range(IDX_HEADS):
                dots = _dot_nt(iq[:, h * IDX_DIM:(h + 1) * IDX_DIM].astype(BF16), ik)
                score = score + iw[:, h:h + 1] * jnp.maximum(dots, 0.0)
            scores.append(score)
        score = jnp.concatenate(scores, axis=0) if sb > 1 else scores[0]
        row_in_tile = jnp.concatenate([lax.broadcasted_iota(I32, (tq, 1), 0)] * sb, axis=0)
        q_pos = q_base + qi * tq + row_in_tile
        k_pos = lax.broadcasted_iota(I32, (1, pfx), 1)
        adm = ((k_pos >> CHUNK_SHIFT) <= (q_pos >> CHUNK_SHIFT)) & (k_pos < l_valid)
        score = jnp.where(adm, score, NEG_INF)
        bits = lax.bitcast_convert_type(score, I32)
        key_ref[:, 0:pfx] = jnp.where(bits < 0, bits ^ jnp.int32(0x7FFFFFFF), bits)

        def keys():
            return key_ref[:, 0:pfx]

        def count(pred):
            return jnp.sum(jnp.where(pred, 1.0, 0.0), axis=-1, keepdims=True)

        kf = float(n_sel)
        t0 = jnp.where(count(keys() >= 0) >= kf, jnp.int32(0), jnp.int32(INT32_MIN))

        def thr_body(i, t):
            cand = t + (jnp.int32(1) << (30 - i))
            return jnp.where(count(keys() >= cand) >= kf, cand, t)

        thr = lax.fori_loop(0, 31, thr_body, t0)
        need = kf - count(keys() > thr)
        tied = (count(keys() == thr) != need) & (thr > NEG_INF_KEY)
        any_tied = jnp.max(jnp.where(tied, 1.0, 0.0)) > 0.5
        nbits = max(pfx - 1, 1).bit_length()

        def tie_search():
            def tie_body(i, jb):
                cand = jb + (jnp.int32(1) << (nbits - 1 - i))
                taken = count((keys() == thr) & (k_pos < cand))
                return jnp.where(taken <= need, cand, jb)

            return lax.fori_loop(0, nbits, tie_body, jnp.zeros((rows, 1), I32))

        jbound = lax.cond(any_tied, tie_search, lambda: jnp.full((rows, 1), pfx, I32))
        key = keys()
        sel = ((key > thr) | ((key == thr) & (k_pos < jbound))) & adm
        madd = jnp.where(sel, 0.0, NEG_INF)
        for s in range(sb):
            for kt in range(lp // tk_out):
                if (kt + 1) * tk_out <= pfx:
                    o_ref[s, 0, kt] = madd[s * tq:(s + 1) * tq, kt * tk_out:(kt + 1) * tk_out]
                else:
                    o_ref[s, 0, kt] = jnp.full((tq, tk_out), NEG_INF, F32)

    if causal_prefix:
        for j in range(lp // tq):
            pl.when(qi == j)(functools.partial(run, (j + 1) * tq))
    else:
        run(lp)


def _select_mask(iq_arr, iq_col, misc_arr, misc_col, kk_arr, kk_col, *, t, tq, lp, l_valid, q_base, n_sel,
                 tk_out, causal_prefix):
    b = iq_arr.shape[0]
    nq = t // tq
    wk = kk_arr.shape[2] if kk_col is None else LANE
    kcol = 0 if kk_col is None else kk_col
    sb = 4 if (nq == 1 and tq <= 64 and b % 4 == 0) else 1
    return pl.pallas_call(
        functools.partial(_select_kernel, sb=sb, tq=tq, lp=lp, l_valid=l_valid, q_base=q_base, n_sel=n_sel,
                          tk_out=tk_out, causal_prefix=causal_prefix),
        grid=(b // sb, nq),
        in_specs=[
            pl.BlockSpec((sb, tq, IDX_HEADS * IDX_DIM), lambda i, j: (i, j, iq_col)),
            pl.BlockSpec((sb, tq, LANE), lambda i, j: (i, j, misc_col)),
            pl.BlockSpec((sb, lp, wk), lambda i, j: (i, 0, kcol)),
        ],
        out_specs=pl.BlockSpec((sb, 1, lp // tk_out, tq, tk_out), lambda i, j: (i, j, 0, 0, 0)),
        out_shape=jax.ShapeDtypeStruct((b, nq, lp // tk_out, tq, tk_out), F32),
        scratch_shapes=[pltpu.VMEM((sb * tq, lp), I32)],
        compiler_params=_cparams("parallel", "parallel"),
        name="a_select",
    )(iq_arr, misc_arr, kk_arr)


def _online_stats(s, m_s, l_s):
    blocks = _lane_blocks(s)
    mx = blocks[0]
    for blk in blocks[1:]:
        mx = jnp.maximum(mx, blk)
    m_prev = m_s[...]
    m_new = jnp.maximum(m_prev, jnp.max(mx, axis=-1, keepdims=True))
    alpha = jnp.exp(m_prev - m_new)
    ps = [jnp.exp(blk - m_new) for blk in blocks]
    psum = ps[0]
    for p in ps[1:]:
        psum = psum + p
    l_s[...] = alpha * l_s[...] + psum
    m_s[...] = m_new
    p_all = jnp.concatenate([p.astype(BF16) for p in ps], axis=1) if len(ps) > 1 else ps[0].astype(BF16)
    return alpha, p_all


def _online_update(s, v_bf16, m_s, l_s, acc_s):
    alpha, p_all = _online_stats(s, m_s, l_s)
    acc_s[...] = alpha * acc_s[...] + _dot(p_all, v_bf16)


def _online_init(m_s, l_s, acc_s):
    m_s[...] = jnp.full_like(m_s, NEG_INF)
    l_s[...] = jnp.zeros_like(l_s)
    acc_s[...] = jnp.zeros_like(acc_s)


def _online_result(l_s, acc_s):
    return acc_s[...] / jnp.sum(l_s[...], axis=-1, keepdims=True)


def _stack_heads(q):
    return jnp.concatenate([q[:, r * HEAD_DIM:(r + 1) * HEAD_DIM] for r in range(REP_A)], axis=0).astype(BF16)


def _attn_a_prompt_kernel(q_ref, k_ref, v_ref, mask_ref, tz_ref, o_ref, m_s, l_s, acc_s, *, tq):
    qi = pl.program_id(2)
    qs = _stack_heads(q_ref[0] * SCALE)
    _online_init(m_s, l_s, acc_s)

    def tile(kt, bias):
        off = pl.multiple_of(kt * tq, tq)
        k = k_ref[0, pl.ds(off, tq), :].astype(BF16)
        v = v_ref[0, pl.ds(off, tq), :].astype(BF16)
        s = _dot_nt(qs, k).reshape(REP_A, tq, tq) + mask_ref[0, 0, kt][None]
        if bias is not None:
            s = s + bias
        _online_update(s.reshape(REP_A * tq, tq), v, m_s, l_s, acc_s)

    def plain_body(kt, carry):
        tile(kt, None)
        return carry

    lax.fori_loop(0, jnp.maximum(qi - 1, 0), plain_body, 0)

    @pl.when(qi >= 1)
    def _():
        tile(qi - 1, tz_ref[:, :, 0:tq])

    tile(qi, tz_ref[:, :, tq:2 * tq])
    out = _online_result(l_s, acc_s)
    for r in range(REP_A):
        o_ref[0, :, r * HEAD_DIM:(r + 1) * HEAD_DIM] = out[r * tq:(r + 1) * tq].astype(o_ref.dtype)


def _attn_a_prompt(proj3, mask, tz, *, tq):
    b, t, _ = proj3.shape
    nq = t // tq
    qc, kc, vc = EV_AQ // (REP_A * HEAD_DIM), EV_AK // HEAD_DIM, EV_AV // HEAD_DIM
    rows = REP_A * tq
    return pl.pallas_call(
        functools.partial(_attn_a_prompt_kernel, tq=tq),
        grid=(b, KV_A, nq),
        in_specs=[
            pl.BlockSpec((1, tq, REP_A * HEAD_DIM), lambda i, g, j: (i, j, qc + g)),
            pl.BlockSpec((1, t, HEAD_DIM), lambda i, g, j: (i, 0, kc + g)),
            pl.BlockSpec((1, t, HEAD_DIM), lambda i, g, j: (i, 0, vc + g)),
            pl.BlockSpec((1, 1, nq, tq, tq), lambda i, g, j: (i, j, 0, 0, 0)),
            pl.BlockSpec((REP_A, tq, 2 * tq), lambda i, g, j: (g, 0, 0)),
        ],
        out_specs=pl.BlockSpec((1, tq, REP_A * HEAD_DIM), lambda i, g, j: (i, j, g)),
        out_shape=jax.ShapeDtypeStruct((b, t, H_A * HEAD_DIM), BF16),
        scratch_shapes=[pltpu.VMEM((rows, LANE), F32), pltpu.VMEM((rows, LANE), F32),
                        pltpu.VMEM((rows, HEAD_DIM), F32)],
        compiler_params=_cparams("parallel", "parallel", "parallel"),
        name="attn_a_prompt",
    )(proj3, proj3, proj3, mask, tz)


def _attn_a_step_kernel(q_ref, kc_ref, vc_ref, kn_ref, vn_ref, mask_ref, tz_ref, o_ref, *, t, p):
    g = pl.program_id(1)
    qs = _stack_heads(q_ref[0] * SCALE)
    kc = kc_ref[0, pl.ds(g, p, stride=KV_A), :].astype(BF16)
    vc = vc_ref[0, pl.ds(g, p, stride=KV_A), :].astype(BF16)
    kn = _pad_keys(kn_ref[0], LANE).astype(BF16)
    vn = _pad_keys(vn_ref[0], LANE).astype(BF16)
    mask = mask_ref[0, 0, 0]
    s_c = _dot_nt(qs, kc).reshape(REP_A, t, p) + mask[None, :, 0:p]
    s_n = _dot_nt(qs, kn).reshape(REP_A, t, LANE) + mask[None, :, p:p + LANE] + tz_ref[:, :, LANE:2 * LANE]
    blocks = _lane_blocks(s_c)
    blocks[-1] = blocks[-1] + tz_ref[:, :, 0:LANE]
    blocks.append(s_n)
    mx = blocks[0]
    for blk in blocks[1:]:
        mx = jnp.maximum(mx, blk)
    m = jnp.max(mx, axis=-1, keepdims=True)
    ps = [jnp.exp(blk - m) for blk in blocks]
    psum = ps[0]
    for pb in ps[1:]:
        psum = psum + pb
    l = jnp.sum(psum, axis=-1, keepdims=True)
    p_c = jnp.concatenate([pb.astype(BF16) for pb in ps[:-1]], axis=-1).reshape(REP_A * t, p)
    out = _dot(p_c, vc) + _dot(ps[-1].astype(BF16).reshape(REP_A * t, LANE), vn)
    out = out.reshape(REP_A, t, HEAD_DIM) / l
    for r in range(REP_A):
        o_ref[0, :, r * HEAD_DIM:(r + 1) * HEAD_DIM] = out[r].astype(o_ref.dtype)


def _attn_a_step(proj3, cache_k, cache_v, mask, tz):
    b, t, _ = proj3.shape
    p = cache_k.shape[1] // KV_A
    qc, kc, vc = EV_AQ // (REP_A * HEAD_DIM), EV_AK // HEAD_DIM, EV_AV // HEAD_DIM
    return pl.pallas_call(
        functools.partial(_attn_a_step_kernel, t=t, p=p),
        grid=(b, KV_A),
        in_specs=[
            pl.BlockSpec((1, t, REP_A * HEAD_DIM), lambda i, g: (i, 0, qc + g)),
            pl.BlockSpec((1, p * KV_A, HEAD_DIM), lambda i, g: (i, 0, 0)),
            pl.BlockSpec((1, p * KV_A, HEAD_DIM), lambda i, g: (i, 0, 0)),
            pl.BlockSpec((1, t, HEAD_DIM), lambda i, g: (i, 0, kc + g)),
            pl.BlockSpec((1, t, HEAD_DIM), lambda i, g: (i, 0, vc + g)),
            pl.BlockSpec((1, 1, 1, t, p + LANE), lambda i, g: (i, 0, 0, 0, 0)),
            pl.BlockSpec((REP_A, t, 2 * LANE), lambda i, g: (g, 0, 0)),
        ],
        out_specs=pl.BlockSpec((1, t, REP_A * HEAD_DIM), lambda i, g: (i, 0, g)),
        out_shape=jax.ShapeDtypeStruct((b, t, H_A * HEAD_DIM), BF16),
        compiler_params=_cparams("parallel", "parallel"),
        name="attn_a_step",
    )(proj3, cache_k, cache_v, proj3, proj3, mask, tz)


def _causal_mask(s, row0=0):
    r = row0 + lax.broadcasted_iota(I32, s.shape, 0)
    c = lax.broadcasted_iota(I32, s.shape, 1)
    return jnp.where(c <= r, s, NEG_INF)


def _attn_b_prompt_kernel(q_ref, k_ref, v_ref, ck_ref, o_ref, m_s, l_s, acc_s, s_buf, *, tq):
    qi = pl.program_id(2)
    q = (q_ref[0] * SCALE).astype(BF16)
    _online_init(m_s, l_s, acc_s)

    def logits(kt, diag):
        off = pl.multiple_of(kt * tq, tq)
        s = _dot_nt(q, k_ref[0, pl.ds(off, tq), :].astype(BF16)) - ck_ref[0, 0, kt]
        return _causal_mask(s) if diag else s

    def consume(s, kt):
        off = pl.multiple_of(kt * tq, tq)
        _online_update(s, v_ref[0, pl.ds(off, tq), :].astype(BF16), m_s, l_s, acc_s)

    @pl.when(qi == 0)
    def _():
        consume(logits(0, True), 0)

    @pl.when(qi > 0)
    def _():
        s_buf[0] = logits(0, False)

        def body(t, carry):
            nxt = logits(t + 1, False)
            consume(s_buf[t % 2], t)
            s_buf[(t + 1) % 2] = nxt
            return carry

        lax.fori_loop(0, qi - 1, body, 0)
        last = logits(qi, True)
        consume(s_buf[(qi - 1) % 2], qi - 1)
        consume(last, qi)

    o_ref[0] = _online_result(l_s, acc_s).astype(o_ref.dtype)


def _attn_b_prompt(proj3, cum_h, *, tq=512):
    b, t, _ = proj3.shape
    nq = t // tq
    ck = cum_h.reshape(b, H_B, nq, 1, tq)
    qc, kc, vc = EV_BQ // HEAD_DIM, EV_BK // HEAD_DIM, EV_BV // HEAD_DIM
    return pl.pallas_call(
        functools.partial(_attn_b_prompt_kernel, tq=tq),
        grid=(b, H_B, nq),
        in_specs=[
            pl.BlockSpec((1, tq, HEAD_DIM), lambda i, h, j: (i, j, qc + h)),
            pl.BlockSpec((1, t, HEAD_DIM), lambda i, h, j: (i, 0, kc + h)),
            pl.BlockSpec((1, t, HEAD_DIM), lambda i, h, j: (i, 0, vc + h)),
            pl.BlockSpec((1, 1, nq, 1, tq), lambda i, h, j: (i, h, 0, 0, 0)),
        ],
        out_specs=pl.BlockSpec((1, tq, HEAD_DIM), lambda i, h, j: (i, j, h)),
        out_shape=jax.ShapeDtypeStruct((b, t, H_B * HEAD_DIM), BF16),
        scratch_shapes=[pltpu.VMEM((tq, LANE), F32), pltpu.VMEM((tq, LANE), F32), pltpu.VMEM((tq, HEAD_DIM), F32),
                        pltpu.VMEM((2, tq, tq), F32)],
        compiler_params=_cparams("parallel", "parallel", "parallel"),
        name="attn_b_prompt",
    )(proj3, proj3, proj3, ck)


def _attn_b_step_kernel(q_ref, kc_ref, vc_ref, kn_ref, vn_ref, ckc_ref, ckn_ref, o_ref, m_s, l_s, acc_s,
                        *, nkc, t, tkc):
    kb = pl.program_id(1)

    @pl.when(kb == 0)
    def _():
        _online_init(m_s, l_s, acc_s)

    def q_head(h):
        return (q_ref[0, :, h * HEAD_DIM:(h + 1) * HEAD_DIM] * SCALE).astype(BF16)

    @pl.when(kb < nkc)
    def _():
        ck = ckc_ref[0, 0]
        logits = [_dot_nt(q_head(h), kc_ref[0, pl.ds(h, tkc, stride=H_B), :].astype(BF16)) - ck[h:h + 1, :]
                  for h in range(H_B)]
        stats = [_online_stats(logits[h], m_s.at[h], l_s.at[h]) for h in range(H_B)]
        for h in range(H_B):
            alpha, p_all = stats[h]
            v = vc_ref[0, pl.ds(h, tkc, stride=H_B), :].astype(BF16)
            acc_s[h] = alpha * acc_s[h] + _dot(p_all, v)

    @pl.when(kb == nkc)
    def _():
        ckn = ckn_ref[0]
        heads = [slice(h * HEAD_DIM, (h + 1) * HEAD_DIM) for h in range(H_B)]
        logits = [_causal_mask(_dot_nt(q_head(h), _pad_keys(kn_ref[0, :, heads[h]], LANE).astype(BF16))
                               - ckn[h:h + 1, :]) for h in range(H_B)]
        stats = [_online_stats(logits[h], m_s.at[h], l_s.at[h]) for h in range(H_B)]
        for h in range(H_B):
            alpha, p_all = stats[h]
            v = _pad_keys(vn_ref[0, :, heads[h]], LANE).astype(BF16)
            acc_s[h] = alpha * acc_s[h] + _dot(p_all, v)
            o_ref[0, :, heads[h]] = _online_result(l_s.at[h], acc_s.at[h]).astype(o_ref.dtype)


def _attn_b_step(proj3, cache_k, cache_v, cum_h, *, tkc=1024):
    b, t, _ = proj3.shape
    p = cache_k.shape[1] // H_B
    nkc = p // tkc
    w = H_B * HEAD_DIM
    ck_cache = cum_h[:, :, :p].reshape(b, H_B, nkc, tkc).transpose(0, 2, 1, 3)
    ck_new = jnp.pad(cum_h[:, :, p:], ((0, 0), (0, 0), (0, LANE - t)))
    last = nkc - 1
    return pl.pallas_call(
        functools.partial(_attn_b_step_kernel, nkc=nkc, t=t, tkc=tkc),
        grid=(b, nkc + 1),
        in_specs=[
            pl.BlockSpec((1, t, w), lambda i, j: (i, 0, EV_BQ // w)),
            pl.BlockSpec((1, tkc * H_B, HEAD_DIM), lambda i, j: (i, jnp.minimum(j, last), 0)),
            pl.BlockSpec((1, tkc * H_B, HEAD_DIM), lambda i, j: (i, jnp.minimum(j, last), 0)),
            pl.BlockSpec((1, t, w), lambda i, j: (i, 0, EV_BK // w)),
            pl.BlockSpec((1, t, w), lambda i, j: (i, 0, EV_BV // w)),
            pl.BlockSpec((1, 1, H_B, tkc), lambda i, j: (i, jnp.minimum(j, last), 0, 0)),
            pl.BlockSpec((1, H_B, LANE), lambda i, j: (i, 0, 0)),
        ],
        out_specs=pl.BlockSpec((1, t, w), lambda i, j: (i, 0, 0)),
        out_shape=jax.ShapeDtypeStruct((b, t, w), BF16),
        scratch_shapes=[pltpu.VMEM((H_B, t, LANE), F32), pltpu.VMEM((H_B, t, LANE), F32),
                        pltpu.VMEM((H_B, t, HEAD_DIM), F32)],
        compiler_params=_cparams("parallel", "arbitrary"),
        name="attn_b_step",
    )(proj3, cache_k, cache_v, proj3, proj3, ck_cache, ck_new)


def _tri_ones():
    r = lax.broadcasted_iota(I32, (2 * LANE, 2 * LANE), 0) & (LANE - 1)
    c = lax.broadcasted_iota(I32, (2 * LANE, 2 * LANE), 1)
    return jnp.where((r > c) | (c >= LANE), 1.0, 0.0).astype(BF16)


def _stick_scores(z, row_minus_col, tri_ones):
    rc_blocks = None if row_minus_col is None else _lane_blocks(row_minus_col)
    out = []
    for j, zb in enumerate(_lane_blocks(z)):
        tail = jnp.log(1.0 + jnp.exp(-jnp.abs(zb)))
        log_beta = jnp.minimum(zb, 0.0) - tail
        log_keep = -jnp.maximum(zb, 0.0) - tail
        strict = None
        if rc_blocks is not None:
            strict = rc_blocks[j] > 0
            log_keep = jnp.where(strict, log_keep, 0.0)
        hi = log_keep.astype(BF16)
        lo = (log_keep - hi.astype(F32)).astype(BF16)
        out.append((log_beta, strict, _dot(jnp.concatenate([hi, lo], axis=1), tri_ones)))
    return out


def _stick_weights(blocks, run):
    ws = [None] * len(blocks)
    for j in reversed(range(len(blocks))):
        log_beta, strict, sums = blocks[j]
        w = jnp.exp(log_beta + sums[:, :LANE] + run)
        if strict is not None:
            w = jnp.where(strict, w, 0.0)
        ws[j] = w.astype(BF16)
        run = run + sums[:, LANE:]
    return (jnp.concatenate(ws, axis=1) if len(ws) > 1 else ws[0]), run


def _stick_tile(q_bf16, k_bf16, v_bf16, run, tri_ones, row_minus_col):
    w_all, run = _stick_weights(_stick_scores(_dot_nt(q_bf16, k_bf16), row_minus_col, tri_ones), run)
    return _dot(w_all, v_bf16), run


def _attn_c_prompt_kernel(q_ref, k_ref, v_ref, o_ref, run_s, acc_s, *, tq):
    qi = pl.program_id(2)
    q = (q_ref[0] * SCALE).astype(BF16)
    tri_ones = _tri_ones()

    def load(kt):
        off = pl.multiple_of(kt * tq, tq)
        return k_ref[0, pl.ds(off, tq), :].astype(BF16), v_ref[0, pl.ds(off, tq), :].astype(BF16)

    rc = lax.broadcasted_iota(I32, (tq, tq), 0) - lax.broadcasted_iota(I32, (tq, tq), 1)
    k, v = load(qi)
    out, run = _stick_tile(q, k, v, jnp.zeros((tq, LANE), F32), tri_ones, rc)
    acc_s[...] = out
    run_s[...] = run

    def body(i, carry):
        k, v = load(qi - 1 - i)
        out, run = _stick_tile(q, k, v, run_s[...], tri_ones, None)
        acc_s[...] += out
        run_s[...] = run
        return carry

    lax.fori_loop(0, qi, body, 0)
    o_ref[0] = acc_s[...].astype(o_ref.dtype)


def _attn_c_prompt(proj3, *, tq=512):
    b, t, _ = proj3.shape
    nq = t // tq
    qc, kc, vc = OD_CQ // HEAD_DIM, OD_CK // HEAD_DIM, OD_CV // HEAD_DIM
    return pl.pallas_call(
        functools.partial(_attn_c_prompt_kernel, tq=tq),
        grid=(b, H_C, nq),
        in_specs=[
            pl.BlockSpec((1, tq, HEAD_DIM), lambda i, h, j: (i, j, qc + h)),
            pl.BlockSpec((1, t, HEAD_DIM), lambda i, h, j: (i, 0, kc + h)),
            pl.BlockSpec((1, t, HEAD_DIM), lambda i, h, j: (i, 0, vc + h)),
        ],
        out_specs=pl.BlockSpec((1, tq, HEAD_DIM), lambda i, h, j: (i, j, h)),
        out_shape=jax.ShapeDtypeStruct((b, t, H_C * HEAD_DIM), BF16),
        scratch_shapes=[pltpu.VMEM((tq, LANE), F32), pltpu.VMEM((tq, HEAD_DIM), F32)],
        compiler_params=_cparams("parallel", "parallel", "parallel"),
        name="attn_c_prompt",
    )(proj3, proj3, proj3)


def _attn_c_step_kernel(q_ref, kc_ref, vc_ref, kn_ref, vn_ref, o_ref, run_s, acc_s, *, nkc, t, tkc, sub):
    kb = pl.program_id(1)
    tri_ones = _tri_ones()

    def q_head(h):
        return (q_ref[0, :, h * HEAD_DIM:(h + 1) * HEAD_DIM] * SCALE).astype(BF16)

    @pl.when(kb == 0)
    def _():
        rc = lax.broadcasted_iota(I32, (t, LANE), 0) - lax.broadcasted_iota(I32, (t, LANE), 1)
        for h in range(H_C):
            sl = slice(h * HEAD_DIM, (h + 1) * HEAD_DIM)
            k = _pad_keys(kn_ref[0, :, sl], LANE).astype(BF16)
            v = _pad_keys(vn_ref[0, :, sl], LANE).astype(BF16)
            out, run = _stick_tile(q_head(h), k, v, jnp.zeros((t, LANE), F32), tri_ones, rc)
            acc_s[h] = out
            run_s[h] = run

    @pl.when(kb > 0)
    def _():
        def body(i, carry):
            row0 = (tkc - sub - i * sub) * H_C
            zs = [_dot_nt(q_head(h), kc_ref[0, pl.ds(row0 + h, sub, stride=H_C), :].astype(BF16))
                  for h in range(H_C)]
            scored = [_stick_scores(z, None, tri_ones) for z in zs]
            for h in range(H_C):
                w_all, run = _stick_weights(scored[h], run_s[h])
                v = vc_ref[0, pl.ds(row0 + h, sub, stride=H_C), :].astype(BF16)
                acc_s[h] += _dot(w_all, v)
                run_s[h] = run
            return carry

        lax.fori_loop(0, tkc // sub, body, 0)

    @pl.when(kb == nkc)
    def _():
        for h in range(H_C):
            o_ref[0, :, h * HEAD_DIM:(h + 1) * HEAD_DIM] = acc_s[h].astype(o_ref.dtype)


def _attn_c_step(proj3, cache_k, cache_v, *, tkc=1024, sub=1024):
    b, t, _ = proj3.shape
    p = cache_k.shape[1] // H_C
    nkc = p // tkc
    w = H_C * HEAD_DIM

    def cache_idx(i, j):
        return (i, jnp.clip(nkc - j, 0, nkc - 1), 0)

    return pl.pallas_call(
        functools.partial(_attn_c_step_kernel, nkc=nkc, t=t, tkc=tkc, sub=sub),
        grid=(b, nkc + 1),
        in_specs=[
            pl.BlockSpec((1, t, w), lambda i, j: (i, 0, OD_CQ // w)),
            pl.BlockSpec((1, tkc * H_C, HEAD_DIM), cache_idx),
            pl.BlockSpec((1, tkc * H_C, HEAD_DIM), cache_idx),
            pl.BlockSpec((1, t, w), lambda i, j: (i, 0, OD_CK // w)),
            pl.BlockSpec((1, t, w), lambda i, j: (i, 0, OD_CV // w)),
        ],
        out_specs=pl.BlockSpec((1, t, w), lambda i, j: (i, 0, 0)),
        out_shape=jax.ShapeDtypeStruct((b, t, w), BF16),
        scratch_shapes=[pltpu.VMEM((H_C, t, LANE), F32), pltpu.VMEM((H_C, t, HEAD_DIM), F32)],
        compiler_params=_cparams("parallel", "arbitrary"),
        name="attn_c_step",
    )(proj3, cache_k, cache_v, proj3, proj3)


def _attn_d_kernel(q_ref, kp_ref, vp_ref, ko_ref, vo_ref, bias_ref, o_ref, *, tq, q_base):
    qi = pl.program_id(2)
    q = (q_ref[0] * SCALE).astype(BF16)
    s_o = _dot_nt(q, ko_ref[0].astype(BF16)) + bias_ref[0, :, D_BAND_LEFT:D_BAND_LEFT + tq]
    m_o = jnp.max(s_o, axis=-1, keepdims=True)

    def with_left():
        s_p = _dot_nt(q, kp_ref[0].astype(BF16)) + bias_ref[0, :, 0:D_BAND_LEFT]
        m = jnp.maximum(jnp.max(s_p, axis=-1, keepdims=True), m_o)
        p_p = jnp.exp(s_p - m)
        p_o = jnp.exp(s_o - m)
        l = jnp.sum(p_p, axis=-1, keepdims=True) + jnp.sum(p_o, axis=-1, keepdims=True)
        out = _dot(p_p.astype(BF16), vp_ref[0].astype(BF16)) + _dot(p_o.astype(BF16), vo_ref[0].astype(BF16))
        o_ref[0] = (out / l).astype(o_ref.dtype)

    def own_only():
        p_o = jnp.exp(s_o - m_o)
        l = jnp.sum(p_o, axis=-1, keepdims=True)
        o_ref[0] = (_dot(p_o.astype(BF16), vo_ref[0].astype(BF16)) / l).astype(o_ref.dtype)

    if q_base >= D_BAND_LEFT:
        with_left()
    else:
        has_left = q_base + qi * tq >= D_BAND_LEFT
        pl.when(has_left)(with_left)
        pl.when(jnp.logical_not(has_left))(own_only)


def _attn_d(q_arr, q_col, kprev, kprev_col, vprev, vprev_col, kown, kown_col, vown, vown_col, bias,
            *, t, tq, q_base, prev_follows_own):
    b = q_arr.shape[0]
    nq = t // tq
    if prev_follows_own:
        def prev_idx(col):
            return lambda i, h, j: (i, jnp.maximum(j - 1, 0), col + h)
    else:
        def prev_idx(col):
            return lambda i, h, j: (i, 0, col + h)
    return pl.pallas_call(
        functools.partial(_attn_d_kernel, tq=tq, q_base=q_base),
        grid=(b, H_D, nq),
        in_specs=[
            pl.BlockSpec((1, tq, HEAD_DIM), lambda i, h, j: (i, j, q_col + h)),
            pl.BlockSpec((1, D_BAND_LEFT, HEAD_DIM), prev_idx(kprev_col)),
            pl.BlockSpec((1, D_BAND_LEFT, HEAD_DIM), prev_idx(vprev_col)),
            pl.BlockSpec((1, tq, HEAD_DIM), lambda i, h, j: (i, j, kown_col + h)),
            pl.BlockSpec((1, tq, HEAD_DIM), lambda i, h, j: (i, j, vown_col + h)),
            pl.BlockSpec((1, tq, D_BAND_LEFT + tq), lambda i, h, j: (h, 0, 0)),
        ],
        out_specs=pl.BlockSpec((1, tq, HEAD_DIM), lambda i, h, j: (i, j, h)),
        out_shape=jax.ShapeDtypeStruct((b, t, H_D * HEAD_DIM), BF16),
        compiler_params=_cparams("parallel", "parallel", "parallel"),
        name="attn_d",
    )(q_arr, kprev, vprev, kown, vown, bias)


def _attn_d_step_kernel(q_ref, kp_ref, vp_ref, kn_ref, vn_ref, bias_ref, o_ref, *, t):
    for h in range(H_D):
        sl = slice(h * HEAD_DIM, (h + 1) * HEAD_DIM)
        q = (q_ref[0, :, sl] * SCALE).astype(BF16)
        kp = kp_ref[0, pl.ds(h, D_BAND_LEFT, stride=H_D), :].astype(BF16)
        vp = vp_ref[0, pl.ds(h, D_BAND_LEFT, stride=H_D), :].astype(BF16)
        s_p = _dot_nt(q, kp) + bias_ref[h, :, 0:D_BAND_LEFT]
        s_o = _dot_nt(q, kn_ref[0, :, sl].astype(BF16)) + bias_ref[h, :, D_BAND_LEFT:D_BAND_LEFT + t]
        m = jnp.maximum(jnp.max(s_p, axis=-1, keepdims=True), jnp.max(s_o, axis=-1, keepdims=True))
        p_p = jnp.exp(s_p - m)
        p_o = jnp.exp(s_o - m)
        l = jnp.sum(p_p, axis=-1, keepdims=True) + jnp.sum(p_o, axis=-1, keepdims=True)
        out = _dot(p_p.astype(BF16), vp) + _dot(p_o.astype(BF16), vn_ref[0, :, sl].astype(BF16))
        o_ref[0, :, sl] = (out / l).astype(o_ref.dtype)


def _attn_d_step(proj3, cache_k, cache_v, bias):
    b, t, _ = proj3.shape
    w = H_D * HEAD_DIM
    return pl.pallas_call(
        functools.partial(_attn_d_step_kernel, t=t),
        grid=(b,),
        in_specs=[
            pl.BlockSpec((1, t, w), lambda i: (i, 0, OD_DQ // w)),
            pl.BlockSpec((1, D_BAND_LEFT * H_D, HEAD_DIM), lambda i: (i, 0, 0)),
            pl.BlockSpec((1, D_BAND_LEFT * H_D, HEAD_DIM), lambda i: (i, 0, 0)),
            pl.BlockSpec((1, t, w), lambda i: (i, 0, OD_DK // w)),
            pl.BlockSpec((1, t, w), lambda i: (i, 0, OD_DV // w)),
            pl.BlockSpec((H_D, t, D_BAND_LEFT + t), lambda i: (0, 0, 0)),
        ],
        out_specs=pl.BlockSpec((1, t, w), lambda i: (i, 0, 0)),
        out_shape=jax.ShapeDtypeStruct((b, t, w), BF16),
        compiler_params=_cparams("parallel"),
        name="attn_d_step",
    )(proj3, cache_k, cache_v, proj3, proj3, bias)


def _pad_rows(x, rows):
    return jnp.pad(x, ((0, 0), (0, rows - x.shape[1]), (0, 0)))


def _even_mixer(h2, b, t, past, w, tz_prompt, tz_step, tq_a):
    proj = _matmul(h2, w["w_in"], mode="headnorm", gain=w["gain"], flag=w["flag"], n_norm_tiles=9,
                   name="proj_even")
    proj3 = proj.reshape(b, t, EV_PAD)
    ak = proj3[:, :, EV_AK:EV_AK + KV_A * HEAD_DIM]
    av = proj3[:, :, EV_AV:EV_AV + KV_A * HEAD_DIM]
    ik = proj3[:, :, EV_MISC + MISC_IK:EV_MISC + MISC_IK + IDX_DIM]
    bk = proj3[:, :, EV_BK:EV_BK + H_B * HEAD_DIM]
    bv = proj3[:, :, EV_BV:EV_BV + H_B * HEAD_DIM]
    raw_h = proj3[:, :, EV_MISC + MISC_BF:EV_MISC + MISC_BF + H_B].transpose(0, 2, 1)
    iq_col = EV_IQ // (IDX_HEADS * IDX_DIM)
    misc_col = EV_MISC // LANE
    if past is None:
        logf_h, cum_h = _logf_cum(raw_h, w["f_bias"], None)
        mask = _select_mask(proj3, iq_col, proj3, misc_col, proj3, misc_col, t=t, tq=tq_a, lp=t, l_valid=t,
                            q_base=0, n_sel=min(TOPK_MAX, t // 4), tk_out=tq_a, causal_prefix=True)
        out_a = _attn_a_prompt(proj3, mask, tz_prompt, tq=tq_a)
        out_b = _attn_b_prompt(proj3, cum_h, tq=min(512, t))
    else:
        p_ak, p_av, p_ik, p_bk, p_bv, p_lf = past
        p = p_ak.shape[1]
        l_valid = p + t
        lp = p + LANE
        logf_h, cum_h = _logf_cum(raw_h, w["f_bias"], p_lf.transpose(0, 2, 1))
        ik_all = _pad_rows(jnp.concatenate([p_ik, ik], axis=1), lp)
        mask = _select_mask(proj3, iq_col, proj3, misc_col, ik_all, None, t=t, tq=t, lp=lp, l_valid=l_valid,
                            q_base=p, n_sel=min(TOPK_MAX, l_valid // 4), tk_out=lp, causal_prefix=False)
        out_a = _attn_a_step(proj3, p_ak.reshape(b, p * KV_A, HEAD_DIM), p_av.reshape(b, p * KV_A, HEAD_DIM),
                             mask, tz_step[:, :t])
        out_b = _attn_b_step(proj3, p_bk.reshape(b, p * H_B, HEAD_DIM), p_bv.reshape(b, p * H_B, HEAD_DIM), cum_h)
    mixed = [out_a.reshape(b * t, -1), out_b.reshape(b * t, -1)]
    state = (ak.reshape(b, t, KV_A, HEAD_DIM), av.reshape(b, t, KV_A, HEAD_DIM), ik,
             bk.reshape(b, t, H_B, HEAD_DIM), bv.reshape(b, t, H_B, HEAD_DIM), logf_h.transpose(0, 2, 1))
    return mixed, state


def _odd_mixer(h2, b, t, past, w, band):
    proj = _matmul(h2, w["w_in"], mode="headnorm", gain=w["gain"], flag=w["flag"], n_norm_tiles=4,
                   name="proj_odd")
    proj3 = proj.reshape(b, t, OD_WIDTH)
    ck = proj3[:, :, OD_CK:OD_CK + H_C * HEAD_DIM]
    cv = proj3[:, :, OD_CV:OD_CV + H_C * HEAD_DIM]
    dk = proj3[:, :, OD_DK:OD_DK + H_D * HEAD_DIM]
    dv = proj3[:, :, OD_DV:OD_DV + H_D * HEAD_DIM]
    dq_col, dk_col, dv_col = OD_DQ // HEAD_DIM, OD_DK // HEAD_DIM, OD_DV // HEAD_DIM
    if past is None:
        out_c = _attn_c_prompt(proj3, tq=min(512, t))
        out_d = _attn_d(proj3, dq_col, proj3, dk_col, proj3, dv_col, proj3, dk_col, proj3, dv_col, band,
                        t=t, tq=D_BAND_LEFT, q_base=0, prev_follows_own=True)
        keep = min(D_BAND_LEFT, t)
        d_rows = (dk[:, t - keep:], dv[:, t - keep:])
    else:
        p_ck, p_cv, p_dk, p_dv = past
        p = p_ck.shape[1]
        out_c = _attn_c_step(proj3, p_ck.reshape(b, p * H_C, HEAD_DIM), p_cv.reshape(b, p * H_C, HEAD_DIM))
        wd = p_dk.shape[1]
        out_d = _attn_d_step(proj3, p_dk.reshape(b, wd * H_D, HEAD_DIM), p_dv.reshape(b, wd * H_D, HEAD_DIM),
                             band[:, :t, :D_BAND_LEFT + t])
        d_rows = (dk, dv)
    mixed = [out_c.reshape(b * t, -1), out_d.reshape(b * t, -1)]
    state = (ck.reshape(b, t, H_C, HEAD_DIM), cv.reshape(b, t, H_C, HEAD_DIM),
             d_rows[0].reshape(b, -1, H_D, HEAD_DIM), d_rows[1].reshape(b, -1, H_D, HEAD_DIM))
    return mixed, state


def _trunk(x, caches_even, caches_odd, wts):
    b, t, d = x.shape
    x2 = x.reshape(b * t, d)
    h2 = _rmsnorm(x2, wts["attn_norm"][0])
    mixed, st_even = _even_mixer(h2, b, t, caches_even, wts["even"], wts["tz_prompt"], wts["tz_step"],
                                 wts["tq_a"])
    x2 = _matmul(mixed, wts["even"]["w_out"], mode="residual", residual=x2, name="out_even")
    h2 = _rmsnorm(x2, wts["mlp_norm"][0])
    u = _matmul(h2, wts["w_up"][0], mode="relu2", out_dtype=BF16, tn=1024, name="mlp_up0")
    x2 = _matmul(u, wts["w_down"][0], mode="residual", residual=x2, tm=512, tk=u.shape[1], name="mlp_down0")
    h2 = _rmsnorm(x2, wts["attn_norm"][1])
    mixed, st_odd = _odd_mixer(h2, b, t, caches_odd, wts["odd"], wts["band"])
    x2 = _matmul(mixed, wts["odd"]["w_out"], mode="residual", residual=x2, name="out_odd")
    h2 = _rmsnorm(x2, wts["mlp_norm"][1])
    u = _matmul(h2, wts["w_up"][1], mode="relu2", out_dtype=BF16, tn=1024, name="mlp_up1")
    x2 = _matmul(u, wts["w_down"][1], mode="residual", residual=x2, tm=512, tk=u.shape[1], name="mlp_down1")
    return x2.reshape(b, t, d), tuple(s[None] for s in st_even), tuple(s[None] for s in st_odd)


def _even_weights(w_in, w_out, a_qn, a_kn, b_qn, b_kn, f_bias):
    aq, ak, av, iq, ik, iw, bq, bk, bv, bf = jnp.split(
        w_in, [1024, 1280, 1536, 2048, 2112, 2120, 3144, 4168, 5192], axis=1)
    d = w_in.shape[0]
    pad = jnp.zeros((d, EV_PAD - EV_WIDTH + LANE - (IDX_DIM + IDX_HEADS + H_B)), w_in.dtype)
    w_r = jnp.concatenate([aq, bq, bk, bv, ak, av, iq, ik, iw, bf, pad], axis=1).astype(BF16)
    ones = jnp.ones((EV_PAD,), F32)
    gain = ones.at[EV_AQ:EV_AQ + 1024].set(jnp.tile(a_qn, H_A))
    gain = gain.at[EV_BQ:EV_BQ + 1024].set(jnp.tile(b_qn, H_B))
    gain = gain.at[EV_BK:EV_BK + 1024].set(jnp.tile(b_kn, H_B))
    gain = gain.at[EV_AK:EV_AK + 256].set(jnp.tile(a_kn, KV_A))
    flag = jnp.zeros((EV_PAD,), F32).at[0:EV_BV].set(1.0).at[EV_AK:EV_AV].set(1.0)
    return {"w_in": w_r, "gain": gain.reshape(1, -1), "flag": flag.reshape(1, -1),
            "w_out": w_out.astype(BF16), "f_bias": f_bias}


def _odd_weights(w_in, w_out, d_qn, d_kn):
    cq, ck, cv, dq, dk, dv = jnp.split(w_in, 6, axis=1)
    w_r = jnp.concatenate([dq, dk, cq, ck, cv, dv], axis=1).astype(BF16)
    gain = jnp.ones((OD_WIDTH,), F32)
    gain = gain.at[OD_DQ:OD_DQ + 1024].set(jnp.tile(d_qn, H_D)).at[OD_DK:OD_DK + 1024].set(jnp.tile(d_kn, H_D))
    flag = jnp.zeros((OD_WIDTH,), F32).at[0:OD_CQ].set(1.0)
    return {"w_in": w_r, "gain": gain.reshape(1, -1), "flag": flag.reshape(1, -1), "w_out": w_out.astype(BF16)}


def kernel(x_prompt, x_sample, cache_a_k, cache_a_v, cache_a_kidx, cache_b_k, cache_b_v, cache_b_logf,
           cache_c_k, cache_c_v, cache_d_k, cache_d_v, attn_norm, mlp_norm, w_in_even, w_out_even,
           w_in_odd, w_out_odd, a_q_norm, a_k_norm, b_q_norm, b_k_norm, forget_bias, t5_bias,
           d_q_norm, d_k_norm, d_rel_bias, w_up, w_down):
    tq_a = min(256, x_prompt.shape[1])
    wts = {
        "attn_norm": attn_norm,
        "mlp_norm": mlp_norm,
        "even": _even_weights(w_in_even[0], w_out_even[0], a_q_norm[0], a_k_norm[0], b_q_norm[0], b_k_norm[0],
                              forget_bias[0]),
        "odd": _odd_weights(w_in_odd[0], w_out_odd[0], d_q_norm[0], d_k_norm[0]),
        "w_up": w_up.astype(BF16),
        "w_down": w_down.astype(BF16),
        "tq_a": tq_a,
        "tz_prompt": _t5_strip(t5_bias, tq_a),
        "tz_step": _t5_strip(t5_bias, LANE),
        "band": _band_bias(d_rel_bias[0]),
    }
    y_p, ev_p, od_p = _trunk(x_prompt, None, None, wts)
    caches_even = (cache_a_k[0], cache_a_v[0], cache_a_kidx[0], cache_b_k[0], cache_b_v[0], cache_b_logf[0])
    caches_odd = (cache_c_k[0], cache_c_v[0], cache_d_k[0], cache_d_v[0])
    y_s, ev_s, od_s = _trunk(x_sample, caches_even, caches_odd, wts)
    return (y_p, y_s) + ev_p + od_p + ev_s + od_s
```

```python
import functools

import numpy as np
import jax
import jax.numpy as jnp
from jax import lax
from jax.experimental import pallas as pl
from jax.experimental.pallas import tpu as pltpu

F32 = jnp.float32
BF16 = jnp.bfloat16
I32 = jnp.int32

HEAD_DIM = 128
CHUNK = 64
CHUNK_SHIFT = 6
H_A = 8
KV_A = 2
REP_A = H_A // KV_A
H_B = 8
H_C = 8
H_D = 8
IDX_HEADS = 8
IDX_DIM = 64
TOPK_MAX = 256
T5_BUCKETS = 32
D_LEFT_CHUNKS = 8
D_BAND_LEFT = D_LEFT_CHUNKS * CHUNK
D_REL_CLIP = 128
EPS = 1e-6
NEG_INF = -1e30
SCALE = HEAD_DIM ** -0.5
LANE = 128
INT32_MIN = -(2 ** 31)
NEG_INF_KEY = int(np.float32(NEG_INF).view(np.int32)) ^ 0x7FFFFFFF

VMEM_LIMIT_BYTES = 56 * 1024 * 1024

EV_AQ, EV_BQ, EV_BK, EV_BV, EV_AK, EV_AV, EV_IQ, EV_MISC = 0, 1024, 2048, 3072, 4096, 4352, 4608, 5120
EV_WIDTH = 5248
EV_PAD = 5632
MISC_IK, MISC_IW, MISC_BF = 0, 64, 72
OD_DQ, OD_DK, OD_CQ, OD_CK, OD_CV, OD_DV = 0, 1024, 2048, 3072, 4096, 5120
OD_WIDTH = 6144


def _cparams(*sem):
    return pltpu.CompilerParams(dimension_semantics=sem, vmem_limit_bytes=VMEM_LIMIT_BYTES)


def _dot_nt(a, b):
    return lax.dot_general(a, b, (((1,), (1,)), ((), ())), preferred_element_type=F32)


def _dot(a, b):
    return jnp.dot(a, b, preferred_element_type=F32)


def _lane_blocks(x):
    return [x[..., j * LANE:(j + 1) * LANE] for j in range(x.shape[-1] // LANE)]


def _pad_keys(x, rows):
    return jnp.concatenate([x, jnp.zeros((rows - x.shape[0], x.shape[1]), x.dtype)], axis=0)


def _rmsnorm_kernel(x_ref, g_ref, o_ref):
    x = x_ref[...]
    ms = jnp.mean(x * x, axis=-1, keepdims=True)
    o_ref[...] = (x * lax.rsqrt(ms + EPS) * g_ref[...]).astype(o_ref.dtype)


def _rmsnorm(x2, gain):
    m, d = x2.shape
    tm = min(512, m)
    return pl.pallas_call(
        _rmsnorm_kernel,
        grid=(m // tm,),
        in_specs=[pl.BlockSpec((tm, d), lambda i: (i, 0)), pl.BlockSpec((1, d), lambda i: (0, 0))],
        out_specs=pl.BlockSpec((tm, d), lambda i: (i, 0)),
        out_shape=jax.ShapeDtypeStruct((m, d), BF16),
        compiler_params=_cparams("parallel"),
        name="rmsnorm",
    )(x2, gain.reshape(1, d))


def _mm_kernel(*refs, n_a, nk, mode, n_norm_tiles, cast_b):
    a_refs, b_ref = refs[:n_a], refs[n_a]
    refs = refs[n_a - 1:]
    if mode == "headnorm":
        gain_ref, flag_ref, o_ref = refs[2], refs[3], refs[4]
        rest = refs[5:]
    elif mode == "residual":
        res_ref, o_ref = refs[2], refs[3]
        rest = refs[4:]
    else:
        o_ref = refs[2]
        rest = refs[3:]
    n_axis = 0 if cast_b else 1
    if cast_b:
        wb_ref = rest[0]

        @pl.when(pl.program_id(1) == 0)
        def _():
            wb_ref[...] = b_ref[...].astype(BF16)

        b_ref = wb_ref

    def product():
        row, acc = 0, None
        for a_ref in a_refs:
            kp = a_ref.shape[1]
            part = _dot(a_ref[...], b_ref[row:row + kp, :])
            acc = part if acc is None else acc + part
            row += kp
        return acc

    def epilogue(acc):
        if mode == "headnorm":
            j = pl.program_id(n_axis)

            @pl.when(j < n_norm_tiles)
            def _():
                tn = acc.shape[1]
                for g in range(tn // HEAD_DIM):
                    sl = slice(g * HEAD_DIM, (g + 1) * HEAD_DIM)
                    blk = acc[:, sl]
                    ms = jnp.mean(blk * blk, axis=-1, keepdims=True)
                    normed = blk * lax.rsqrt(ms + EPS) * gain_ref[:, sl]
                    o_ref[:, sl] = jnp.where(flag_ref[:, sl] > 0.0, normed, blk)

            @pl.when(j >= n_norm_tiles)
            def _():
                o_ref[...] = acc
        elif mode == "relu2":
            r = jnp.maximum(acc, 0.0)
            o_ref[...] = (r * r).astype(o_ref.dtype)
        elif mode == "residual":
            o_ref[...] = res_ref[...] + acc
        else:
            o_ref[...] = acc.astype(o_ref.dtype)

    if nk == 1:
        epilogue(product())
    else:
        acc_ref = rest[0]
        k = pl.program_id(2)

        @pl.when(k == 0)
        def _():
            acc_ref[...] = jnp.zeros_like(acc_ref)

        acc_ref[...] += product()

        @pl.when(k == nk - 1)
        def _():
            epilogue(acc_ref[...])


def _matmul(a, b, *, mode="plain", out_dtype=F32, gain=None, flag=None, n_norm_tiles=0, residual=None,
            tm=1024, tn=512, tk=2048, name="matmul"):
    a_parts = list(a) if isinstance(a, (list, tuple)) else [a]
    m = a_parts[0].shape[0]
    kdim, n = b.shape
    tm = tm if m % tm == 0 else min(512, m)
    if len(a_parts) > 1:
        tk = kdim
        in_specs = [pl.BlockSpec((tm, part.shape[1]), lambda i, j, k: (i, 0)) for part in a_parts]
    else:
        in_specs = [pl.BlockSpec((tm, tk), lambda i, j, k: (i, k))]
    nk = kdim // tk
    in_specs.append(pl.BlockSpec((tk, tn), lambda i, j, k: (k, j)))
    args = a_parts + [b]
    if mode == "headnorm":
        in_specs += [pl.BlockSpec((1, tn), lambda i, j, k: (0, j)), pl.BlockSpec((1, tn), lambda i, j, k: (0, j))]
        args += [gain, flag]
    elif mode == "residual":
        in_specs += [pl.BlockSpec((tm, tn), lambda i, j, k: (i, j))]
        args += [residual]
    scratch = [pltpu.VMEM((tm, tn), F32)] if nk > 1 else []
    return pl.pallas_call(
        functools.partial(_mm_kernel, n_a=len(a_parts), nk=nk, mode=mode, n_norm_tiles=n_norm_tiles,
                          cast_b=False),
        grid=(m // tm, n // tn, nk),
        in_specs=in_specs,
        out_specs=pl.BlockSpec((tm, tn), lambda i, j, k: (i, j)),
        out_shape=jax.ShapeDtypeStruct((m, n), out_dtype),
        scratch_shapes=scratch,
        compiler_params=_cparams("parallel", "parallel", "arbitrary"),
        name=name,
    )(*args)


def _matmul_cast(a, b_f32, *, col_block=None, n=None, mode="plain", out_dtype=F32, gain=None, flag=None,
                 n_norm_tiles=0, residual=None, tm=1024, tn=512, name="matmul"):
    a_parts = list(a) if isinstance(a, (list, tuple)) else [a]
    m = a_parts[0].shape[0]
    kdim = b_f32.shape[0]
    n = b_f32.shape[1] if n is None else n
    tm = tm if m % tm == 0 else min(512, m)
    src = (lambda j: j) if col_block is None else col_block
    in_specs = [pl.BlockSpec((tm, part.shape[1]), lambda j, i: (i, 0)) for part in a_parts]
    in_specs.append(pl.BlockSpec((kdim, tn), lambda j, i: (0, src(j))))
    args = a_parts + [b_f32]
    if mode == "headnorm":
        in_specs += [pl.BlockSpec((1, tn), lambda j, i: (0, j)), pl.BlockSpec((1, tn), lambda j, i: (0, j))]
        args += [gain, flag]
    elif mode == "residual":
        in_specs += [pl.BlockSpec((tm, tn), lambda j, i: (i, j))]
        args += [residual]
    return pl.pallas_call(
        functools.partial(_mm_kernel, n_a=len(a_parts), nk=1, mode=mode, n_norm_tiles=n_norm_tiles, cast_b=True),
        grid=(n // tn, m // tm),
        in_specs=in_specs,
        out_specs=[pl.BlockSpec((tm, tn), lambda j, i: (i, j)), pl.BlockSpec((kdim, tn), lambda j, i: (0, j))],
        out_shape=[jax.ShapeDtypeStruct((m, n), out_dtype), jax.ShapeDtypeStruct((kdim, n), BF16)],
        compiler_params=_cparams("parallel", "arbitrary"),
        name=name,
    )(*args)


def _cumsum_block(x, carry):
    n = x.shape[1]
    r = lax.broadcasted_iota(I32, (n, n), 0)
    c = lax.broadcasted_iota(I32, (n, n), 1)
    tri = jnp.where(r <= c, 1.0, 0.0).astype(BF16)
    hi = x.astype(BF16)
    r1 = x - hi.astype(F32)
    mid = r1.astype(BF16)
    lo = (r1 - mid.astype(F32)).astype(BF16)
    cs = _dot(hi, tri) + _dot(mid, tri) + _dot(lo, tri) + carry
    return cs, cs[:, n - 1:n]


def _logf_cum_kernel(*refs, p_len, t_len, blk):
    if p_len:
        raw_ref, fb_ref, past_ref, lf_ref, cum_ref = refs
    else:
        raw_ref, fb_ref, lf_ref, cum_ref = refs
    x = raw_ref[0] + fb_ref[...]
    lf = jnp.minimum(x, 0.0) - jnp.log1p(jnp.exp(-jnp.abs(x)))
    lf_ref[0] = lf
    carry = jnp.zeros((H_B, 1), F32)
    for s in range(0, p_len, blk):
        cs, carry = _cumsum_block(past_ref[0, :, s:s + blk], carry)
        cum_ref[0, :, s:s + blk] = cs
    nb = min(blk, t_len)
    for s in range(0, t_len, nb):
        cs, carry = _cumsum_block(lf[:, s:s + nb], carry)
        cum_ref[0, :, p_len + s:p_len + s + nb] = cs


def _logf_cum(raw_h, fbias, past_h):
    b, h, t = raw_h.shape
    p = 0 if past_h is None else past_h.shape[2]
    in_specs = [pl.BlockSpec((1, h, t), lambda i: (i, 0, 0)), pl.BlockSpec((h, 1), lambda i: (0, 0))]
    args = [raw_h, fbias.reshape(h, 1)]
    if p:
        in_specs.append(pl.BlockSpec((1, h, p), lambda i: (i, 0, 0)))
        args.append(past_h)
    return pl.pallas_call(
        functools.partial(_logf_cum_kernel, p_len=p, t_len=t, blk=512),
        grid=(b,),
        in_specs=in_specs,
        out_specs=[pl.BlockSpec((1, h, t), lambda i: (i, 0, 0)), pl.BlockSpec((1, h, p + t), lambda i: (i, 0, 0))],
        out_shape=[jax.ShapeDtypeStruct((b, h, t), F32), jax.ShapeDtypeStruct((b, h, p + t), F32)],
        compiler_params=_cparams("parallel"),
        name="logf_cum",
    )(*args)


def _toeplitz(row_vals, rows, cols):
    w = row_vals.shape[1]
    full = jnp.broadcast_to(row_vals[0:1, :], (rows, w))
    return pltpu.roll(full, 0, 1, stride=1, stride_axis=0)[:, :cols]


def _t5_strip_kernel(tab_ref, o_ref, *, tile):
    h = pl.program_id(0)
    width = 4 * tile
    m = lax.broadcasted_iota(I32, (8, width), 1)
    d = jnp.where(m < width // 2, m, m - width)
    rel = d - tile
    n = jnp.abs(rel)
    large = jnp.full_like(n, 8)
    for thr in (12, 16, 23, 32, 46, 64, 91):
        large = large + jnp.where(n >= thr, 1, 0)
    bucket = jnp.where(rel > 0, T5_BUCKETS // 2, 0) + jnp.where(n < 8, n, large)
    vals = jnp.zeros((8, width), F32)
    for bkt in range(T5_BUCKETS):
        vals = jnp.where(bucket == bkt, tab_ref[bkt, h], vals)
    vals = vals - tab_ref[T5_BUCKETS // 2 - 1, h]
    o_ref[0] = _toeplitz(vals, tile, 2 * tile)


def _t5_strip(t5_table, tile):
    return pl.pallas_call(
        functools.partial(_t5_strip_kernel, tile=tile),
        grid=(H_A,),
        in_specs=[pl.BlockSpec(memory_space=pltpu.SMEM)],
        out_specs=pl.BlockSpec((1, tile, 2 * tile), lambda h: (h, 0, 0)),
        out_shape=jax.ShapeDtypeStruct((H_A, tile, 2 * tile), F32),
        compiler_params=_cparams("parallel"),
        name="t5_strip",
    )(t5_table)


def _band_bias_kernel(tab_ref, o_ref, *, rows, cols, width):
    h = pl.program_id(0)
    m = lax.broadcasted_iota(I32, (8, width), 1)
    d = jnp.where(m < width // 2, m, m - width)
    idx = jnp.clip(d - D_BAND_LEFT, -D_REL_CLIP, D_REL_CLIP) + D_REL_CLIP
    vals = jnp.zeros((8, width), F32)
    for r in range(2 * D_REL_CLIP + 1):
        vals = jnp.where(idx == r, tab_ref[r, h], vals)
    vals = vals - tab_ref[0, h]
    i_chunk = lax.broadcasted_iota(I32, (rows, cols), 0) >> CHUNK_SHIFT
    c_chunk = lax.broadcasted_iota(I32, (rows, cols), 1) >> CHUNK_SHIFT
    visible = (c_chunk >= i_chunk) & (c_chunk <= i_chunk + D_LEFT_CHUNKS)
    o_ref[0] = jnp.where(visible, _toeplitz(vals, rows, cols), NEG_INF)


def _band_bias(rel_table):
    rows, cols, width = D_BAND_LEFT, 2 * D_BAND_LEFT, 4 * D_BAND_LEFT
    return pl.pallas_call(
        functools.partial(_band_bias_kernel, rows=rows, cols=cols, width=width),
        grid=(H_D,),
        in_specs=[pl.BlockSpec(memory_space=pltpu.SMEM)],
        out_specs=pl.BlockSpec((1, rows, cols), lambda h: (h, 0, 0)),
        out_shape=jax.ShapeDtypeStruct((H_D, rows, cols), F32),
        compiler_params=_cparams("parallel"),
        name="band_bias",
    )(rel_table)


def _select_kernel(*refs, n_g, sb, tq, nq, lp, l_valid, q_base, n_sel, tk_out, causal_prefix):
    iq_refs, misc_refs = refs[:n_g], refs[n_g:2 * n_g]
    kk_ref, o_ref, key_ref = refs[2 * n_g:]
    step = pl.program_id(1)
    rows = sb * tq
    kf = float(n_sel)
    gs = range(n_g)

    def count(pred):
        return jnp.sum(jnp.where(pred, 1.0, 0.0), axis=-1, keepdims=True)

    def run(tiles):
        pfx = [p for _, p in tiles]
        adm, k_pos = [], []
        for g, (q_tile, p) in enumerate(tiles):
            scores = []
            for s in range(sb):
                iq = iq_refs[g][s]
                iw = misc_refs[g][s][:, MISC_IW:MISC_IW + IDX_HEADS]
                ik = kk_ref[s, 0:p, 0:IDX_DIM].astype(BF16)
                score = jnp.zeros((tq, p), F32)
                for h in range(IDX_HEADS):
                    dots = _dot_nt(iq[:, h * IDX_DIM:(h + 1) * IDX_DIM].astype(BF16), ik)
                    score = score + iw[:, h:h + 1] * jnp.maximum(dots, 0.0)
                scores.append(score)
            score = jnp.concatenate(scores, axis=0) if sb > 1 else scores[0]
            row_in_tile = jnp.concatenate([lax.broadcasted_iota(I32, (tq, 1), 0)] * sb, axis=0)
            q_pos = q_base + q_tile * tq + row_in_tile
            kp = lax.broadcasted_iota(I32, (1, p), 1)
            ok = ((kp >> CHUNK_SHIFT) <= (q_pos >> CHUNK_SHIFT)) & (kp < l_valid)
            bits = lax.bitcast_convert_type(jnp.where(ok, score, NEG_INF), I32)
            key_ref[g, :, 0:p] = jnp.where(bits < 0, bits ^ jnp.int32(0x7FFFFFFF), bits)
            adm.append(ok)
            k_pos.append(kp)

        def keys(g):
            return key_ref[g, :, 0:pfx[g]]

        t0 = tuple(jnp.where(count(keys(g) >= 0) >= kf, jnp.int32(0), jnp.int32(INT32_MIN)) for g in gs)

        def thr_body(i, ts):
            bit = jnp.int32(1) << (30 - i)
            return tuple(jnp.where(count(keys(g) >= ts[g] + bit) >= kf, ts[g] + bit, ts[g]) for g in gs)

        thr = lax.fori_loop(0, 31, thr_body, t0)
        need = [kf - count(keys(g) > thr[g]) for g in gs]
        tied = [(count(keys(g) == thr[g]) != need[g]) & (thr[g] > NEG_INF_KEY) for g in gs]
        any_tied = sum(jnp.max(jnp.where(tied[g], 1.0, 0.0)) for g in gs) > 0.5
        nbits = max(max(pfx) - 1, 1).bit_length()

        def tie_search():
            def tie_body(i, jbs):
                bit = jnp.int32(1) << (nbits - 1 - i)
                return tuple(jnp.where(count((keys(g) == thr[g]) & (k_pos[g] < jbs[g] + bit)) <= need[g],
                                       jbs[g] + bit, jbs[g]) for g in gs)

            return lax.fori_loop(0, nbits, tie_body, tuple(jnp.zeros((rows, 1), I32) for _ in gs))

        jbound = lax.cond(any_tied, tie_search, lambda: tuple(jnp.full((rows, 1), pfx[g], I32) for g in gs))
        for g in gs:
            key = keys(g)
            sel = ((key > thr[g]) | ((key == thr[g]) & (k_pos[g] < jbound[g]))) & adm[g]
            madd = jnp.where(sel, 0.0, NEG_INF)
            for s in range(sb):
                for kt in range(lp // tk_out):
                    if (kt + 1) * tk_out <= pfx[g]:
                        o_ref[s, g, 0, kt] = madd[s * tq:(s + 1) * tq, kt * tk_out:(kt + 1) * tk_out]
                    else:
                        o_ref[s, g, 0, kt] = jnp.full((tq, tk_out), NEG_INF, F32)

    if not causal_prefix:
        run([(step, lp)] * n_g)
    elif n_g == 2:
        for j in range(nq // 2):
            pl.when(step == j)(functools.partial(run, [(j, (j + 1) * tq), (nq - 1 - j, (nq - j) * tq)]))
    else:
        for j in range(nq):
            pl.when(step == j)(functools.partial(run, [(j, (j + 1) * tq)]))


def _select_mask(iq_arr, iq_col, misc_arr, misc_col, kk_arr, kk_col, *, t, tq, lp, l_valid, q_base, n_sel,
                 tk_out, causal_prefix):
    b = iq_arr.shape[0]
    nq = t // tq
    wk = kk_arr.shape[2] if kk_col is None else LANE
    kcol = 0 if kk_col is None else kk_col
    sb = 4 if (nq == 1 and tq <= 64 and b % 4 == 0) else 1
    n_g = 2 if (causal_prefix and nq % 2 == 0) else 1
    steps = nq // n_g
    tile_of = [lambda j: j, lambda j: nq - 1 - j]
    nkt = lp // tk_out
    in_specs = [pl.BlockSpec((sb, tq, IDX_HEADS * IDX_DIM), lambda i, j, g=g: (i, tile_of[g](j), iq_col))
                for g in range(n_g)]
    in_specs += [pl.BlockSpec((sb, tq, LANE), lambda i, j, g=g: (i, tile_of[g](j), misc_col)) for g in range(n_g)]
    in_specs.append(pl.BlockSpec((sb, lp, wk), lambda i, j: (i, 0, kcol)))
    return pl.pallas_call(
        functools.partial(_select_kernel, n_g=n_g, sb=sb, tq=tq, nq=nq, lp=lp, l_valid=l_valid, q_base=q_base,
                          n_sel=n_sel, tk_out=tk_out, causal_prefix=causal_prefix),
        grid=(b // sb, steps),
        in_specs=in_specs,
        out_specs=pl.BlockSpec((sb, n_g, 1, nkt, tq, tk_out), lambda i, j: (i, 0, j, 0, 0, 0)),
        out_shape=jax.ShapeDtypeStruct((b, n_g, steps, nkt, tq, tk_out), F32),
        scratch_shapes=[pltpu.VMEM((n_g, sb * tq, lp), I32)],
        compiler_params=_cparams("parallel", "parallel"),
        name="a_select",
    )(*([iq_arr] * n_g), *([misc_arr] * n_g), kk_arr)


def _mask_tile_index(q_tile, nq, n_g):
    if n_g == 1:
        return 0, q_tile
    upper = q_tile >= nq // 2
    return jnp.where(upper, 1, 0), jnp.where(upper, nq - 1 - q_tile, q_tile)


def _online_stats(s, m_s, l_s):
    blocks = _lane_blocks(s)
    mx = blocks[0]
    for blk in blocks[1:]:
        mx = jnp.maximum(mx, blk)
    m_prev = m_s[...]
    m_new = jnp.maximum(m_prev, jnp.max(mx, axis=-1, keepdims=True))
    alpha = jnp.exp(m_prev - m_new)
    ps = [jnp.exp(blk - m_new) for blk in blocks]
    psum = ps[0]
    for p in ps[1:]:
        psum = psum + p
    l_s[...] = alpha * l_s[...] + psum
    m_s[...] = m_new
    p_all = jnp.concatenate([p.astype(BF16) for p in ps], axis=1) if len(ps) > 1 else ps[0].astype(BF16)
    return alpha, p_all


def _online_update(s, v_bf16, m_s, l_s, acc_s):
    alpha, p_all = _online_stats(s, m_s, l_s)
    acc_s[...] = alpha * acc_s[...] + _dot(p_all, v_bf16)


def _online_init(m_s, l_s, acc_s):
    m_s[...] = jnp.full_like(m_s, NEG_INF)
    l_s[...] = jnp.zeros_like(l_s)
    acc_s[...] = jnp.zeros_like(acc_s)


def _online_result(l_s, acc_s):
    return acc_s[...] / jnp.sum(l_s[...], axis=-1, keepdims=True)


def _stack_heads(q):
    return jnp.concatenate([q[:, r * HEAD_DIM:(r + 1) * HEAD_DIM] for r in range(REP_A)], axis=0).astype(BF16)


def _attn_a_prompt_kernel(q_ref, k_ref, v_ref, mask_ref, tz_ref, o_ref, m_s, l_s, acc_s, *, tq):
    qi = pl.program_id(2)
    qs = _stack_heads(q_ref[0] * SCALE)
    _online_init(m_s, l_s, acc_s)

    def tile(kt, bias):
        off = pl.multiple_of(kt * tq, tq)
        k = k_ref[0, pl.ds(off, tq), :].astype(BF16)
        v = v_ref[0, pl.ds(off, tq), :].astype(BF16)
        s = _dot_nt(qs, k).reshape(REP_A, tq, tq) + mask_ref[0, 0, 0, kt][None]
        if bias is not None:
            s = s + bias
        _online_update(s.reshape(REP_A * tq, tq), v, m_s, l_s, acc_s)

    def plain_body(kt, carry):
        tile(kt, None)
        return carry

    lax.fori_loop(0, jnp.maximum(qi - 1, 0), plain_body, 0)

    @pl.when(qi >= 1)
    def _():
        tile(qi - 1, tz_ref[:, :, 0:tq])

    tile(qi, tz_ref[:, :, tq:2 * tq])
    out = _online_result(l_s, acc_s)
    for r in range(REP_A):
        o_ref[0, :, r * HEAD_DIM:(r + 1) * HEAD_DIM] = out[r * tq:(r + 1) * tq].astype(o_ref.dtype)


def _attn_a_prompt(proj3, mask, tz, *, tq):
    b, t, _ = proj3.shape
    nq = t // tq
    qc, kc, vc = EV_AQ // (REP_A * HEAD_DIM), EV_AK // HEAD_DIM, EV_AV // HEAD_DIM
    rows = REP_A * tq
    n_g = mask.shape[1]

    def mask_idx(i, g, j):
        grp, pos = _mask_tile_index(j, nq, n_g)
        return (i, grp, pos, 0, 0, 0)

    return pl.pallas_call(
        functools.partial(_attn_a_prompt_kernel, tq=tq),
        grid=(b, KV_A, nq),
        in_specs=[
            pl.BlockSpec((1, tq, REP_A * HEAD_DIM), lambda i, g, j: (i, j, qc + g)),
            pl.BlockSpec((1, t, HEAD_DIM), lambda i, g, j: (i, 0, kc + g)),
            pl.BlockSpec((1, t, HEAD_DIM), lambda i, g, j: (i, 0, vc + g)),
            pl.BlockSpec((1, 1, 1, nq, tq, tq), mask_idx),
            pl.BlockSpec((REP_A, tq, 2 * tq), lambda i, g, j: (g, 0, 0)),
        ],
        out_specs=pl.BlockSpec((1, tq, REP_A * HEAD_DIM), lambda i, g, j: (i, j, g)),
        out_shape=jax.ShapeDtypeStruct((b, t, H_A * HEAD_DIM), BF16),
        scratch_shapes=[pltpu.VMEM((rows, LANE), F32), pltpu.VMEM((rows, LANE), F32),
                        pltpu.VMEM((rows, HEAD_DIM), F32)],
        compiler_params=_cparams("parallel", "parallel", "parallel"),
        name="attn_a_prompt",
    )(proj3, proj3, proj3, mask, tz)


def _attn_a_step_kernel(q_ref, kc_ref, vc_ref, kn_ref, vn_ref, mask_ref, tz_ref, o_ref, *, t, p):
    g = pl.program_id(1)
    qs = _stack_heads(q_ref[0] * SCALE)
    kc = kc_ref[0, pl.ds(g, p, stride=KV_A), :].astype(BF16)
    vc = vc_ref[0, pl.ds(g, p, stride=KV_A), :].astype(BF16)
    kn = _pad_keys(kn_ref[0], LANE).astype(BF16)
    vn = _pad_keys(vn_ref[0], LANE).astype(BF16)
    mask = mask_ref[0, 0, 0, 0]
    s_c = _dot_nt(qs, kc).reshape(REP_A, t, p) + mask[None, :, 0:p]
    s_n = _dot_nt(qs, kn).reshape(REP_A, t, LANE) + mask[None, :, p:p + LANE] + tz_ref[:, :, LANE:2 * LANE]
    blocks = _lane_blocks(s_c)
    blocks[-1] = blocks[-1] + tz_ref[:, :, 0:LANE]
    blocks.append(s_n)
    mx = blocks[0]
    for blk in blocks[1:]:
        mx = jnp.maximum(mx, blk)
    m = jnp.max(mx, axis=-1, keepdims=True)
    ps = [jnp.exp(blk - m) for blk in blocks]
    psum = ps[0]
    for pb in ps[1:]:
        psum = psum + pb
    l = jnp.sum(psum, axis=-1, keepdims=True)
    p_c = jnp.concatenate([pb.astype(BF16) for pb in ps[:-1]], axis=-1).reshape(REP_A * t, p)
    out = _dot(p_c, vc) + _dot(ps[-1].astype(BF16).reshape(REP_A * t, LANE), vn)
    out = out.reshape(REP_A, t, HEAD_DIM) / l
    for r in range(REP_A):
        o_ref[0, :, r * HEAD_DIM:(r + 1) * HEAD_DIM] = out[r].astype(o_ref.dtype)


def _attn_a_step(proj3, cache_k, cache_v, mask, tz):
    b, t, _ = proj3.shape
    p = cache_k.shape[1] // KV_A
    qc, kc, vc = EV_AQ // (REP_A * HEAD_DIM), EV_AK // HEAD_DIM, EV_AV // HEAD_DIM
    return pl.pallas_call(
        functools.partial(_attn_a_step_kernel, t=t, p=p),
        grid=(b, KV_A),
        in_specs=[
            pl.BlockSpec((1, t, REP_A * HEAD_DIM), lambda i, g: (i, 0, qc + g)),
            pl.BlockSpec((1, p * KV_A, HEAD_DIM), lambda i, g: (i, 0, 0)),
            pl.BlockSpec((1, p * KV_A, HEAD_DIM), lambda i, g: (i, 0, 0)),
            pl.BlockSpec((1, t, HEAD_DIM), lambda i, g: (i, 0, kc + g)),
            pl.BlockSpec((1, t, HEAD_DIM), lambda i, g: (i, 0, vc + g)),
            pl.BlockSpec((1, 1, 1, 1, t, p + LANE), lambda i, g: (i, 0, 0, 0, 0, 0)),
            pl.BlockSpec((REP_A, t, 2 * LANE), lambda i, g: (g, 0, 0)),
        ],
        out_specs=pl.BlockSpec((1, t, REP_A * HEAD_DIM), lambda i, g: (i, 0, g)),
        out_shape=jax.ShapeDtypeStruct((b, t, H_A * HEAD_DIM), BF16),
        compiler_params=_cparams("parallel", "parallel"),
        name="attn_a_step",
    )(proj3, cache_k, cache_v, proj3, proj3, mask, tz)


def _causal_mask(s, row0=0):
    r = row0 + lax.broadcasted_iota(I32, s.shape, 0)
    c = lax.broadcasted_iota(I32, s.shape, 1)
    return jnp.where(c <= r, s, NEG_INF)


def _attn_b_prompt_kernel(q_ref, k_ref, v_ref, ck_ref, o_ref, m_s, l_s, acc_s, *, tq):
    qi = pl.program_id(2)
    q = (q_ref[0] * SCALE).astype(BF16)
    _online_init(m_s, l_s, acc_s)

    def tile(kt, diag):
        off = pl.multiple_of(kt * tq, tq)
        k = k_ref[0, pl.ds(off, tq), :].astype(BF16)
        v = v_ref[0, pl.ds(off, tq), :].astype(BF16)
        s = _dot_nt(q, k) - ck_ref[0, 0, kt]
        if diag:
            s = _causal_mask(s)
        _online_update(s, v, m_s, l_s, acc_s)

    def body(kt, carry):
        tile(kt, False)
        return carry

    lax.fori_loop(0, qi, body, 0)
    tile(qi, True)
    o_ref[0] = _online_result(l_s, acc_s).astype(o_ref.dtype)


def _attn_b_prompt(proj3, cum_h, *, tq=512):
    b, t, _ = proj3.shape
    nq = t // tq
    ck = cum_h.reshape(b, H_B, nq, 1, tq)
    qc, kc, vc = EV_BQ // HEAD_DIM, EV_BK // HEAD_DIM, EV_BV // HEAD_DIM
    return pl.pallas_call(
        functools.partial(_attn_b_prompt_kernel, tq=tq),
        grid=(b, H_B, nq),
        in_specs=[
            pl.BlockSpec((1, tq, HEAD_DIM), lambda i, h, j: (i, j, qc + h)),
            pl.BlockSpec((1, t, HEAD_DIM), lambda i, h, j: (i, 0, kc + h)),
            pl.BlockSpec((1, t, HEAD_DIM), lambda i, h, j: (i, 0, vc + h)),
            pl.BlockSpec((1, 1, nq, 1, tq), lambda i, h, j: (i, h, 0, 0, 0)),
        ],
        out_specs=pl.BlockSpec((1, tq, HEAD_DIM), lambda i, h, j: (i, j, h)),
        out_shape=jax.ShapeDtypeStruct((b, t, H_B * HEAD_DIM), BF16),
        scratch_shapes=[pltpu.VMEM((tq, LANE), F32), pltpu.VMEM((tq, LANE), F32), pltpu.VMEM((tq, HEAD_DIM), F32)],
        compiler_params=_cparams("parallel", "parallel", "parallel"),
        name="attn_b_prompt",
    )(proj3, proj3, proj3, ck)


def _attn_b_step_kernel(q_ref, kc_ref, vc_ref, kn_ref, vn_ref, ckc_ref, ckn_ref, o_ref, m_s, l_s, acc_s,
                        *, nkc, t, tkc):
    kb = pl.program_id(1)

    @pl.when(kb == 0)
    def _():
        _online_init(m_s, l_s, acc_s)

    def q_head(h):
        return (q_ref[0, :, h * HEAD_DIM:(h + 1) * HEAD_DIM] * SCALE).astype(BF16)

    @pl.when(kb < nkc)
    def _():
        ck = ckc_ref[0, 0]
        logits = [_dot_nt(q_head(h), kc_ref[0, pl.ds(h, tkc, stride=H_B), :].astype(BF16)) - ck[h:h + 1, :]
                  for h in range(H_B)]
        stats = [_online_stats(logits[h], m_s.at[h], l_s.at[h]) for h in range(H_B)]
        for h in range(H_B):
            alpha, p_all = stats[h]
            v = vc_ref[0, pl.ds(h, tkc, stride=H_B), :].astype(BF16)
            acc_s[h] = alpha * acc_s[h] + _dot(p_all, v)

    @pl.when(kb == nkc)
    def _():
        ckn = ckn_ref[0]
        heads = [slice(h * HEAD_DIM, (h + 1) * HEAD_DIM) for h in range(H_B)]
        logits = [_causal_mask(_dot_nt(q_head(h), _pad_keys(kn_ref[0, :, heads[h]], LANE).astype(BF16))
                               - ckn[h:h + 1, :]) for h in range(H_B)]
        stats = [_online_stats(logits[h], m_s.at[h], l_s.at[h]) for h in range(H_B)]
        for h in range(H_B):
            alpha, p_all = stats[h]
            v = _pad_keys(vn_ref[0, :, heads[h]], LANE).astype(BF16)
            acc_s[h] = alpha * acc_s[h] + _dot(p_all, v)
            o_ref[0, :, heads[h]] = _online_result(l_s.at[h], acc_s.at[h]).astype(o_ref.dtype)


def _attn_b_step(proj3, cache_k, cache_v, cum_h, *, tkc=1024):
    b, t, _ = proj3.shape
    p = cache_k.shape[1] // H_B
    nkc = p // tkc
    w = H_B * HEAD_DIM
    ck_cache = cum_h[:, :, :p].reshape(b, H_B, nkc, tkc).transpose(0, 2, 1, 3)
    ck_new = jnp.pad(cum_h[:, :, p:], ((0, 0), (0, 0), (0, LANE - t)))
    last = nkc - 1
    return pl.pallas_call(
        functools.partial(_attn_b_step_kernel, nkc=nkc, t=t, tkc=tkc),
        grid=(b, nkc + 1),
        in_specs=[
            pl.BlockSpec((1, t, w), lambda i, j: (i, 0, EV_BQ // w)),
            pl.BlockSpec((1, tkc * H_B, HEAD_DIM), lambda i, j: (i, jnp.minimum(j, last), 0)),
            pl.BlockSpec((1, tkc * H_B, HEAD_DIM), lambda i, j: (i, jnp.minimum(j, last), 0)),
            pl.BlockSpec((1, t, w), lambda i, j: (i, 0, EV_BK // w)),
            pl.BlockSpec((1, t, w), lambda i, j: (i, 0, EV_BV // w)),
            pl.BlockSpec((1, 1, H_B, tkc), lambda i, j: (i, jnp.minimum(j, last), 0, 0)),
            pl.BlockSpec((1, H_B, LANE), lambda i, j: (i, 0, 0)),
        ],
        out_specs=pl.BlockSpec((1, t, w), lambda i, j: (i, 0, 0)),
        out_shape=jax.ShapeDtypeStruct((b, t, w), BF16),
        scratch_shapes=[pltpu.VMEM((H_B, t, LANE), F32), pltpu.VMEM((H_B, t, LANE), F32),
                        pltpu.VMEM((H_B, t, HEAD_DIM), F32)],
        compiler_params=_cparams("parallel", "arbitrary"),
        name="attn_b_step",
    )(proj3, cache_k, cache_v, proj3, proj3, ck_cache, ck_new)


def _tri_ones():
    r = lax.broadcasted_iota(I32, (2 * LANE, 2 * LANE), 0) & (LANE - 1)
    c = lax.broadcasted_iota(I32, (2 * LANE, 2 * LANE), 1)
    return jnp.where((r > c) | (c >= LANE), 1.0, 0.0).astype(BF16)


def _stick_scores(z, row_minus_col, tri_ones):
    rc_blocks = None if row_minus_col is None else _lane_blocks(row_minus_col)
    out = []
    for j, zb in enumerate(_lane_blocks(z)):
        tail = jnp.log(1.0 + jnp.exp(-jnp.abs(zb)))
        log_beta = jnp.minimum(zb, 0.0) - tail
        log_keep = -jnp.maximum(zb, 0.0) - tail
        strict = None
        if rc_blocks is not None:
            strict = rc_blocks[j] > 0
            log_keep = jnp.where(strict, log_keep, 0.0)
        hi = log_keep.astype(BF16)
        lo = (log_keep - hi.astype(F32)).astype(BF16)
        out.append((log_beta, strict, _dot(jnp.concatenate([hi, lo], axis=1), tri_ones)))
    return out


def _stick_weights(blocks, run):
    ws = [None] * len(blocks)
    for j in reversed(range(len(blocks))):
        log_beta, strict, sums = blocks[j]
        w = jnp.exp(log_beta + sums[:, :LANE] + run)
        if strict is not None:
            w = jnp.where(strict, w, 0.0)
        ws[j] = w.astype(BF16)
        run = run + sums[:, LANE:]
    return (jnp.concatenate(ws, axis=1) if len(ws) > 1 else ws[0]), run


def _stick_tile(q_bf16, k_bf16, v_bf16, run, tri_ones, row_minus_col):
    w_all, run = _stick_weights(_stick_scores(_dot_nt(q_bf16, k_bf16), row_minus_col, tri_ones), run)
    return _dot(w_all, v_bf16), run


def _attn_c_prompt_kernel(q_ref, k_ref, v_ref, o_ref, run_s, acc_s, *, tq):
    qi = pl.program_id(2)
    q = (q_ref[0] * SCALE).astype(BF16)
    tri_ones = _tri_ones()

    def load(kt):
        off = pl.multiple_of(kt * tq, tq)
        return k_ref[0, pl.ds(off, tq), :].astype(BF16), v_ref[0, pl.ds(off, tq), :].astype(BF16)

    rc = lax.broadcasted_iota(I32, (tq, tq), 0) - lax.broadcasted_iota(I32, (tq, tq), 1)
    k, v = load(qi)
    out, run = _stick_tile(q, k, v, jnp.zeros((tq, LANE), F32), tri_ones, rc)
    acc_s[...] = out
    run_s[...] = run

    def body(i, carry):
        k, v = load(qi - 1 - i)
        out, run = _stick_tile(q, k, v, run_s[...], tri_ones, None)
        acc_s[...] += out
        run_s[...] = run
        return carry

    lax.fori_loop(0, qi, body, 0)
    o_ref[0] = acc_s[...].astype(o_ref.dtype)


def _attn_c_prompt(proj3, *, tq=512):
    b, t, _ = proj3.shape
    nq = t // tq
    qc, kc, vc = OD_CQ // HEAD_DIM, OD_CK // HEAD_DIM, OD_CV // HEAD_DIM
    return pl.pallas_call(
        functools.partial(_attn_c_prompt_kernel, tq=tq),
        grid=(b, H_C, nq),
        in_specs=[
            pl.BlockSpec((1, tq, HEAD_DIM), lambda i, h, j: (i, j, qc + h)),
            pl.BlockSpec((1, t, HEAD_DIM), lambda i, h, j: (i, 0, kc + h)),
            pl.BlockSpec((1, t, HEAD_DIM), lambda i, h, j: (i, 0, vc + h)),
        ],
        out_specs=pl.BlockSpec((1, tq, HEAD_DIM), lambda i, h, j: (i, j, h)),
        out_shape=jax.ShapeDtypeStruct((b, t, H_C * HEAD_DIM), BF16),
        scratch_shapes=[pltpu.VMEM((tq, LANE), F32), pltpu.VMEM((tq, HEAD_DIM), F32)],
        compiler_params=_cparams("parallel", "parallel", "parallel"),
        name="attn_c_prompt",
    )(proj3, proj3, proj3)


def _attn_c_step_kernel(q_ref, kc_ref, vc_ref, kn_ref, vn_ref, o_ref, run_s, acc_s, *, nkc, t, tkc, sub):
    kb = pl.program_id(1)
    tri_ones = _tri_ones()

    def q_head(h):
        return (q_ref[0, :, h * HEAD_DIM:(h + 1) * HEAD_DIM] * SCALE).astype(BF16)

    @pl.when(kb == 0)
    def _():
        rc = lax.broadcasted_iota(I32, (t, LANE), 0) - lax.broadcasted_iota(I32, (t, LANE), 1)
        for h in range(H_C):
            sl = slice(h * HEAD_DIM, (h + 1) * HEAD_DIM)
            k = _pad_keys(kn_ref[0, :, sl], LANE).astype(BF16)
            v = _pad_keys(vn_ref[0, :, sl], LANE).astype(BF16)
            out, run = _stick_tile(q_head(h), k, v, jnp.zeros((t, LANE), F32), tri_ones, rc)
            acc_s[h] = out
            run_s[h] = run

    @pl.when(kb > 0)
    def _():
        def body(i, carry):
            row0 = (tkc - sub - i * sub) * H_C
            zs = [_dot_nt(q_head(h), kc_ref[0, pl.ds(row0 + h, sub, stride=H_C), :].astype(BF16))
                  for h in range(H_C)]
            scored = [_stick_scores(z, None, tri_ones) for z in zs]
            for h in range(H_C):
                w_all, run = _stick_weights(scored[h], run_s[h])
                v = vc_ref[0, pl.ds(row0 + h, sub, stride=H_C), :].astype(BF16)
                acc_s[h] += _dot(w_all, v)
                run_s[h] = run
            return carry

        lax.fori_loop(0, tkc // sub, body, 0)

    @pl.when(kb == nkc)
    def _():
        for h in range(H_C):
            o_ref[0, :, h * HEAD_DIM:(h + 1) * HEAD_DIM] = acc_s[h].astype(o_ref.dtype)


def _attn_c_step(proj3, cache_k, cache_v, *, tkc=1024, sub=1024):
    b, t, _ = proj3.shape
    p = cache_k.shape[1] // H_C
    nkc = p // tkc
    w = H_C * HEAD_DIM

    def cache_idx(i, j):
        return (i, jnp.clip(nkc - j, 0, nkc - 1), 0)

    return pl.pallas_call(
        functools.partial(_attn_c_step_kernel, nkc=nkc, t=t, tkc=tkc, sub=sub),
        grid=(b, nkc + 1),
        in_specs=[
            pl.BlockSpec((1, t, w), lambda i, j: (i, 0, OD_CQ // w)),
            pl.BlockSpec((1, tkc * H_C, HEAD_DIM), cache_idx),
            pl.BlockSpec((1, tkc * H_C, HEAD_DIM), cache_idx),
            pl.BlockSpec((1, t, w), lambda i, j: (i, 0, OD_CK // w)),
            pl.BlockSpec((1, t, w), lambda i, j: (i, 0, OD_CV // w)),
        ],
        out_specs=pl.BlockSpec((1, t, w), lambda i, j: (i, 0, 0)),
        out_shape=jax.ShapeDtypeStruct((b, t, w), BF16),
        scratch_shapes=[pltpu.VMEM((H_C, t, LANE), F32), pltpu.VMEM((H_C, t, HEAD_DIM), F32)],
        compiler_params=_cparams("parallel", "arbitrary"),
        name="attn_c_step",
    )(proj3, cache_k, cache_v, proj3, proj3)


def _attn_d_kernel(q_ref, kp_ref, vp_ref, ko_ref, vo_ref, bias_ref, o_ref, *, tq, q_base):
    qi = pl.program_id(2)
    q = (q_ref[0] * SCALE).astype(BF16)
    s_o = _dot_nt(q, ko_ref[0].astype(BF16)) + bias_ref[0, :, D_BAND_LEFT:D_BAND_LEFT + tq]
    m_o = jnp.max(s_o, axis=-1, keepdims=True)

    def with_left():
        s_p = _dot_nt(q, kp_ref[0].astype(BF16)) + bias_ref[0, :, 0:D_BAND_LEFT]
        m = jnp.maximum(jnp.max(s_p, axis=-1, keepdims=True), m_o)
        p_p = jnp.exp(s_p - m)
        p_o = jnp.exp(s_o - m)
        l = jnp.sum(p_p, axis=-1, keepdims=True) + jnp.sum(p_o, axis=-1, keepdims=True)
        out = _dot(p_p.astype(BF16), vp_ref[0].astype(BF16)) + _dot(p_o.astype(BF16), vo_ref[0].astype(BF16))
        o_ref[0] = (out / l).astype(o_ref.dtype)

    def own_only():
        p_o = jnp.exp(s_o - m_o)
        l = jnp.sum(p_o, axis=-1, keepdims=True)
        o_ref[0] = (_dot(p_o.astype(BF16), vo_ref[0].astype(BF16)) / l).astype(o_ref.dtype)

    if q_base >= D_BAND_LEFT:
        with_left()
    else:
        has_left = q_base + qi * tq >= D_BAND_LEFT
        pl.when(has_left)(with_left)
        pl.when(jnp.logical_not(has_left))(own_only)


def _attn_d(q_arr, q_col, kprev, kprev_col, vprev, vprev_col, kown, kown_col, vown, vown_col, bias,
            *, t, tq, q_base, prev_follows_own):
    b = q_arr.shape[0]
    nq = t // tq
    if prev_follows_own:
        def prev_idx(col):
            return lambda i, h, j: (i, jnp.maximum(j - 1, 0), col + h)
    else:
        def prev_idx(col):
            return lambda i, h, j: (i, 0, col + h)
    return pl.pallas_call(
        functools.partial(_attn_d_kernel, tq=tq, q_base=q_base),
        grid=(b, H_D, nq),
        in_specs=[
            pl.BlockSpec((1, tq, HEAD_DIM), lambda i, h, j: (i, j, q_col + h)),
            pl.BlockSpec((1, D_BAND_LEFT, HEAD_DIM), prev_idx(kprev_col)),
            pl.BlockSpec((1, D_BAND_LEFT, HEAD_DIM), prev_idx(vprev_col)),
            pl.BlockSpec((1, tq, HEAD_DIM), lambda i, h, j: (i, j, kown_col + h)),
            pl.BlockSpec((1, tq, HEAD_DIM), lambda i, h, j: (i, j, vown_col + h)),
            pl.BlockSpec((1, tq, D_BAND_LEFT + tq), lambda i, h, j: (h, 0, 0)),
        ],
        out_specs=pl.BlockSpec((1, tq, HEAD_DIM), lambda i, h, j: (i, j, h)),
        out_shape=jax.ShapeDtypeStruct((b, t, H_D * HEAD_DIM), BF16),
        compiler_params=_cparams("parallel", "parallel", "parallel"),
        name="attn_d",
    )(q_arr, kprev, vprev, kown, vown, bias)


def _attn_d_step_kernel(q_ref, kp_ref, vp_ref, kn_ref, vn_ref, bias_ref, o_ref, *, t):
    for h in range(H_D):
        sl = slice(h * HEAD_DIM, (h + 1) * HEAD_DIM)
        q = (q_ref[0, :, sl] * SCALE).astype(BF16)
        kp = kp_ref[0, pl.ds(h, D_BAND_LEFT, stride=H_D), :].astype(BF16)
        vp = vp_ref[0, pl.ds(h, D_BAND_LEFT, stride=H_D), :].astype(BF16)
        s_p = _dot_nt(q, kp) + bias_ref[h, :, 0:D_BAND_LEFT]
        s_o = _dot_nt(q, kn_ref[0, :, sl].astype(BF16)) + bias_ref[h, :, D_BAND_LEFT:D_BAND_LEFT + t]
        m = jnp.maximum(jnp.max(s_p, axis=-1, keepdims=True), jnp.max(s_o, axis=-1, keepdims=True))
        p_p = jnp.exp(s_p - m)
        p_o = jnp.exp(s_o - m)
        l = jnp.sum(p_p, axis=-1, keepdims=True) + jnp.sum(p_o, axis=-1, keepdims=True)
        out = _dot(p_p.astype(BF16), vp) + _dot(p_o.astype(BF16), vn_ref[0, :, sl].astype(BF16))
        o_ref[0, :, sl] = (out / l).astype(o_ref.dtype)


def _attn_d_step(proj3, cache_k, cache_v, bias):
    b, t, _ = proj3.shape
    w = H_D * HEAD_DIM
    return pl.pallas_call(
        functools.partial(_attn_d_step_kernel, t=t),
        grid=(b,),
        in_specs=[
            pl.BlockSpec((1, t, w), lambda i: (i, 0, OD_DQ // w)),
            pl.BlockSpec((1, D_BAND_LEFT * H_D, HEAD_DIM), lambda i: (i, 0, 0)),
            pl.BlockSpec((1, D_BAND_LEFT * H_D, HEAD_DIM), lambda i: (i, 0, 0)),
            pl.BlockSpec((1, t, w), lambda i: (i, 0, OD_DK // w)),
            pl.BlockSpec((1, t, w), lambda i: (i, 0, OD_DV // w)),
            pl.BlockSpec((H_D, t, D_BAND_LEFT + t), lambda i: (0, 0, 0)),
        ],
        out_specs=pl.BlockSpec((1, t, w), lambda i: (i, 0, 0)),
        out_shape=jax.ShapeDtypeStruct((b, t, w), BF16),
        compiler_params=_cparams("parallel"),
        name="attn_d_step",
    )(proj3, cache_k, cache_v, proj3, proj3, bias)


def _pad_rows(x, rows):
    return jnp.pad(x, ((0, 0), (0, rows - x.shape[1]), (0, 0)))


def _even_mixer(h2, b, t, past, w, tz_prompt, tz_step, tq_a, mm):
    proj = mm("in_even", h2, w["w_in"], mode="headnorm", gain=w["gain"], flag=w["flag"], n_norm_tiles=9,
              name="proj_even")
    proj3 = proj.reshape(b, t, EV_PAD)
    ak = proj3[:, :, EV_AK:EV_AK + KV_A * HEAD_DIM]
    av = proj3[:, :, EV_AV:EV_AV + KV_A * HEAD_DIM]
    ik = proj3[:, :, EV_MISC + MISC_IK:EV_MISC + MISC_IK + IDX_DIM]
    bk = proj3[:, :, EV_BK:EV_BK + H_B * HEAD_DIM]
    bv = proj3[:, :, EV_BV:EV_BV + H_B * HEAD_DIM]
    raw_h = proj3[:, :, EV_MISC + MISC_BF:EV_MISC + MISC_BF + H_B].transpose(0, 2, 1)
    iq_col = EV_IQ // (IDX_HEADS * IDX_DIM)
    misc_col = EV_MISC // LANE
    if past is None:
        logf_h, cum_h = _logf_cum(raw_h, w["f_bias"], None)
        mask = _select_mask(proj3, iq_col, proj3, misc_col, proj3, misc_col, t=t, tq=tq_a, lp=t, l_valid=t,
                            q_base=0, n_sel=min(TOPK_MAX, t // 4), tk_out=tq_a, causal_prefix=True)
        out_a = _attn_a_prompt(proj3, mask, tz_prompt, tq=tq_a)
        out_b = _attn_b_prompt(proj3, cum_h, tq=min(512, t))
    else:
        p_ak, p_av, p_ik, p_bk, p_bv, p_lf = past
        p = p_ak.shape[1]
        l_valid = p + t
        lp = p + LANE
        logf_h, cum_h = _logf_cum(raw_h, w["f_bias"], p_lf.transpose(0, 2, 1))
        ik_all = _pad_rows(jnp.concatenate([p_ik, ik], axis=1), lp)
        mask = _select_mask(proj3, iq_col, proj3, misc_col, ik_all, None, t=t, tq=t, lp=lp, l_valid=l_valid,
                            q_base=p, n_sel=min(TOPK_MAX, l_valid // 4), tk_out=lp, causal_prefix=False)
        out_a = _attn_a_step(proj3, p_ak.reshape(b, p * KV_A, HEAD_DIM), p_av.reshape(b, p * KV_A, HEAD_DIM),
                             mask, tz_step[:, :t])
        out_b = _attn_b_step(proj3, p_bk.reshape(b, p * H_B, HEAD_DIM), p_bv.reshape(b, p * H_B, HEAD_DIM), cum_h)
    mixed = [out_a.reshape(b * t, -1), out_b.reshape(b * t, -1)]
    state = (ak.reshape(b, t, KV_A, HEAD_DIM), av.reshape(b, t, KV_A, HEAD_DIM), ik,
             bk.reshape(b, t, H_B, HEAD_DIM), bv.reshape(b, t, H_B, HEAD_DIM), logf_h.transpose(0, 2, 1))
    return mixed, state


def _odd_mixer(h2, b, t, past, w, band, mm):
    proj = mm("in_odd", h2, w["w_in"], col_block=_odd_col_block, mode="headnorm", gain=w["gain"], flag=w["flag"],
              n_norm_tiles=4, name="proj_odd")
    proj3 = proj.reshape(b, t, OD_WIDTH)
    ck = proj3[:, :, OD_CK:OD_CK + H_C * HEAD_DIM]
    cv = proj3[:, :, OD_CV:OD_CV + H_C * HEAD_DIM]
    dk = proj3[:, :, OD_DK:OD_DK + H_D * HEAD_DIM]
    dv = proj3[:, :, OD_DV:OD_DV + H_D * HEAD_DIM]
    dq_col, dk_col, dv_col = OD_DQ // HEAD_DIM, OD_DK // HEAD_DIM, OD_DV // HEAD_DIM
    if past is None:
        out_c = _attn_c_prompt(proj3, tq=min(512, t))
        out_d = _attn_d(proj3, dq_col, proj3, dk_col, proj3, dv_col, proj3, dk_col, proj3, dv_col, band,
                        t=t, tq=D_BAND_LEFT, q_base=0, prev_follows_own=True)
        keep = min(D_BAND_LEFT, t)
        d_rows = (dk[:, t - keep:], dv[:, t - keep:])
    else:
        p_ck, p_cv, p_dk, p_dv = past
        p = p_ck.shape[1]
        out_c = _attn_c_step(proj3, p_ck.reshape(b, p * H_C, HEAD_DIM), p_cv.reshape(b, p * H_C, HEAD_DIM))
        wd = p_dk.shape[1]
        out_d = _attn_d_step(proj3, p_dk.reshape(b, wd * H_D, HEAD_DIM), p_dv.reshape(b, wd * H_D, HEAD_DIM),
                             band[:, :t, :D_BAND_LEFT + t])
        d_rows = (dk, dv)
    mixed = [out_c.reshape(b * t, -1), out_d.reshape(b * t, -1)]
    state = (ck.reshape(b, t, H_C, HEAD_DIM), cv.reshape(b, t, H_C, HEAD_DIM),
             d_rows[0].reshape(b, -1, H_D, HEAD_DIM), d_rows[1].reshape(b, -1, H_D, HEAD_DIM))
    return mixed, state


def _odd_col_block(j):
    return jnp.where(j < 4, j + 6, jnp.where(j < 10, j - 4, j))


def _trunk(x, caches_even, caches_odd, wts, bf16_weights):
    def mm(key, a, w_f32, col_block=None, **kw):
        if key in bf16_weights:
            return _matmul(a, bf16_weights[key], **kw)
        out, bf16_weights[key] = _matmul_cast(a, w_f32, col_block=col_block, **kw)
        return out

    b, t, d = x.shape
    x2 = x.reshape(b * t, d)
    h2 = _rmsnorm(x2, wts["attn_norm"][0])
    mixed, st_even = _even_mixer(h2, b, t, caches_even, wts["even"], wts["tz_prompt"], wts["tz_step"],
                                 wts["tq_a"], mm)
    x2 = mm("out_even", mixed, wts["even"]["w_out"], mode="residual", residual=x2, name="out_even")
    h2 = _rmsnorm(x2, wts["mlp_norm"][0])
    u = mm("up0", h2, wts["w_up"][0], mode="relu2", out_dtype=BF16, tn=1024, name="mlp_up0")
    x2 = _matmul(u, wts["w_down"][0], mode="residual", residual=x2, tm=512, tk=u.shape[1], name="mlp_down0")
    h2 = _rmsnorm(x2, wts["attn_norm"][1])
    mixed, st_odd = _odd_mixer(h2, b, t, caches_odd, wts["odd"], wts["band"], mm)
    x2 = mm("out_odd", mixed, wts["odd"]["w_out"], mode="residual", residual=x2, name="out_odd")
    h2 = _rmsnorm(x2, wts["mlp_norm"][1])
    u = mm("up1", h2, wts["w_up"][1], mode="relu2", out_dtype=BF16, tn=1024, name="mlp_up1")
    x2 = _matmul(u, wts["w_down"][1], mode="residual", residual=x2, tm=512, tk=u.shape[1], name="mlp_down1")
    return x2.reshape(b, t, d), tuple(s[None] for s in st_even), tuple(s[None] for s in st_odd)


def _even_weights(w_in, w_out, a_qn, a_kn, b_qn, b_kn, f_bias):
    aq, ak, av, iq, ik, iw, bq, bk, bv, bf = jnp.split(
        w_in, [1024, 1280, 1536, 2048, 2112, 2120, 3144, 4168, 5192], axis=1)
    d = w_in.shape[0]
    pad = jnp.zeros((d, EV_PAD - EV_WIDTH + LANE - (IDX_DIM + IDX_HEADS + H_B)), w_in.dtype)
    w_r = jnp.concatenate([aq, bq, bk, bv, ak, av, iq, ik, iw, bf, pad], axis=1)
    ones = jnp.ones((EV_PAD,), F32)
    gain = ones.at[EV_AQ:EV_AQ + 1024].set(jnp.tile(a_qn, H_A))
    gain = gain.at[EV_BQ:EV_BQ + 1024].set(jnp.tile(b_qn, H_B))
    gain = gain.at[EV_BK:EV_BK + 1024].set(jnp.tile(b_kn, H_B))
    gain = gain.at[EV_AK:EV_AK + 256].set(jnp.tile(a_kn, KV_A))
    flag = jnp.zeros((EV_PAD,), F32).at[0:EV_BV].set(1.0).at[EV_AK:EV_AV].set(1.0)
    return {"w_in": w_r, "gain": gain.reshape(1, -1), "flag": flag.reshape(1, -1), "w_out": w_out,
            "f_bias": f_bias}


def _odd_weights(w_in, w_out, d_qn, d_kn):
    gain = jnp.ones((OD_WIDTH,), F32)
    gain = gain.at[OD_DQ:OD_DQ + 1024].set(jnp.tile(d_qn, H_D)).at[OD_DK:OD_DK + 1024].set(jnp.tile(d_kn, H_D))
    flag = jnp.zeros((OD_WIDTH,), F32).at[0:OD_CQ].set(1.0)
    return {"w_in": w_in, "gain": gain.reshape(1, -1), "flag": flag.reshape(1, -1), "w_out": w_out}


def kernel(x_prompt, x_sample, cache_a_k, cache_a_v, cache_a_kidx, cache_b_k, cache_b_v, cache_b_logf,
           cache_c_k, cache_c_v, cache_d_k, cache_d_v, attn_norm, mlp_norm, w_in_even, w_out_even,
           w_in_odd, w_out_odd, a_q_norm, a_k_norm, b_q_norm, b_k_norm, forget_bias, t5_bias,
           d_q_norm, d_k_norm, d_rel_bias, w_up, w_down):
    tq_a = min(256, x_prompt.shape[1])
    wts = {
        "attn_norm": attn_norm,
        "mlp_norm": mlp_norm,
        "even": _even_weights(w_in_even[0], w_out_even[0], a_q_norm[0], a_k_norm[0], b_q_norm[0], b_k_norm[0],
                              forget_bias[0]),
        "odd": _odd_weights(w_in_odd[0], w_out_odd[0], d_q_norm[0], d_k_norm[0]),
        "w_up": w_up,
        "w_down": w_down.astype(BF16),
        "tq_a": tq_a,
        "tz_prompt": _t5_strip(t5_bias, tq_a),
        "tz_step": _t5_strip(t5_bias, LANE),
        "band": _band_bias(d_rel_bias[0]),
    }
    bf16_weights = {}
    y_p, ev_p, od_p = _trunk(x_prompt, None, None, wts, bf16_weights)
    caches_even = (cache_a_k[0], cache_a_v[0], cache_a_kidx[0], cache_b_k[0], cache_b_v[0], cache_b_logf[0])
    caches_odd = (cache_c_k[0], cache_c_v[0], cache_d_k[0], cache_d_v[0])
    y_s, ev_s, od_s = _trunk(x_sample, caches_even, caches_odd, wts, bf16_weights)
    return (y_p, y_s) + ev_p + od_p + ev_s + od_s
```

```python
import functools

import numpy as np
import jax
import jax.numpy as jnp
from jax import lax
from jax.experimental import pallas as pl
from jax.experimental.pallas import tpu as pltpu

F32 = jnp.float32
BF16 = jnp.bfloat16
I32 = jnp.int32

HEAD_DIM = 128
CHUNK = 64
CHUNK_SHIFT = 6
H_A = 8
KV_A = 2
REP_A = H_A // KV_A
H_B = 8
H_C = 8
H_D = 8
IDX_HEADS = 8
IDX_DIM = 64
TOPK_MAX = 256
T5_BUCKETS = 32
D_LEFT_CHUNKS = 8
D_BAND_LEFT = D_LEFT_CHUNKS * CHUNK
D_REL_CLIP = 128
EPS = 1e-6
NEG_INF = -1e30
SCALE = HEAD_DIM ** -0.5
LANE = 128
INT32_MIN = -(2 ** 31)
NEG_INF_KEY = int(np.float32(NEG_INF).view(np.int32)) ^ 0x7FFFFFFF

VMEM_LIMIT_BYTES = 56 * 1024 * 1024

EV_AQ, EV_BQ, EV_BK, EV_BV, EV_AK, EV_AV, EV_IQ, EV_MISC = 0, 1024, 2048, 3072, 4096, 4352, 4608, 5120
EV_WIDTH = 5248
EV_PAD = 5632
MISC_IK, MISC_IW, MISC_BF = 0, 64, 72
OD_DQ, OD_DK, OD_CQ, OD_CK, OD_CV, OD_DV = 0, 1024, 2048, 3072, 4096, 5120
OD_WIDTH = 6144


def _cparams(*sem):
    return pltpu.CompilerParams(dimension_semantics=sem, vmem_limit_bytes=VMEM_LIMIT_BYTES)


def _dot_nt(a, b):
    return lax.dot_general(a, b, (((1,), (1,)), ((), ())), preferred_element_type=F32)


def _dot(a, b):
    return jnp.dot(a, b, preferred_element_type=F32)


def _lane_blocks(x):
    return [x[..., j * LANE:(j + 1) * LANE] for j in range(x.shape[-1] // LANE)]


def _pad_keys(x, rows):
    return jnp.concatenate([x, jnp.zeros((rows - x.shape[0], x.shape[1]), x.dtype)], axis=0)


def _rmsnorm_kernel(x_ref, g_ref, o_ref):
    x = x_ref[...]
    ms = jnp.mean(x * x, axis=-1, keepdims=True)
    o_ref[...] = (x * lax.rsqrt(ms + EPS) * g_ref[...]).astype(o_ref.dtype)


def _rmsnorm(x2, gain):
    m, d = x2.shape
    tm = min(512, m)
    return pl.pallas_call(
        _rmsnorm_kernel,
        grid=(m // tm,),
        in_specs=[pl.BlockSpec((tm, d), lambda i: (i, 0)), pl.BlockSpec((1, d), lambda i: (0, 0))],
        out_specs=pl.BlockSpec((tm, d), lambda i: (i, 0)),
        out_shape=jax.ShapeDtypeStruct((m, d), BF16),
        compiler_params=_cparams("parallel"),
        name="rmsnorm",
    )(x2, gain.reshape(1, d))


def _mm_kernel(*refs, n_a, nk, mode, n_norm_tiles, cast_b):
    a_refs, b_ref = refs[:n_a], refs[n_a]
    refs = refs[n_a - 1:]
    if mode == "headnorm":
        gain_ref, flag_ref, o_ref = refs[2], refs[3], refs[4]
        rest = refs[5:]
    elif mode == "residual":
        res_ref, o_ref = refs[2], refs[3]
        rest = refs[4:]
    else:
        o_ref = refs[2]
        rest = refs[3:]
    n_axis = 0 if cast_b else 1
    if cast_b:
        wb_ref = rest[0]

        @pl.when(pl.program_id(1) == 0)
        def _():
            wb_ref[...] = b_ref[...].astype(BF16)

        b_ref = wb_ref

    def product():
        row, acc = 0, None
        for a_ref in a_refs:
            kp = a_ref.shape[1]
            part = _dot(a_ref[...], b_ref[row:row + kp, :])
            acc = part if acc is None else acc + part
            row += kp
        return acc

    def epilogue(acc):
        if mode == "headnorm":
            j = pl.program_id(n_axis)

            @pl.when(j < n_norm_tiles)
            def _():
                tn = acc.shape[1]
                for g in range(tn // HEAD_DIM):
                    sl = slice(g * HEAD_DIM, (g + 1) * HEAD_DIM)
                    blk = acc[:, sl]
                    ms = jnp.mean(blk * blk, axis=-1, keepdims=True)
                    normed = blk * lax.rsqrt(ms + EPS) * gain_ref[:, sl]
                    o_ref[:, sl] = jnp.where(flag_ref[:, sl] > 0.0, normed, blk)

            @pl.when(j >= n_norm_tiles)
            def _():
                o_ref[...] = acc
        elif mode == "relu2":
            r = jnp.maximum(acc, 0.0)
            o_ref[...] = (r * r).astype(o_ref.dtype)
        elif mode == "residual":
            o_ref[...] = res_ref[...] + acc
        else:
            o_ref[...] = acc.astype(o_ref.dtype)

    if nk == 1:
        epilogue(product())
    else:
        acc_ref = rest[0]
        k = pl.program_id(2)

        @pl.when(k == 0)
        def _():
            acc_ref[...] = jnp.zeros_like(acc_ref)

        acc_ref[...] += product()

        @pl.when(k == nk - 1)
        def _():
            epilogue(acc_ref[...])


def _matmul(a, b, *, mode="plain", out_dtype=F32, gain=None, flag=None, n_norm_tiles=0, residual=None,
            tm=1024, tn=512, tk=2048, name="matmul"):
    a_parts = list(a) if isinstance(a, (list, tuple)) else [a]
    m = a_parts[0].shape[0]
    kdim, n = b.shape
    tm = tm if m % tm == 0 else min(512, m)
    if len(a_parts) > 1:
        tk = kdim
        in_specs = [pl.BlockSpec((tm, part.shape[1]), lambda i, j, k: (i, 0)) for part in a_parts]
    else:
        in_specs = [pl.BlockSpec((tm, tk), lambda i, j, k: (i, k))]
    nk = kdim // tk
    in_specs.append(pl.BlockSpec((tk, tn), lambda i, j, k: (k, j)))
    args = a_parts + [b]
    if mode == "headnorm":
        in_specs += [pl.BlockSpec((1, tn), lambda i, j, k: (0, j)), pl.BlockSpec((1, tn), lambda i, j, k: (0, j))]
        args += [gain, flag]
    elif mode == "residual":
        in_specs += [pl.BlockSpec((tm, tn), lambda i, j, k: (i, j))]
        args += [residual]
    scratch = [pltpu.VMEM((tm, tn), F32)] if nk > 1 else []
    return pl.pallas_call(
        functools.partial(_mm_kernel, n_a=len(a_parts), nk=nk, mode=mode, n_norm_tiles=n_norm_tiles,
                          cast_b=False),
        grid=(m // tm, n // tn, nk),
        in_specs=in_specs,
        out_specs=pl.BlockSpec((tm, tn), lambda i, j, k: (i, j)),
        out_shape=jax.ShapeDtypeStruct((m, n), out_dtype),
        scratch_shapes=scratch,
        compiler_params=_cparams("parallel", "parallel", "arbitrary"),
        name=name,
    )(*args)


def _matmul_cast(a, b_f32, *, col_block=None, n=None, mode="plain", out_dtype=F32, gain=None, flag=None,
                 n_norm_tiles=0, residual=None, tm=1024, tn=512, name="matmul"):
    a_parts = list(a) if isinstance(a, (list, tuple)) else [a]
    m = a_parts[0].shape[0]
    kdim = b_f32.shape[0]
    n = b_f32.shape[1] if n is None else n
    tm = tm if m % tm == 0 else min(512, m)
    src = (lambda j: j) if col_block is None else col_block
    in_specs = [pl.BlockSpec((tm, part.shape[1]), lambda j, i: (i, 0)) for part in a_parts]
    in_specs.append(pl.BlockSpec((kdim, tn), lambda j, i: (0, src(j))))
    args = a_parts + [b_f32]
    if mode == "headnorm":
        in_specs += [pl.BlockSpec((1, tn), lambda j, i: (0, j)), pl.BlockSpec((1, tn), lambda j, i: (0, j))]
        args += [gain, flag]
    elif mode == "residual":
        in_specs += [pl.BlockSpec((tm, tn), lambda j, i: (i, j))]
        args += [residual]
    return pl.pallas_call(
        functools.partial(_mm_kernel, n_a=len(a_parts), nk=1, mode=mode, n_norm_tiles=n_norm_tiles, cast_b=True),
        grid=(n // tn, m // tm),
        in_specs=in_specs,
        out_specs=[pl.BlockSpec((tm, tn), lambda j, i: (i, j)), pl.BlockSpec((kdim, tn), lambda j, i: (0, j))],
        out_shape=[jax.ShapeDtypeStruct((m, n), out_dtype), jax.ShapeDtypeStruct((kdim, n), BF16)],
        compiler_params=_cparams("parallel", "arbitrary"),
        name=name,
    )(*args)


def _cumsum_block(x, carry):
    n = x.shape[1]
    r = lax.broadcasted_iota(I32, (n, n), 0)
    c = lax.broadcasted_iota(I32, (n, n), 1)
    tri = jnp.where(r <= c, 1.0, 0.0).astype(BF16)
    hi = x.astype(BF16)
    r1 = x - hi.astype(F32)
    mid = r1.astype(BF16)
    lo = (r1 - mid.astype(F32)).astype(BF16)
    cs = _dot(hi, tri) + _dot(mid, tri) + _dot(lo, tri) + carry
    return cs, cs[:, n - 1:n]


def _logf_cum_kernel(*refs, p_len, t_len, blk):
    if p_len:
        raw_ref, fb_ref, past_ref, lf_ref, cum_ref = refs
    else:
        raw_ref, fb_ref, lf_ref, cum_ref = refs
    x = raw_ref[0] + fb_ref[...]
    lf = jnp.minimum(x, 0.0) - jnp.log1p(jnp.exp(-jnp.abs(x)))
    lf_ref[0] = lf
    carry = jnp.zeros((H_B, 1), F32)
    for s in range(0, p_len, blk):
        cs, carry = _cumsum_block(past_ref[0, :, s:s + blk], carry)
        cum_ref[0, :, s:s + blk] = cs
    nb = min(blk, t_len)
    for s in range(0, t_len, nb):
        cs, carry = _cumsum_block(lf[:, s:s + nb], carry)
        cum_ref[0, :, p_len + s:p_len + s + nb] = cs


def _logf_cum(raw_h, fbias, past_h):
    b, h, t = raw_h.shape
    p = 0 if past_h is None else past_h.shape[2]
    in_specs = [pl.BlockSpec((1, h, t), lambda i: (i, 0, 0)), pl.BlockSpec((h, 1), lambda i: (0, 0))]
    args = [raw_h, fbias.reshape(h, 1)]
    if p:
        in_specs.append(pl.BlockSpec((1, h, p), lambda i: (i, 0, 0)))
        args.append(past_h)
    return pl.pallas_call(
        functools.partial(_logf_cum_kernel, p_len=p, t_len=t, blk=512),
        grid=(b,),
        in_specs=in_specs,
        out_specs=[pl.BlockSpec((1, h, t), lambda i: (i, 0, 0)), pl.BlockSpec((1, h, p + t), lambda i: (i, 0, 0))],
        out_shape=[jax.ShapeDtypeStruct((b, h, t), F32), jax.ShapeDtypeStruct((b, h, p + t), F32)],
        compiler_params=_cparams("parallel"),
        name="logf_cum",
    )(*args)


def _toeplitz(row_vals, rows, cols):
    w = row_vals.shape[1]
    full = jnp.broadcast_to(row_vals[0:1, :], (rows, w))
    return pltpu.roll(full, 0, 1, stride=1, stride_axis=0)[:, :cols]


def _t5_strip_kernel(tab_ref, o_ref, *, tile):
    h = pl.program_id(0)
    width = 4 * tile
    m = lax.broadcasted_iota(I32, (8, width), 1)
    d = jnp.where(m < width // 2, m, m - width)
    rel = d - tile
    n = jnp.abs(rel)
    large = jnp.full_like(n, 8)
    for thr in (12, 16, 23, 32, 46, 64, 91):
        large = large + jnp.where(n >= thr, 1, 0)
    bucket = jnp.where(rel > 0, T5_BUCKETS // 2, 0) + jnp.where(n < 8, n, large)
    vals = jnp.zeros((8, width), F32)
    for bkt in range(T5_BUCKETS):
        vals = jnp.where(bucket == bkt, tab_ref[bkt, h], vals)
    vals = vals - tab_ref[T5_BUCKETS // 2 - 1, h]
    o_ref[0] = _toeplitz(vals, tile, 2 * tile)


def _t5_strip(t5_table, tile):
    return pl.pallas_call(
        functools.partial(_t5_strip_kernel, tile=tile),
        grid=(H_A,),
        in_specs=[pl.BlockSpec(memory_space=pltpu.SMEM)],
        out_specs=pl.BlockSpec((1, tile, 2 * tile), lambda h: (h, 0, 0)),
        out_shape=jax.ShapeDtypeStruct((H_A, tile, 2 * tile), F32),
        compiler_params=_cparams("parallel"),
        name="t5_strip",
    )(t5_table)


def _band_bias_kernel(tab_ref, o_ref, *, rows, cols, width):
    h = pl.program_id(0)
    m = lax.broadcasted_iota(I32, (8, width), 1)
    d = jnp.where(m < width // 2, m, m - width)
    idx = jnp.clip(d - D_BAND_LEFT, -D_REL_CLIP, D_REL_CLIP) + D_REL_CLIP
    vals = jnp.zeros((8, width), F32)
    for r in range(2 * D_REL_CLIP + 1):
        vals = jnp.where(idx == r, tab_ref[r, h], vals)
    vals = vals - tab_ref[0, h]
    i_chunk = lax.broadcasted_iota(I32, (rows, cols), 0) >> CHUNK_SHIFT
    c_chunk = lax.broadcasted_iota(I32, (rows, cols), 1) >> CHUNK_SHIFT
    visible = (c_chunk >= i_chunk) & (c_chunk <= i_chunk + D_LEFT_CHUNKS)
    o_ref[0] = jnp.where(visible, _toeplitz(vals, rows, cols), NEG_INF)


def _band_bias(rel_table):
    rows, cols, width = D_BAND_LEFT, 2 * D_BAND_LEFT, 4 * D_BAND_LEFT
    return pl.pallas_call(
        functools.partial(_band_bias_kernel, rows=rows, cols=cols, width=width),
        grid=(H_D,),
        in_specs=[pl.BlockSpec(memory_space=pltpu.SMEM)],
        out_specs=pl.BlockSpec((1, rows, cols), lambda h: (h, 0, 0)),
        out_shape=jax.ShapeDtypeStruct((H_D, rows, cols), F32),
        compiler_params=_cparams("parallel"),
        name="band_bias",
    )(rel_table)


def _select_kernel(*refs, n_g, sb, tq, nq, lp, l_valid, q_base, n_sel, tk_out, causal_prefix):
    iq_refs, misc_refs = refs[:n_g], refs[n_g:2 * n_g]
    kk_ref, o_ref, key_ref = refs[2 * n_g:]
    step = pl.program_id(1)
    rows = sb * tq
    kf = float(n_sel)
    gs = range(n_g)

    def count(pred):
        return jnp.sum(jnp.where(pred, 1.0, 0.0), axis=-1, keepdims=True)

    def run(tiles):
        pfx = [p for _, p in tiles]
        adm, k_pos = [], []
        for g, (q_tile, p) in enumerate(tiles):
            scores = []
            for s in range(sb):
                iq = iq_refs[g][s]
                iw = misc_refs[g][s][:, MISC_IW:MISC_IW + IDX_HEADS]
                ik = kk_ref[s, 0:p, 0:IDX_DIM].astype(BF16)
                score = jnp.zeros((tq, p), F32)
                for h in range(IDX_HEADS):
                    dots = _dot_nt(iq[:, h * IDX_DIM:(h + 1) * IDX_DIM].astype(BF16), ik)
                    score = score + iw[:, h:h + 1] * jnp.maximum(dots, 0.0)
                scores.append(score)
            score = jnp.concatenate(scores, axis=0) if sb > 1 else scores[0]
            row_in_tile = jnp.concatenate([lax.broadcasted_iota(I32, (tq, 1), 0)] * sb, axis=0)
            q_pos = q_base + q_tile * tq + row_in_tile
            kp = lax.broadcasted_iota(I32, (1, p), 1)
            ok = ((kp >> CHUNK_SHIFT) <= (q_pos >> CHUNK_SHIFT)) & (kp < l_valid)
            bits = lax.bitcast_convert_type(jnp.where(ok, score, NEG_INF), I32)
            key_ref[g, :, 0:p] = jnp.where(bits < 0, bits ^ jnp.int32(0x7FFFFFFF), bits)
            adm.append(ok)
            k_pos.append(kp)

        def keys(g):
            return key_ref[g, :, 0:pfx[g]]

        t0 = tuple(jnp.where(count(keys(g) >= 0) >= kf, jnp.int32(0), jnp.int32(INT32_MIN)) for g in gs)

        def thr_body(i, ts):
            bit = jnp.int32(1) << (30 - i)
            return tuple(jnp.where(count(keys(g) >= ts[g] + bit) >= kf, ts[g] + bit, ts[g]) for g in gs)

        thr = lax.fori_loop(0, 31, thr_body, t0)
        need = [kf - count(keys(g) > thr[g]) for g in gs]
        tied = [(count(keys(g) == thr[g]) != need[g]) & (thr[g] > NEG_INF_KEY) for g in gs]
        any_tied = sum(jnp.max(jnp.where(tied[g], 1.0, 0.0)) for g in gs) > 0.5
        nbits = max(max(pfx) - 1, 1).bit_length()

        def tie_search():
            def tie_body(i, jbs):
                bit = jnp.int32(1) << (nbits - 1 - i)
                return tuple(jnp.where(count((keys(g) == thr[g]) & (k_pos[g] < jbs[g] + bit)) <= need[g],
                                       jbs[g] + bit, jbs[g]) for g in gs)

            return lax.fori_loop(0, nbits, tie_body, tuple(jnp.zeros((rows, 1), I32) for _ in gs))

        jbound = lax.cond(any_tied, tie_search, lambda: tuple(jnp.full((rows, 1), pfx[g], I32) for g in gs))
        for g in gs:
            key = keys(g)
            sel = ((key > thr[g]) | ((key == thr[g]) & (k_pos[g] < jbound[g]))) & adm[g]
            madd = jnp.where(sel, 0.0, NEG_INF)
            for s in range(sb):
                for kt in range(lp // tk_out):
                    if (kt + 1) * tk_out <= pfx[g]:
                        o_ref[s, g, 0, kt] = madd[s * tq:(s + 1) * tq, kt * tk_out:(kt + 1) * tk_out]
                    else:
                        o_ref[s, g, 0, kt] = jnp.full((tq, tk_out), NEG_INF, F32)

    if not causal_prefix:
        run([(step, lp)] * n_g)
    elif n_g == 2:
        for j in range(nq // 2):
            pl.when(step == j)(functools.partial(run, [(j, (j + 1) * tq), (nq - 1 - j, (nq - j) * tq)]))
    else:
        for j in range(nq):
            pl.when(step == j)(functools.partial(run, [(j, (j + 1) * tq)]))


def _select_mask(iq_arr, iq_col, misc_arr, misc_col, kk_arr, kk_col, *, t, tq, lp, l_valid, q_base, n_sel,
                 tk_out, causal_prefix):
    b = iq_arr.shape[0]
    nq = t // tq
    wk = kk_arr.shape[2] if kk_col is None else LANE
    kcol = 0 if kk_col is None else kk_col
    sb = 4 if (nq == 1 and tq <= 64 and b % 4 == 0) else 1
    n_g = 2 if (causal_prefix and nq % 2 == 0) else 1
    steps = nq // n_g
    tile_of = [lambda j: j, lambda j: nq - 1 - j]
    nkt = lp // tk_out
    in_specs = [pl.BlockSpec((sb, tq, IDX_HEADS * IDX_DIM), lambda i, j, g=g: (i, tile_of[g](j), iq_col))
                for g in range(n_g)]
    in_specs += [pl.BlockSpec((sb, tq, LANE), lambda i, j, g=g: (i, tile_of[g](j), misc_col)) for g in range(n_g)]
    in_specs.append(pl.BlockSpec((sb, lp, wk), lambda i, j: (i, 0, kcol)))
    return pl.pallas_call(
        functools.partial(_select_kernel, n_g=n_g, sb=sb, tq=tq, nq=nq, lp=lp, l_valid=l_valid, q_base=q_base,
                          n_sel=n_sel, tk_out=tk_out, causal_prefix=causal_prefix),
        grid=(b // sb, steps),
        in_specs=in_specs,
        out_specs=pl.BlockSpec((sb, n_g, 1, nkt, tq, tk_out), lambda i, j: (i, 0, j, 0, 0, 0)),
        out_shape=jax.ShapeDtypeStruct((b, n_g, steps, nkt, tq, tk_out), F32),
        scratch_shapes=[pltpu.VMEM((n_g, sb * tq, lp), I32)],
        compiler_params=_cparams("parallel", "parallel"),
        name="a_select",
    )(*([iq_arr] * n_g), *([misc_arr] * n_g), kk_arr)


def _mask_tile_index(q_tile, nq, n_g):
    if n_g == 1:
        return 0, q_tile
    upper = q_tile >= nq // 2
    return jnp.where(upper, 1, 0), jnp.where(upper, nq - 1 - q_tile, q_tile)


def _online_stats(s, m_s, l_s):
    blocks = _lane_blocks(s)
    mx = blocks[0]
    for blk in blocks[1:]:
        mx = jnp.maximum(mx, blk)
    m_prev = m_s[...]
    m_new = jnp.maximum(m_prev, jnp.max(mx, axis=-1, keepdims=True))
    alpha = jnp.exp(m_prev - m_new)
    ps = [jnp.exp(blk - m_new) for blk in blocks]
    psum = ps[0]
    for p in ps[1:]:
        psum = psum + p
    l_s[...] = alpha * l_s[...] + psum
    m_s[...] = m_new
    p_all = jnp.concatenate([p.astype(BF16) for p in ps], axis=1) if len(ps) > 1 else ps[0].astype(BF16)
    return alpha, p_all


def _online_update(s, v_bf16, m_s, l_s, acc_s):
    alpha, p_all = _online_stats(s, m_s, l_s)
    acc_s[...] = alpha * acc_s[...] + _dot(p_all, v_bf16)


def _online_init(m_s, l_s, acc_s):
    m_s[...] = jnp.full_like(m_s, NEG_INF)
    l_s[...] = jnp.zeros_like(l_s)
    acc_s[...] = jnp.zeros_like(acc_s)


def _online_result(l_s, acc_s):
    return acc_s[...] / jnp.sum(l_s[...], axis=-1, keepdims=True)


def _stack_heads(q):
    return jnp.concatenate([q[:, r * HEAD_DIM:(r + 1) * HEAD_DIM] for r in range(REP_A)], axis=0).astype(BF16)


def _attn_a_prompt_kernel(q_ref, k_ref, v_ref, mask_ref, tz_ref, o_ref, m_s, l_s, acc_s, *, tq):
    qi = pl.program_id(1)
    width = REP_A * HEAD_DIM
    qs = [_stack_heads(q_ref[0, :, g * width:(g + 1) * width] * SCALE) for g in range(KV_A)]
    _online_init(m_s, l_s, acc_s)

    def tile(kt, bias_cols):
        off = pl.multiple_of(kt * tq, tq)
        mask = mask_ref[0, 0, 0, kt][None]
        logits = []
        for g in range(KV_A):
            k = k_ref[0, pl.ds(off, tq), g * HEAD_DIM:(g + 1) * HEAD_DIM].astype(BF16)
            s = _dot_nt(qs[g], k).reshape(REP_A, tq, tq) + mask
            if bias_cols is not None:
                s = s + tz_ref[g * REP_A:(g + 1) * REP_A, :, bias_cols]
            logits.append(s.reshape(REP_A * tq, tq))
        stats = [_online_stats(logits[g], m_s.at[g], l_s.at[g]) for g in range(KV_A)]
        for g in range(KV_A):
            alpha, p_all = stats[g]
            v = v_ref[0, pl.ds(off, tq), g * HEAD_DIM:(g + 1) * HEAD_DIM].astype(BF16)
            acc_s[g] = alpha * acc_s[g] + _dot(p_all, v)

    def plain_body(kt, carry):
        tile(kt, None)
        return carry

    lax.fori_loop(0, jnp.maximum(qi - 1, 0), plain_body, 0)

    @pl.when(qi >= 1)
    def _():
        tile(qi - 1, slice(0, tq))

    tile(qi, slice(tq, 2 * tq))
    for g in range(KV_A):
        out = _online_result(l_s.at[g], acc_s.at[g])
        for r in range(REP_A):
            col = (g * REP_A + r) * HEAD_DIM
            o_ref[0, :, col:col + HEAD_DIM] = out[r * tq:(r + 1) * tq].astype(o_ref.dtype)


def _attn_a_prompt(proj3, mask, tz, *, tq):
    b, t, _ = proj3.shape
    nq = t // tq
    qw, kw = H_A * HEAD_DIM, KV_A * HEAD_DIM
    rows = REP_A * tq
    n_g = mask.shape[1]

    def mask_idx(i, j):
        grp, pos = _mask_tile_index(j, nq, n_g)
        return (i, grp, pos, 0, 0, 0)

    return pl.pallas_call(
        functools.partial(_attn_a_prompt_kernel, tq=tq),
        grid=(b, nq),
        in_specs=[
            pl.BlockSpec((1, tq, qw), lambda i, j: (i, j, EV_AQ // qw)),
            pl.BlockSpec((1, t, kw), lambda i, j: (i, 0, EV_AK // kw)),
            pl.BlockSpec((1, t, kw), lambda i, j: (i, 0, EV_AV // kw)),
            pl.BlockSpec((1, 1, 1, nq, tq, tq), mask_idx),
            pl.BlockSpec((H_A, tq, 2 * tq), lambda i, j: (0, 0, 0)),
        ],
        out_specs=pl.BlockSpec((1, tq, qw), lambda i, j: (i, j, 0)),
        out_shape=jax.ShapeDtypeStruct((b, t, qw), BF16),
        scratch_shapes=[pltpu.VMEM((KV_A, rows, LANE), F32), pltpu.VMEM((KV_A, rows, LANE), F32),
                        pltpu.VMEM((KV_A, rows, HEAD_DIM), F32)],
        compiler_params=_cparams("parallel", "parallel"),
        name="attn_a_prompt",
    )(proj3, proj3, proj3, mask, tz)


def _attn_a_step_kernel(q_ref, kc_ref, vc_ref, kn_ref, vn_ref, mask_ref, tz_ref, o_ref, *, t, p):
    g = pl.program_id(1)
    qs = _stack_heads(q_ref[0] * SCALE)
    kc = kc_ref[0, pl.ds(g, p, stride=KV_A), :].astype(BF16)
    vc = vc_ref[0, pl.ds(g, p, stride=KV_A), :].astype(BF16)
    kn = _pad_keys(kn_ref[0], LANE).astype(BF16)
    vn = _pad_keys(vn_ref[0], LANE).astype(BF16)
    mask = mask_ref[0, 0, 0, 0]
    s_c = _dot_nt(qs, kc).reshape(REP_A, t, p) + mask[None, :, 0:p]
    s_n = _dot_nt(qs, kn).reshape(REP_A, t, LANE) + mask[None, :, p:p + LANE] + tz_ref[:, :, LANE:2 * LANE]
    blocks = _lane_blocks(s_c)
    blocks[-1] = blocks[-1] + tz_ref[:, :, 0:LANE]
    blocks.append(s_n)
    mx = blocks[0]
    for blk in blocks[1:]:
        mx = jnp.maximum(mx, blk)
    m = jnp.max(mx, axis=-1, keepdims=True)
    ps = [jnp.exp(blk - m) for blk in blocks]
    psum = ps[0]
    for pb in ps[1:]:
        psum = psum + pb
    l = jnp.sum(psum, axis=-1, keepdims=True)
    p_c = jnp.concatenate([pb.astype(BF16) for pb in ps[:-1]], axis=-1).reshape(REP_A * t, p)
    out = _dot(p_c, vc) + _dot(ps[-1].astype(BF16).reshape(REP_A * t, LANE), vn)
    out = out.reshape(REP_A, t, HEAD_DIM) / l
    for r in range(REP_A):
        o_ref[0, :, r * HEAD_DIM:(r + 1) * HEAD_DIM] = out[r].astype(o_ref.dtype)


def _attn_a_step(proj3, cache_k, cache_v, mask, tz):
    b, t, _ = proj3.shape
    p = cache_k.shape[1] // KV_A
    qc, kc, vc = EV_AQ // (REP_A * HEAD_DIM), EV_AK // HEAD_DIM, EV_AV // HEAD_DIM
    return pl.pallas_call(
        functools.partial(_attn_a_step_kernel, t=t, p=p),
        grid=(b, KV_A),
        in_specs=[
            pl.BlockSpec((1, t, REP_A * HEAD_DIM), lambda i, g: (i, 0, qc + g)),
            pl.BlockSpec((1, p * KV_A, HEAD_DIM), lambda i, g: (i, 0, 0)),
            pl.BlockSpec((1, p * KV_A, HEAD_DIM), lambda i, g: (i, 0, 0)),
            pl.BlockSpec((1, t, HEAD_DIM), lambda i, g: (i, 0, kc + g)),
            pl.BlockSpec((1, t, HEAD_DIM), lambda i, g: (i, 0, vc + g)),
            pl.BlockSpec((1, 1, 1, 1, t, p + LANE), lambda i, g: (i, 0, 0, 0, 0, 0)),
            pl.BlockSpec((REP_A, t, 2 * LANE), lambda i, g: (g, 0, 0)),
        ],
        out_specs=pl.BlockSpec((1, t, REP_A * HEAD_DIM), lambda i, g: (i, 0, g)),
        out_shape=jax.ShapeDtypeStruct((b, t, H_A * HEAD_DIM), BF16),
        compiler_params=_cparams("parallel", "parallel"),
        name="attn_a_step",
    )(proj3, cache_k, cache_v, proj3, proj3, mask, tz)


def _causal_mask(s, row0=0):
    r = row0 + lax.broadcasted_iota(I32, s.shape, 0)
    c = lax.broadcasted_iota(I32, s.shape, 1)
    return jnp.where(c <= r, s, NEG_INF)


B_HEADS_PER_STEP = 2


def _attn_b_prompt_kernel(q_ref, k_ref, v_ref, ck_ref, o_ref, m_s, l_s, acc_s, *, tq):
    qi = pl.program_id(2)
    heads = [slice(h * HEAD_DIM, (h + 1) * HEAD_DIM) for h in range(B_HEADS_PER_STEP)]
    qs = [(q_ref[0, :, sl] * SCALE).astype(BF16) for sl in heads]
    _online_init(m_s, l_s, acc_s)

    def tile(kt, diag):
        off = pl.multiple_of(kt * tq, tq)
        logits = []
        for h, sl in enumerate(heads):
            s = _dot_nt(qs[h], k_ref[0, pl.ds(off, tq), sl].astype(BF16)) - ck_ref[0, h, kt]
            logits.append(_causal_mask(s) if diag else s)
        stats = [_online_stats(logits[h], m_s.at[h], l_s.at[h]) for h in range(B_HEADS_PER_STEP)]
        for h, sl in enumerate(heads):
            alpha, p_all = stats[h]
            acc_s[h] = alpha * acc_s[h] + _dot(p_all, v_ref[0, pl.ds(off, tq), sl].astype(BF16))

    def body(kt, carry):
        tile(kt, False)
        return carry

    lax.fori_loop(0, qi, body, 0)
    tile(qi, True)
    for h, sl in enumerate(heads):
        o_ref[0, :, sl] = _online_result(l_s.at[h], acc_s.at[h]).astype(o_ref.dtype)


def _attn_b_prompt(proj3, cum_h, *, tq=512):
    b, t, _ = proj3.shape
    nq = t // tq
    ck = cum_h.reshape(b, H_B, nq, 1, tq)
    hp = B_HEADS_PER_STEP
    w = hp * HEAD_DIM
    return pl.pallas_call(
        functools.partial(_attn_b_prompt_kernel, tq=tq),
        grid=(b, H_B // hp, nq),
        in_specs=[
            pl.BlockSpec((1, tq, w), lambda i, h, j: (i, j, EV_BQ // w + h)),
            pl.BlockSpec((1, t, w), lambda i, h, j: (i, 0, EV_BK // w + h)),
            pl.BlockSpec((1, t, w), lambda i, h, j: (i, 0, EV_BV // w + h)),
            pl.BlockSpec((1, hp, nq, 1, tq), lambda i, h, j: (i, h, 0, 0, 0)),
        ],
        out_specs=pl.BlockSpec((1, tq, w), lambda i, h, j: (i, j, h)),
        out_shape=jax.ShapeDtypeStruct((b, t, H_B * HEAD_DIM), BF16),
        scratch_shapes=[pltpu.VMEM((hp, tq, LANE), F32), pltpu.VMEM((hp, tq, LANE), F32),
                        pltpu.VMEM((hp, tq, HEAD_DIM), F32)],
        compiler_params=_cparams("parallel", "parallel", "parallel"),
        name="attn_b_prompt",
    )(proj3, proj3, proj3, ck)


def _attn_b_step_kernel(q_ref, kc_ref, vc_ref, kn_ref, vn_ref, ckc_ref, ckn_ref, o_ref, m_s, l_s, acc_s,
                        *, nkc, t, tkc):
    kb = pl.program_id(1)

    @pl.when(kb == 0)
    def _():
        _online_init(m_s, l_s, acc_s)

    def q_head(h):
        return (q_ref[0, :, h * HEAD_DIM:(h + 1) * HEAD_DIM] * SCALE).astype(BF16)

    @pl.when(kb < nkc)
    def _():
        ck = ckc_ref[0, 0]
        logits = [_dot_nt(q_head(h), kc_ref[0, pl.ds(h, tkc, stride=H_B), :].astype(BF16)) - ck[h:h + 1, :]
                  for h in range(H_B)]
        stats = [_online_stats(logits[h], m_s.at[h], l_s.at[h]) for h in range(H_B)]
        for h in range(H_B):
            alpha, p_all = stats[h]
            v = vc_ref[0, pl.ds(h, tkc, stride=H_B), :].astype(BF16)
            acc_s[h] = alpha * acc_s[h] + _dot(p_all, v)

    @pl.when(kb == nkc)
    def _():
        ckn = ckn_ref[0]
        heads = [slice(h * HEAD_DIM, (h + 1) * HEAD_DIM) for h in range(H_B)]
        logits = [_causal_mask(_dot_nt(q_head(h), _pad_keys(kn_ref[0, :, heads[h]], LANE).astype(BF16))
                               - ckn[h:h + 1, :]) for h in range(H_B)]
        stats = [_online_stats(logits[h], m_s.at[h], l_s.at[h]) for h in range(H_B)]
        for h in range(H_B):
            alpha, p_all = stats[h]
            v = _pad_keys(vn_ref[0, :, heads[h]], LANE).astype(BF16)
            acc_s[h] = alpha * acc_s[h] + _dot(p_all, v)
            o_ref[0, :, heads[h]] = _online_result(l_s.at[h], acc_s.at[h]).astype(o_ref.dtype)


def _attn_b_step(proj3, cache_k, cache_v, cum_h, *, tkc=1024):
    b, t, _ = proj3.shape
    p = cache_k.shape[1] // H_B
    nkc = p // tkc
    w = H_B * HEAD_DIM
    ck_cache = cum_h[:, :, :p].reshape(b, H_B, nkc, tkc).transpose(0, 2, 1, 3)
    ck_new = jnp.pad(cum_h[:, :, p:], ((0, 0), (0, 0), (0, LANE - t)))
    last = nkc - 1
    return pl.pallas_call(
        functools.partial(_attn_b_step_kernel, nkc=nkc, t=t, tkc=tkc),
        grid=(b, nkc + 1),
        in_specs=[
            pl.BlockSpec((1, t, w), lambda i, j: (i, 0, EV_BQ // w)),
            pl.BlockSpec((1, tkc * H_B, HEAD_DIM), lambda i, j: (i, jnp.minimum(j, last), 0)),
            pl.BlockSpec((1, tkc * H_B, HEAD_DIM), lambda i, j: (i, jnp.minimum(j, last), 0)),
            pl.BlockSpec((1, t, w), lambda i, j: (i, 0, EV_BK // w)),
            pl.BlockSpec((1, t, w), lambda i, j: (i, 0, EV_BV // w)),
            pl.BlockSpec((1, 1, H_B, tkc), lambda i, j: (i, jnp.minimum(j, last), 0, 0)),
            pl.BlockSpec((1, H_B, LANE), lambda i, j: (i, 0, 0)),
        ],
        out_specs=pl.BlockSpec((1, t, w), lambda i, j: (i, 0, 0)),
        out_shape=jax.ShapeDtypeStruct((b, t, w), BF16),
        scratch_shapes=[pltpu.VMEM((H_B, t, LANE), F32), pltpu.VMEM((H_B, t, LANE), F32),
                        pltpu.VMEM((H_B, t, HEAD_DIM), F32)],
        compiler_params=_cparams("parallel", "arbitrary"),
        name="attn_b_step",
    )(proj3, cache_k, cache_v, proj3, proj3, ck_cache, ck_new)


def _tri_ones():
    r = lax.broadcasted_iota(I32, (2 * LANE, 2 * LANE), 0) & (LANE - 1)
    c = lax.broadcasted_iota(I32, (2 * LANE, 2 * LANE), 1)
    return jnp.where((r > c) | (c >= LANE), 1.0, 0.0).astype(BF16)


def _stick_scores(z, row_minus_col, tri_ones):
    rc_blocks = None if row_minus_col is None else _lane_blocks(row_minus_col)
    out = []
    for j, zb in enumerate(_lane_blocks(z)):
        tail = jnp.log(1.0 + jnp.exp(-jnp.abs(zb)))
        log_beta = jnp.minimum(zb, 0.0) - tail
        log_keep = -jnp.maximum(zb, 0.0) - tail
        strict = None
        if rc_blocks is not None:
            strict = rc_blocks[j] > 0
            log_keep = jnp.where(strict, log_keep, 0.0)
        hi = log_keep.astype(BF16)
        lo = (log_keep - hi.astype(F32)).astype(BF16)
        out.append((log_beta, strict, _dot(jnp.concatenate([hi, lo], axis=1), tri_ones)))
    return out


def _stick_weights(blocks, run):
    ws = [None] * len(blocks)
    for j in reversed(range(len(blocks))):
        log_beta, strict, sums = blocks[j]
        w = jnp.exp(log_beta + sums[:, :LANE] + run)
        if strict is not None:
            w = jnp.where(strict, w, 0.0)
        ws[j] = w.astype(BF16)
        run = run + sums[:, LANE:]
    return (jnp.concatenate(ws, axis=1) if len(ws) > 1 else ws[0]), run


def _stick_tile(q_bf16, k_bf16, v_bf16, run, tri_ones, row_minus_col):
    w_all, run = _stick_weights(_stick_scores(_dot_nt(q_bf16, k_bf16), row_minus_col, tri_ones), run)
    return _dot(w_all, v_bf16), run


def _attn_c_prompt_kernel(q_ref, k_ref, v_ref, o_ref, run_s, acc_s, *, tq):
    qi = pl.program_id(2)
    q = (q_ref[0] * SCALE).astype(BF16)
    tri_ones = _tri_ones()

    def load(kt):
        off = pl.multiple_of(kt * tq, tq)
        return k_ref[0, pl.ds(off, tq), :].astype(BF16), v_ref[0, pl.ds(off, tq), :].astype(BF16)

    rc = lax.broadcasted_iota(I32, (tq, tq), 0) - lax.broadcasted_iota(I32, (tq, tq), 1)
    k, v = load(qi)
    out, run = _stick_tile(q, k, v, jnp.zeros((tq, LANE), F32), tri_ones, rc)
    acc_s[...] = out
    run_s[...] = run

    def body(i, carry):
        k, v = load(qi - 1 - i)
        out, run = _stick_tile(q, k, v, run_s[...], tri_ones, None)
        acc_s[...] += out
        run_s[...] = run
        return carry

    lax.fori_loop(0, qi, body, 0)
    o_ref[0] = acc_s[...].astype(o_ref.dtype)


def _attn_c_prompt(proj3, *, tq=512):
    b, t, _ = proj3.shape
    nq = t // tq
    qc, kc, vc = OD_CQ // HEAD_DIM, OD_CK // HEAD_DIM, OD_CV // HEAD_DIM
    return pl.pallas_call(
        functools.partial(_attn_c_prompt_kernel, tq=tq),
        grid=(b, H_C, nq),
        in_specs=[
            pl.BlockSpec((1, tq, HEAD_DIM), lambda i, h, j: (i, j, qc + h)),
            pl.BlockSpec((1, t, HEAD_DIM), lambda i, h, j: (i, 0, kc + h)),
            pl.BlockSpec((1, t, HEAD_DIM), lambda i, h, j: (i, 0, vc + h)),
        ],
        out_specs=pl.BlockSpec((1, tq, HEAD_DIM), lambda i, h, j: (i, j, h)),
        out_shape=jax.ShapeDtypeStruct((b, t, H_C * HEAD_DIM), BF16),
        scratch_shapes=[pltpu.VMEM((tq, LANE), F32), pltpu.VMEM((tq, HEAD_DIM), F32)],
        compiler_params=_cparams("parallel", "parallel", "parallel"),
        name="attn_c_prompt",
    )(proj3, proj3, proj3)


def _attn_c_step_kernel(q_ref, kc_ref, vc_ref, kn_ref, vn_ref, o_ref, run_s, acc_s, *, nkc, t, tkc, sub):
    kb = pl.program_id(1)
    tri_ones = _tri_ones()

    def q_head(h):
        return (q_ref[0, :, h * HEAD_DIM:(h + 1) * HEAD_DIM] * SCALE).astype(BF16)

    @pl.when(kb == 0)
    def _():
        rc = lax.broadcasted_iota(I32, (t, LANE), 0) - lax.broadcasted_iota(I32, (t, LANE), 1)
        for h in range(H_C):
            sl = slice(h * HEAD_DIM, (h + 1) * HEAD_DIM)
            k = _pad_keys(kn_ref[0, :, sl], LANE).astype(BF16)
            v = _pad_keys(vn_ref[0, :, sl], LANE).astype(BF16)
            out, run = _stick_tile(q_head(h), k, v, jnp.zeros((t, LANE), F32), tri_ones, rc)
            acc_s[h] = out
            run_s[h] = run

    @pl.when(kb > 0)
    def _():
        def body(i, carry):
            row0 = (tkc - sub - i * sub) * H_C
            zs = [_dot_nt(q_head(h), kc_ref[0, pl.ds(row0 + h, sub, stride=H_C), :].astype(BF16))
                  for h in range(H_C)]
            scored = [_stick_scores(z, None, tri_ones) for z in zs]
            for h in range(H_C):
                w_all, run = _stick_weights(scored[h], run_s[h])
                v = vc_ref[0, pl.ds(row0 + h, sub, stride=H_C), :].astype(BF16)
                acc_s[h] += _dot(w_all, v)
                run_s[h] = run
            return carry

        lax.fori_loop(0, tkc // sub, body, 0)

    @pl.when(kb == nkc)
    def _():
        for h in range(H_C):
            o_ref[0, :, h * HEAD_DIM:(h + 1) * HEAD_DIM] = acc_s[h].astype(o_ref.dtype)


def _attn_c_step(proj3, cache_k, cache_v, *, tkc=1024, sub=1024):
    b, t, _ = proj3.shape
    p = cache_k.shape[1] // H_C
    nkc = p // tkc
    w = H_C * HEAD_DIM

    def cache_idx(i, j):
        return (i, jnp.clip(nkc - j, 0, nkc - 1), 0)

    return pl.pallas_call(
        functools.partial(_attn_c_step_kernel, nkc=nkc, t=t, tkc=tkc, sub=sub),
        grid=(b, nkc + 1),
        in_specs=[
            pl.BlockSpec((1, t, w), lambda i, j: (i, 0, OD_CQ // w)),
            pl.BlockSpec((1, tkc * H_C, HEAD_DIM), cache_idx),
            pl.BlockSpec((1, tkc * H_C, HEAD_DIM), cache_idx),
            pl.BlockSpec((1, t, w), lambda i, j: (i, 0, OD_CK // w)),
            pl.BlockSpec((1, t, w), lambda i, j: (i, 0, OD_CV // w)),
        ],
        out_specs=pl.BlockSpec((1, t, w), lambda i, j: (i, 0, 0)),
        out_shape=jax.ShapeDtypeStruct((b, t, w), BF16),
        scratch_shapes=[pltpu.VMEM((H_C, t, LANE), F32), pltpu.VMEM((H_C, t, HEAD_DIM), F32)],
        compiler_params=_cparams("parallel", "arbitrary"),
        name="attn_c_step",
    )(proj3, cache_k, cache_v, proj3, proj3)


def _attn_d_kernel(q_ref, kp_ref, vp_ref, ko_ref, vo_ref, bias_ref, o_ref, *, tq, q_base):
    qi = pl.program_id(2)
    q = (q_ref[0] * SCALE).astype(BF16)
    s_o = _dot_nt(q, ko_ref[0].astype(BF16)) + bias_ref[0, :, D_BAND_LEFT:D_BAND_LEFT + tq]
    m_o = jnp.max(s_o, axis=-1, keepdims=True)

    def with_left():
        s_p = _dot_nt(q, kp_ref[0].astype(BF16)) + bias_ref[0, :, 0:D_BAND_LEFT]
        m = jnp.maximum(jnp.max(s_p, axis=-1, keepdims=True), m_o)
        p_p = jnp.exp(s_p - m)
        p_o = jnp.exp(s_o - m)
        l = jnp.sum(p_p, axis=-1, keepdims=True) + jnp.sum(p_o, axis=-1, keepdims=True)
        out = _dot(p_p.astype(BF16), vp_ref[0].astype(BF16)) + _dot(p_o.astype(BF16), vo_ref[0].astype(BF16))
        o_ref[0] = (out / l).astype(o_ref.dtype)

    def own_only():
        p_o = jnp.exp(s_o - m_o)
        l = jnp.sum(p_o, axis=-1, keepdims=True)
        o_ref[0] = (_dot(p_o.astype(BF16), vo_ref[0].astype(BF16)) / l).astype(o_ref.dtype)

    if q_base >= D_BAND_LEFT:
        with_left()
    else:
        has_left = q_base + qi * tq >= D_BAND_LEFT
        pl.when(has_left)(with_left)
        pl.when(jnp.logical_not(has_left))(own_only)


def _attn_d(q_arr, q_col, kprev, kprev_col, vprev, vprev_col, kown, kown_col, vown, vown_col, bias,
            *, t, tq, q_base, prev_follows_own):
    b = q_arr.shape[0]
    nq = t // tq
    if prev_follows_own:
        def prev_idx(col):
            return lambda i, h, j: (i, jnp.maximum(j - 1, 0), col + h)
    else:
        def prev_idx(col):
            return lambda i, h, j: (i, 0, col + h)
    return pl.pallas_call(
        functools.partial(_attn_d_kernel, tq=tq, q_base=q_base),
        grid=(b, H_D, nq),
        in_specs=[
            pl.BlockSpec((1, tq, HEAD_DIM), lambda i, h, j: (i, j, q_col + h)),
            pl.BlockSpec((1, D_BAND_LEFT, HEAD_DIM), prev_idx(kprev_col)),
            pl.BlockSpec((1, D_BAND_LEFT, HEAD_DIM), prev_idx(vprev_col)),
            pl.BlockSpec((1, tq, HEAD_DIM), lambda i, h, j: (i, j, kown_col + h)),
            pl.BlockSpec((1, tq, HEAD_DIM), lambda i, h, j: (i, j, vown_col + h)),
            pl.BlockSpec((1, tq, D_BAND_LEFT + tq), lambda i, h, j: (h, 0, 0)),
        ],
        out_specs=pl.BlockSpec((1, tq, HEAD_DIM), lambda i, h, j: (i, j, h)),
        out_shape=jax.ShapeDtypeStruct((b, t, H_D * HEAD_DIM), BF16),
        compiler_params=_cparams("parallel", "parallel", "parallel"),
        name="attn_d",
    )(q_arr, kprev, vprev, kown, vown, bias)


def _attn_d_step_kernel(q_ref, kp_ref, vp_ref, kn_ref, vn_ref, bias_ref, o_ref, *, t):
    for h in range(H_D):
        sl = slice(h * HEAD_DIM, (h + 1) * HEAD_DIM)
        q = (q_ref[0, :, sl] * SCALE).astype(BF16)
        kp = kp_ref[0, pl.ds(h, D_BAND_LEFT, stride=H_D), :].astype(BF16)
        vp = vp_ref[0, pl.ds(h, D_BAND_LEFT, stride=H_D), :].astype(BF16)
        s_p = _dot_nt(q, kp) + bias_ref[h, :, 0:D_BAND_LEFT]
        s_o = _dot_nt(q, kn_ref[0, :, sl].astype(BF16)) + bias_ref[h, :, D_BAND_LEFT:D_BAND_LEFT + t]
        m = jnp.maximum(jnp.max(s_p, axis=-1, keepdims=True), jnp.max(s_o, axis=-1, keepdims=True))
        p_p = jnp.exp(s_p - m)
        p_o = jnp.exp(s_o - m)
        l = jnp.sum(p_p, axis=-1, keepdims=True) + jnp.sum(p_o, axis=-1, keepdims=True)
        out = _dot(p_p.astype(BF16), vp) + _dot(p_o.astype(BF16), vn_ref[0, :, sl].astype(BF16))
        o_ref[0, :, sl] = (out / l).astype(o_ref.dtype)


def _attn_d_step(proj3, cache_k, cache_v, bias):
    b, t, _ = proj3.shape
    w = H_D * HEAD_DIM
    return pl.pallas_call(
        functools.partial(_attn_d_step_kernel, t=t),
        grid=(b,),
        in_specs=[
            pl.BlockSpec((1, t, w), lambda i: (i, 0, OD_DQ // w)),
            pl.BlockSpec((1, D_BAND_LEFT * H_D, HEAD_DIM), lambda i: (i, 0, 0)),
            pl.BlockSpec((1, D_BAND_LEFT * H_D, HEAD_DIM), lambda i: (i, 0, 0)),
            pl.BlockSpec((1, t, w), lambda i: (i, 0, OD_DK // w)),
            pl.BlockSpec((1, t, w), lambda i: (i, 0, OD_DV // w)),
            pl.BlockSpec((H_D, t, D_BAND_LEFT + t), lambda i: (0, 0, 0)),
        ],
        out_specs=pl.BlockSpec((1, t, w), lambda i: (i, 0, 0)),
        out_shape=jax.ShapeDtypeStruct((b, t, w), BF16),
        compiler_params=_cparams("parallel"),
        name="attn_d_step",
    )(proj3, cache_k, cache_v, proj3, proj3, bias)


def _pad_rows(x, rows):
    return jnp.pad(x, ((0, 0), (0, rows - x.shape[1]), (0, 0)))


def _even_mixer(h2, b, t, past, w, tz_prompt, tz_step, tq_a, mm):
    proj = mm("in_even", h2, w["w_in"], mode="headnorm", gain=w["gain"], flag=w["flag"], n_norm_tiles=9,
              name="proj_even")
    proj3 = proj.reshape(b, t, EV_PAD)
    ak = proj3[:, :, EV_AK:EV_AK + KV_A * HEAD_DIM]
    av = proj3[:, :, EV_AV:EV_AV + KV_A * HEAD_DIM]
    ik = proj3[:, :, EV_MISC + MISC_IK:EV_MISC + MISC_IK + IDX_DIM]
    bk = proj3[:, :, EV_BK:EV_BK + H_B * HEAD_DIM]
    bv = proj3[:, :, EV_BV:EV_BV + H_B * HEAD_DIM]
    raw_h = proj3[:, :, EV_MISC + MISC_BF:EV_MISC + MISC_BF + H_B].transpose(0, 2, 1)
    iq_col = EV_IQ // (IDX_HEADS * IDX_DIM)
    misc_col = EV_MISC // LANE
    if past is None:
        logf_h, cum_h = _logf_cum(raw_h, w["f_bias"], None)
        mask = _select_mask(proj3, iq_col, proj3, misc_col, proj3, misc_col, t=t, tq=tq_a, lp=t, l_valid=t,
                            q_base=0, n_sel=min(TOPK_MAX, t // 4), tk_out=tq_a, causal_prefix=True)
        out_a = _attn_a_prompt(proj3, mask, tz_prompt, tq=tq_a)
        out_b = _attn_b_prompt(proj3, cum_h, tq=min(512, t))
    else:
        p_ak, p_av, p_ik, p_bk, p_bv, p_lf = past
        p = p_ak.shape[1]
        l_valid = p + t
        lp = p + LANE
        logf_h, cum_h = _logf_cum(raw_h, w["f_bias"], p_lf.transpose(0, 2, 1))
        ik_all = _pad_rows(jnp.concatenate([p_ik, ik], axis=1), lp)
        mask = _select_mask(proj3, iq_col, proj3, misc_col, ik_all, None, t=t, tq=t, lp=lp, l_valid=l_valid,
                            q_base=p, n_sel=min(TOPK_MAX, l_valid // 4), tk_out=lp, causal_prefix=False)
        out_a = _attn_a_step(proj3, p_ak.reshape(b, p * KV_A, HEAD_DIM), p_av.reshape(b, p * KV_A, HEAD_DIM),
                             mask, tz_step[:, :t])
        out_b = _attn_b_step(proj3, p_bk.reshape(b, p * H_B, HEAD_DIM), p_bv.reshape(b, p * H_B, HEAD_DIM), cum_h)
    mixed = [out_a.reshape(b * t, -1), out_b.reshape(b * t, -1)]
    state = (ak.reshape(b, t, KV_A, HEAD_DIM), av.reshape(b, t, KV_A, HEAD_DIM), ik,
             bk.reshape(b, t, H_B, HEAD_DIM), bv.reshape(b, t, H_B, HEAD_DIM), logf_h.transpose(0, 2, 1))
    return mixed, state


def _odd_mixer(h2, b, t, past, w, band, mm):
    proj = mm("in_odd", h2, w["w_in"], col_block=_odd_col_block, mode="headnorm", gain=w["gain"], flag=w["flag"],
              n_norm_tiles=4, name="proj_odd")
    proj3 = proj.reshape(b, t, OD_WIDTH)
    ck = proj3[:, :, OD_CK:OD_CK + H_C * HEAD_DIM]
    cv = proj3[:, :, OD_CV:OD_CV + H_C * HEAD_DIM]
    dk = proj3[:, :, OD_DK:OD_DK + H_D * HEAD_DIM]
    dv = proj3[:, :, OD_DV:OD_DV + H_D * HEAD_DIM]
    dq_col, dk_col, dv_col = OD_DQ // HEAD_DIM, OD_DK // HEAD_DIM, OD_DV // HEAD_DIM
    if past is None:
        out_c = _attn_c_prompt(proj3, tq=min(512, t))
        out_d = _attn_d(proj3, dq_col, proj3, dk_col, proj3, dv_col, proj3, dk_col, proj3, dv_col, band,
                        t=t, tq=D_BAND_LEFT, q_base=0, prev_follows_own=True)
        keep = min(D_BAND_LEFT, t)
        d_rows = (dk[:, t - keep:], dv[:, t - keep:])
    else:
        p_ck, p_cv, p_dk, p_dv = past
        p = p_ck.shape[1]
        out_c = _attn_c_step(proj3, p_ck.reshape(b, p * H_C, HEAD_DIM), p_cv.reshape(b, p * H_C, HEAD_DIM))
        wd = p_dk.shape[1]
        out_d = _attn_d_step(proj3, p_dk.reshape(b, wd * H_D, HEAD_DIM), p_dv.reshape(b, wd * H_D, HEAD_DIM),
                             band[:, :t, :D_BAND_LEFT + t])
        d_rows = (dk, dv)
    mixed = [out_c.reshape(b * t, -1), out_d.reshape(b * t, -1)]
    state = (ck.reshape(b, t, H_C, HEAD_DIM), cv.reshape(b, t, H_C, HEAD_DIM),
             d_rows[0].reshape(b, -1, H_D, HEAD_DIM), d_rows[1].reshape(b, -1, H_D, HEAD_DIM))
    return mixed, state


def _odd_col_block(j):
    return jnp.where(j < 4, j + 6, jnp.where(j < 10, j - 4, j))


def _trunk(x, caches_even, caches_odd, wts, bf16_weights):
    def mm(key, a, w_f32, col_block=None, **kw):
        if key in bf16_weights:
            w_bf16 = bf16_weights[key]
            if kw.get("tn", 512) == 512 and w_bf16.shape[1] % 1024 == 0:
                kw = dict(kw, tn=1024, n_norm_tiles=(kw.get("n_norm_tiles", 0) + 1) // 2)
            return _matmul(a, w_bf16, **kw)
        out, bf16_weights[key] = _matmul_cast(a, w_f32, col_block=col_block, **kw)
        return out

    b, t, d = x.shape
    x2 = x.reshape(b * t, d)
    h2 = _rmsnorm(x2, wts["attn_norm"][0])
    mixed, st_even = _even_mixer(h2, b, t, caches_even, wts["even"], wts["tz_prompt"], wts["tz_step"],
                                 wts["tq_a"], mm)
    x2 = mm("out_even", mixed, wts["even"]["w_out"], mode="residual", residual=x2, name="out_even")
    h2 = _rmsnorm(x2, wts["mlp_norm"][0])
    u = mm("up0", h2, wts["w_up"][0], mode="relu2", out_dtype=BF16, tn=1024, name="mlp_up0")
    x2 = _matmul(u, wts["w_down"][0], mode="residual", residual=x2, tm=512, tk=u.shape[1], name="mlp_down0")
    h2 = _rmsnorm(x2, wts["attn_norm"][1])
    mixed, st_odd = _odd_mixer(h2, b, t, caches_odd, wts["odd"], wts["band"], mm)
    x2 = mm("out_odd", mixed, wts["odd"]["w_out"], mode="residual", residual=x2, name="out_odd")
    h2 = _rmsnorm(x2, wts["mlp_norm"][1])
    u = mm("up1", h2, wts["w_up"][1], mode="relu2", out_dtype=BF16, tn=1024, name="mlp_up1")
    x2 = _matmul(u, wts["w_down"][1], mode="residual", residual=x2, tm=512, tk=u.shape[1], name="mlp_down1")
    return x2.reshape(b, t, d), tuple(s[None] for s in st_even), tuple(s[None] for s in st_odd)


def _even_weights(w_in, w_out, a_qn, a_kn, b_qn, b_kn, f_bias):
    aq, ak, av, iq, ik, iw, bq, bk, bv, bf = jnp.split(
        w_in, [1024, 1280, 1536, 2048, 2112, 2120, 3144, 4168, 5192], axis=1)
    d = w_in.shape[0]
    pad = jnp.zeros((d, EV_PAD - EV_WIDTH + LANE - (IDX_DIM + IDX_HEADS + H_B)), w_in.dtype)
    w_r = jnp.concatenate([aq, bq, bk, bv, ak, av, iq, ik, iw, bf, pad], axis=1)
    ones = jnp.ones((EV_PAD,), F32)
    gain = ones.at[EV_AQ:EV_AQ + 1024].set(jnp.tile(a_qn, H_A))
    gain = gain.at[EV_BQ:EV_BQ + 1024].set(jnp.tile(b_qn, H_B))
    gain = gain.at[EV_BK:EV_BK + 1024].set(jnp.tile(b_kn, H_B))
    gain = gain.at[EV_AK:EV_AK + 256].set(jnp.tile(a_kn, KV_A))
    flag = jnp.zeros((EV_PAD,), F32).at[0:EV_BV].set(1.0).at[EV_AK:EV_AV].set(1.0)
    return {"w_in": w_r, "gain": gain.reshape(1, -1), "flag": flag.reshape(1, -1), "w_out": w_out,
            "f_bias": f_bias}


def _odd_weights(w_in, w_out, d_qn, d_kn):
    gain = jnp.ones((OD_WIDTH,), F32)
    gain = gain.at[OD_DQ:OD_DQ + 1024].set(jnp.tile(d_qn, H_D)).at[OD_DK:OD_DK + 1024].set(jnp.tile(d_kn, H_D))
    flag = jnp.zeros((OD_WIDTH,), F32).at[0:OD_CQ].set(1.0)
    return {"w_in": w_in, "gain": gain.reshape(1, -1), "flag": flag.reshape(1, -1), "w_out": w_out}


def kernel(x_prompt, x_sample, cache_a_k, cache_a_v, cache_a_kidx, cache_b_k, cache_b_v, cache_b_logf,
           cache_c_k, cache_c_v, cache_d_k, cache_d_v, attn_norm, mlp_norm, w_in_even, w_out_even,
           w_in_odd, w_out_odd, a_q_norm, a_k_norm, b_q_norm, b_k_norm, forget_bias, t5_bias,
           d_q_norm, d_k_norm, d_rel_bias, w_up, w_down):
    tq_a = min(256, x_prompt.shape[1])
    wts = {
        "attn_norm": attn_norm,
        "mlp_norm": mlp_norm,
        "even": _even_weights(w_in_even[0], w_out_even[0], a_q_norm[0], a_k_norm[0], b_q_norm[0], b_k_norm[0],
                              forget_bias[0]),
        "odd": _odd_weights(w_in_odd[0], w_out_odd[0], d_q_norm[0], d_k_norm[0]),
        "w_up": w_up,
        "w_down": w_down.astype(BF16),
        "tq_a": tq_a,
        "tz_prompt": _t5_strip(t5_bias, tq_a),
        "tz_step": _t5_strip(t5_bias, LANE),
        "band": _band_bias(d_rel_bias[0]),
    }
    bf16_weights = {}
    y_p, ev_p, od_p = _trunk(x_prompt, None, None, wts, bf16_weights)
    caches_even = (cache_a_k[0], cache_a_v[0], cache_a_kidx[0], cache_b_k[0], cache_b_v[0], cache_b_logf[0])
    caches_odd = (cache_c_k[0], cache_c_v[0], cache_d_k[0], cache_d_v[0])
    y_s, ev_s, od_s = _trunk(x_sample, caches_even, caches_odd, wts, bf16_weights)
    return (y_p, y_s) + ev_p + od_p + ev_s + od_s
```

```python
import functools

import numpy as np
import jax
import jax.numpy as jnp
from jax import lax
from jax.experimental import pallas as pl
from jax.experimental.pallas import tpu as pltpu

F32 = jnp.float32
BF16 = jnp.bfloat16
I32 = jnp.int32

HEAD_DIM = 128
CHUNK = 64
CHUNK_SHIFT = 6
H_A = 8
KV_A = 2
REP_A = H_A // KV_A
H_B = 8
H_C = 8
H_D = 8
IDX_HEADS = 8
IDX_DIM = 64
TOPK_MAX = 256
T5_BUCKETS = 32
D_LEFT_CHUNKS = 8
D_BAND_LEFT = D_LEFT_CHUNKS * CHUNK
D_REL_CLIP = 128
EPS = 1e-6
NEG_INF = -1e30
SCALE = HEAD_DIM ** -0.5
LANE = 128
MXU_WIDTH = 256
INT32_MIN = -(2 ** 31)
NEG_INF_KEY = int(np.float32(NEG_INF).view(np.int32)) ^ 0x7FFFFFFF

VMEM_LIMIT_BYTES = 56 * 1024 * 1024

EV_AQ, EV_BQ, EV_BK, EV_BV, EV_AK, EV_AV, EV_IQ, EV_MISC = 0, 1024, 2048, 3072, 4096, 4352, 4608, 5120
EV_WIDTH = 5248
EV_PAD = 5632
MISC_IK, MISC_IW, MISC_BF = 0, 64, 72
OD_DQ, OD_DK, OD_CQ, OD_CK, OD_CV, OD_DV = 0, 1024, 2048, 3072, 4096, 5120
OD_WIDTH = 6144


def _cparams(*sem):
    return pltpu.CompilerParams(dimension_semantics=sem, vmem_limit_bytes=VMEM_LIMIT_BYTES)


def _dot_nt(a, b):
    return lax.dot_general(a, b, (((1,), (1,)), ((), ())), preferred_element_type=F32)


def _dot(a, b):
    return jnp.dot(a, b, preferred_element_type=F32)


def _lane_blocks(x):
    return [x[..., j * LANE:(j + 1) * LANE] for j in range(x.shape[-1] // LANE)]


def _pad_keys(x, rows):
    return jnp.concatenate([x, jnp.zeros((rows - x.shape[0], x.shape[1]), x.dtype)], axis=0)


def _rmsnorm_kernel(x_ref, g_ref, o_ref):
    x = x_ref[...]
    ms = jnp.mean(x * x, axis=-1, keepdims=True)
    o_ref[...] = (x * lax.rsqrt(ms + EPS) * g_ref[...]).astype(o_ref.dtype)


def _rmsnorm(x2, gain):
    m, d = x2.shape
    tm = min(512, m)
    return pl.pallas_call(
        _rmsnorm_kernel,
        grid=(m // tm,),
        in_specs=[pl.BlockSpec((tm, d), lambda i: (i, 0)), pl.BlockSpec((1, d), lambda i: (0, 0))],
        out_specs=pl.BlockSpec((tm, d), lambda i: (i, 0)),
        out_shape=jax.ShapeDtypeStruct((m, d), BF16),
        compiler_params=_cparams("parallel"),
        name="rmsnorm",
    )(x2, gain.reshape(1, d))


def _mm_kernel(*refs, n_a, nk, mode, n_norm_tiles, cast_b):
    a_refs, b_ref = refs[:n_a], refs[n_a]
    refs = refs[n_a - 1:]
    if mode == "headnorm":
        gain_ref, flag_ref, o_ref = refs[2], refs[3], refs[4]
        rest = refs[5:]
    elif mode == "residual":
        res_ref, o_ref = refs[2], refs[3]
        rest = refs[4:]
    else:
        o_ref = refs[2]
        rest = refs[3:]
    n_axis = 0 if cast_b else 1
    if cast_b:
        wb_ref = rest[0]

        @pl.when(pl.program_id(1) == 0)
        def _():
            wb_ref[...] = b_ref[...].astype(BF16)

        b_ref = wb_ref

    tn = o_ref.shape[1]

    def product(c0=0, c1=tn):
        row, acc = 0, None
        for a_ref in a_refs:
            kp = a_ref.shape[1]
            part = _dot(a_ref[...], b_ref[row:row + kp, c0:c1])
            acc = part if acc is None else acc + part
            row += kp
        return acc

    def headnorm(acc, c0):
        for g in range(acc.shape[1] // HEAD_DIM):
            sl = slice(c0 + g * HEAD_DIM, c0 + (g + 1) * HEAD_DIM)
            blk = acc[:, g * HEAD_DIM:(g + 1) * HEAD_DIM]
            ms = jnp.mean(blk * blk, axis=-1, keepdims=True)
            normed = blk * lax.rsqrt(ms + EPS) * gain_ref[:, sl]
            o_ref[:, sl] = jnp.where(flag_ref[:, sl] > 0.0, normed, blk)

    def epilogue(acc, c0=0):
        sl = slice(c0, c0 + acc.shape[1])
        if mode == "headnorm":
            headnorm(acc, c0)
        elif mode == "relu2":
            r = jnp.maximum(acc, 0.0)
            o_ref[:, sl] = (r * r).astype(o_ref.dtype)
        elif mode == "residual":
            o_ref[:, sl] = res_ref[:, sl] + acc
        else:
            o_ref[:, sl] = acc.astype(o_ref.dtype)

    def chunked():
        starts = list(range(0, tn, MXU_WIDTH))
        accs = [product(c0, c0 + MXU_WIDTH) for c0 in starts]
        for c0, acc in zip(starts, accs):
            epilogue(acc, c0)

    if nk == 1 and mode == "headnorm":
        j = pl.program_id(n_axis)
        pl.when(j < n_norm_tiles)(chunked)

        @pl.when(j >= n_norm_tiles)
        def _():
            o_ref[...] = product()
    elif nk == 1:
        chunked()
    else:
        acc_ref = rest[0]
        k = pl.program_id(2)

        @pl.when(k == 0)
        def _():
            acc_ref[...] = jnp.zeros_like(acc_ref)

        acc_ref[...] += product()

        @pl.when(k == nk - 1)
        def _():
            epilogue(acc_ref[...])


def _matmul(a, b, *, mode="plain", out_dtype=F32, gain=None, flag=None, n_norm_tiles=0, residual=None,
            tm=1024, tn=512, tk=2048, name="matmul"):
    a_parts = list(a) if isinstance(a, (list, tuple)) else [a]
    m = a_parts[0].shape[0]
    kdim, n = b.shape
    tm = tm if m % tm == 0 else min(512, m)
    if len(a_parts) > 1:
        tk = kdim
        in_specs = [pl.BlockSpec((tm, part.shape[1]), lambda i, j, k: (i, 0)) for part in a_parts]
    else:
        in_specs = [pl.BlockSpec((tm, tk), lambda i, j, k: (i, k))]
    nk = kdim // tk
    in_specs.append(pl.BlockSpec((tk, tn), lambda i, j, k: (k, j)))
    args = a_parts + [b]
    if mode == "headnorm":
        in_specs += [pl.BlockSpec((1, tn), lambda i, j, k: (0, j)), pl.BlockSpec((1, tn), lambda i, j, k: (0, j))]
        args += [gain, flag]
    elif mode == "residual":
        in_specs += [pl.BlockSpec((tm, tn), lambda i, j, k: (i, j))]
        args += [residual]
    scratch = [pltpu.VMEM((tm, tn), F32)] if nk > 1 else []
    return pl.pallas_call(
        functools.partial(_mm_kernel, n_a=len(a_parts), nk=nk, mode=mode, n_norm_tiles=n_norm_tiles,
                          cast_b=False),
        grid=(m // tm, n // tn, nk),
        in_specs=in_specs,
        out_specs=pl.BlockSpec((tm, tn), lambda i, j, k: (i, j)),
        out_shape=jax.ShapeDtypeStruct((m, n), out_dtype),
        scratch_shapes=scratch,
        compiler_params=_cparams("parallel", "parallel", "arbitrary"),
        name=name,
    )(*args)


def _matmul_cast(a, b_f32, *, col_block=None, n=None, mode="plain", out_dtype=F32, gain=None, flag=None,
                 n_norm_tiles=0, residual=None, tm=1024, tn=512, name="matmul"):
    a_parts = list(a) if isinstance(a, (list, tuple)) else [a]
    m = a_parts[0].shape[0]
    kdim = b_f32.shape[0]
    n = b_f32.shape[1] if n is None else n
    tm = tm if m % tm == 0 else min(512, m)
    src = (lambda j: j) if col_block is None else col_block
    in_specs = [pl.BlockSpec((tm, part.shape[1]), lambda j, i: (i, 0)) for part in a_parts]
    in_specs.append(pl.BlockSpec((kdim, tn), lambda j, i: (0, src(j))))
    args = a_parts + [b_f32]
    if mode == "headnorm":
        in_specs += [pl.BlockSpec((1, tn), lambda j, i: (0, j)), pl.BlockSpec((1, tn), lambda j, i: (0, j))]
        args += [gain, flag]
    elif mode == "residual":
        in_specs += [pl.BlockSpec((tm, tn), lambda j, i: (i, j))]
        args += [residual]
    return pl.pallas_call(
        functools.partial(_mm_kernel, n_a=len(a_parts), nk=1, mode=mode, n_norm_tiles=n_norm_tiles, cast_b=True),
        grid=(n // tn, m // tm),
        in_specs=in_specs,
        out_specs=[pl.BlockSpec((tm, tn), lambda j, i: (i, j)), pl.BlockSpec((kdim, tn), lambda j, i: (0, j))],
        out_shape=[jax.ShapeDtypeStruct((m, n), out_dtype), jax.ShapeDtypeStruct((kdim, n), BF16)],
        compiler_params=_cparams("parallel", "arbitrary"),
        name=name,
    )(*args)


def _cumsum_block(x, carry):
    n = x.shape[1]
    r = lax.broadcasted_iota(I32, (n, n), 0)
    c = lax.broadcasted_iota(I32, (n, n), 1)
    tri = jnp.where(r <= c, 1.0, 0.0).astype(BF16)
    hi = x.astype(BF16)
    r1 = x - hi.astype(F32)
    mid = r1.astype(BF16)
    lo = (r1 - mid.astype(F32)).astype(BF16)
    cs = _dot(hi, tri) + _dot(mid, tri) + _dot(lo, tri) + carry
    return cs, cs[:, n - 1:n]


def _logf_cum_kernel(*refs, p_len, t_len, blk):
    if p_len:
        raw_ref, fb_ref, past_ref, lf_ref, cum_ref = refs
    else:
        raw_ref, fb_ref, lf_ref, cum_ref = refs
    x = raw_ref[0] + fb_ref[...]
    lf = jnp.minimum(x, 0.0) - jnp.log1p(jnp.exp(-jnp.abs(x)))
    lf_ref[0] = lf
    carry = jnp.zeros((H_B, 1), F32)
    for s in range(0, p_len, blk):
        cs, carry = _cumsum_block(past_ref[0, :, s:s + blk], carry)
        cum_ref[0, :, s:s + blk] = cs
    nb = min(blk, t_len)
    for s in range(0, t_len, nb):
        cs, carry = _cumsum_block(lf[:, s:s + nb], carry)
        cum_ref[0, :, p_len + s:p_len + s + nb] = cs


def _logf_cum(raw_h, fbias, past_h):
    b, h, t = raw_h.shape
    p = 0 if past_h is None else past_h.shape[2]
    in_specs = [pl.BlockSpec((1, h, t), lambda i: (i, 0, 0)), pl.BlockSpec((h, 1), lambda i: (0, 0))]
    args = [raw_h, fbias.reshape(h, 1)]
    if p:
        in_specs.append(pl.BlockSpec((1, h, p), lambda i: (i, 0, 0)))
        args.append(past_h)
    return pl.pallas_call(
        functools.partial(_logf_cum_kernel, p_len=p, t_len=t, blk=512),
        grid=(b,),
        in_specs=in_specs,
        out_specs=[pl.BlockSpec((1, h, t), lambda i: (i, 0, 0)), pl.BlockSpec((1, h, p + t), lambda i: (i, 0, 0))],
        out_shape=[jax.ShapeDtypeStruct((b, h, t), F32), jax.ShapeDtypeStruct((b, h, p + t), F32)],
        compiler_params=_cparams("parallel"),
        name="logf_cum",
    )(*args)


def _toeplitz(row_vals, rows, cols):
    w = row_vals.shape[1]
    full = jnp.broadcast_to(row_vals[0:1, :], (rows, w))
    return pltpu.roll(full, 0, 1, stride=1, stride_axis=0)[:, :cols]


def _t5_strip_kernel(tab_ref, o_ref, *, tile):
    h = pl.program_id(0)
    width = 4 * tile
    m = lax.broadcasted_iota(I32, (8, width), 1)
    d = jnp.where(m < width // 2, m, m - width)
    rel = d - tile
    n = jnp.abs(rel)
    large = jnp.full_like(n, 8)
    for thr in (12, 16, 23, 32, 46, 64, 91):
        large = large + jnp.where(n >= thr, 1, 0)
    bucket = jnp.where(rel > 0, T5_BUCKETS // 2, 0) + jnp.where(n < 8, n, large)
    vals = jnp.zeros((8, width), F32)
    for bkt in range(T5_BUCKETS):
        vals = jnp.where(bucket == bkt, tab_ref[bkt, h], vals)
    vals = vals - tab_ref[T5_BUCKETS // 2 - 1, h]
    o_ref[0] = _toeplitz(vals, tile, 2 * tile)


def _t5_strip(t5_table, tile):
    return pl.pallas_call(
        functools.partial(_t5_strip_kernel, tile=tile),
        grid=(H_A,),
        in_specs=[pl.BlockSpec(memory_space=pltpu.SMEM)],
        out_specs=pl.BlockSpec((1, tile, 2 * tile), lambda h: (h, 0, 0)),
        out_shape=jax.ShapeDtypeStruct((H_A, tile, 2 * tile), F32),
        compiler_params=_cparams("parallel"),
        name="t5_strip",
    )(t5_table)


def _band_bias_kernel(tab_ref, o_ref, *, rows, cols, width):
    h = pl.program_id(0)
    m = lax.broadcasted_iota(I32, (8, width), 1)
    d = jnp.where(m < width // 2, m, m - width)
    idx = jnp.clip(d - D_BAND_LEFT, -D_REL_CLIP, D_REL_CLIP) + D_REL_CLIP
    vals = jnp.zeros((8, width), F32)
    for r in range(2 * D_REL_CLIP + 1):
        vals = jnp.where(idx == r, tab_ref[r, h], vals)
    vals = vals - tab_ref[0, h]
    i_chunk = lax.broadcasted_iota(I32, (rows, cols), 0) >> CHUNK_SHIFT
    c_chunk = lax.broadcasted_iota(I32, (rows, cols), 1) >> CHUNK_SHIFT
    visible = (c_chunk >= i_chunk) & (c_chunk <= i_chunk + D_LEFT_CHUNKS)
    o_ref[0] = jnp.where(visible, _toeplitz(vals, rows, cols), NEG_INF)


def _band_bias(rel_table):
    rows, cols, width = D_BAND_LEFT, 2 * D_BAND_LEFT, 4 * D_BAND_LEFT
    return pl.pallas_call(
        functools.partial(_band_bias_kernel, rows=rows, cols=cols, width=width),
        grid=(H_D,),
        in_specs=[pl.BlockSpec(memory_space=pltpu.SMEM)],
        out_specs=pl.BlockSpec((1, rows, cols), lambda h: (h, 0, 0)),
        out_shape=jax.ShapeDtypeStruct((H_D, rows, cols), F32),
        compiler_params=_cparams("parallel"),
        name="band_bias",
    )(rel_table)


def _select_kernel(*refs, n_g, sb, tq, nq, lp, l_valid, q_base, n_sel, tk_out, causal_prefix):
    iq_refs, misc_refs = refs[:n_g], refs[n_g:2 * n_g]
    kk_ref, o_ref, key_ref = refs[2 * n_g:]
    step = pl.program_id(1)
    rows = sb * tq
    kf = float(n_sel)
    gs = range(n_g)

    def count(pred):
        return jnp.sum(jnp.where(pred, 1.0, 0.0), axis=-1, keepdims=True)

    def run(tiles):
        pfx = [p for _, p in tiles]
        adm, k_pos = [], []
        for g, (q_tile, p) in enumerate(tiles):
            scores = []
            for s in range(sb):
                iq = iq_refs[g][s]
                iw = misc_refs[g][s][:, MISC_IW:MISC_IW + IDX_HEADS]
                ik = kk_ref[s, 0:p, 0:IDX_DIM].astype(BF16)
                score = jnp.zeros((tq, p), F32)
                for h in range(IDX_HEADS):
                    dots = _dot_nt(iq[:, h * IDX_DIM:(h + 1) * IDX_DIM].astype(BF16), ik)
                    score = score + iw[:, h:h + 1] * jnp.maximum(dots, 0.0)
                scores.append(score)
            score = jnp.concatenate(scores, axis=0) if sb > 1 else scores[0]
            row_in_tile = jnp.concatenate([lax.broadcasted_iota(I32, (tq, 1), 0)] * sb, axis=0)
            q_pos = q_base + q_tile * tq + row_in_tile
            kp = lax.broadcasted_iota(I32, (1, p), 1)
            ok = ((kp >> CHUNK_SHIFT) <= (q_pos >> CHUNK_SHIFT)) & (kp < l_valid)
            bits = lax.bitcast_convert_type(jnp.where(ok, score, NEG_INF), I32)
            key_ref[g, :, 0:p] = jnp.where(bits < 0, bits ^ jnp.int32(0x7FFFFFFF), bits)
            adm.append(ok)
            k_pos.append(kp)

        def keys(g):
            return key_ref[g, :, 0:pfx[g]]

        t0 = tuple(jnp.where(count(keys(g) >= 0) >= kf, jnp.int32(0), jnp.int32(INT32_MIN)) for g in gs)

        def thr_body(i, ts):
            bit = jnp.int32(1) << (30 - i)
            return tuple(jnp.where(count(keys(g) >= ts[g] + bit) >= kf, ts[g] + bit, ts[g]) for g in gs)

        thr = lax.fori_loop(0, 31, thr_body, t0)
        need = [kf - count(keys(g) > thr[g]) for g in gs]
        tied = [(count(keys(g) == thr[g]) != need[g]) & (thr[g] > NEG_INF_KEY) for g in gs]
        any_tied = sum(jnp.max(jnp.where(tied[g], 1.0, 0.0)) for g in gs) > 0.5
        nbits = max(max(pfx) - 1, 1).bit_length()

        def tie_search():
            def tie_body(i, jbs):
                bit = jnp.int32(1) << (nbits - 1 - i)
                return tuple(jnp.where(count((keys(g) == thr[g]) & (k_pos[g] < jbs[g] + bit)) <= need[g],
                                       jbs[g] + bit, jbs[g]) for g in gs)

            return lax.fori_loop(0, nbits, tie_body, tuple(jnp.zeros((rows, 1), I32) for _ in gs))

        jbound = lax.cond(any_tied, tie_search, lambda: tuple(jnp.full((rows, 1), pfx[g], I32) for g in gs))
        for g in gs:
            key = keys(g)
            sel = ((key > thr[g]) | ((key == thr[g]) & (k_pos[g] < jbound[g]))) & adm[g]
            madd = jnp.where(sel, 0.0, NEG_INF)
            for s in range(sb):
                for kt in range(lp // tk_out):
                    if (kt + 1) * tk_out <= pfx[g]:
                        o_ref[s, g, 0, kt] = madd[s * tq:(s + 1) * tq, kt * tk_out:(kt + 1) * tk_out]
                    else:
                        o_ref[s, g, 0, kt] = jnp.full((tq, tk_out), NEG_INF, F32)

    if not causal_prefix:
        run([(step, lp)] * n_g)
    elif n_g == 2:
        for j in range(nq // 2):
            pl.when(step == j)(functools.partial(run, [(j, (j + 1) * tq), (nq - 1 - j, (nq - j) * tq)]))
    else:
        for j in range(nq):
            pl.when(step == j)(functools.partial(run, [(j, (j + 1) * tq)]))


def _select_mask(iq_arr, iq_col, misc_arr, misc_col, kk_arr, kk_col, *, t, tq, lp, l_valid, q_base, n_sel,
                 tk_out, causal_prefix):
    b = iq_arr.shape[0]
    nq = t // tq
    wk = kk_arr.shape[2] if kk_col is None else LANE
    kcol = 0 if kk_col is None else kk_col
    sb = 4 if (nq == 1 and tq <= 64 and b % 4 == 0) else 1
    n_g = 2 if (causal_prefix and nq % 2 == 0) else 1
    steps = nq // n_g
    tile_of = [lambda j: j, lambda j: nq - 1 - j]
    nkt = lp // tk_out
    in_specs = [pl.BlockSpec((sb, tq, IDX_HEADS * IDX_DIM), lambda i, j, g=g: (i, tile_of[g](j), iq_col))
                for g in range(n_g)]
    in_specs += [pl.BlockSpec((sb, tq, LANE), lambda i, j, g=g: (i, tile_of[g](j), misc_col)) for g in range(n_g)]
    in_specs.append(pl.BlockSpec((sb, lp, wk), lambda i, j: (i, 0, kcol)))
    return pl.pallas_call(
        functools.partial(_select_kernel, n_g=n_g, sb=sb, tq=tq, nq=nq, lp=lp, l_valid=l_valid, q_base=q_base,
                          n_sel=n_sel, tk_out=tk_out, causal_prefix=causal_prefix),
        grid=(b // sb, steps),
        in_specs=in_specs,
        out_specs=pl.BlockSpec((sb, n_g, 1, nkt, tq, tk_out), lambda i, j: (i, 0, j, 0, 0, 0)),
        out_shape=jax.ShapeDtypeStruct((b, n_g, steps, nkt, tq, tk_out), F32),
        scratch_shapes=[pltpu.VMEM((n_g, sb * tq, lp), I32)],
        compiler_params=_cparams("parallel", "parallel"),
        name="a_select",
    )(*([iq_arr] * n_g), *([misc_arr] * n_g), kk_arr)


def _mask_tile_index(q_tile, nq, n_g):
    if n_g == 1:
        return 0, q_tile
    upper = q_tile >= nq // 2
    return jnp.where(upper, 1, 0), jnp.where(upper, nq - 1 - q_tile, q_tile)


def _online_stats(s, m_s, l_s):
    blocks = _lane_blocks(s)
    mx = blocks[0]
    for blk in blocks[1:]:
        mx = jnp.maximum(mx, blk)
    m_prev = m_s[...]
    m_new = jnp.maximum(m_prev, jnp.max(mx, axis=-1, keepdims=True))
    alpha = jnp.exp(m_prev - m_new)
    ps = [jnp.exp(blk - m_new) for blk in blocks]
    psum = ps[0]
    for p in ps[1:]:
        psum = psum + p
    l_s[...] = alpha * l_s[...] + psum
    m_s[...] = m_new
    p_all = jnp.concatenate([p.astype(BF16) for p in ps], axis=1) if len(ps) > 1 else ps[0].astype(BF16)
    return alpha, p_all


def _online_update(s, v_bf16, m_s, l_s, acc_s):
    alpha, p_all = _online_stats(s, m_s, l_s)
    acc_s[...] = alpha * acc_s[...] + _dot(p_all, v_bf16)


def _online_init(m_s, l_s, acc_s):
    m_s[...] = jnp.full_like(m_s, NEG_INF)
    l_s[...] = jnp.zeros_like(l_s)
    acc_s[...] = jnp.zeros_like(acc_s)


def _online_result(l_s, acc_s):
    return acc_s[...] / jnp.sum(l_s[...], axis=-1, keepdims=True)


def _stack_heads(q):
    return jnp.concatenate([q[:, r * HEAD_DIM:(r + 1) * HEAD_DIM] for r in range(REP_A)], axis=0).astype(BF16)


def _attn_a_prompt_kernel(q_ref, k_ref, v_ref, mask_ref, tz_ref, o_ref, m_s, l_s, acc_s, *, tq):
    qi = pl.program_id(1)
    width = REP_A * HEAD_DIM
    qs = [_stack_heads(q_ref[0, :, g * width:(g + 1) * width] * SCALE) for g in range(KV_A)]
    _online_init(m_s, l_s, acc_s)

    def tile(kt, bias_cols):
        off = pl.multiple_of(kt * tq, tq)
        mask = mask_ref[0, 0, 0, kt][None]
        logits = []
        for g in range(KV_A):
            k = k_ref[0, pl.ds(off, tq), g * HEAD_DIM:(g + 1) * HEAD_DIM].astype(BF16)
            s = _dot_nt(qs[g], k).reshape(REP_A, tq, tq) + mask
            if bias_cols is not None:
                s = s + tz_ref[g * REP_A:(g + 1) * REP_A, :, bias_cols]
            logits.append(s.reshape(REP_A * tq, tq))
        stats = [_online_stats(logits[g], m_s.at[g], l_s.at[g]) for g in range(KV_A)]
        for g in range(KV_A):
            alpha, p_all = stats[g]
            v = v_ref[0, pl.ds(off, tq), g * HEAD_DIM:(g + 1) * HEAD_DIM].astype(BF16)
            acc_s[g] = alpha * acc_s[g] + _dot(p_all, v)

    def plain_body(kt, carry):
        tile(kt, None)
        return carry

    lax.fori_loop(0, jnp.maximum(qi - 1, 0), plain_body, 0)

    @pl.when(qi >= 1)
    def _():
        tile(qi - 1, slice(0, tq))

    tile(qi, slice(tq, 2 * tq))
    for g in range(KV_A):
        out = _online_result(l_s.at[g], acc_s.at[g])
        for r in range(REP_A):
            col = (g * REP_A + r) * HEAD_DIM
            o_ref[0, :, col:col + HEAD_DIM] = out[r * tq:(r + 1) * tq].astype(o_ref.dtype)


def _attn_a_prompt(proj3, mask, tz, *, tq):
    b, t, _ = proj3.shape
    nq = t // tq
    qw, kw = H_A * HEAD_DIM, KV_A * HEAD_DIM
    rows = REP_A * tq
    n_g = mask.shape[1]

    def mask_idx(i, j):
        grp, pos = _mask_tile_index(j, nq, n_g)
        return (i, grp, pos, 0, 0, 0)

    return pl.pallas_call(
        functools.partial(_attn_a_prompt_kernel, tq=tq),
        grid=(b, nq),
        in_specs=[
            pl.BlockSpec((1, tq, qw), lambda i, j: (i, j, EV_AQ // qw)),
            pl.BlockSpec((1, t, kw), lambda i, j: (i, 0, EV_AK // kw)),
            pl.BlockSpec((1, t, kw), lambda i, j: (i, 0, EV_AV // kw)),
            pl.BlockSpec((1, 1, 1, nq, tq, tq), mask_idx),
            pl.BlockSpec((H_A, tq, 2 * tq), lambda i, j: (0, 0, 0)),
        ],
        out_specs=pl.BlockSpec((1, tq, qw), lambda i, j: (i, j, 0)),
        out_shape=jax.ShapeDtypeStruct((b, t, qw), BF16),
        scratch_shapes=[pltpu.VMEM((KV_A, rows, LANE), F32), pltpu.VMEM((KV_A, rows, LANE), F32),
                        pltpu.VMEM((KV_A, rows, HEAD_DIM), F32)],
        compiler_params=_cparams("parallel", "parallel"),
        name="attn_a_prompt",
    )(proj3, proj3, proj3, mask, tz)


def _attn_a_step_kernel(q_ref, kc_ref, vc_ref, kn_ref, vn_ref, mask_ref, tz_ref, o_ref, *, t, p):
    width = REP_A * HEAD_DIM
    mask = mask_ref[0, 0, 0, 0]
    logits = []
    for g in range(KV_A):
        heads = slice(g * REP_A, (g + 1) * REP_A)
        qs = _stack_heads(q_ref[0, :, g * width:(g + 1) * width] * SCALE)
        kc = kc_ref[0, pl.ds(g, p, stride=KV_A), :].astype(BF16)
        kn = _pad_keys(kn_ref[0, :, g * HEAD_DIM:(g + 1) * HEAD_DIM], LANE).astype(BF16)
        s_c = _dot_nt(qs, kc).reshape(REP_A, t, p) + mask[None, :, 0:p]
        s_n = (_dot_nt(qs, kn).reshape(REP_A, t, LANE) + mask[None, :, p:p + LANE]
               + tz_ref[heads, :, LANE:2 * LANE])
        blocks = _lane_blocks(s_c)
        blocks[-1] = blocks[-1] + tz_ref[heads, :, 0:LANE]
        blocks.append(s_n)
        logits.append(blocks)
    probs = []
    for blocks in logits:
        mx = blocks[0]
        for blk in blocks[1:]:
            mx = jnp.maximum(mx, blk)
        m = jnp.max(mx, axis=-1, keepdims=True)
        ps = [jnp.exp(blk - m) for blk in blocks]
        psum = ps[0]
        for pb in ps[1:]:
            psum = psum + pb
        l = jnp.sum(psum, axis=-1, keepdims=True)
        p_c = jnp.concatenate([pb.astype(BF16) for pb in ps[:-1]], axis=-1).reshape(REP_A * t, p)
        probs.append((p_c, ps[-1].astype(BF16).reshape(REP_A * t, LANE), l))
    for g, (p_c, p_n, l) in enumerate(probs):
        vc = vc_ref[0, pl.ds(g, p, stride=KV_A), :].astype(BF16)
        vn = _pad_keys(vn_ref[0, :, g * HEAD_DIM:(g + 1) * HEAD_DIM], LANE).astype(BF16)
        out = (_dot(p_c, vc) + _dot(p_n, vn)).reshape(REP_A, t, HEAD_DIM) / l
        for r in range(REP_A):
            col = (g * REP_A + r) * HEAD_DIM
            o_ref[0, :, col:col + HEAD_DIM] = out[r].astype(o_ref.dtype)


def _attn_a_step(proj3, cache_k, cache_v, mask, tz):
    b, t, _ = proj3.shape
    p = cache_k.shape[1] // KV_A
    qw, kw = H_A * HEAD_DIM, KV_A * HEAD_DIM
    return pl.pallas_call(
        functools.partial(_attn_a_step_kernel, t=t, p=p),
        grid=(b,),
        in_specs=[
            pl.BlockSpec((1, t, qw), lambda i: (i, 0, EV_AQ // qw)),
            pl.BlockSpec((1, p * KV_A, HEAD_DIM), lambda i: (i, 0, 0)),
            pl.BlockSpec((1, p * KV_A, HEAD_DIM), lambda i: (i, 0, 0)),
            pl.BlockSpec((1, t, kw), lambda i: (i, 0, EV_AK // kw)),
            pl.BlockSpec((1, t, kw), lambda i: (i, 0, EV_AV // kw)),
            pl.BlockSpec((1, 1, 1, 1, t, p + LANE), lambda i: (i, 0, 0, 0, 0, 0)),
            pl.BlockSpec((H_A, t, 2 * LANE), lambda i: (0, 0, 0)),
        ],
        out_specs=pl.BlockSpec((1, t, qw), lambda i: (i, 0, 0)),
        out_shape=jax.ShapeDtypeStruct((b, t, qw), BF16),
        compiler_params=_cparams("parallel"),
        name="attn_a_step",
    )(proj3, cache_k, cache_v, proj3, proj3, mask, tz)


def _causal_mask(s, row0=0):
    r = row0 + lax.broadcasted_iota(I32, s.shape, 0)
    c = lax.broadcasted_iota(I32, s.shape, 1)
    return jnp.where(c <= r, s, NEG_INF)


B_HEADS_PER_STEP = 2


def _attn_b_prompt_kernel(q_ref, k_ref, v_ref, ck_ref, o_ref, m_s, l_s, acc_s, *, tq):
    qi = pl.program_id(2)
    heads = [slice(h * HEAD_DIM, (h + 1) * HEAD_DIM) for h in range(B_HEADS_PER_STEP)]
    qs = [(q_ref[0, :, sl] * SCALE).astype(BF16) for sl in heads]
    _online_init(m_s, l_s, acc_s)

    def tile(kt, diag):
        off = pl.multiple_of(kt * tq, tq)
        logits = []
        for h, sl in enumerate(heads):
            s = _dot_nt(qs[h], k_ref[0, pl.ds(off, tq), sl].astype(BF16)) - ck_ref[0, h, kt]
            logits.append(_causal_mask(s) if diag else s)
        stats = [_online_stats(logits[h], m_s.at[h], l_s.at[h]) for h in range(B_HEADS_PER_STEP)]
        for h, sl in enumerate(heads):
            alpha, p_all = stats[h]
            acc_s[h] = alpha * acc_s[h] + _dot(p_all, v_ref[0, pl.ds(off, tq), sl].astype(BF16))

    def body(kt, carry):
        tile(kt, False)
        return carry

    lax.fori_loop(0, qi, body, 0)
    tile(qi, True)
    for h, sl in enumerate(heads):
        o_ref[0, :, sl] = _online_result(l_s.at[h], acc_s.at[h]).astype(o_ref.dtype)


def _attn_b_prompt(proj3, cum_h, *, tq=512):
    b, t, _ = proj3.shape
    nq = t // tq
    ck = cum_h.reshape(b, H_B, nq, 1, tq)
    hp = B_HEADS_PER_STEP
    w = hp * HEAD_DIM
    return pl.pallas_call(
        functools.partial(_attn_b_prompt_kernel, tq=tq),
        grid=(b, H_B // hp, nq),
        in_specs=[
            pl.BlockSpec((1, tq, w), lambda i, h, j: (i, j, EV_BQ // w + h)),
            pl.BlockSpec((1, t, w), lambda i, h, j: (i, 0, EV_BK // w + h)),
            pl.BlockSpec((1, t, w), lambda i, h, j: (i, 0, EV_BV // w + h)),
            pl.BlockSpec((1, hp, nq, 1, tq), lambda i, h, j: (i, h, 0, 0, 0)),
        ],
        out_specs=pl.BlockSpec((1, tq, w), lambda i, h, j: (i, j, h)),
        out_shape=jax.ShapeDtypeStruct((b, t, H_B * HEAD_DIM), BF16),
        scratch_shapes=[pltpu.VMEM((hp, tq, LANE), F32), pltpu.VMEM((hp, tq, LANE), F32),
                        pltpu.VMEM((hp, tq, HEAD_DIM), F32)],
        compiler_params=_cparams("parallel", "parallel", "parallel"),
        name="attn_b_prompt",
    )(proj3, proj3, proj3, ck)


def _attn_b_step_kernel(q_ref, kc_ref, vc_ref, kn_ref, vn_ref, ckc_ref, ckn_ref, o_ref, m_s, l_s, acc_s,
                        *, nkc, t, tkc):
    kb = pl.program_id(1)

    @pl.when(kb == 0)
    def _():
        _online_init(m_s, l_s, acc_s)

    def q_head(h):
        return (q_ref[0, :, h * HEAD_DIM:(h + 1) * HEAD_DIM] * SCALE).astype(BF16)

    @pl.when(kb < nkc)
    def _():
        ck = ckc_ref[0, 0]
        logits = [_dot_nt(q_head(h), kc_ref[0, pl.ds(h, tkc, stride=H_B), :].astype(BF16)) - ck[h:h + 1, :]
                  for h in range(H_B)]
        stats = [_online_stats(logits[h], m_s.at[h], l_s.at[h]) for h in range(H_B)]
        for h in range(H_B):
            alpha, p_all = stats[h]
            v = vc_ref[0, pl.ds(h, tkc, stride=H_B), :].astype(BF16)
            acc_s[h] = alpha * acc_s[h] + _dot(p_all, v)

    @pl.when(kb == nkc)
    def _():
        ckn = ckn_ref[0]
        heads = [slice(h * HEAD_DIM, (h + 1) * HEAD_DIM) for h in range(H_B)]
        logits = [_causal_mask(_dot_nt(q_head(h), _pad_keys(kn_ref[0, :, heads[h]], LANE).astype(BF16))
                               - ckn[h:h + 1, :]) for h in range(H_B)]
        stats = [_online_stats(logits[h], m_s.at[h], l_s.at[h]) for h in range(H_B)]
        for h in range(H_B):
            alpha, p_all = stats[h]
            v = _pad_keys(vn_ref[0, :, heads[h]], LANE).astype(BF16)
            acc_s[h] = alpha * acc_s[h] + _dot(p_all, v)
            o_ref[0, :, heads[h]] = _online_result(l_s.at[h], acc_s.at[h]).astype(o_ref.dtype)


def _attn_b_step(proj3, cache_k, cache_v, cum_h, *, tkc=1024):
    b, t, _ = proj3.shape
    p = cache_k.shape[1] // H_B
    nkc = p // tkc
    w = H_B * HEAD_DIM
    ck_cache = cum_h[:, :, :p].reshape(b, H_B, nkc, tkc).transpose(0, 2, 1, 3)
    ck_new = jnp.pad(cum_h[:, :, p:], ((0, 0), (0, 0), (0, LANE - t)))
    last = nkc - 1
    return pl.pallas_call(
        functools.partial(_attn_b_step_kernel, nkc=nkc, t=t, tkc=tkc),
        grid=(b, nkc + 1),
        in_specs=[
            pl.BlockSpec((1, t, w), lambda i, j: (i, 0, EV_BQ // w)),
            pl.BlockSpec((1, tkc * H_B, HEAD_DIM), lambda i, j: (i, jnp.minimum(j, last), 0)),
            pl.BlockSpec((1, tkc * H_B, HEAD_DIM), lambda i, j: (i, jnp.minimum(j, last), 0)),
            pl.BlockSpec((1, t, w), lambda i, j: (i, 0, EV_BK // w)),
            pl.BlockSpec((1, t, w), lambda i, j: (i, 0, EV_BV // w)),
            pl.BlockSpec((1, 1, H_B, tkc), lambda i, j: (i, jnp.minimum(j, last), 0, 0)),
            pl.BlockSpec((1, H_B, LANE), lambda i, j: (i, 0, 0)),
        ],
        out_specs=pl.BlockSpec((1, t, w), lambda i, j: (i, 0, 0)),
        out_shape=jax.ShapeDtypeStruct((b, t, w), BF16),
        scratch_shapes=[pltpu.VMEM((H_B, t, LANE), F32), pltpu.VMEM((H_B, t, LANE), F32),
                        pltpu.VMEM((H_B, t, HEAD_DIM), F32)],
        compiler_params=_cparams("parallel", "arbitrary"),
        name="attn_b_step",
    )(proj3, cache_k, cache_v, proj3, proj3, ck_cache, ck_new)


def _tri_ones():
    r = lax.broadcasted_iota(I32, (2 * LANE, 2 * LANE), 0) & (LANE - 1)
    c = lax.broadcasted_iota(I32, (2 * LANE, 2 * LANE), 1)
    return jnp.where((r > c) | (c >= LANE), 1.0, 0.0).astype(BF16)


def _stick_scores(z, row_minus_col, tri_ones):
    rc_blocks = None if row_minus_col is None else _lane_blocks(row_minus_col)
    out = []
    for j, zb in enumerate(_lane_blocks(z)):
        tail = jnp.log(1.0 + jnp.exp(-jnp.abs(zb)))
        log_beta = jnp.minimum(zb, 0.0) - tail
        log_keep = -jnp.maximum(zb, 0.0) - tail
        strict = None
        if rc_blocks is not None:
            strict = rc_blocks[j] > 0
            log_keep = jnp.where(strict, log_keep, 0.0)
        hi = log_keep.astype(BF16)
        lo = (log_keep - hi.astype(F32)).astype(BF16)
        out.append((log_beta, strict, _dot(jnp.concatenate([hi, lo], axis=1), tri_ones)))
    return out


def _stick_weights(blocks, run):
    ws = [None] * len(blocks)
    for j in reversed(range(len(blocks))):
        log_beta, strict, sums = blocks[j]
        w = jnp.exp(log_beta + sums[:, :LANE] + run)
        if strict is not None:
            w = jnp.where(strict, w, 0.0)
        ws[j] = w.astype(BF16)
        run = run + sums[:, LANE:]
    return (jnp.concatenate(ws, axis=1) if len(ws) > 1 else ws[0]), run


def _stick_tile(q_bf16, k_bf16, v_bf16, run, tri_ones, row_minus_col):
    w_all, run = _stick_weights(_stick_scores(_dot_nt(q_bf16, k_bf16), row_minus_col, tri_ones), run)
    return _dot(w_all, v_bf16), run


C_HEADS_PER_STEP = 2


def _attn_c_prompt_kernel(q_ref, k_ref, v_ref, o_ref, run_s, acc_s, *, tq):
    qi = pl.program_id(2)
    heads = [slice(h * HEAD_DIM, (h + 1) * HEAD_DIM) for h in range(C_HEADS_PER_STEP)]
    qs = [(q_ref[0, :, sl] * SCALE).astype(BF16) for sl in heads]
    tri_ones = _tri_ones()

    def tile(kt, row_minus_col, first):
        off = pl.multiple_of(kt * tq, tq)
        zs = [_dot_nt(qs[h], k_ref[0, pl.ds(off, tq), sl].astype(BF16)) for h, sl in enumerate(heads)]
        scored = [_stick_scores(z, row_minus_col, tri_ones) for z in zs]
        for h, sl in enumerate(heads):
            run = jnp.zeros((tq, LANE), F32) if first else run_s[h]
            w_all, run = _stick_weights(scored[h], run)
            out = _dot(w_all, v_ref[0, pl.ds(off, tq), sl].astype(BF16))
            acc_s[h] = out if first else acc_s[h] + out
            run_s[h] = run

    rc = lax.broadcasted_iota(I32, (tq, tq), 0) - lax.broadcasted_iota(I32, (tq, tq), 1)
    tile(qi, rc, True)

    def body(i, carry):
        tile(qi - 1 - i, None, False)
        return carry

    lax.fori_loop(0, qi, body, 0)
    for h, sl in enumerate(heads):
        o_ref[0, :, sl] = acc_s[h].astype(o_ref.dtype)


def _attn_c_prompt(proj3, *, tq=512):
    b, t, _ = proj3.shape
    nq = t // tq
    hp = C_HEADS_PER_STEP
    w = hp * HEAD_DIM
    return pl.pallas_call(
        functools.partial(_attn_c_prompt_kernel, tq=tq),
        grid=(b, H_C // hp, nq),
        in_specs=[
            pl.BlockSpec((1, tq, w), lambda i, h, j: (i, j, OD_CQ // w + h)),
            pl.BlockSpec((1, t, w), lambda i, h, j: (i, 0, OD_CK // w + h)),
            pl.BlockSpec((1, t, w), lambda i, h, j: (i, 0, OD_CV // w + h)),
        ],
        out_specs=pl.BlockSpec((1, tq, w), lambda i, h, j: (i, j, h)),
        out_shape=jax.ShapeDtypeStruct((b, t, H_C * HEAD_DIM), BF16),
        scratch_shapes=[pltpu.VMEM((hp, tq, LANE), F32), pltpu.VMEM((hp, tq, HEAD_DIM), F32)],
        compiler_params=_cparams("parallel", "parallel", "parallel"),
        name="attn_c_prompt",
    )(proj3, proj3, proj3)


def _attn_c_step_kernel(q_ref, kc_ref, vc_ref, kn_ref, vn_ref, o_ref, run_s, acc_s, *, nkc, t, tkc, sub):
    kb = pl.program_id(1)
    tri_ones = _tri_ones()

    def q_head(h):
        return (q_ref[0, :, h * HEAD_DIM:(h + 1) * HEAD_DIM] * SCALE).astype(BF16)

    @pl.when(kb == 0)
    def _():
        rc = lax.broadcasted_iota(I32, (t, LANE), 0) - lax.broadcasted_iota(I32, (t, LANE), 1)
        for h in range(H_C):
            sl = slice(h * HEAD_DIM, (h + 1) * HEAD_DIM)
            k = _pad_keys(kn_ref[0, :, sl], LANE).astype(BF16)
            v = _pad_keys(vn_ref[0, :, sl], LANE).astype(BF16)
            out, run = _stick_tile(q_head(h), k, v, jnp.zeros((t, LANE), F32), tri_ones, rc)
            acc_s[h] = out
            run_s[h] = run

    @pl.when(kb > 0)
    def _():
        def body(i, carry):
            row0 = (tkc - sub - i * sub) * H_C
            zs = [_dot_nt(q_head(h), kc_ref[0, pl.ds(row0 + h, sub, stride=H_C), :].astype(BF16))
                  for h in range(H_C)]
            scored = [_stick_scores(z, None, tri_ones) for z in zs]
            for h in range(H_C):
                w_all, run = _stick_weights(scored[h], run_s[h])
                v = vc_ref[0, pl.ds(row0 + h, sub, stride=H_C), :].astype(BF16)
                acc_s[h] += _dot(w_all, v)
                run_s[h] = run
            return carry

        lax.fori_loop(0, tkc // sub, body, 0)

    @pl.when(kb == nkc)
    def _():
        for h in range(H_C):
            o_ref[0, :, h * HEAD_DIM:(h + 1) * HEAD_DIM] = acc_s[h].astype(o_ref.dtype)


def _attn_c_step(proj3, cache_k, cache_v, *, tkc=1024, sub=1024):
    b, t, _ = proj3.shape
    p = cache_k.shape[1] // H_C
    nkc = p // tkc
    w = H_C * HEAD_DIM

    def cache_idx(i, j):
        return (i, jnp.clip(nkc - j, 0, nkc - 1), 0)

    return pl.pallas_call(
        functools.partial(_attn_c_step_kernel, nkc=nkc, t=t, tkc=tkc, sub=sub),
        grid=(b, nkc + 1),
        in_specs=[
            pl.BlockSpec((1, t, w), lambda i, j: (i, 0, OD_CQ // w)),
            pl.BlockSpec((1, tkc * H_C, HEAD_DIM), cache_idx),
            pl.BlockSpec((1, tkc * H_C, HEAD_DIM), cache_idx),
            pl.BlockSpec((1, t, w), lambda i, j: (i, 0, OD_CK // w)),
            pl.BlockSpec((1, t, w), lambda i, j: (i, 0, OD_CV // w)),
        ],
        out_specs=pl.BlockSpec((1, t, w), lambda i, j: (i, 0, 0)),
        out_shape=jax.ShapeDtypeStruct((b, t, w), BF16),
        scratch_shapes=[pltpu.VMEM((H_C, t, LANE), F32), pltpu.VMEM((H_C, t, HEAD_DIM), F32)],
        compiler_params=_cparams("parallel", "arbitrary"),
        name="attn_c_step",
    )(proj3, cache_k, cache_v, proj3, proj3)


D_HEADS_PER_STEP = 2


def _attn_d_kernel(q_ref, kp_ref, vp_ref, ko_ref, vo_ref, bias_ref, o_ref, *, tq):
    qi = pl.program_id(2)
    heads = [slice(h * HEAD_DIM, (h + 1) * HEAD_DIM) for h in range(D_HEADS_PER_STEP)]
    qs = [(q_ref[0, :, sl] * SCALE).astype(BF16) for sl in heads]

    def own_logits(h):
        return _dot_nt(qs[h], ko_ref[0, :, heads[h]].astype(BF16)) + bias_ref[h, :, D_BAND_LEFT:D_BAND_LEFT + tq]

    def with_left():
        s_o = [own_logits(h) for h in range(D_HEADS_PER_STEP)]
        s_p = [_dot_nt(qs[h], kp_ref[0, :, heads[h]].astype(BF16)) + bias_ref[h, :, 0:D_BAND_LEFT]
               for h in range(D_HEADS_PER_STEP)]
        probs = []
        for h in range(D_HEADS_PER_STEP):
            m = jnp.maximum(jnp.max(s_p[h], axis=-1, keepdims=True), jnp.max(s_o[h], axis=-1, keepdims=True))
            p_p = jnp.exp(s_p[h] - m)
            p_o = jnp.exp(s_o[h] - m)
            l = jnp.sum(p_p, axis=-1, keepdims=True) + jnp.sum(p_o, axis=-1, keepdims=True)
            probs.append((p_p.astype(BF16), p_o.astype(BF16), l))
        for h, sl in enumerate(heads):
            p_p, p_o, l = probs[h]
            out = _dot(p_p, vp_ref[0, :, sl].astype(BF16)) + _dot(p_o, vo_ref[0, :, sl].astype(BF16))
            o_ref[0, :, sl] = (out / l).astype(o_ref.dtype)

    def own_only():
        s_o = [own_logits(h) for h in range(D_HEADS_PER_STEP)]
        for h, sl in enumerate(heads):
            p_o = jnp.exp(s_o[h] - jnp.max(s_o[h], axis=-1, keepdims=True))
            l = jnp.sum(p_o, axis=-1, keepdims=True)
            o_ref[0, :, sl] = (_dot(p_o.astype(BF16), vo_ref[0, :, sl].astype(BF16)) / l).astype(o_ref.dtype)

    pl.when(qi > 0)(with_left)
    pl.when(qi == 0)(own_only)


def _attn_d(proj3, bias):
    b, t, _ = proj3.shape
    tq = D_BAND_LEFT
    nq = t // tq
    hp = D_HEADS_PER_STEP
    w = hp * HEAD_DIM
    qc, kc, vc = OD_DQ // w, OD_DK // w, OD_DV // w
    return pl.pallas_call(
        functools.partial(_attn_d_kernel, tq=tq),
        grid=(b, H_D // hp, nq),
        in_specs=[
            pl.BlockSpec((1, tq, w), lambda i, h, j: (i, j, qc + h)),
            pl.BlockSpec((1, tq, w), lambda i, h, j: (i, jnp.maximum(j - 1, 0), kc + h)),
            pl.BlockSpec((1, tq, w), lambda i, h, j: (i, jnp.maximum(j - 1, 0), vc + h)),
            pl.BlockSpec((1, tq, w), lambda i, h, j: (i, j, kc + h)),
            pl.BlockSpec((1, tq, w), lambda i, h, j: (i, j, vc + h)),
            pl.BlockSpec((hp, tq, 2 * tq), lambda i, h, j: (h, 0, 0)),
        ],
        out_specs=pl.BlockSpec((1, tq, w), lambda i, h, j: (i, j, h)),
        out_shape=jax.ShapeDtypeStruct((b, t, H_D * HEAD_DIM), BF16),
        compiler_params=_cparams("parallel", "parallel", "parallel"),
        name="attn_d",
    )(proj3, proj3, proj3, proj3, proj3, bias)


def _attn_d_step_kernel(q_ref, kp_ref, vp_ref, kn_ref, vn_ref, bias_ref, o_ref, *, t):
    for h in range(H_D):
        sl = slice(h * HEAD_DIM, (h + 1) * HEAD_DIM)
        q = (q_ref[0, :, sl] * SCALE).astype(BF16)
        kp = kp_ref[0, pl.ds(h, D_BAND_LEFT, stride=H_D), :].astype(BF16)
        vp = vp_ref[0, pl.ds(h, D_BAND_LEFT, stride=H_D), :].astype(BF16)
        s_p = _dot_nt(q, kp) + bias_ref[h, :, 0:D_BAND_LEFT]
        s_o = _dot_nt(q, kn_ref[0, :, sl].astype(BF16)) + bias_ref[h, :, D_BAND_LEFT:D_BAND_LEFT + t]
        m = jnp.maximum(jnp.max(s_p, axis=-1, keepdims=True), jnp.max(s_o, axis=-1, keepdims=True))
        p_p = jnp.exp(s_p - m)
        p_o = jnp.exp(s_o - m)
        l = jnp.sum(p_p, axis=-1, keepdims=True) + jnp.sum(p_o, axis=-1, keepdims=True)
        out = _dot(p_p.astype(BF16), vp) + _dot(p_o.astype(BF16), vn_ref[0, :, sl].astype(BF16))
        o_ref[0, :, sl] = (out / l).astype(o_ref.dtype)


def _attn_d_step(proj3, cache_k, cache_v, bias):
    b, t, _ = proj3.shape
    w = H_D * HEAD_DIM
    return pl.pallas_call(
        functools.partial(_attn_d_step_kernel, t=t),
        grid=(b,),
        in_specs=[
            pl.BlockSpec((1, t, w), lambda i: (i, 0, OD_DQ // w)),
            pl.BlockSpec((1, D_BAND_LEFT * H_D, HEAD_DIM), lambda i: (i, 0, 0)),
            pl.BlockSpec((1, D_BAND_LEFT * H_D, HEAD_DIM), lambda i: (i, 0, 0)),
            pl.BlockSpec((1, t, w), lambda i: (i, 0, OD_DK // w)),
            pl.BlockSpec((1, t, w), lambda i: (i, 0, OD_DV // w)),
            pl.BlockSpec((H_D, t, D_BAND_LEFT + t), lambda i: (0, 0, 0)),
        ],
        out_specs=pl.BlockSpec((1, t, w), lambda i: (i, 0, 0)),
        out_shape=jax.ShapeDtypeStruct((b, t, w), BF16),
        compiler_params=_cparams("parallel"),
        name="attn_d_step",
    )(proj3, cache_k, cache_v, proj3, proj3, bias)


def _pad_rows(x, rows):
    return jnp.pad(x, ((0, 0), (0, rows - x.shape[1]), (0, 0)))


def _even_mixer(h2, b, t, past, w, tz_prompt, tz_step, tq_a, mm):
    proj = mm("in_even", h2, w["w_in"], mode="headnorm", gain=w["gain"], flag=w["flag"], n_norm_tiles=9,
              name="proj_even")
    proj3 = proj.reshape(b, t, EV_PAD)
    ak = proj3[:, :, EV_AK:EV_AK + KV_A * HEAD_DIM]
    av = proj3[:, :, EV_AV:EV_AV + KV_A * HEAD_DIM]
    ik = proj3[:, :, EV_MISC + MISC_IK:EV_MISC + MISC_IK + IDX_DIM]
    bk = proj3[:, :, EV_BK:EV_BK + H_B * HEAD_DIM]
    bv = proj3[:, :, EV_BV:EV_BV + H_B * HEAD_DIM]
    raw_h = proj3[:, :, EV_MISC + MISC_BF:EV_MISC + MISC_BF + H_B].transpose(0, 2, 1)
    iq_col = EV_IQ // (IDX_HEADS * IDX_DIM)
    misc_col = EV_MISC // LANE
    if past is None:
        logf_h, cum_h = _logf_cum(raw_h, w["f_bias"], None)
        mask = _select_mask(proj3, iq_col, proj3, misc_col, proj3, misc_col, t=t, tq=tq_a, lp=t, l_valid=t,
                            q_base=0, n_sel=min(TOPK_MAX, t // 4), tk_out=tq_a, causal_prefix=True)
        out_a = _attn_a_prompt(proj3, mask, tz_prompt, tq=tq_a)
        out_b = _attn_b_prompt(proj3, cum_h, tq=min(512, t))
    else:
        p_ak, p_av, p_ik, p_bk, p_bv, p_lf = past
        p = p_ak.shape[1]
        l_valid = p + t
        lp = p + LANE
        logf_h, cum_h = _logf_cum(raw_h, w["f_bias"], p_lf.transpose(0, 2, 1))
        ik_all = _pad_rows(jnp.concatenate([p_ik, ik], axis=1), lp)
        mask = _select_mask(proj3, iq_col, proj3, misc_col, ik_all, None, t=t, tq=t, lp=lp, l_valid=l_valid,
                            q_base=p, n_sel=min(TOPK_MAX, l_valid // 4), tk_out=lp, causal_prefix=False)
        out_a = _attn_a_step(proj3, p_ak.reshape(b, p * KV_A, HEAD_DIM), p_av.reshape(b, p * KV_A, HEAD_DIM),
                             mask, tz_step[:, :t])
        out_b = _attn_b_step(proj3, p_bk.reshape(b, p * H_B, HEAD_DIM), p_bv.reshape(b, p * H_B, HEAD_DIM), cum_h)
    mixed = [out_a.reshape(b * t, -1), out_b.reshape(b * t, -1)]
    state = (ak.reshape(b, t, KV_A, HEAD_DIM), av.reshape(b, t, KV_A, HEAD_DIM), ik,
             bk.reshape(b, t, H_B, HEAD_DIM), bv.reshape(b, t, H_B, HEAD_DIM), logf_h.transpose(0, 2, 1))
    return mixed, state


def _odd_mixer(h2, b, t, past, w, band, mm):
    proj = mm("in_odd", h2, w["w_in"], col_block=_odd_col_block, mode="headnorm", gain=w["gain"], flag=w["flag"],
              n_norm_tiles=4, name="proj_odd")
    proj3 = proj.reshape(b, t, OD_WIDTH)
    ck = proj3[:, :, OD_CK:OD_CK + H_C * HEAD_DIM]
    cv = proj3[:, :, OD_CV:OD_CV + H_C * HEAD_DIM]
    dk = proj3[:, :, OD_DK:OD_DK + H_D * HEAD_DIM]
    dv = proj3[:, :, OD_DV:OD_DV + H_D * HEAD_DIM]
    dq_col, dk_col, dv_col = OD_DQ // HEAD_DIM, OD_DK // HEAD_DIM, OD_DV // HEAD_DIM
    if past is None:
        out_c = _attn_c_prompt(proj3, tq=min(512, t))
        out_d = _attn_d(proj3, band)
        keep = min(D_BAND_LEFT, t)
        d_rows = (dk[:, t - keep:], dv[:, t - keep:])
    else:
        p_ck, p_cv, p_dk, p_dv = past
        p = p_ck.shape[1]
        out_c = _attn_c_step(proj3, p_ck.reshape(b, p * H_C, HEAD_DIM), p_cv.reshape(b, p * H_C, HEAD_DIM))
        wd = p_dk.shape[1]
        out_d = _attn_d_step(proj3, p_dk.reshape(b, wd * H_D, HEAD_DIM), p_dv.reshape(b, wd * H_D, HEAD_DIM),
                             band[:, :t, :D_BAND_LEFT + t])
        d_rows = (dk, dv)
    mixed = [out_c.reshape(b * t, -1), out_d.reshape(b * t, -1)]
    state = (ck.reshape(b, t, H_C, HEAD_DIM), cv.reshape(b, t, H_C, HEAD_DIM),
             d_rows[0].reshape(b, -1, H_D, HEAD_DIM), d_rows[1].reshape(b, -1, H_D, HEAD_DIM))
    return mixed, state


def _odd_col_block(j):
    return jnp.where(j < 4, j + 6, jnp.where(j < 10, j - 4, j))


def _trunk(x, caches_even, caches_odd, wts, bf16_weights):
    def mm(key, a, w_f32, col_block=None, **kw):
        if key in bf16_weights:
            w_bf16 = bf16_weights[key]
            if kw.get("tn", 512) == 512 and w_bf16.shape[1] % 1024 == 0:
                kw = dict(kw, tn=1024, n_norm_tiles=(kw.get("n_norm_tiles", 0) + 1) // 2)
            return _matmul(a, w_bf16, **kw)
        out, bf16_weights[key] = _matmul_cast(a, w_f32, col_block=col_block, **kw)
        return out

    b, t, d = x.shape
    x2 = x.reshape(b * t, d)
    h2 = _rmsnorm(x2, wts["attn_norm"][0])
    mixed, st_even = _even_mixer(h2, b, t, caches_even, wts["even"], wts["tz_prompt"], wts["tz_step"],
                                 wts["tq_a"], mm)
    x2 = mm("out_even", mixed, wts["even"]["w_out"], mode="residual", residual=x2, name="out_even")
    h2 = _rmsnorm(x2, wts["mlp_norm"][0])
    u = mm("up0", h2, wts["w_up"][0], mode="relu2", out_dtype=BF16, tn=1024, name="mlp_up0")
    x2 = _matmul(u, wts["w_down"][0], mode="residual", residual=x2, tm=512, tk=u.shape[1], name="mlp_down0")
    h2 = _rmsnorm(x2, wts["attn_norm"][1])
    mixed, st_odd = _odd_mixer(h2, b, t, caches_odd, wts["odd"], wts["band"], mm)
    x2 = mm("out_odd", mixed, wts["odd"]["w_out"], mode="residual", residual=x2, name="out_odd")
    h2 = _rmsnorm(x2, wts["mlp_norm"][1])
    u = mm("up1", h2, wts["w_up"][1], mode="relu2", out_dtype=BF16, tn=1024, name="mlp_up1")
    x2 = _matmul(u, wts["w_down"][1], mode="residual", residual=x2, tm=512, tk=u.shape[1], name="mlp_down1")
    return x2.reshape(b, t, d), tuple(s[None] for s in st_even), tuple(s[None] for s in st_odd)


def _even_weights(w_in, w_out, a_qn, a_kn, b_qn, b_kn, f_bias):
    aq, ak, av, iq, ik, iw, bq, bk, bv, bf = jnp.split(
        w_in, [1024, 1280, 1536, 2048, 2112, 2120, 3144, 4168, 5192], axis=1)
    d = w_in.shape[0]
    pad = jnp.zeros((d, EV_PAD - EV_WIDTH + LANE - (IDX_DIM + IDX_HEADS + H_B)), w_in.dtype)
    w_r = jnp.concatenate([aq, bq, bk, bv, ak, av, iq, ik, iw, bf, pad], axis=1)
    ones = jnp.ones((EV_PAD,), F32)
    gain = ones.at[EV_AQ:EV_AQ + 1024].set(jnp.tile(a_qn, H_A))
    gain = gain.at[EV_BQ:EV_BQ + 1024].set(jnp.tile(b_qn, H_B))
    gain = gain.at[EV_BK:EV_BK + 1024].set(jnp.tile(b_kn, H_B))
    gain = gain.at[EV_AK:EV_AK + 256].set(jnp.tile(a_kn, KV_A))
    flag = jnp.zeros((EV_PAD,), F32).at[0:EV_BV].set(1.0).at[EV_AK:EV_AV].set(1.0)
    return {"w_in": w_r, "gain": gain.reshape(1, -1), "flag": flag.reshape(1, -1), "w_out": w_out,
            "f_bias": f_bias}


def _odd_weights(w_in, w_out, d_qn, d_kn):
    gain = jnp.ones((OD_WIDTH,), F32)
    gain = gain.at[OD_DQ:OD_DQ + 1024].set(jnp.tile(d_qn, H_D)).at[OD_DK:OD_DK + 1024].set(jnp.tile(d_kn, H_D))
    flag = jnp.zeros((OD_WIDTH,), F32).at[0:OD_CQ].set(1.0)
    return {"w_in": w_in, "gain": gain.reshape(1, -1), "flag": flag.reshape(1, -1), "w_out": w_out}


def kernel(x_prompt, x_sample, cache_a_k, cache_a_v, cache_a_kidx, cache_b_k, cache_b_v, cache_b_logf,
           cache_c_k, cache_c_v, cache_d_k, cache_d_v, attn_norm, mlp_norm, w_in_even, w_out_even,
           w_in_odd, w_out_odd, a_q_norm, a_k_norm, b_q_norm, b_k_norm, forget_bias, t5_bias,
           d_q_norm, d_k_norm, d_rel_bias, w_up, w_down):
    tq_a = min(256, x_prompt.shape[1])
    wts = {
        "attn_norm": attn_norm,
        "mlp_norm": mlp_norm,
        "even": _even_weights(w_in_even[0], w_out_even[0], a_q_norm[0], a_k_norm[0], b_q_norm[0], b_k_norm[0],
                              forget_bias[0]),
        "odd": _odd_weights(w_in_odd[0], w_out_odd[0], d_q_norm[0], d_k_norm[0]),
        "w_up": w_up,
        "w_down": w_down.astype(BF16),
        "tq_a": tq_a,
        "tz_prompt": _t5_strip(t5_bias, tq_a),
        "tz_step": _t5_strip(t5_bias, LANE),
        "band": _band_bias(d_rel_bias[0]),
    }
    bf16_weights = {}
    y_p, ev_p, od_p = _trunk(x_prompt, None, None, wts, bf16_weights)
    caches_even = (cache_a_k[0], cache_a_v[0], cache_a_kidx[0], cache_b_k[0], cache_b_v[0], cache_b_logf[0])
    caches_odd = (cache_c_k[0], cache_c_v[0], cache_d_k[0], cache_d_v[0])
    y_s, ev_s, od_s = _trunk(x_sample, caches_even, caches_odd, wts, bf16_weights)
    return (y_p, y_s) + ev_p + od_p + ev_s + od_s
```

```python
import functools

import numpy as np
import jax
import jax.numpy as jnp
from jax import lax
from jax.experimental import pallas as pl
from jax.experimental.pallas import tpu as pltpu

F32 = jnp.float32
BF16 = jnp.bfloat16
I32 = jnp.int32

HEAD_DIM = 128
CHUNK = 64
CHUNK_SHIFT = 6
H_A = 8
KV_A = 2
REP_A = H_A // KV_A
H_B = 8
H_C = 8
H_D = 8
IDX_HEADS = 8
IDX_DIM = 64
TOPK_MAX = 256
T5_BUCKETS = 32
D_LEFT_CHUNKS = 8
D_BAND_LEFT = D_LEFT_CHUNKS * CHUNK
D_REL_CLIP = 128
EPS = 1e-6
NEG_INF = -1e30
SCALE = HEAD_DIM ** -0.5
LANE = 128
MXU_WIDTH = 256
INT32_MIN = -(2 ** 31)
NEG_INF_KEY = int(np.float32(NEG_INF).view(np.int32)) ^ 0x7FFFFFFF

VMEM_LIMIT_BYTES = 56 * 1024 * 1024

EV_AQ, EV_BQ, EV_BK, EV_BV, EV_AK, EV_AV, EV_IQ, EV_MISC = 0, 1024, 2048, 3072, 4096, 4352, 4608, 5120
EV_WIDTH = 5248
EV_PAD = 5632
MISC_IK, MISC_IW, MISC_BF = 0, 64, 72
OD_DQ, OD_DK, OD_CQ, OD_CK, OD_CV, OD_DV = 0, 1024, 2048, 3072, 4096, 5120
OD_WIDTH = 6144


def _cparams(*sem):
    return pltpu.CompilerParams(dimension_semantics=sem, vmem_limit_bytes=VMEM_LIMIT_BYTES)


def _dot_nt(a, b):
    return lax.dot_general(a, b, (((1,), (1,)), ((), ())), preferred_element_type=F32)


def _dot(a, b):
    return jnp.dot(a, b, preferred_element_type=F32)


def _lane_blocks(x):
    return [x[..., j * LANE:(j + 1) * LANE] for j in range(x.shape[-1] // LANE)]


def _pad_keys(x, rows):
    return jnp.concatenate([x, jnp.zeros((rows - x.shape[0], x.shape[1]), x.dtype)], axis=0)


def _rmsnorm_kernel(x_ref, g_ref, o_ref):
    x = x_ref[...]
    ms = jnp.mean(x * x, axis=-1, keepdims=True)
    o_ref[...] = (x * lax.rsqrt(ms + EPS) * g_ref[...]).astype(o_ref.dtype)


def _rmsnorm(x2, gain):
    m, d = x2.shape
    tm = min(512, m)
    return pl.pallas_call(
        _rmsnorm_kernel,
        grid=(m // tm,),
        in_specs=[pl.BlockSpec((tm, d), lambda i: (i, 0)), pl.BlockSpec((1, d), lambda i: (0, 0))],
        out_specs=pl.BlockSpec((tm, d), lambda i: (i, 0)),
        out_shape=jax.ShapeDtypeStruct((m, d), BF16),
        compiler_params=_cparams("parallel"),
        name="rmsnorm",
    )(x2, gain.reshape(1, d))


def _mm_kernel(*refs, n_a, nk, mode, n_norm_tiles, cast_b):
    a_refs, b_ref = refs[:n_a], refs[n_a]
    refs = refs[n_a - 1:]
    if mode == "headnorm":
        gain_ref, flag_ref, o_ref = refs[2], refs[3], refs[4]
        rest = refs[5:]
    elif mode == "residual":
        res_ref, o_ref = refs[2], refs[3]
        rest = refs[4:]
    else:
        o_ref = refs[2]
        rest = refs[3:]
    n_axis = 0 if cast_b else 1
    if cast_b:
        wb_ref = rest[0]

        @pl.when(pl.program_id(1) == 0)
        def _():
            wb_ref[...] = b_ref[...].astype(BF16)

        b_ref = wb_ref

    tn = o_ref.shape[1]

    def product(c0=0, c1=tn):
        row, acc = 0, None
        for a_ref in a_refs:
            kp = a_ref.shape[1]
            part = _dot(a_ref[...], b_ref[row:row + kp, c0:c1])
            acc = part if acc is None else acc + part
            row += kp
        return acc

    def headnorm(acc, c0):
        for g in range(acc.shape[1] // HEAD_DIM):
            sl = slice(c0 + g * HEAD_DIM, c0 + (g + 1) * HEAD_DIM)
            blk = acc[:, g * HEAD_DIM:(g + 1) * HEAD_DIM]
            ms = jnp.mean(blk * blk, axis=-1, keepdims=True)
            normed = blk * lax.rsqrt(ms + EPS) * gain_ref[:, sl]
            o_ref[:, sl] = jnp.where(flag_ref[:, sl] > 0.0, normed, blk)

    def epilogue(acc, c0=0):
        sl = slice(c0, c0 + acc.shape[1])
        if mode == "headnorm":
            headnorm(acc, c0)
        elif mode == "relu2":
            r = jnp.maximum(acc, 0.0)
            o_ref[:, sl] = (r * r).astype(o_ref.dtype)
        elif mode == "residual":
            o_ref[:, sl] = res_ref[:, sl] + acc
        else:
            o_ref[:, sl] = acc.astype(o_ref.dtype)

    def chunked():
        starts = list(range(0, tn, MXU_WIDTH))
        accs = [product(c0, c0 + MXU_WIDTH) for c0 in starts]
        for c0, acc in zip(starts, accs):
            epilogue(acc, c0)

    if nk == 1 and mode == "headnorm":
        j = pl.program_id(n_axis)
        pl.when(j < n_norm_tiles)(chunked)

        @pl.when(j >= n_norm_tiles)
        def _():
            o_ref[...] = product()
    elif nk == 1:
        chunked()
    else:
        acc_ref = rest[0]
        k = pl.program_id(2)

        @pl.when(k == 0)
        def _():
            acc_ref[...] = jnp.zeros_like(acc_ref)

        acc_ref[...] += product()

        @pl.when(k == nk - 1)
        def _():
            epilogue(acc_ref[...])


def _matmul(a, b, *, mode="plain", out_dtype=F32, gain=None, flag=None, n_norm_tiles=0, residual=None,
            tm=1024, tn=512, tk=2048, name="matmul"):
    a_parts = list(a) if isinstance(a, (list, tuple)) else [a]
    m = a_parts[0].shape[0]
    kdim, n = b.shape
    tm = tm if m % tm == 0 else min(512, m)
    if len(a_parts) > 1:
        tk = kdim
        in_specs = [pl.BlockSpec((tm, part.shape[1]), lambda i, j, k: (i, 0)) for part in a_parts]
    else:
        in_specs = [pl.BlockSpec((tm, tk), lambda i, j, k: (i, k))]
    nk = kdim // tk
    in_specs.append(pl.BlockSpec((tk, tn), lambda i, j, k: (k, j)))
    args = a_parts + [b]
    if mode == "headnorm":
        in_specs += [pl.BlockSpec((1, tn), lambda i, j, k: (0, j)), pl.BlockSpec((1, tn), lambda i, j, k: (0, j))]
        args += [gain, flag]
    elif mode == "residual":
        in_specs += [pl.BlockSpec((tm, tn), lambda i, j, k: (i, j))]
        args += [residual]
    scratch = [pltpu.VMEM((tm, tn), F32)] if nk > 1 else []
    return pl.pallas_call(
        functools.partial(_mm_kernel, n_a=len(a_parts), nk=nk, mode=mode, n_norm_tiles=n_norm_tiles,
                          cast_b=False),
        grid=(m // tm, n // tn, nk),
        in_specs=in_specs,
        out_specs=pl.BlockSpec((tm, tn), lambda i, j, k: (i, j)),
        out_shape=jax.ShapeDtypeStruct((m, n), out_dtype),
        scratch_shapes=scratch,
        compiler_params=_cparams("parallel", "parallel", "arbitrary"),
        name=name,
    )(*args)


def _matmul_cast(a, b_f32, *, col_block=None, n=None, mode="plain", out_dtype=F32, gain=None, flag=None,
                 n_norm_tiles=0, residual=None, tm=1024, tn=512, name="matmul"):
    a_parts = list(a) if isinstance(a, (list, tuple)) else [a]
    m = a_parts[0].shape[0]
    kdim = b_f32.shape[0]
    n = b_f32.shape[1] if n is None else n
    tm = tm if m % tm == 0 else min(512, m)
    src = (lambda j: j) if col_block is None else col_block
    in_specs = [pl.BlockSpec((tm, part.shape[1]), lambda j, i: (i, 0)) for part in a_parts]
    in_specs.append(pl.BlockSpec((kdim, tn), lambda j, i: (0, src(j))))
    args = a_parts + [b_f32]
    if mode == "headnorm":
        in_specs += [pl.BlockSpec((1, tn), lambda j, i: (0, j)), pl.BlockSpec((1, tn), lambda j, i: (0, j))]
        args += [gain, flag]
    elif mode == "residual":
        in_specs += [pl.BlockSpec((tm, tn), lambda j, i: (i, j))]
        args += [residual]
    return pl.pallas_call(
        functools.partial(_mm_kernel, n_a=len(a_parts), nk=1, mode=mode, n_norm_tiles=n_norm_tiles, cast_b=True),
        grid=(n // tn, m // tm),
        in_specs=in_specs,
        out_specs=[pl.BlockSpec((tm, tn), lambda j, i: (i, j)), pl.BlockSpec((kdim, tn), lambda j, i: (0, j))],
        out_shape=[jax.ShapeDtypeStruct((m, n), out_dtype), jax.ShapeDtypeStruct((kdim, n), BF16)],
        compiler_params=_cparams("parallel", "arbitrary"),
        name=name,
    )(*args)


def _cumsum_block(x, carry):
    n = x.shape[1]
    r = lax.broadcasted_iota(I32, (n, n), 0)
    c = lax.broadcasted_iota(I32, (n, n), 1)
    tri = jnp.where(r <= c, 1.0, 0.0).astype(BF16)
    hi = x.astype(BF16)
    r1 = x - hi.astype(F32)
    mid = r1.astype(BF16)
    lo = (r1 - mid.astype(F32)).astype(BF16)
    cs = _dot(hi, tri) + _dot(mid, tri) + _dot(lo, tri) + carry
    return cs, cs[:, n - 1:n]


def _logf_cum_kernel(*refs, p_len, t_len, blk):
    if p_len:
        raw_ref, fb_ref, past_ref, lf_ref, cum_ref = refs
    else:
        raw_ref, fb_ref, lf_ref, cum_ref = refs
    x = raw_ref[0] + fb_ref[...]
    lf = jnp.minimum(x, 0.0) - jnp.log1p(jnp.exp(-jnp.abs(x)))
    lf_ref[0] = lf
    carry = jnp.zeros((H_B, 1), F32)
    for s in range(0, p_len, blk):
        cs, carry = _cumsum_block(past_ref[0, :, s:s + blk], carry)
        cum_ref[0, :, s:s + blk] = cs
    nb = min(blk, t_len)
    for s in range(0, t_len, nb):
        cs, carry = _cumsum_block(lf[:, s:s + nb], carry)
        cum_ref[0, :, p_len + s:p_len + s + nb] = cs


def _logf_cum(raw_h, fbias, past_h):
    b, h, t = raw_h.shape
    p = 0 if past_h is None else past_h.shape[2]
    in_specs = [pl.BlockSpec((1, h, t), lambda i: (i, 0, 0)), pl.BlockSpec((h, 1), lambda i: (0, 0))]
    args = [raw_h, fbias.reshape(h, 1)]
    if p:
        in_specs.append(pl.BlockSpec((1, h, p), lambda i: (i, 0, 0)))
        args.append(past_h)
    return pl.pallas_call(
        functools.partial(_logf_cum_kernel, p_len=p, t_len=t, blk=512),
        grid=(b,),
        in_specs=in_specs,
        out_specs=[pl.BlockSpec((1, h, t), lambda i: (i, 0, 0)), pl.BlockSpec((1, h, p + t), lambda i: (i, 0, 0))],
        out_shape=[jax.ShapeDtypeStruct((b, h, t), F32), jax.ShapeDtypeStruct((b, h, p + t), F32)],
        compiler_params=_cparams("parallel"),
        name="logf_cum",
    )(*args)


def _toeplitz(row_vals, rows, cols):
    w = row_vals.shape[1]
    full = jnp.broadcast_to(row_vals[0:1, :], (rows, w))
    return pltpu.roll(full, 0, 1, stride=1, stride_axis=0)[:, :cols]


def _t5_strip_kernel(tab_ref, o_ref, *, tile):
    h = pl.program_id(0)
    width = 4 * tile
    m = lax.broadcasted_iota(I32, (8, width), 1)
    d = jnp.where(m < width // 2, m, m - width)
    rel = d - tile
    n = jnp.abs(rel)
    large = jnp.full_like(n, 8)
    for thr in (12, 16, 23, 32, 46, 64, 91):
        large = large + jnp.where(n >= thr, 1, 0)
    bucket = jnp.where(rel > 0, T5_BUCKETS // 2, 0) + jnp.where(n < 8, n, large)
    vals = jnp.zeros((8, width), F32)
    for bkt in range(T5_BUCKETS):
        vals = jnp.where(bucket == bkt, tab_ref[bkt, h], vals)
    vals = vals - tab_ref[T5_BUCKETS // 2 - 1, h]
    o_ref[0] = _toeplitz(vals, tile, 2 * tile)


def _t5_strip(t5_table, tile):
    return pl.pallas_call(
        functools.partial(_t5_strip_kernel, tile=tile),
        grid=(H_A,),
        in_specs=[pl.BlockSpec(memory_space=pltpu.SMEM)],
        out_specs=pl.BlockSpec((1, tile, 2 * tile), lambda h: (h, 0, 0)),
        out_shape=jax.ShapeDtypeStruct((H_A, tile, 2 * tile), F32),
        compiler_params=_cparams("parallel"),
        name="t5_strip",
    )(t5_table)


def _band_bias_kernel(tab_ref, o_ref, *, rows, cols, width):
    h = pl.program_id(0)
    m = lax.broadcasted_iota(I32, (8, width), 1)
    d = jnp.where(m < width // 2, m, m - width)
    idx = jnp.clip(d - D_BAND_LEFT, -D_REL_CLIP, D_REL_CLIP) + D_REL_CLIP
    vals = jnp.zeros((8, width), F32)
    for r in range(2 * D_REL_CLIP + 1):
        vals = jnp.where(idx == r, tab_ref[r, h], vals)
    vals = vals - tab_ref[0, h]
    i_chunk = lax.broadcasted_iota(I32, (rows, cols), 0) >> CHUNK_SHIFT
    c_chunk = lax.broadcasted_iota(I32, (rows, cols), 1) >> CHUNK_SHIFT
    visible = (c_chunk >= i_chunk) & (c_chunk <= i_chunk + D_LEFT_CHUNKS)
    o_ref[0] = jnp.where(visible, _toeplitz(vals, rows, cols), NEG_INF)


def _band_bias(rel_table):
    rows, cols, width = D_BAND_LEFT, 2 * D_BAND_LEFT, 4 * D_BAND_LEFT
    return pl.pallas_call(
        functools.partial(_band_bias_kernel, rows=rows, cols=cols, width=width),
        grid=(H_D,),
        in_specs=[pl.BlockSpec(memory_space=pltpu.SMEM)],
        out_specs=pl.BlockSpec((1, rows, cols), lambda h: (h, 0, 0)),
        out_shape=jax.ShapeDtypeStruct((H_D, rows, cols), F32),
        compiler_params=_cparams("parallel"),
        name="band_bias",
    )(rel_table)


def _select_kernel(*refs, n_g, sb, tq, nq, lp, l_valid, q_base, n_sel, tk_out, causal_prefix):
    iq_refs, misc_refs = refs[:n_g], refs[n_g:2 * n_g]
    kk_ref, o_ref, key_ref = refs[2 * n_g:]
    step = pl.program_id(1)
    rows = sb * tq
    kf = float(n_sel)
    gs = range(n_g)

    def count(pred):
        return jnp.sum(jnp.where(pred, 1.0, 0.0), axis=-1, keepdims=True)

    def run(tiles):
        pfx = [p for _, p in tiles]
        adm, k_pos = [], []
        for g, (q_tile, p) in enumerate(tiles):
            scores = []
            for s in range(sb):
                iq = iq_refs[g][s]
                iw = misc_refs[g][s][:, MISC_IW:MISC_IW + IDX_HEADS]
                ik = kk_ref[s, 0:p, 0:IDX_DIM].astype(BF16)
                score = jnp.zeros((tq, p), F32)
                for h in range(IDX_HEADS):
                    dots = _dot_nt(iq[:, h * IDX_DIM:(h + 1) * IDX_DIM].astype(BF16), ik)
                    score = score + iw[:, h:h + 1] * jnp.maximum(dots, 0.0)
                scores.append(score)
            score = jnp.concatenate(scores, axis=0) if sb > 1 else scores[0]
            row_in_tile = jnp.concatenate([lax.broadcasted_iota(I32, (tq, 1), 0)] * sb, axis=0)
            q_pos = q_base + q_tile * tq + row_in_tile
            kp = lax.broadcasted_iota(I32, (1, p), 1)
            ok = ((kp >> CHUNK_SHIFT) <= (q_pos >> CHUNK_SHIFT)) & (kp < l_valid)
            bits = lax.bitcast_convert_type(jnp.where(ok, score, NEG_INF), I32)
            key_ref[g, :, 0:p] = jnp.where(bits < 0, bits ^ jnp.int32(0x7FFFFFFF), bits)
            adm.append(ok)
            k_pos.append(kp)

        def keys(g):
            return key_ref[g, :, 0:pfx[g]]

        t0 = tuple(jnp.where(count(keys(g) >= 0) >= kf, jnp.int32(0), jnp.int32(INT32_MIN)) for g in gs)

        def thr_body(i, ts):
            bit = jnp.int32(1) << (30 - i)
            return tuple(jnp.where(count(keys(g) >= ts[g] + bit) >= kf, ts[g] + bit, ts[g]) for g in gs)

        thr = lax.fori_loop(0, 31, thr_body, t0)
        need = [kf - count(keys(g) > thr[g]) for g in gs]
        tied = [(count(keys(g) == thr[g]) != need[g]) & (thr[g] > NEG_INF_KEY) for g in gs]
        any_tied = sum(jnp.max(jnp.where(tied[g], 1.0, 0.0)) for g in gs) > 0.5

        def all_tied():
            return tuple(jnp.where(keys(g) == thr[g], 1.0, 0.0) for g in gs)

        def lowest_tied():
            r = lax.broadcasted_iota(I32, (LANE, 2 * LANE), 0)
            c = lax.broadcasted_iota(I32, (LANE, 2 * LANE), 1)
            count_mat = jnp.where((r <= c) | (c >= LANE), 1.0, 0.0).astype(BF16)
            out = []
            for g, tied_mask in enumerate(all_tied()):
                seen = jnp.zeros((rows, LANE), F32)
                kept = []
                for blk in _lane_blocks(tied_mask):
                    both = _dot(blk.astype(BF16), count_mat)
                    kept.append(jnp.where(both[:, :LANE] + seen <= need[g], blk, 0.0))
                    seen = seen + both[:, LANE:]
                out.append(jnp.concatenate(kept, axis=1) if len(kept) > 1 else kept[0])
            return tuple(out)

        take_tied = lax.cond(any_tied, lowest_tied, all_tied)
        for g in gs:
            sel = ((keys(g) > thr[g]) | (take_tied[g] > 0.5)) & adm[g]
            madd = jnp.where(sel, 0.0, NEG_INF)
            for s in range(sb):
                for kt in range(lp // tk_out):
                    if (kt + 1) * tk_out <= pfx[g]:
                        o_ref[s, g, 0, kt] = madd[s * tq:(s + 1) * tq, kt * tk_out:(kt + 1) * tk_out]
                    else:
                        o_ref[s, g, 0, kt] = jnp.full((tq, tk_out), NEG_INF, F32)

    if not causal_prefix:
        run([(step, lp)] * n_g)
    elif n_g == 2:
        for j in range(nq // 2):
            pl.when(step == j)(functools.partial(run, [(j, (j + 1) * tq), (nq - 1 - j, (nq - j) * tq)]))
    else:
        for j in range(nq):
            pl.when(step == j)(functools.partial(run, [(j, (j + 1) * tq)]))


def _select_mask(iq_arr, iq_col, misc_arr, misc_col, kk_arr, kk_col, *, t, tq, lp, l_valid, q_base, n_sel,
                 tk_out, causal_prefix):
    b = iq_arr.shape[0]
    nq = t // tq
    wk = kk_arr.shape[2] if kk_col is None else LANE
    kcol = 0 if kk_col is None else kk_col
    sb = 4 if (nq == 1 and tq <= 64 and b % 4 == 0) else 1
    n_g = 2 if (causal_prefix and nq % 2 == 0) else 1
    steps = nq // n_g
    tile_of = [lambda j: j, lambda j: nq - 1 - j]
    nkt = lp // tk_out
    in_specs = [pl.BlockSpec((sb, tq, IDX_HEADS * IDX_DIM), lambda i, j, g=g: (i, tile_of[g](j), iq_col))
                for g in range(n_g)]
    in_specs += [pl.BlockSpec((sb, tq, LANE), lambda i, j, g=g: (i, tile_of[g](j), misc_col)) for g in range(n_g)]
    in_specs.append(pl.BlockSpec((sb, lp, wk), lambda i, j: (i, 0, kcol)))
    return pl.pallas_call(
        functools.partial(_select_kernel, n_g=n_g, sb=sb, tq=tq, nq=nq, lp=lp, l_valid=l_valid, q_base=q_base,
                          n_sel=n_sel, tk_out=tk_out, causal_prefix=causal_prefix),
        grid=(b // sb, steps),
        in_specs=in_specs,
        out_specs=pl.BlockSpec((sb, n_g, 1, nkt, tq, tk_out), lambda i, j: (i, 0, j, 0, 0, 0)),
        out_shape=jax.ShapeDtypeStruct((b, n_g, steps, nkt, tq, tk_out), F32),
        scratch_shapes=[pltpu.VMEM((n_g, sb * tq, lp), I32)],
        compiler_params=_cparams("parallel", "parallel"),
        name="a_select",
    )(*([iq_arr] * n_g), *([misc_arr] * n_g), kk_arr)


def _mask_tile_index(q_tile, nq, n_g):
    if n_g == 1:
        return 0, q_tile
    upper = q_tile >= nq // 2
    return jnp.where(upper, 1, 0), jnp.where(upper, nq - 1 - q_tile, q_tile)


def _online_stats(s, m_s, l_s):
    blocks = _lane_blocks(s)
    mx = blocks[0]
    for blk in blocks[1:]:
        mx = jnp.maximum(mx, blk)
    m_prev = m_s[...]
    m_new = jnp.maximum(m_prev, jnp.max(mx, axis=-1, keepdims=True))
    alpha = jnp.exp(m_prev - m_new)
    ps = [jnp.exp(blk - m_new) for blk in blocks]
    psum = ps[0]
    for p in ps[1:]:
        psum = psum + p
    l_s[...] = alpha * l_s[...] + psum
    m_s[...] = m_new
    p_all = jnp.concatenate([p.astype(BF16) for p in ps], axis=1) if len(ps) > 1 else ps[0].astype(BF16)
    return alpha, p_all


def _online_update(s, v_bf16, m_s, l_s, acc_s):
    alpha, p_all = _online_stats(s, m_s, l_s)
    acc_s[...] = alpha * acc_s[...] + _dot(p_all, v_bf16)


def _online_init(m_s, l_s, acc_s):
    m_s[...] = jnp.full_like(m_s, NEG_INF)
    l_s[...] = jnp.zeros_like(l_s)
    acc_s[...] = jnp.zeros_like(acc_s)


def _online_result(l_s, acc_s):
    return acc_s[...] / jnp.sum(l_s[...], axis=-1, keepdims=True)


def _stack_heads(q):
    return jnp.concatenate([q[:, r * HEAD_DIM:(r + 1) * HEAD_DIM] for r in range(REP_A)], axis=0).astype(BF16)


def _attn_a_prompt_kernel(q_ref, k_ref, v_ref, mask_ref, tz_ref, o_ref, m_s, l_s, acc_s, *, tq):
    qi = pl.program_id(1)
    width = REP_A * HEAD_DIM
    qs = [_stack_heads(q_ref[0, :, g * width:(g + 1) * width] * SCALE) for g in range(KV_A)]
    _online_init(m_s, l_s, acc_s)

    def tile(kt, bias_cols):
        off = pl.multiple_of(kt * tq, tq)
        mask = mask_ref[0, 0, 0, kt][None]
        logits = []
        for g in range(KV_A):
            k = k_ref[0, pl.ds(off, tq), g * HEAD_DIM:(g + 1) * HEAD_DIM].astype(BF16)
            s = _dot_nt(qs[g], k).reshape(REP_A, tq, tq) + mask
            if bias_cols is not None:
                s = s + tz_ref[g * REP_A:(g + 1) * REP_A, :, bias_cols]
            logits.append(s.reshape(REP_A * tq, tq))
        stats = [_online_stats(logits[g], m_s.at[g], l_s.at[g]) for g in range(KV_A)]
        for g in range(KV_A):
            alpha, p_all = stats[g]
            v = v_ref[0, pl.ds(off, tq), g * HEAD_DIM:(g + 1) * HEAD_DIM].astype(BF16)
            acc_s[g] = alpha * acc_s[g] + _dot(p_all, v)

    def plain_body(kt, carry):
        tile(kt, None)
        return carry

    lax.fori_loop(0, jnp.maximum(qi - 1, 0), plain_body, 0)

    @pl.when(qi >= 1)
    def _():
        tile(qi - 1, slice(0, tq))

    tile(qi, slice(tq, 2 * tq))
    for g in range(KV_A):
        out = _online_result(l_s.at[g], acc_s.at[g])
        for r in range(REP_A):
            col = (g * REP_A + r) * HEAD_DIM
            o_ref[0, :, col:col + HEAD_DIM] = out[r * tq:(r + 1) * tq].astype(o_ref.dtype)


def _attn_a_prompt(proj3, mask, tz, *, tq):
    b, t, _ = proj3.shape
    nq = t // tq
    qw, kw = H_A * HEAD_DIM, KV_A * HEAD_DIM
    rows = REP_A * tq
    n_g = mask.shape[1]

    def mask_idx(i, j):
        grp, pos = _mask_tile_index(j, nq, n_g)
        return (i, grp, pos, 0, 0, 0)

    return pl.pallas_call(
        functools.partial(_attn_a_prompt_kernel, tq=tq),
        grid=(b, nq),
        in_specs=[
            pl.BlockSpec((1, tq, qw), lambda i, j: (i, j, EV_AQ // qw)),
            pl.BlockSpec((1, t, kw), lambda i, j: (i, 0, EV_AK // kw)),
            pl.BlockSpec((1, t, kw), lambda i, j: (i, 0, EV_AV // kw)),
            pl.BlockSpec((1, 1, 1, nq, tq, tq), mask_idx),
            pl.BlockSpec((H_A, tq, 2 * tq), lambda i, j: (0, 0, 0)),
        ],
        out_specs=pl.BlockSpec((1, tq, qw), lambda i, j: (i, j, 0)),
        out_shape=jax.ShapeDtypeStruct((b, t, qw), BF16),
        scratch_shapes=[pltpu.VMEM((KV_A, rows, LANE), F32), pltpu.VMEM((KV_A, rows, LANE), F32),
                        pltpu.VMEM((KV_A, rows, HEAD_DIM), F32)],
        compiler_params=_cparams("parallel", "parallel"),
        name="attn_a_prompt",
    )(proj3, proj3, proj3, mask, tz)


def _attn_a_step_kernel(q_ref, kc_ref, vc_ref, kn_ref, vn_ref, mask_ref, tz_ref, o_ref, *, t, p):
    width = REP_A * HEAD_DIM
    mask = mask_ref[0, 0, 0, 0]
    logits = []
    for g in range(KV_A):
        heads = slice(g * REP_A, (g + 1) * REP_A)
        qs = _stack_heads(q_ref[0, :, g * width:(g + 1) * width] * SCALE)
        kc = kc_ref[0, pl.ds(g, p, stride=KV_A), :].astype(BF16)
        kn = _pad_keys(kn_ref[0, :, g * HEAD_DIM:(g + 1) * HEAD_DIM], LANE).astype(BF16)
        s_c = _dot_nt(qs, kc).reshape(REP_A, t, p) + mask[None, :, 0:p]
        s_n = (_dot_nt(qs, kn).reshape(REP_A, t, LANE) + mask[None, :, p:p + LANE]
               + tz_ref[heads, :, LANE:2 * LANE])
        blocks = _lane_blocks(s_c)
        blocks[-1] = blocks[-1] + tz_ref[heads, :, 0:LANE]
        blocks.append(s_n)
        logits.append(blocks)
    probs = []
    for blocks in logits:
        mx = blocks[0]
        for blk in blocks[1:]:
            mx = jnp.maximum(mx, blk)
        m = jnp.max(mx, axis=-1, keepdims=True)
        ps = [jnp.exp(blk - m) for blk in blocks]
        psum = ps[0]
        for pb in ps[1:]:
            psum = psum + pb
        l = jnp.sum(psum, axis=-1, keepdims=True)
        p_c = jnp.concatenate([pb.astype(BF16) for pb in ps[:-1]], axis=-1).reshape(REP_A * t, p)
        probs.append((p_c, ps[-1].astype(BF16).reshape(REP_A * t, LANE), l))
    for g, (p_c, p_n, l) in enumerate(probs):
        vc = vc_ref[0, pl.ds(g, p, stride=KV_A), :].astype(BF16)
        vn = _pad_keys(vn_ref[0, :, g * HEAD_DIM:(g + 1) * HEAD_DIM], LANE).astype(BF16)
        out = (_dot(p_c, vc) + _dot(p_n, vn)).reshape(REP_A, t, HEAD_DIM) / l
        for r in range(REP_A):
            col = (g * REP_A + r) * HEAD_DIM
            o_ref[0, :, col:col + HEAD_DIM] = out[r].astype(o_ref.dtype)


def _attn_a_step(proj3, cache_k, cache_v, mask, tz):
    b, t, _ = proj3.shape
    p = cache_k.shape[1] // KV_A
    qw, kw = H_A * HEAD_DIM, KV_A * HEAD_DIM
    return pl.pallas_call(
        functools.partial(_attn_a_step_kernel, t=t, p=p),
        grid=(b,),
        in_specs=[
            pl.BlockSpec((1, t, qw), lambda i: (i, 0, EV_AQ // qw)),
            pl.BlockSpec((1, p * KV_A, HEAD_DIM), lambda i: (i, 0, 0)),
            pl.BlockSpec((1, p * KV_A, HEAD_DIM), lambda i: (i, 0, 0)),
            pl.BlockSpec((1, t, kw), lambda i: (i, 0, EV_AK // kw)),
            pl.BlockSpec((1, t, kw), lambda i: (i, 0, EV_AV // kw)),
            pl.BlockSpec((1, 1, 1, 1, t, p + LANE), lambda i: (i, 0, 0, 0, 0, 0)),
            pl.BlockSpec((H_A, t, 2 * LANE), lambda i: (0, 0, 0)),
        ],
        out_specs=pl.BlockSpec((1, t, qw), lambda i: (i, 0, 0)),
        out_shape=jax.ShapeDtypeStruct((b, t, qw), BF16),
        compiler_params=_cparams("parallel"),
        name="attn_a_step",
    )(proj3, cache_k, cache_v, proj3, proj3, mask, tz)


def _causal_mask(s, row0=0):
    r = row0 + lax.broadcasted_iota(I32, s.shape, 0)
    c = lax.broadcasted_iota(I32, s.shape, 1)
    return jnp.where(c <= r, s, NEG_INF)


B_HEADS_PER_STEP = 2


def _attn_b_prompt_kernel(q_ref, k_ref, v_ref, ck_ref, o_ref, m_s, l_s, acc_s, *, tq):
    qi = pl.program_id(2)
    heads = [slice(h * HEAD_DIM, (h + 1) * HEAD_DIM) for h in range(B_HEADS_PER_STEP)]
    qs = [(q_ref[0, :, sl] * SCALE).astype(BF16) for sl in heads]
    _online_init(m_s, l_s, acc_s)

    def tile(kt, diag):
        off = pl.multiple_of(kt * tq, tq)
        logits = []
        for h, sl in enumerate(heads):
            s = _dot_nt(qs[h], k_ref[0, pl.ds(off, tq), sl].astype(BF16)) - ck_ref[0, h, kt]
            logits.append(_causal_mask(s) if diag else s)
        stats = [_online_stats(logits[h], m_s.at[h], l_s.at[h]) for h in range(B_HEADS_PER_STEP)]
        for h, sl in enumerate(heads):
            alpha, p_all = stats[h]
            acc_s[h] = alpha * acc_s[h] + _dot(p_all, v_ref[0, pl.ds(off, tq), sl].astype(BF16))

    def body(kt, carry):
        tile(kt, False)
        return carry

    lax.fori_loop(0, qi, body, 0)
    tile(qi, True)
    for h, sl in enumerate(heads):
        o_ref[0, :, sl] = _online_result(l_s.at[h], acc_s.at[h]).astype(o_ref.dtype)


def _attn_b_prompt(proj3, cum_h, *, tq=512):
    b, t, _ = proj3.shape
    nq = t // tq
    ck = cum_h.reshape(b, H_B, nq, 1, tq)
    hp = B_HEADS_PER_STEP
    w = hp * HEAD_DIM
    return pl.pallas_call(
        functools.partial(_attn_b_prompt_kernel, tq=tq),
        grid=(b, H_B // hp, nq),
        in_specs=[
            pl.BlockSpec((1, tq, w), lambda i, h, j: (i, j, EV_BQ // w + h)),
            pl.BlockSpec((1, t, w), lambda i, h, j: (i, 0, EV_BK // w + h)),
            pl.BlockSpec((1, t, w), lambda i, h, j: (i, 0, EV_BV // w + h)),
            pl.BlockSpec((1, hp, nq, 1, tq), lambda i, h, j: (i, h, 0, 0, 0)),
        ],
        out_specs=pl.BlockSpec((1, tq, w), lambda i, h, j: (i, j, h)),
        out_shape=jax.ShapeDtypeStruct((b, t, H_B * HEAD_DIM), BF16),
        scratch_shapes=[pltpu.VMEM((hp, tq, LANE), F32), pltpu.VMEM((hp, tq, LANE), F32),
                        pltpu.VMEM((hp, tq, HEAD_DIM), F32)],
        compiler_params=_cparams("parallel", "parallel", "parallel"),
        name="attn_b_prompt",
    )(proj3, proj3, proj3, ck)


def _attn_b_step_kernel(q_ref, kc_ref, vc_ref, kn_ref, vn_ref, ckc_ref, ckn_ref, o_ref, m_s, l_s, acc_s,
                        *, nkc, t, tkc):
    kb = pl.program_id(1)

    @pl.when(kb == 0)
    def _():
        _online_init(m_s, l_s, acc_s)

    def q_head(h):
        return (q_ref[0, :, h * HEAD_DIM:(h + 1) * HEAD_DIM] * SCALE).astype(BF16)

    @pl.when(kb < nkc)
    def _():
        ck = ckc_ref[0, 0]
        logits = [_dot_nt(q_head(h), kc_ref[0, pl.ds(h, tkc, stride=H_B), :].astype(BF16)) - ck[h:h + 1, :]
                  for h in range(H_B)]
        stats = [_online_stats(logits[h], m_s.at[h], l_s.at[h]) for h in range(H_B)]
        for h in range(H_B):
            alpha, p_all = stats[h]
            v = vc_ref[0, pl.ds(h, tkc, stride=H_B), :].astype(BF16)
            acc_s[h] = alpha * acc_s[h] + _dot(p_all, v)

    @pl.when(kb == nkc)
    def _():
        ckn = ckn_ref[0]
        heads = [slice(h * HEAD_DIM, (h + 1) * HEAD_DIM) for h in range(H_B)]
        logits = [_causal_mask(_dot_nt(q_head(h), _pad_keys(kn_ref[0, :, heads[h]], LANE).astype(BF16))
                               - ckn[h:h + 1, :]) for h in range(H_B)]
        stats = [_online_stats(logits[h], m_s.at[h], l_s.at[h]) for h in range(H_B)]
        for h in range(H_B):
            alpha, p_all = stats[h]
            v = _pad_keys(vn_ref[0, :, heads[h]], LANE).astype(BF16)
            acc_s[h] = alpha * acc_s[h] + _dot(p_all, v)
            o_ref[0, :, heads[h]] = _online_result(l_s.at[h], acc_s.at[h]).astype(o_ref.dtype)


def _attn_b_step(proj3, cache_k, cache_v, cum_h, *, tkc=1024):
    b, t, _ = proj3.shape
    p = cache_k.shape[1] // H_B
    nkc = p // tkc
    w = H_B * HEAD_DIM
    ck_cache = cum_h[:, :, :p].reshape(b, H_B, nkc, tkc).transpose(0, 2, 1, 3)
    ck_new = jnp.pad(cum_h[:, :, p:], ((0, 0), (0, 0), (0, LANE - t)))
    last = nkc - 1
    return pl.pallas_call(
        functools.partial(_attn_b_step_kernel, nkc=nkc, t=t, tkc=tkc),
        grid=(b, nkc + 1),
        in_specs=[
            pl.BlockSpec((1, t, w), lambda i, j: (i, 0, EV_BQ // w)),
            pl.BlockSpec((1, tkc * H_B, HEAD_DIM), lambda i, j: (i, jnp.minimum(j, last), 0)),
            pl.BlockSpec((1, tkc * H_B, HEAD_DIM), lambda i, j: (i, jnp.minimum(j, last), 0)),
            pl.BlockSpec((1, t, w), lambda i, j: (i, 0, EV_BK // w)),
            pl.BlockSpec((1, t, w), lambda i, j: (i, 0, EV_BV // w)),
            pl.BlockSpec((1, 1, H_B, tkc), lambda i, j: (i, jnp.minimum(j, last), 0, 0)),
            pl.BlockSpec((1, H_B, LANE), lambda i, j: (i, 0, 0)),
        ],
        out_specs=pl.BlockSpec((1, t, w), lambda i, j: (i, 0, 0)),
        out_shape=jax.ShapeDtypeStruct((b, t, w), BF16),
        scratch_shapes=[pltpu.VMEM((H_B, t, LANE), F32), pltpu.VMEM((H_B, t, LANE), F32),
                        pltpu.VMEM((H_B, t, HEAD_DIM), F32)],
        compiler_params=_cparams("parallel", "arbitrary"),
        name="attn_b_step",
    )(proj3, cache_k, cache_v, proj3, proj3, ck_cache, ck_new)


def _tri_ones():
    r = lax.broadcasted_iota(I32, (2 * LANE, 2 * LANE), 0) & (LANE - 1)
    c = lax.broadcasted_iota(I32, (2 * LANE, 2 * LANE), 1)
    return jnp.where((r > c) | (c >= LANE), 1.0, 0.0).astype(BF16)


def _stick_scores(z, row_minus_col, tri_ones):
    rc_blocks = None if row_minus_col is None else _lane_blocks(row_minus_col)
    out = []
    for j, zb in enumerate(_lane_blocks(z)):
        tail = jnp.log(1.0 + jnp.exp(-jnp.abs(zb)))
        log_beta = jnp.minimum(zb, 0.0) - tail
        log_keep = -jnp.maximum(zb, 0.0) - tail
        strict = None
        if rc_blocks is not None:
            strict = rc_blocks[j] > 0
            log_keep = jnp.where(strict, log_keep, 0.0)
        hi = log_keep.astype(BF16)
        lo = (log_keep - hi.astype(F32)).astype(BF16)
        out.append((log_beta, strict, _dot(jnp.concatenate([hi, lo], axis=1), tri_ones)))
    return out


def _stick_weights(blocks, run):
    ws = [None] * len(blocks)
    for j in reversed(range(len(blocks))):
        log_beta, strict, sums = blocks[j]
        w = jnp.exp(log_beta + sums[:, :LANE] + run)
        if strict is not None:
            w = jnp.where(strict, w, 0.0)
        ws[j] = w.astype(BF16)
        run = run + sums[:, LANE:]
    return (jnp.concatenate(ws, axis=1) if len(ws) > 1 else ws[0]), run


def _stick_tile(q_bf16, k_bf16, v_bf16, run, tri_ones, row_minus_col):
    w_all, run = _stick_weights(_stick_scores(_dot_nt(q_bf16, k_bf16), row_minus_col, tri_ones), run)
    return _dot(w_all, v_bf16), run


C_HEADS_PER_STEP = 2


def _attn_c_prompt_kernel(q_ref, k_ref, v_ref, o_ref, run_s, acc_s, *, tq):
    qi = pl.program_id(2)
    heads = [slice(h * HEAD_DIM, (h + 1) * HEAD_DIM) for h in range(C_HEADS_PER_STEP)]
    qs = [(q_ref[0, :, sl] * SCALE).astype(BF16) for sl in heads]
    tri_ones = _tri_ones()

    def tile(kt, row_minus_col, first):
        off = pl.multiple_of(kt * tq, tq)
        zs = [_dot_nt(qs[h], k_ref[0, pl.ds(off, tq), sl].astype(BF16)) for h, sl in enumerate(heads)]
        scored = [_stick_scores(z, row_minus_col, tri_ones) for z in zs]
        for h, sl in enumerate(heads):
            run = jnp.zeros((tq, LANE), F32) if first else run_s[h]
            w_all, run = _stick_weights(scored[h], run)
            out = _dot(w_all, v_ref[0, pl.ds(off, tq), sl].astype(BF16))
            acc_s[h] = out if first else acc_s[h] + out
            run_s[h] = run

    rc = lax.broadcasted_iota(I32, (tq, tq), 0) - lax.broadcasted_iota(I32, (tq, tq), 1)
    tile(qi, rc, True)

    def body(i, carry):
        tile(qi - 1 - i, None, False)
        return carry

    lax.fori_loop(0, qi, body, 0)
    for h, sl in enumerate(heads):
        o_ref[0, :, sl] = acc_s[h].astype(o_ref.dtype)


def _attn_c_prompt(proj3, *, tq=512):
    b, t, _ = proj3.shape
    nq = t // tq
    hp = C_HEADS_PER_STEP
    w = hp * HEAD_DIM
    return pl.pallas_call(
        functools.partial(_attn_c_prompt_kernel, tq=tq),
        grid=(b, H_C // hp, nq),
        in_specs=[
            pl.BlockSpec((1, tq, w), lambda i, h, j: (i, j, OD_CQ // w + h)),
            pl.BlockSpec((1, t, w), lambda i, h, j: (i, 0, OD_CK // w + h)),
            pl.BlockSpec((1, t, w), lambda i, h, j: (i, 0, OD_CV // w + h)),
        ],
        out_specs=pl.BlockSpec((1, tq, w), lambda i, h, j: (i, j, h)),
        out_shape=jax.ShapeDtypeStruct((b, t, H_C * HEAD_DIM), BF16),
        scratch_shapes=[pltpu.VMEM((hp, tq, LANE), F32), pltpu.VMEM((hp, tq, HEAD_DIM), F32)],
        compiler_params=_cparams("parallel", "parallel", "parallel"),
        name="attn_c_prompt",
    )(proj3, proj3, proj3)


def _attn_c_step_kernel(q_ref, kc_ref, vc_ref, kn_ref, vn_ref, o_ref, run_s, acc_s, *, nkc, t, tkc, sub):
    kb = pl.program_id(1)
    tri_ones = _tri_ones()

    def q_head(h):
        return (q_ref[0, :, h * HEAD_DIM:(h + 1) * HEAD_DIM] * SCALE).astype(BF16)

    @pl.when(kb == 0)
    def _():
        rc = lax.broadcasted_iota(I32, (t, LANE), 0) - lax.broadcasted_iota(I32, (t, LANE), 1)
        for h in range(H_C):
            sl = slice(h * HEAD_DIM, (h + 1) * HEAD_DIM)
            k = _pad_keys(kn_ref[0, :, sl], LANE).astype(BF16)
            v = _pad_keys(vn_ref[0, :, sl], LANE).astype(BF16)
            out, run = _stick_tile(q_head(h), k, v, jnp.zeros((t, LANE), F32), tri_ones, rc)
            acc_s[h] = out
            run_s[h] = run

    @pl.when(kb > 0)
    def _():
        def body(i, carry):
            row0 = (tkc - sub - i * sub) * H_C
            zs = [_dot_nt(q_head(h), kc_ref[0, pl.ds(row0 + h, sub, stride=H_C), :].astype(BF16))
                  for h in range(H_C)]
            scored = [_stick_scores(z, None, tri_ones) for z in zs]
            for h in range(H_C):
                w_all, run = _stick_weights(scored[h], run_s[h])
                v = vc_ref[0, pl.ds(row0 + h, sub, stride=H_C), :].astype(BF16)
                acc_s[h] += _dot(w_all, v)
                run_s[h] = run
            return carry

        lax.fori_loop(0, tkc // sub, body, 0)

    @pl.when(kb == nkc)
    def _():
        for h in range(H_C):
            o_ref[0, :, h * HEAD_DIM:(h + 1) * HEAD_DIM] = acc_s[h].astype(o_ref.dtype)


def _attn_c_step(proj3, cache_k, cache_v, *, tkc=1024, sub=1024):
    b, t, _ = proj3.shape
    p = cache_k.shape[1] // H_C
    nkc = p // tkc
    w = H_C * HEAD_DIM

    def cache_idx(i, j):
        return (i, jnp.clip(nkc - j, 0, nkc - 1), 0)

    return pl.pallas_call(
        functools.partial(_attn_c_step_kernel, nkc=nkc, t=t, tkc=tkc, sub=sub),
        grid=(b, nkc + 1),
        in_specs=[
            pl.BlockSpec((1, t, w), lambda i, j: (i, 0, OD_CQ // w)),
            pl.BlockSpec((1, tkc * H_C, HEAD_DIM), cache_idx),
            pl.BlockSpec((1, tkc * H_C, HEAD_DIM), cache_idx),
            pl.BlockSpec((1, t, w), lambda i, j: (i, 0, OD_CK // w)),
            pl.BlockSpec((1, t, w), lambda i, j: (i, 0, OD_CV // w)),
        ],
        out_specs=pl.BlockSpec((1, t, w), lambda i, j: (i, 0, 0)),
        out_shape=jax.ShapeDtypeStruct((b, t, w), BF16),
        scratch_shapes=[pltpu.VMEM((H_C, t, LANE), F32), pltpu.VMEM((H_C, t, HEAD_DIM), F32)],
        compiler_params=_cparams("parallel", "arbitrary"),
        name="attn_c_step",
    )(proj3, cache_k, cache_v, proj3, proj3)


D_HEADS_PER_STEP = 4


def _attn_d_kernel(q_ref, kp_ref, vp_ref, ko_ref, vo_ref, bias_ref, o_ref, *, tq):
    qi = pl.program_id(2)
    heads = [slice(h * HEAD_DIM, (h + 1) * HEAD_DIM) for h in range(D_HEADS_PER_STEP)]
    qs = [(q_ref[0, :, sl] * SCALE).astype(BF16) for sl in heads]

    def own_logits(h):
        return _dot_nt(qs[h], ko_ref[0, :, heads[h]].astype(BF16)) + bias_ref[h, :, D_BAND_LEFT:D_BAND_LEFT + tq]

    def with_left():
        s_o = [own_logits(h) for h in range(D_HEADS_PER_STEP)]
        s_p = [_dot_nt(qs[h], kp_ref[0, :, heads[h]].astype(BF16)) + bias_ref[h, :, 0:D_BAND_LEFT]
               for h in range(D_HEADS_PER_STEP)]
        probs = []
        for h in range(D_HEADS_PER_STEP):
            m = jnp.maximum(jnp.max(s_p[h], axis=-1, keepdims=True), jnp.max(s_o[h], axis=-1, keepdims=True))
            p_p = jnp.exp(s_p[h] - m)
            p_o = jnp.exp(s_o[h] - m)
            l = jnp.sum(p_p, axis=-1, keepdims=True) + jnp.sum(p_o, axis=-1, keepdims=True)
            probs.append((p_p.astype(BF16), p_o.astype(BF16), l))
        for h, sl in enumerate(heads):
            p_p, p_o, l = probs[h]
            out = _dot(p_p, vp_ref[0, :, sl].astype(BF16)) + _dot(p_o, vo_ref[0, :, sl].astype(BF16))
            o_ref[0, :, sl] = (out / l).astype(o_ref.dtype)

    def own_only():
        s_o = [own_logits(h) for h in range(D_HEADS_PER_STEP)]
        for h, sl in enumerate(heads):
            p_o = jnp.exp(s_o[h] - jnp.max(s_o[h], axis=-1, keepdims=True))
            l = jnp.sum(p_o, axis=-1, keepdims=True)
            o_ref[0, :, sl] = (_dot(p_o.astype(BF16), vo_ref[0, :, sl].astype(BF16)) / l).astype(o_ref.dtype)

    pl.when(qi > 0)(with_left)
    pl.when(qi == 0)(own_only)


def _attn_d(proj3, bias):
    b, t, _ = proj3.shape
    tq = D_BAND_LEFT
    nq = t // tq
    hp = D_HEADS_PER_STEP
    w = hp * HEAD_DIM
    qc, kc, vc = OD_DQ // w, OD_DK // w, OD_DV // w
    return pl.pallas_call(
        functools.partial(_attn_d_kernel, tq=tq),
        grid=(b, H_D // hp, nq),
        in_specs=[
            pl.BlockSpec((1, tq, w), lambda i, h, j: (i, j, qc + h)),
            pl.BlockSpec((1, tq, w), lambda i, h, j: (i, jnp.maximum(j - 1, 0), kc + h)),
            pl.BlockSpec((1, tq, w), lambda i, h, j: (i, jnp.maximum(j - 1, 0), vc + h)),
            pl.BlockSpec((1, tq, w), lambda i, h, j: (i, j, kc + h)),
            pl.BlockSpec((1, tq, w), lambda i, h, j: (i, j, vc + h)),
            pl.BlockSpec((hp, tq, 2 * tq), lambda i, h, j: (h, 0, 0)),
        ],
        out_specs=pl.BlockSpec((1, tq, w), lambda i, h, j: (i, j, h)),
        out_shape=jax.ShapeDtypeStruct((b, t, H_D * HEAD_DIM), BF16),
        compiler_params=_cparams("parallel", "parallel", "parallel"),
        name="attn_d",
    )(proj3, proj3, proj3, proj3, proj3, bias)


def _attn_d_step_kernel(q_ref, kp_ref, vp_ref, kn_ref, vn_ref, bias_ref, o_ref, *, t):
    for h in range(H_D):
        sl = slice(h * HEAD_DIM, (h + 1) * HEAD_DIM)
        q = (q_ref[0, :, sl] * SCALE).astype(BF16)
        kp = kp_ref[0, pl.ds(h, D_BAND_LEFT, stride=H_D), :].astype(BF16)
        vp = vp_ref[0, pl.ds(h, D_BAND_LEFT, stride=H_D), :].astype(BF16)
        s_p = _dot_nt(q, kp) + bias_ref[h, :, 0:D_BAND_LEFT]
        s_o = _dot_nt(q, kn_ref[0, :, sl].astype(BF16)) + bias_ref[h, :, D_BAND_LEFT:D_BAND_LEFT + t]
        m = jnp.maximum(jnp.max(s_p, axis=-1, keepdims=True), jnp.max(s_o, axis=-1, keepdims=True))
        p_p = jnp.exp(s_p - m)
        p_o = jnp.exp(s_o - m)
        l = jnp.sum(p_p, axis=-1, keepdims=True) + jnp.sum(p_o, axis=-1, keepdims=True)
        out = _dot(p_p.astype(BF16), vp) + _dot(p_o.astype(BF16), vn_ref[0, :, sl].astype(BF16))
        o_ref[0, :, sl] = (out / l).astype(o_ref.dtype)


def _attn_d_step(proj3, cache_k, cache_v, bias):
    b, t, _ = proj3.shape
    w = H_D * HEAD_DIM
    return pl.pallas_call(
        functools.partial(_attn_d_step_kernel, t=t),
        grid=(b,),
        in_specs=[
            pl.BlockSpec((1, t, w), lambda i: (i, 0, OD_DQ // w)),
            pl.BlockSpec((1, D_BAND_LEFT * H_D, HEAD_DIM), lambda i: (i, 0, 0)),
            pl.BlockSpec((1, D_BAND_LEFT * H_D, HEAD_DIM), lambda i: (i, 0, 0)),
            pl.BlockSpec((1, t, w), lambda i: (i, 0, OD_DK // w)),
            pl.BlockSpec((1, t, w), lambda i: (i, 0, OD_DV // w)),
            pl.BlockSpec((H_D, t, D_BAND_LEFT + t), lambda i: (0, 0, 0)),
        ],
        out_specs=pl.BlockSpec((1, t, w), lambda i: (i, 0, 0)),
        out_shape=jax.ShapeDtypeStruct((b, t, w), BF16),
        compiler_params=_cparams("parallel"),
        name="attn_d_step",
    )(proj3, cache_k, cache_v, proj3, proj3, bias)


def _pad_rows(x, rows):
    return jnp.pad(x, ((0, 0), (0, rows - x.shape[1]), (0, 0)))


def _even_mixer(h2, b, t, past, w, tz_prompt, tz_step, tq_a, mm):
    proj = mm("in_even", h2, w["w_in"], mode="headnorm", gain=w["gain"], flag=w["flag"], n_norm_tiles=9,
              name="proj_even")
    proj3 = proj.reshape(b, t, EV_PAD)
    ak = proj3[:, :, EV_AK:EV_AK + KV_A * HEAD_DIM]
    av = proj3[:, :, EV_AV:EV_AV + KV_A * HEAD_DIM]
    ik = proj3[:, :, EV_MISC + MISC_IK:EV_MISC + MISC_IK + IDX_DIM]
    bk = proj3[:, :, EV_BK:EV_BK + H_B * HEAD_DIM]
    bv = proj3[:, :, EV_BV:EV_BV + H_B * HEAD_DIM]
    raw_h = proj3[:, :, EV_MISC + MISC_BF:EV_MISC + MISC_BF + H_B].transpose(0, 2, 1)
    iq_col = EV_IQ // (IDX_HEADS * IDX_DIM)
    misc_col = EV_MISC // LANE
    if past is None:
        logf_h, cum_h = _logf_cum(raw_h, w["f_bias"], None)
        mask = _select_mask(proj3, iq_col, proj3, misc_col, proj3, misc_col, t=t, tq=tq_a, lp=t, l_valid=t,
                            q_base=0, n_sel=min(TOPK_MAX, t // 4), tk_out=tq_a, causal_prefix=True)
        out_a = _attn_a_prompt(proj3, mask, tz_prompt, tq=tq_a)
        out_b = _attn_b_prompt(proj3, cum_h, tq=min(512, t))
    else:
        p_ak, p_av, p_ik, p_bk, p_bv, p_lf = past
        p = p_ak.shape[1]
        l_valid = p + t
        lp = p + LANE
        logf_h, cum_h = _logf_cum(raw_h, w["f_bias"], p_lf.transpose(0, 2, 1))
        ik_all = _pad_rows(jnp.concatenate([p_ik, ik], axis=1), lp)
        mask = _select_mask(proj3, iq_col, proj3, misc_col, ik_all, None, t=t, tq=t, lp=lp, l_valid=l_valid,
                            q_base=p, n_sel=min(TOPK_MAX, l_valid // 4), tk_out=lp, causal_prefix=False)
        out_a = _attn_a_step(proj3, p_ak.reshape(b, p * KV_A, HEAD_DIM), p_av.reshape(b, p * KV_A, HEAD_DIM),
                             mask, tz_step[:, :t])
        out_b = _attn_b_step(proj3, p_bk.reshape(b, p * H_B, HEAD_DIM), p_bv.reshape(b, p * H_B, HEAD_DIM), cum_h)
    mixed = [out_a.reshape(b * t, -1), out_b.reshape(b * t, -1)]
    state = (ak.reshape(b, t, KV_A, HEAD_DIM), av.reshape(b, t, KV_A, HEAD_DIM), ik,
             bk.reshape(b, t, H_B, HEAD_DIM), bv.reshape(b, t, H_B, HEAD_DIM), logf_h.transpose(0, 2, 1))
    return mixed, state


def _odd_mixer(h2, b, t, past, w, band, mm):
    proj = mm("in_odd", h2, w["w_in"], col_block=_odd_col_block, mode="headnorm", gain=w["gain"], flag=w["flag"],
              n_norm_tiles=4, name="proj_odd")
    proj3 = proj.reshape(b, t, OD_WIDTH)
    ck = proj3[:, :, OD_CK:OD_CK + H_C * HEAD_DIM]
    cv = proj3[:, :, OD_CV:OD_CV + H_C * HEAD_DIM]
    dk = proj3[:, :, OD_DK:OD_DK + H_D * HEAD_DIM]
    dv = proj3[:, :, OD_DV:OD_DV + H_D * HEAD_DIM]
    dq_col, dk_col, dv_col = OD_DQ // HEAD_DIM, OD_DK // HEAD_DIM, OD_DV // HEAD_DIM
    if past is None:
        out_c = _attn_c_prompt(proj3, tq=min(512, t))
        out_d = _attn_d(proj3, band)
        keep = min(D_BAND_LEFT, t)
        d_rows = (dk[:, t - keep:], dv[:, t - keep:])
    else:
        p_ck, p_cv, p_dk, p_dv = past
        p = p_ck.shape[1]
        out_c = _attn_c_step(proj3, p_ck.reshape(b, p * H_C, HEAD_DIM), p_cv.reshape(b, p * H_C, HEAD_DIM))
        wd = p_dk.shape[1]
        out_d = _attn_d_step(proj3, p_dk.reshape(b, wd * H_D, HEAD_DIM), p_dv.reshape(b, wd * H_D, HEAD_DIM),
                             band[:, :t, :D_BAND_LEFT + t])
        d_rows = (dk, dv)
    mixed = [out_c.reshape(b * t, -1), out_d.reshape(b * t, -1)]
    state = (ck.reshape(b, t, H_C, HEAD_DIM), cv.reshape(b, t, H_C, HEAD_DIM),
             d_rows[0].reshape(b, -1, H_D, HEAD_DIM), d_rows[1].reshape(b, -1, H_D, HEAD_DIM))
    return mixed, state


def _odd_col_block(j):
    return jnp.where(j < 4, j + 6, jnp.where(j < 10, j - 4, j))


def _trunk(x, caches_even, caches_odd, wts, bf16_weights):
    def mm(key, a, w_f32, col_block=None, **kw):
        if key in bf16_weights:
            w_bf16 = bf16_weights[key]
            if kw.get("tn", 512) == 512 and w_bf16.shape[1] % 1024 == 0:
                kw = dict(kw, tn=1024, n_norm_tiles=(kw.get("n_norm_tiles", 0) + 1) // 2)
            return _matmul(a, w_bf16, **kw)
        out, bf16_weights[key] = _matmul_cast(a, w_f32, col_block=col_block, **kw)
        return out

    b, t, d = x.shape
    x2 = x.reshape(b * t, d)
    h2 = _rmsnorm(x2, wts["attn_norm"][0])
    mixed, st_even = _even_mixer(h2, b, t, caches_even, wts["even"], wts["tz_prompt"], wts["tz_step"],
                                 wts["tq_a"], mm)
    x2 = mm("out_even", mixed, wts["even"]["w_out"], mode="residual", residual=x2, name="out_even")
    h2 = _rmsnorm(x2, wts["mlp_norm"][0])
    u = mm("up0", h2, wts["w_up"][0], mode="relu2", out_dtype=BF16, tn=1024, name="mlp_up0")
    x2 = _matmul(u, wts["w_down"][0], mode="residual", residual=x2, tn=256, tk=u.shape[1], name="mlp_down0")
    h2 = _rmsnorm(x2, wts["attn_norm"][1])
    mixed, st_odd = _odd_mixer(h2, b, t, caches_odd, wts["odd"], wts["band"], mm)
    x2 = mm("out_odd", mixed, wts["odd"]["w_out"], mode="residual", residual=x2, name="out_odd")
    h2 = _rmsnorm(x2, wts["mlp_norm"][1])
    u = mm("up1", h2, wts["w_up"][1], mode="relu2", out_dtype=BF16, tn=1024, name="mlp_up1")
    x2 = _matmul(u, wts["w_down"][1], mode="residual", residual=x2, tn=256, tk=u.shape[1], name="mlp_down1")
    return x2.reshape(b, t, d), tuple(s[None] for s in st_even), tuple(s[None] for s in st_odd)


def _even_weights(w_in, w_out, a_qn, a_kn, b_qn, b_kn, f_bias):
    aq, ak, av, iq, ik, iw, bq, bk, bv, bf = jnp.split(
        w_in, [1024, 1280, 1536, 2048, 2112, 2120, 3144, 4168, 5192], axis=1)
    d = w_in.shape[0]
    pad = jnp.zeros((d, EV_PAD - EV_WIDTH + LANE - (IDX_DIM + IDX_HEADS + H_B)), w_in.dtype)
    w_r = jnp.concatenate([aq, bq, bk, bv, ak, av, iq, ik, iw, bf, pad], axis=1)
    ones = jnp.ones((EV_PAD,), F32)
    gain = ones.at[EV_AQ:EV_AQ + 1024].set(jnp.tile(a_qn, H_A))
    gain = gain.at[EV_BQ:EV_BQ + 1024].set(jnp.tile(b_qn, H_B))
    gain = gain.at[EV_BK:EV_BK + 1024].set(jnp.tile(b_kn, H_B))
    gain = gain.at[EV_AK:EV_AK + 256].set(jnp.tile(a_kn, KV_A))
    flag = jnp.zeros((EV_PAD,), F32).at[0:EV_BV].set(1.0).at[EV_AK:EV_AV].set(1.0)
    return {"w_in": w_r, "gain": gain.reshape(1, -1), "flag": flag.reshape(1, -1), "w_out": w_out,
            "f_bias": f_bias}


def _odd_weights(w_in, w_out, d_qn, d_kn):
    gain = jnp.ones((OD_WIDTH,), F32)
    gain = gain.at[OD_DQ:OD_DQ + 1024].set(jnp.tile(d_qn, H_D)).at[OD_DK:OD_DK + 1024].set(jnp.tile(d_kn, H_D))
    flag = jnp.zeros((OD_WIDTH,), F32).at[0:OD_CQ].set(1.0)
    return {"w_in": w_in, "gain": gain.reshape(1, -1), "flag": flag.reshape(1, -1), "w_out": w_out}


def kernel(x_prompt, x_sample, cache_a_k, cache_a_v, cache_a_kidx, cache_b_k, cache_b_v, cache_b_logf,
           cache_c_k, cache_c_v, cache_d_k, cache_d_v, attn_norm, mlp_norm, w_in_even, w_out_even,
           w_in_odd, w_out_odd, a_q_norm, a_k_norm, b_q_norm, b_k_norm, forget_bias, t5_bias,
           d_q_norm, d_k_norm, d_rel_bias, w_up, w_down):
    tq_a = min(256, x_prompt.shape[1])
    wts = {
        "attn_norm": attn_norm,
        "mlp_norm": mlp_norm,
        "even": _even_weights(w_in_even[0], w_out_even[0], a_q_norm[0], a_k_norm[0], b_q_norm[0], b_k_norm[0],
                              forget_bias[0]),
        "odd": _odd_weights(w_in_odd[0], w_out_odd[0], d_q_norm[0], d_k_norm[0]),
        "w_up": w_up,
        "w_down": w_down.astype(BF16),
        "tq_a": tq_a,
        "tz_prompt": _t5_strip(t5_bias, tq_a),
        "tz_step": _t5_strip(t5_bias, LANE),
        "band": _band_bias(d_rel_bias[0]),
    }
    bf16_weights = {}
    y_p, ev_p, od_p = _trunk(x_prompt, None, None, wts, bf16_weights)
    caches_even = (cache_a_k[0], cache_a_v[0], cache_a_kidx[0], cache_b_k[0], cache_b_v[0], cache_b_logf[0])
    caches_odd = (cache_c_k[0], cache_c_v[0], cache_d_k[0], cache_d_v[0])
    y_s, ev_s, od_s = _trunk(x_sample, caches_even, caches_odd, wts, bf16_weights)
    return (y_p, y_s) + ev_p + od_p + ev_s + od_s
```

```python
import functools

import numpy as np
import jax
import jax.numpy as jnp
from jax import lax
from jax.experimental import pallas as pl
from jax.experimental.pallas import tpu as pltpu

F32 = jnp.float32
BF16 = jnp.bfloat16
I32 = jnp.int32

HEAD_DIM = 128
CHUNK = 64
CHUNK_SHIFT = 6
H_A = 8
KV_A = 2
REP_A = H_A // KV_A
H_B = 8
H_C = 8
H_D = 8
IDX_HEADS = 8
IDX_DIM = 64
TOPK_MAX = 256
T5_BUCKETS = 32
D_LEFT_CHUNKS = 8
D_BAND_LEFT = D_LEFT_CHUNKS * CHUNK
D_REL_CLIP = 128
EPS = 1e-6
NEG_INF = -1e30
SCALE = HEAD_DIM ** -0.5
LANE = 128
MXU_WIDTH = 256
INT32_MIN = -(2 ** 31)
NEG_INF_KEY = int(np.float32(NEG_INF).view(np.int32)) ^ 0x7FFFFFFF

VMEM_LIMIT_BYTES = 56 * 1024 * 1024

EV_AQ, EV_BQ, EV_BK, EV_BV, EV_AK, EV_AV, EV_IQ, EV_MISC = 0, 1024, 2048, 3072, 4096, 4352, 4608, 5120
EV_WIDTH = 5248
EV_PAD = 5632
MISC_IK, MISC_IW, MISC_BF = 0, 64, 72
OD_DQ, OD_DK, OD_CQ, OD_CK, OD_CV, OD_DV = 0, 1024, 2048, 3072, 4096, 5120
OD_WIDTH = 6144


def _cparams(*sem):
    return pltpu.CompilerParams(dimension_semantics=sem, vmem_limit_bytes=VMEM_LIMIT_BYTES)


def _dot_nt(a, b):
    return lax.dot_general(a, b, (((1,), (1,)), ((), ())), preferred_element_type=F32)


def _dot(a, b):
    return jnp.dot(a, b, preferred_element_type=F32)


def _lane_blocks(x):
    return [x[..., j * LANE:(j + 1) * LANE] for j in range(x.shape[-1] // LANE)]


def _pad_keys(x, rows):
    return jnp.concatenate([x, jnp.zeros((rows - x.shape[0], x.shape[1]), x.dtype)], axis=0)


def _rmsnorm_kernel(x_ref, g_ref, o_ref):
    x = x_ref[...]
    ms = jnp.mean(x * x, axis=-1, keepdims=True)
    o_ref[...] = (x * lax.rsqrt(ms + EPS) * g_ref[...]).astype(o_ref.dtype)


def _rmsnorm(x2, gain):
    m, d = x2.shape
    tm = min(512, m)
    return pl.pallas_call(
        _rmsnorm_kernel,
        grid=(m // tm,),
        in_specs=[pl.BlockSpec((tm, d), lambda i: (i, 0)), pl.BlockSpec((1, d), lambda i: (0, 0))],
        out_specs=pl.BlockSpec((tm, d), lambda i: (i, 0)),
        out_shape=jax.ShapeDtypeStruct((m, d), BF16),
        compiler_params=_cparams("parallel"),
        name="rmsnorm",
    )(x2, gain.reshape(1, d))


def _mm_kernel(*refs, n_a, nk, mode, n_norm_tiles, cast_b):
    a_refs, b_ref = refs[:n_a], refs[n_a]
    refs = refs[n_a - 1:]
    if mode == "headnorm":
        gain_ref, flag_ref, o_ref = refs[2], refs[3], refs[4]
        rest = refs[5:]
    elif mode == "residual":
        res_ref, o_ref = refs[2], refs[3]
        rest = refs[4:]
    else:
        o_ref = refs[2]
        rest = refs[3:]
    n_axis = 0 if cast_b else 1
    if cast_b:
        wb_ref = rest[0]

        @pl.when(pl.program_id(1) == 0)
        def _():
            wb_ref[...] = b_ref[...].astype(BF16)

        b_ref = wb_ref

    tn = o_ref.shape[1]

    def product(c0=0, c1=tn):
        row, acc = 0, None
        for a_ref in a_refs:
            kp = a_ref.shape[1]
            part = _dot(a_ref[...], b_ref[row:row + kp, c0:c1])
            acc = part if acc is None else acc + part
            row += kp
        return acc

    def headnorm(acc, c0):
        for g in range(acc.shape[1] // HEAD_DIM):
            sl = slice(c0 + g * HEAD_DIM, c0 + (g + 1) * HEAD_DIM)
            blk = acc[:, g * HEAD_DIM:(g + 1) * HEAD_DIM]
            ms = jnp.mean(blk * blk, axis=-1, keepdims=True)
            normed = blk * lax.rsqrt(ms + EPS) * gain_ref[:, sl]
            o_ref[:, sl] = jnp.where(flag_ref[:, sl] > 0.0, normed, blk)

    def epilogue(acc, c0=0):
        sl = slice(c0, c0 + acc.shape[1])
        if mode == "headnorm":
            headnorm(acc, c0)
        elif mode == "relu2":
            r = jnp.maximum(acc, 0.0)
            o_ref[:, sl] = (r * r).astype(o_ref.dtype)
        elif mode == "residual":
            o_ref[:, sl] = res_ref[:, sl] + acc
        else:
            o_ref[:, sl] = acc.astype(o_ref.dtype)

    def chunked():
        starts = list(range(0, tn, MXU_WIDTH))
        accs = [product(c0, c0 + MXU_WIDTH) for c0 in starts]
        for c0, acc in zip(starts, accs):
            epilogue(acc, c0)

    if nk == 1 and mode == "headnorm":
        j = pl.program_id(n_axis)
        pl.when(j < n_norm_tiles)(chunked)

        @pl.when(j >= n_norm_tiles)
        def _():
            o_ref[...] = product()
    elif nk == 1:
        chunked()
    else:
        acc_ref = rest[0]
        k = pl.program_id(2)

        @pl.when(k == 0)
        def _():
            acc_ref[...] = jnp.zeros_like(acc_ref)

        acc_ref[...] += product()

        @pl.when(k == nk - 1)
        def _():
            epilogue(acc_ref[...])


def _matmul(a, b, *, mode="plain", out_dtype=F32, gain=None, flag=None, n_norm_tiles=0, residual=None,
            tm=1024, tn=512, tk=2048, name="matmul"):
    a_parts = list(a) if isinstance(a, (list, tuple)) else [a]
    m = a_parts[0].shape[0]
    kdim, n = b.shape
    tm = tm if m % tm == 0 else min(512, m)
    if len(a_parts) > 1:
        tk = kdim
        in_specs = [pl.BlockSpec((tm, part.shape[1]), lambda i, j, k: (i, 0)) for part in a_parts]
    else:
        in_specs = [pl.BlockSpec((tm, tk), lambda i, j, k: (i, k))]
    nk = kdim // tk
    in_specs.append(pl.BlockSpec((tk, tn), lambda i, j, k: (k, j)))
    args = a_parts + [b]
    if mode == "headnorm":
        in_specs += [pl.BlockSpec((1, tn), lambda i, j, k: (0, j)), pl.BlockSpec((1, tn), lambda i, j, k: (0, j))]
        args += [gain, flag]
    elif mode == "residual":
        in_specs += [pl.BlockSpec((tm, tn), lambda i, j, k: (i, j))]
        args += [residual]
    scratch = [pltpu.VMEM((tm, tn), F32)] if nk > 1 else []
    return pl.pallas_call(
        functools.partial(_mm_kernel, n_a=len(a_parts), nk=nk, mode=mode, n_norm_tiles=n_norm_tiles,
                          cast_b=False),
        grid=(m // tm, n // tn, nk),
        in_specs=in_specs,
        out_specs=pl.BlockSpec((tm, tn), lambda i, j, k: (i, j)),
        out_shape=jax.ShapeDtypeStruct((m, n), out_dtype),
        scratch_shapes=scratch,
        compiler_params=_cparams("parallel", "parallel", "arbitrary"),
        name=name,
    )(*args)


def _matmul_cast(a, b_f32, *, col_block=None, n=None, mode="plain", out_dtype=F32, gain=None, flag=None,
                 n_norm_tiles=0, residual=None, tm=1024, tn=512, name="matmul"):
    a_parts = list(a) if isinstance(a, (list, tuple)) else [a]
    m = a_parts[0].shape[0]
    kdim = b_f32.shape[0]
    n = b_f32.shape[1] if n is None else n
    tm = tm if m % tm == 0 else min(512, m)
    src = (lambda j: j) if col_block is None else col_block
    in_specs = [pl.BlockSpec((tm, part.shape[1]), lambda j, i: (i, 0)) for part in a_parts]
    in_specs.append(pl.BlockSpec((kdim, tn), lambda j, i: (0, src(j))))
    args = a_parts + [b_f32]
    if mode == "headnorm":
        in_specs += [pl.BlockSpec((1, tn), lambda j, i: (0, j)), pl.BlockSpec((1, tn), lambda j, i: (0, j))]
        args += [gain, flag]
    elif mode == "residual":
        in_specs += [pl.BlockSpec((tm, tn), lambda j, i: (i, j))]
        args += [residual]
    return pl.pallas_call(
        functools.partial(_mm_kernel, n_a=len(a_parts), nk=1, mode=mode, n_norm_tiles=n_norm_tiles, cast_b=True),
        grid=(n // tn, m // tm),
        in_specs=in_specs,
        out_specs=[pl.BlockSpec((tm, tn), lambda j, i: (i, j)), pl.BlockSpec((kdim, tn), lambda j, i: (0, j))],
        out_shape=[jax.ShapeDtypeStruct((m, n), out_dtype), jax.ShapeDtypeStruct((kdim, n), BF16)],
        compiler_params=_cparams("parallel", "arbitrary"),
        name=name,
    )(*args)


def _cumsum_block(x, carry):
    n = x.shape[1]
    r = lax.broadcasted_iota(I32, (n, n), 0)
    c = lax.broadcasted_iota(I32, (n, n), 1)
    tri = jnp.where(r <= c, 1.0, 0.0).astype(BF16)
    hi = x.astype(BF16)
    r1 = x - hi.astype(F32)
    mid = r1.astype(BF16)
    lo = (r1 - mid.astype(F32)).astype(BF16)
    cs = _dot(hi, tri) + _dot(mid, tri) + _dot(lo, tri) + carry
    return cs, cs[:, n - 1:n]


def _logf_cum_kernel(*refs, p_len, t_len, blk):
    if p_len:
        raw_ref, fb_ref, past_ref, lf_ref, cum_ref = refs
    else:
        raw_ref, fb_ref, lf_ref, cum_ref = refs
    x = raw_ref[0] + fb_ref[...]
    lf = jnp.minimum(x, 0.0) - jnp.log1p(jnp.exp(-jnp.abs(x)))
    lf_ref[0] = lf
    carry = jnp.zeros((H_B, 1), F32)
    for s in range(0, p_len, blk):
        cs, carry = _cumsum_block(past_ref[0, :, s:s + blk], carry)
        cum_ref[0, :, s:s + blk] = cs
    nb = min(blk, t_len)
    for s in range(0, t_len, nb):
        cs, carry = _cumsum_block(lf[:, s:s + nb], carry)
        cum_ref[0, :, p_len + s:p_len + s + nb] = cs


def _logf_cum(raw_h, fbias, past_h):
    b, h, t = raw_h.shape
    p = 0 if past_h is None else past_h.shape[2]
    in_specs = [pl.BlockSpec((1, h, t), lambda i: (i, 0, 0)), pl.BlockSpec((h, 1), lambda i: (0, 0))]
    args = [raw_h, fbias.reshape(h, 1)]
    if p:
        in_specs.append(pl.BlockSpec((1, h, p), lambda i: (i, 0, 0)))
        args.append(past_h)
    return pl.pallas_call(
        functools.partial(_logf_cum_kernel, p_len=p, t_len=t, blk=512),
        grid=(b,),
        in_specs=in_specs,
        out_specs=[pl.BlockSpec((1, h, t), lambda i: (i, 0, 0)), pl.BlockSpec((1, h, p + t), lambda i: (i, 0, 0))],
        out_shape=[jax.ShapeDtypeStruct((b, h, t), F32), jax.ShapeDtypeStruct((b, h, p + t), F32)],
        compiler_params=_cparams("parallel"),
        name="logf_cum",
    )(*args)


def _toeplitz(row_vals, rows, cols):
    w = row_vals.shape[1]
    full = jnp.broadcast_to(row_vals[0:1, :], (rows, w))
    return pltpu.roll(full, 0, 1, stride=1, stride_axis=0)[:, :cols]


def _t5_strip_kernel(tab_ref, o_ref, *, tile):
    h = pl.program_id(0)
    width = 4 * tile
    m = lax.broadcasted_iota(I32, (8, width), 1)
    d = jnp.where(m < width // 2, m, m - width)
    rel = d - tile
    n = jnp.abs(rel)
    large = jnp.full_like(n, 8)
    for thr in (12, 16, 23, 32, 46, 64, 91):
        large = large + jnp.where(n >= thr, 1, 0)
    bucket = jnp.where(rel > 0, T5_BUCKETS // 2, 0) + jnp.where(n < 8, n, large)
    vals = jnp.zeros((8, width), F32)
    for bkt in range(T5_BUCKETS):
        vals = jnp.where(bucket == bkt, tab_ref[bkt, h], vals)
    vals = vals - tab_ref[T5_BUCKETS // 2 - 1, h]
    o_ref[0] = _toeplitz(vals, tile, 2 * tile)


def _t5_strip(t5_table, tile):
    return pl.pallas_call(
        functools.partial(_t5_strip_kernel, tile=tile),
        grid=(H_A,),
        in_specs=[pl.BlockSpec(memory_space=pltpu.SMEM)],
        out_specs=pl.BlockSpec((1, tile, 2 * tile), lambda h: (h, 0, 0)),
        out_shape=jax.ShapeDtypeStruct((H_A, tile, 2 * tile), F32),
        compiler_params=_cparams("parallel"),
        name="t5_strip",
    )(t5_table)


def _band_bias_kernel(tab_ref, o_ref, *, rows, cols, width):
    h = pl.program_id(0)
    m = lax.broadcasted_iota(I32, (8, width), 1)
    d = jnp.where(m < width // 2, m, m - width)
    idx = jnp.clip(d - D_BAND_LEFT, -D_REL_CLIP, D_REL_CLIP) + D_REL_CLIP
    vals = jnp.zeros((8, width), F32)
    for r in range(2 * D_REL_CLIP + 1):
        vals = jnp.where(idx == r, tab_ref[r, h], vals)
    vals = vals - tab_ref[0, h]
    i_chunk = lax.broadcasted_iota(I32, (rows, cols), 0) >> CHUNK_SHIFT
    c_chunk = lax.broadcasted_iota(I32, (rows, cols), 1) >> CHUNK_SHIFT
    visible = (c_chunk >= i_chunk) & (c_chunk <= i_chunk + D_LEFT_CHUNKS)
    o_ref[0] = jnp.where(visible, _toeplitz(vals, rows, cols), NEG_INF)


def _band_bias(rel_table):
    rows, cols, width = D_BAND_LEFT, 2 * D_BAND_LEFT, 4 * D_BAND_LEFT
    return pl.pallas_call(
        functools.partial(_band_bias_kernel, rows=rows, cols=cols, width=width),
        grid=(H_D,),
        in_specs=[pl.BlockSpec(memory_space=pltpu.SMEM)],
        out_specs=pl.BlockSpec((1, rows, cols), lambda h: (h, 0, 0)),
        out_shape=jax.ShapeDtypeStruct((H_D, rows, cols), F32),
        compiler_params=_cparams("parallel"),
        name="band_bias",
    )(rel_table)


def _select_kernel(*refs, n_g, sb, tq, nq, lp, l_valid, q_base, n_sel, tk_out, causal_prefix):
    iq_refs, misc_refs = refs[:n_g], refs[n_g:2 * n_g]
    kk_ref, o_ref, key_ref = refs[2 * n_g:]
    step = pl.program_id(1)
    rows = sb * tq
    kf = float(n_sel)
    gs = range(n_g)

    def count(pred):
        return jnp.sum(jnp.where(pred, 1.0, 0.0), axis=-1, keepdims=True)

    def run(tiles):
        pfx = [p for _, p in tiles]
        adm = []
        for g, (q_tile, p) in enumerate(tiles):
            scores = []
            for s in range(sb):
                iq = iq_refs[g][s]
                iw = misc_refs[g][s][:, MISC_IW:MISC_IW + IDX_HEADS]
                ik = kk_ref[s, 0:p, 0:IDX_DIM].astype(BF16)
                score = jnp.zeros((tq, p), F32)
                for h in range(IDX_HEADS):
                    dots = _dot_nt(iq[:, h * IDX_DIM:(h + 1) * IDX_DIM].astype(BF16), ik)
                    score = score + iw[:, h:h + 1] * jnp.maximum(dots, 0.0)
                scores.append(score)
            score = jnp.concatenate(scores, axis=0) if sb > 1 else scores[0]
            row_in_tile = jnp.concatenate([lax.broadcasted_iota(I32, (tq, 1), 0)] * sb, axis=0)
            q_pos = q_base + q_tile * tq + row_in_tile
            kp = lax.broadcasted_iota(I32, (1, p), 1)
            ok = ((kp >> CHUNK_SHIFT) <= (q_pos >> CHUNK_SHIFT)) & (kp < l_valid)
            bits = lax.bitcast_convert_type(jnp.where(ok, score, NEG_INF), I32)
            key_ref[g, :, 0:p] = jnp.where(bits < 0, bits ^ jnp.int32(0x7FFFFFFF), bits)
            adm.append(ok)

        def keys(g):
            return key_ref[g, :, 0:pfx[g]]

        t0 = tuple(jnp.where(count(keys(g) >= 0) >= kf, jnp.int32(0), jnp.int32(INT32_MIN)) for g in gs)

        def thr_body(i, ts):
            bit = jnp.int32(1) << (30 - i)
            return tuple(jnp.where(count(keys(g) >= ts[g] + bit) >= kf, ts[g] + bit, ts[g]) for g in gs)

        thr = lax.fori_loop(0, 31, thr_body, t0)
        need = [kf - count(keys(g) > thr[g]) for g in gs]
        tied = [(count(keys(g) == thr[g]) != need[g]) & (thr[g] > NEG_INF_KEY) for g in gs]
        any_tied = sum(jnp.max(jnp.where(tied[g], 1.0, 0.0)) for g in gs) > 0.5

        def all_tied():
            return tuple(jnp.where(keys(g) == thr[g], 1.0, 0.0) for g in gs)

        def lowest_tied():
            r = lax.broadcasted_iota(I32, (LANE, 2 * LANE), 0)
            c = lax.broadcasted_iota(I32, (LANE, 2 * LANE), 1)
            count_mat = jnp.where((r <= c) | (c >= LANE), 1.0, 0.0).astype(BF16)
            out = []
            for g, tied_mask in enumerate(all_tied()):
                seen = jnp.zeros((rows, LANE), F32)
                kept = []
                for blk in _lane_blocks(tied_mask):
                    both = _dot(blk.astype(BF16), count_mat)
                    kept.append(jnp.where(both[:, :LANE] + seen <= need[g], blk, 0.0))
                    seen = seen + both[:, LANE:]
                out.append(jnp.concatenate(kept, axis=1) if len(kept) > 1 else kept[0])
            return tuple(out)

        take_tied = lax.cond(any_tied, lowest_tied, all_tied)
        for g in gs:
            sel = ((keys(g) > thr[g]) | (take_tied[g] > 0.5)) & adm[g]
            madd = jnp.where(sel, 0.0, NEG_INF)
            for s in range(sb):
                for kt in range(lp // tk_out):
                    if (kt + 1) * tk_out <= pfx[g]:
                        o_ref[s, g, 0, kt] = madd[s * tq:(s + 1) * tq, kt * tk_out:(kt + 1) * tk_out]
                    else:
                        o_ref[s, g, 0, kt] = jnp.full((tq, tk_out), NEG_INF, F32)

    if not causal_prefix:
        run([(step, lp)] * n_g)
    elif n_g == 2:
        for j in range(nq // 2):
            pl.when(step == j)(functools.partial(run, [(j, (j + 1) * tq), (nq - 1 - j, (nq - j) * tq)]))
    else:
        for j in range(nq):
            pl.when(step == j)(functools.partial(run, [(j, (j + 1) * tq)]))


def _select_mask(iq_arr, iq_col, misc_arr, misc_col, kk_arr, kk_col, *, t, tq, lp, l_valid, q_base, n_sel,
                 tk_out, causal_prefix):
    b = iq_arr.shape[0]
    nq = t // tq
    wk = kk_arr.shape[2] if kk_col is None else LANE
    kcol = 0 if kk_col is None else kk_col
    sb = 4 if (nq == 1 and tq <= 64 and b % 4 == 0) else 1
    n_g = 2 if (causal_prefix and nq % 2 == 0) else 1
    steps = nq // n_g
    tile_of = [lambda j: j, lambda j: nq - 1 - j]
    nkt = lp // tk_out
    in_specs = [pl.BlockSpec((sb, tq, IDX_HEADS * IDX_DIM), lambda i, j, g=g: (i, tile_of[g](j), iq_col))
                for g in range(n_g)]
    in_specs += [pl.BlockSpec((sb, tq, LANE), lambda i, j, g=g: (i, tile_of[g](j), misc_col)) for g in range(n_g)]
    in_specs.append(pl.BlockSpec((sb, lp, wk), lambda i, j: (i, 0, kcol)))
    return pl.pallas_call(
        functools.partial(_select_kernel, n_g=n_g, sb=sb, tq=tq, nq=nq, lp=lp, l_valid=l_valid, q_base=q_base,
                          n_sel=n_sel, tk_out=tk_out, causal_prefix=causal_prefix),
        grid=(b // sb, steps),
        in_specs=in_specs,
        out_specs=pl.BlockSpec((sb, n_g, 1, nkt, tq, tk_out), lambda i, j: (i, 0, j, 0, 0, 0)),
        out_shape=jax.ShapeDtypeStruct((b, n_g, steps, nkt, tq, tk_out), F32),
        scratch_shapes=[pltpu.VMEM((n_g, sb * tq, lp), I32)],
        compiler_params=_cparams("parallel", "parallel"),
        name="a_select",
    )(*([iq_arr] * n_g), *([misc_arr] * n_g), kk_arr)


def _mask_tile_index(q_tile, nq, n_g):
    if n_g == 1:
        return 0, q_tile
    upper = q_tile >= nq // 2
    return jnp.where(upper, 1, 0), jnp.where(upper, nq - 1 - q_tile, q_tile)


def _online_stats(s, m_s, l_s):
    blocks = _lane_blocks(s)
    mx = blocks[0]
    for blk in blocks[1:]:
        mx = jnp.maximum(mx, blk)
    m_prev = m_s[...]
    m_new = jnp.maximum(m_prev, jnp.max(mx, axis=-1, keepdims=True))
    alpha = jnp.exp(m_prev - m_new)
    ps = [jnp.exp(blk - m_new) for blk in blocks]
    psum = ps[0]
    for p in ps[1:]:
        psum = psum + p
    l_s[...] = alpha * l_s[...] + psum
    m_s[...] = m_new
    p_all = jnp.concatenate([p.astype(BF16) for p in ps], axis=1) if len(ps) > 1 else ps[0].astype(BF16)
    return alpha, p_all


def _online_init(m_s, l_s, acc_s):
    m_s[...] = jnp.full_like(m_s, NEG_INF)
    l_s[...] = jnp.zeros_like(l_s)
    acc_s[...] = jnp.zeros_like(acc_s)


def _online_result(l_s, acc_s):
    return acc_s[...] / jnp.sum(l_s[...], axis=-1, keepdims=True)


def _stack_heads(q):
    return jnp.concatenate([q[:, r * HEAD_DIM:(r + 1) * HEAD_DIM] for r in range(REP_A)], axis=0).astype(BF16)


def _attn_a_prompt_kernel(q_ref, k_ref, v_ref, mask_ref, tz_ref, o_ref, m_s, l_s, acc_s, *, tq):
    qi = pl.program_id(1)
    width = REP_A * HEAD_DIM
    qs = [_stack_heads(q_ref[0, :, g * width:(g + 1) * width] * SCALE) for g in range(KV_A)]
    _online_init(m_s, l_s, acc_s)

    def tile(kt, bias_cols):
        off = pl.multiple_of(kt * tq, tq)
        mask = mask_ref[0, 0, 0, kt][None]
        logits = []
        for g in range(KV_A):
            k = k_ref[0, pl.ds(off, tq), g * HEAD_DIM:(g + 1) * HEAD_DIM].astype(BF16)
            s = _dot_nt(qs[g], k).reshape(REP_A, tq, tq) + mask
            if bias_cols is not None:
                s = s + tz_ref[g * REP_A:(g + 1) * REP_A, :, bias_cols]
            logits.append(s.reshape(REP_A * tq, tq))
        stats = [_online_stats(logits[g], m_s.at[g], l_s.at[g]) for g in range(KV_A)]
        for g in range(KV_A):
            alpha, p_all = stats[g]
            v = v_ref[0, pl.ds(off, tq), g * HEAD_DIM:(g + 1) * HEAD_DIM].astype(BF16)
            acc_s[g] = alpha * acc_s[g] + _dot(p_all, v)

    def plain_body(kt, carry):
        tile(kt, None)
        return carry

    lax.fori_loop(0, jnp.maximum(qi - 1, 0), plain_body, 0)

    @pl.when(qi >= 1)
    def _():
        tile(qi - 1, slice(0, tq))

    tile(qi, slice(tq, 2 * tq))
    for g in range(KV_A):
        out = _online_result(l_s.at[g], acc_s.at[g])
        for r in range(REP_A):
            col = (g * REP_A + r) * HEAD_DIM
            o_ref[0, :, col:col + HEAD_DIM] = out[r * tq:(r + 1) * tq].astype(o_ref.dtype)


def _attn_a_prompt(proj3, mask, tz, *, tq):
    b, t, _ = proj3.shape
    nq = t // tq
    qw, kw = H_A * HEAD_DIM, KV_A * HEAD_DIM
    rows = REP_A * tq
    n_g = mask.shape[1]

    def mask_idx(i, j):
        grp, pos = _mask_tile_index(j, nq, n_g)
        return (i, grp, pos, 0, 0, 0)

    return pl.pallas_call(
        functools.partial(_attn_a_prompt_kernel, tq=tq),
        grid=(b, nq),
        in_specs=[
            pl.BlockSpec((1, tq, qw), lambda i, j: (i, j, EV_AQ // qw)),
            pl.BlockSpec((1, t, kw), lambda i, j: (i, 0, EV_AK // kw)),
            pl.BlockSpec((1, t, kw), lambda i, j: (i, 0, EV_AV // kw)),
            pl.BlockSpec((1, 1, 1, nq, tq, tq), mask_idx),
            pl.BlockSpec((H_A, tq, 2 * tq), lambda i, j: (0, 0, 0)),
        ],
        out_specs=pl.BlockSpec((1, tq, qw), lambda i, j: (i, j, 0)),
        out_shape=jax.ShapeDtypeStruct((b, t, qw), BF16),
        scratch_shapes=[pltpu.VMEM((KV_A, rows, LANE), F32), pltpu.VMEM((KV_A, rows, LANE), F32),
                        pltpu.VMEM((KV_A, rows, HEAD_DIM), F32)],
        compiler_params=_cparams("parallel", "parallel"),
        name="attn_a_prompt",
    )(proj3, proj3, proj3, mask, tz)


def _attn_a_step_kernel(q_ref, kc_ref, vc_ref, kn_ref, vn_ref, mask_ref, tz_ref, o_ref, *, t, p):
    width = REP_A * HEAD_DIM
    mask = mask_ref[0, 0, 0, 0]
    logits = []
    for g in range(KV_A):
        heads = slice(g * REP_A, (g + 1) * REP_A)
        qs = _stack_heads(q_ref[0, :, g * width:(g + 1) * width] * SCALE)
        kc = kc_ref[0, pl.ds(g, p, stride=KV_A), :].astype(BF16)
        kn = _pad_keys(kn_ref[0, :, g * HEAD_DIM:(g + 1) * HEAD_DIM], LANE).astype(BF16)
        s_c = _dot_nt(qs, kc).reshape(REP_A, t, p) + mask[None, :, 0:p]
        s_n = (_dot_nt(qs, kn).reshape(REP_A, t, LANE) + mask[None, :, p:p + LANE]
               + tz_ref[heads, :, LANE:2 * LANE])
        blocks = _lane_blocks(s_c)
        blocks[-1] = blocks[-1] + tz_ref[heads, :, 0:LANE]
        blocks.append(s_n)
        logits.append(blocks)
    probs = []
    for blocks in logits:
        mx = blocks[0]
        for blk in blocks[1:]:
            mx = jnp.maximum(mx, blk)
        m = jnp.max(mx, axis=-1, keepdims=True)
        ps = [jnp.exp(blk - m) for blk in blocks]
        psum = ps[0]
        for pb in ps[1:]:
            psum = psum + pb
        l = jnp.sum(psum, axis=-1, keepdims=True)
        p_c = jnp.concatenate([pb.astype(BF16) for pb in ps[:-1]], axis=-1).reshape(REP_A * t, p)
        probs.append((p_c, ps[-1].astype(BF16).reshape(REP_A * t, LANE), l))
    for g, (p_c, p_n, l) in enumerate(probs):
        vc = vc_ref[0, pl.ds(g, p, stride=KV_A), :].astype(BF16)
        vn = _pad_keys(vn_ref[0, :, g * HEAD_DIM:(g + 1) * HEAD_DIM], LANE).astype(BF16)
        out = (_dot(p_c, vc) + _dot(p_n, vn)).reshape(REP_A, t, HEAD_DIM) / l
        for r in range(REP_A):
            col = (g * REP_A + r) * HEAD_DIM
            o_ref[0, :, col:col + HEAD_DIM] = out[r].astype(o_ref.dtype)


def _attn_a_step(proj3, cache_k, cache_v, mask, tz):
    b, t, _ = proj3.shape
    p = cache_k.shape[1] // KV_A
    qw, kw = H_A * HEAD_DIM, KV_A * HEAD_DIM
    return pl.pallas_call(
        functools.partial(_attn_a_step_kernel, t=t, p=p),
        grid=(b,),
        in_specs=[
            pl.BlockSpec((1, t, qw), lambda i: (i, 0, EV_AQ // qw)),
            pl.BlockSpec((1, p * KV_A, HEAD_DIM), lambda i: (i, 0, 0)),
            pl.BlockSpec((1, p * KV_A, HEAD_DIM), lambda i: (i, 0, 0)),
            pl.BlockSpec((1, t, kw), lambda i: (i, 0, EV_AK // kw)),
            pl.BlockSpec((1, t, kw), lambda i: (i, 0, EV_AV // kw)),
            pl.BlockSpec((1, 1, 1, 1, t, p + LANE), lambda i: (i, 0, 0, 0, 0, 0)),
            pl.BlockSpec((H_A, t, 2 * LANE), lambda i: (0, 0, 0)),
        ],
        out_specs=pl.BlockSpec((1, t, qw), lambda i: (i, 0, 0)),
        out_shape=jax.ShapeDtypeStruct((b, t, qw), BF16),
        compiler_params=_cparams("parallel"),
        name="attn_a_step",
    )(proj3, cache_k, cache_v, proj3, proj3, mask, tz)


def _causal_mask(s, row0=0):
    r = row0 + lax.broadcasted_iota(I32, s.shape, 0)
    c = lax.broadcasted_iota(I32, s.shape, 1)
    return jnp.where(c <= r, s, NEG_INF)


B_HEADS_PER_STEP = 2


def _attn_b_prompt_kernel(q_ref, k_ref, v_ref, ck_ref, o_ref, m_s, l_s, acc_s, *, tq):
    qi = pl.program_id(2)
    heads = [slice(h * HEAD_DIM, (h + 1) * HEAD_DIM) for h in range(B_HEADS_PER_STEP)]
    qs = [(q_ref[0, :, sl] * SCALE).astype(BF16) for sl in heads]
    _online_init(m_s, l_s, acc_s)

    def tile(kt, diag):
        off = pl.multiple_of(kt * tq, tq)
        logits = []
        for h, sl in enumerate(heads):
            s = _dot_nt(qs[h], k_ref[0, pl.ds(off, tq), sl].astype(BF16)) - ck_ref[0, h, kt]
            logits.append(_causal_mask(s) if diag else s)
        stats = [_online_stats(logits[h], m_s.at[h], l_s.at[h]) for h in range(B_HEADS_PER_STEP)]
        for h, sl in enumerate(heads):
            alpha, p_all = stats[h]
            acc_s[h] = alpha * acc_s[h] + _dot(p_all, v_ref[0, pl.ds(off, tq), sl].astype(BF16))

    def body(kt, carry):
        tile(kt, False)
        return carry

    lax.fori_loop(0, qi, body, 0)
    tile(qi, True)
    for h, sl in enumerate(heads):
        o_ref[0, :, sl] = _online_result(l_s.at[h], acc_s.at[h]).astype(o_ref.dtype)


def _attn_b_prompt(proj3, cum_h, *, tq=512):
    b, t, _ = proj3.shape
    nq = t // tq
    ck = cum_h.reshape(b, H_B, nq, 1, tq)
    hp = B_HEADS_PER_STEP
    w = hp * HEAD_DIM
    return pl.pallas_call(
        functools.partial(_attn_b_prompt_kernel, tq=tq),
        grid=(b, H_B // hp, nq),
        in_specs=[
            pl.BlockSpec((1, tq, w), lambda i, h, j: (i, j, EV_BQ // w + h)),
            pl.BlockSpec((1, t, w), lambda i, h, j: (i, 0, EV_BK // w + h)),
            pl.BlockSpec((1, t, w), lambda i, h, j: (i, 0, EV_BV // w + h)),
            pl.BlockSpec((1, hp, nq, 1, tq), lambda i, h, j: (i, h, 0, 0, 0)),
        ],
        out_specs=pl.BlockSpec((1, tq, w), lambda i, h, j: (i, j, h)),
        out_shape=jax.ShapeDtypeStruct((b, t, H_B * HEAD_DIM), BF16),
        scratch_shapes=[pltpu.VMEM((hp, tq, LANE), F32), pltpu.VMEM((hp, tq, LANE), F32),
                        pltpu.VMEM((hp, tq, HEAD_DIM), F32)],
        compiler_params=_cparams("parallel", "parallel", "parallel"),
        name="attn_b_prompt",
    )(proj3, proj3, proj3, ck)


def _attn_b_step_kernel(q_ref, kc_ref, vc_ref, kn_ref, vn_ref, ckc_ref, ckn_ref, o_ref, m_s, l_s, acc_s,
                        *, nkc, t, tkc):
    kb = pl.program_id(1)

    @pl.when(kb == 0)
    def _():
        _online_init(m_s, l_s, acc_s)

    def q_head(h):
        return (q_ref[0, :, h * HEAD_DIM:(h + 1) * HEAD_DIM] * SCALE).astype(BF16)

    @pl.when(kb < nkc)
    def _():
        ck = ckc_ref[0, 0]
        logits = [_dot_nt(q_head(h), kc_ref[0, pl.ds(h, tkc, stride=H_B), :].astype(BF16)) - ck[h:h + 1, :]
                  for h in range(H_B)]
        stats = [_online_stats(logits[h], m_s.at[h], l_s.at[h]) for h in range(H_B)]
        for h in range(H_B):
            alpha, p_all = stats[h]
            v = vc_ref[0, pl.ds(h, tkc, stride=H_B), :].astype(BF16)
            acc_s[h] = alpha * acc_s[h] + _dot(p_all, v)

    @pl.when(kb == nkc)
    def _():
        ckn = ckn_ref[0]
        heads = [slice(h * HEAD_DIM, (h + 1) * HEAD_DIM) for h in range(H_B)]
        logits = [_causal_mask(_dot_nt(q_head(h), _pad_keys(kn_ref[0, :, heads[h]], LANE).astype(BF16))
                               - ckn[h:h + 1, :]) for h in range(H_B)]
        stats = [_online_stats(logits[h], m_s.at[h], l_s.at[h]) for h in range(H_B)]
        for h in range(H_B):
            alpha, p_all = stats[h]
            v = _pad_keys(vn_ref[0, :, heads[h]], LANE).astype(BF16)
            acc_s[h] = alpha * acc_s[h] + _dot(p_all, v)
            o_ref[0, :, heads[h]] = _online_result(l_s.at[h], acc_s.at[h]).astype(o_ref.dtype)


def _attn_b_step(proj3, cache_k, cache_v, cum_h, *, tkc=1024):
    b, t, _ = proj3.shape
    p = cache_k.shape[1] // H_B
    assert p % tkc == 0 and t <= LANE, (p, tkc, t)
    nkc = p // tkc
    w = H_B * HEAD_DIM
    ck_cache = cum_h[:, :, :p].reshape(b, H_B, nkc, tkc).transpose(0, 2, 1, 3)
    ck_new = jnp.pad(cum_h[:, :, p:], ((0, 0), (0, 0), (0, LANE - t)))
    last = nkc - 1
    return pl.pallas_call(
        functools.partial(_attn_b_step_kernel, nkc=nkc, t=t, tkc=tkc),
        grid=(b, nkc + 1),
        in_specs=[
            pl.BlockSpec((1, t, w), lambda i, j: (i, 0, EV_BQ // w)),
            pl.BlockSpec((1, tkc * H_B, HEAD_DIM), lambda i, j: (i, jnp.minimum(j, last), 0)),
            pl.BlockSpec((1, tkc * H_B, HEAD_DIM), lambda i, j: (i, jnp.minimum(j, last), 0)),
            pl.BlockSpec((1, t, w), lambda i, j: (i, 0, EV_BK // w)),
            pl.BlockSpec((1, t, w), lambda i, j: (i, 0, EV_BV // w)),
            pl.BlockSpec((1, 1, H_B, tkc), lambda i, j: (i, jnp.minimum(j, last), 0, 0)),
            pl.BlockSpec((1, H_B, LANE), lambda i, j: (i, 0, 0)),
        ],
        out_specs=pl.BlockSpec((1, t, w), lambda i, j: (i, 0, 0)),
        out_shape=jax.ShapeDtypeStruct((b, t, w), BF16),
        scratch_shapes=[pltpu.VMEM((H_B, t, LANE), F32), pltpu.VMEM((H_B, t, LANE), F32),
                        pltpu.VMEM((H_B, t, HEAD_DIM), F32)],
        compiler_params=_cparams("parallel", "arbitrary"),
        name="attn_b_step",
    )(proj3, cache_k, cache_v, proj3, proj3, ck_cache, ck_new)


def _tri_ones():
    r = lax.broadcasted_iota(I32, (2 * LANE, 2 * LANE), 0) & (LANE - 1)
    c = lax.broadcasted_iota(I32, (2 * LANE, 2 * LANE), 1)
    return jnp.where((r > c) | (c >= LANE), 1.0, 0.0).astype(BF16)


def _stick_scores(z, row_minus_col, tri_ones):
    rc_blocks = None if row_minus_col is None else _lane_blocks(row_minus_col)
    out = []
    for j, zb in enumerate(_lane_blocks(z)):
        tail = jnp.log(1.0 + jnp.exp(-jnp.abs(zb)))
        log_beta = jnp.minimum(zb, 0.0) - tail
        log_keep = -jnp.maximum(zb, 0.0) - tail
        strict = None
        if rc_blocks is not None:
            strict = rc_blocks[j] > 0
            log_keep = jnp.where(strict, log_keep, 0.0)
        hi = log_keep.astype(BF16)
        lo = (log_keep - hi.astype(F32)).astype(BF16)
        out.append((log_beta, strict, _dot(jnp.concatenate([hi, lo], axis=1), tri_ones)))
    return out


def _stick_weights(blocks, run):
    ws = [None] * len(blocks)
    for j in reversed(range(len(blocks))):
        log_beta, strict, sums = blocks[j]
        w = jnp.exp(log_beta + sums[:, :LANE] + run)
        if strict is not None:
            w = jnp.where(strict, w, 0.0)
        ws[j] = w.astype(BF16)
        run = run + sums[:, LANE:]
    return (jnp.concatenate(ws, axis=1) if len(ws) > 1 else ws[0]), run


C_HEADS_PER_STEP = 2


def _attn_c_prompt_kernel(q_ref, k_ref, v_ref, o_ref, run_s, acc_s, *, tq):
    qi = pl.program_id(2)
    heads = [slice(h * HEAD_DIM, (h + 1) * HEAD_DIM) for h in range(C_HEADS_PER_STEP)]
    qs = [(q_ref[0, :, sl] * SCALE).astype(BF16) for sl in heads]
    tri_ones = _tri_ones()

    def tile(kt, row_minus_col, first):
        off = pl.multiple_of(kt * tq, tq)
        zs = [_dot_nt(qs[h], k_ref[0, pl.ds(off, tq), sl].astype(BF16)) for h, sl in enumerate(heads)]
        scored = [_stick_scores(z, row_minus_col, tri_ones) for z in zs]
        for h, sl in enumerate(heads):
            run = jnp.zeros((tq, LANE), F32) if first else run_s[h]
            w_all, run = _stick_weights(scored[h], run)
            out = _dot(w_all, v_ref[0, pl.ds(off, tq), sl].astype(BF16))
            acc_s[h] = out if first else acc_s[h] + out
            run_s[h] = run

    rc = lax.broadcasted_iota(I32, (tq, tq), 0) - lax.broadcasted_iota(I32, (tq, tq), 1)
    tile(qi, rc, True)

    def body(i, carry):
        tile(qi - 1 - i, None, False)
        return carry

    lax.fori_loop(0, qi, body, 0)
    for h, sl in enumerate(heads):
        o_ref[0, :, sl] = acc_s[h].astype(o_ref.dtype)


def _attn_c_prompt(proj3, *, tq=512):
    b, t, _ = proj3.shape
    nq = t // tq
    hp = C_HEADS_PER_STEP
    w = hp * HEAD_DIM
    return pl.pallas_call(
        functools.partial(_attn_c_prompt_kernel, tq=tq),
        grid=(b, H_C // hp, nq),
        in_specs=[
            pl.BlockSpec((1, tq, w), lambda i, h, j: (i, j, OD_CQ // w + h)),
            pl.BlockSpec((1, t, w), lambda i, h, j: (i, 0, OD_CK // w + h)),
            pl.BlockSpec((1, t, w), lambda i, h, j: (i, 0, OD_CV // w + h)),
        ],
        out_specs=pl.BlockSpec((1, tq, w), lambda i, h, j: (i, j, h)),
        out_shape=jax.ShapeDtypeStruct((b, t, H_C * HEAD_DIM), BF16),
        scratch_shapes=[pltpu.VMEM((hp, tq, LANE), F32), pltpu.VMEM((hp, tq, HEAD_DIM), F32)],
        compiler_params=_cparams("parallel", "parallel", "parallel"),
        name="attn_c_prompt",
    )(proj3, proj3, proj3)


def _attn_c_step_kernel(q_ref, kc_ref, vc_ref, kn_ref, vn_ref, o_ref, run_s, acc_s, *, nkc, t, tkc, sub):
    kb = pl.program_id(1)
    tri_ones = _tri_ones()

    def q_head(h):
        return (q_ref[0, :, h * HEAD_DIM:(h + 1) * HEAD_DIM] * SCALE).astype(BF16)

    @pl.when(kb == 0)
    def _():
        rc = lax.broadcasted_iota(I32, (t, LANE), 0) - lax.broadcasted_iota(I32, (t, LANE), 1)
        heads = [slice(h * HEAD_DIM, (h + 1) * HEAD_DIM) for h in range(H_C)]
        zs = [_dot_nt(q_head(h), _pad_keys(kn_ref[0, :, heads[h]], LANE).astype(BF16)) for h in range(H_C)]
        scored = [_stick_scores(z, rc, tri_ones) for z in zs]
        for h in range(H_C):
            w_all, run = _stick_weights(scored[h], jnp.zeros((t, LANE), F32))
            acc_s[h] = _dot(w_all, _pad_keys(vn_ref[0, :, heads[h]], LANE).astype(BF16))
            run_s[h] = run

    @pl.when(kb > 0)
    def _():
        def body(i, carry):
            row0 = (tkc - sub - i * sub) * H_C
            zs = [_dot_nt(q_head(h), kc_ref[0, pl.ds(row0 + h, sub, stride=H_C), :].astype(BF16))
                  for h in range(H_C)]
            scored = [_stick_scores(z, None, tri_ones) for z in zs]
            for h in range(H_C):
                w_all, run = _stick_weights(scored[h], run_s[h])
                v = vc_ref[0, pl.ds(row0 + h, sub, stride=H_C), :].astype(BF16)
                acc_s[h] += _dot(w_all, v)
                run_s[h] = run
            return carry

        lax.fori_loop(0, tkc // sub, body, 0)

    @pl.when(kb == nkc)
    def _():
        for h in range(H_C):
            o_ref[0, :, h * HEAD_DIM:(h + 1) * HEAD_DIM] = acc_s[h].astype(o_ref.dtype)


def _attn_c_step(proj3, cache_k, cache_v, *, tkc=1024, sub=1024):
    b, t, _ = proj3.shape
    p = cache_k.shape[1] // H_C
    assert p % tkc == 0 and t <= LANE, (p, tkc, t)
    nkc = p // tkc
    w = H_C * HEAD_DIM

    def cache_idx(i, j):
        return (i, jnp.clip(nkc - j, 0, nkc - 1), 0)

    return pl.pallas_call(
        functools.partial(_attn_c_step_kernel, nkc=nkc, t=t, tkc=tkc, sub=sub),
        grid=(b, nkc + 1),
        in_specs=[
            pl.BlockSpec((1, t, w), lambda i, j: (i, 0, OD_CQ // w)),
            pl.BlockSpec((1, tkc * H_C, HEAD_DIM), cache_idx),
            pl.BlockSpec((1, tkc * H_C, HEAD_DIM), cache_idx),
            pl.BlockSpec((1, t, w), lambda i, j: (i, 0, OD_CK // w)),
            pl.BlockSpec((1, t, w), lambda i, j: (i, 0, OD_CV // w)),
        ],
        out_specs=pl.BlockSpec((1, t, w), lambda i, j: (i, 0, 0)),
        out_shape=jax.ShapeDtypeStruct((b, t, w), BF16),
        scratch_shapes=[pltpu.VMEM((H_C, t, LANE), F32), pltpu.VMEM((H_C, t, HEAD_DIM), F32)],
        compiler_params=_cparams("parallel", "arbitrary"),
        name="attn_c_step",
    )(proj3, cache_k, cache_v, proj3, proj3)


D_HEADS_PER_STEP = 4


def _attn_d_kernel(q_ref, kp_ref, vp_ref, ko_ref, vo_ref, bias_ref, o_ref, *, tq):
    qi = pl.program_id(2)
    heads = [slice(h * HEAD_DIM, (h + 1) * HEAD_DIM) for h in range(D_HEADS_PER_STEP)]
    qs = [(q_ref[0, :, sl] * SCALE).astype(BF16) for sl in heads]

    def own_logits(h):
        return _dot_nt(qs[h], ko_ref[0, :, heads[h]].astype(BF16)) + bias_ref[h, :, D_BAND_LEFT:D_BAND_LEFT + tq]

    def with_left():
        s_o = [own_logits(h) for h in range(D_HEADS_PER_STEP)]
        s_p = [_dot_nt(qs[h], kp_ref[0, :, heads[h]].astype(BF16)) + bias_ref[h, :, 0:D_BAND_LEFT]
               for h in range(D_HEADS_PER_STEP)]
        probs = []
        for h in range(D_HEADS_PER_STEP):
            m = jnp.maximum(jnp.max(s_p[h], axis=-1, keepdims=True), jnp.max(s_o[h], axis=-1, keepdims=True))
            p_p = jnp.exp(s_p[h] - m)
            p_o = jnp.exp(s_o[h] - m)
            l = jnp.sum(p_p, axis=-1, keepdims=True) + jnp.sum(p_o, axis=-1, keepdims=True)
            probs.append((p_p.astype(BF16), p_o.astype(BF16), l))
        for h, sl in enumerate(heads):
            p_p, p_o, l = probs[h]
            out = _dot(p_p, vp_ref[0, :, sl].astype(BF16)) + _dot(p_o, vo_ref[0, :, sl].astype(BF16))
            o_ref[0, :, sl] = (out / l).astype(o_ref.dtype)

    def own_only():
        s_o = [own_logits(h) for h in range(D_HEADS_PER_STEP)]
        for h, sl in enumerate(heads):
            p_o = jnp.exp(s_o[h] - jnp.max(s_o[h], axis=-1, keepdims=True))
            l = jnp.sum(p_o, axis=-1, keepdims=True)
            o_ref[0, :, sl] = (_dot(p_o.astype(BF16), vo_ref[0, :, sl].astype(BF16)) / l).astype(o_ref.dtype)

    pl.when(qi > 0)(with_left)
    pl.when(qi == 0)(own_only)


def _attn_d(proj3, bias):
    b, t, _ = proj3.shape
    tq = D_BAND_LEFT
    nq = t // tq
    hp = D_HEADS_PER_STEP
    w = hp * HEAD_DIM
    qc, kc, vc = OD_DQ // w, OD_DK // w, OD_DV // w
    return pl.pallas_call(
        functools.partial(_attn_d_kernel, tq=tq),
        grid=(b, H_D // hp, nq),
        in_specs=[
            pl.BlockSpec((1, tq, w), lambda i, h, j: (i, j, qc + h)),
            pl.BlockSpec((1, tq, w), lambda i, h, j: (i, jnp.maximum(j - 1, 0), kc + h)),
            pl.BlockSpec((1, tq, w), lambda i, h, j: (i, jnp.maximum(j - 1, 0), vc + h)),
            pl.BlockSpec((1, tq, w), lambda i, h, j: (i, j, kc + h)),
            pl.BlockSpec((1, tq, w), lambda i, h, j: (i, j, vc + h)),
            pl.BlockSpec((hp, tq, 2 * tq), lambda i, h, j: (h, 0, 0)),
        ],
        out_specs=pl.BlockSpec((1, tq, w), lambda i, h, j: (i, j, h)),
        out_shape=jax.ShapeDtypeStruct((b, t, H_D * HEAD_DIM), BF16),
        compiler_params=_cparams("parallel", "parallel", "parallel"),
        name="attn_d",
    )(proj3, proj3, proj3, proj3, proj3, bias)


def _attn_d_step_kernel(q_ref, kp_ref, vp_ref, kn_ref, vn_ref, bias_ref, o_ref, *, t):
    heads = [slice(h * HEAD_DIM, (h + 1) * HEAD_DIM) for h in range(H_D)]
    logits = []
    for h, sl in enumerate(heads):
        q = (q_ref[0, :, sl] * SCALE).astype(BF16)
        kp = kp_ref[0, pl.ds(h, D_BAND_LEFT, stride=H_D), :].astype(BF16)
        s_p = _dot_nt(q, kp) + bias_ref[h, :, 0:D_BAND_LEFT]
        s_o = _dot_nt(q, kn_ref[0, :, sl].astype(BF16)) + bias_ref[h, :, D_BAND_LEFT:D_BAND_LEFT + t]
        logits.append((s_p, s_o))
    probs = []
    for s_p, s_o in logits:
        m = jnp.maximum(jnp.max(s_p, axis=-1, keepdims=True), jnp.max(s_o, axis=-1, keepdims=True))
        p_p = jnp.exp(s_p - m)
        p_o = jnp.exp(s_o - m)
        l = jnp.sum(p_p, axis=-1, keepdims=True) + jnp.sum(p_o, axis=-1, keepdims=True)
        probs.append((p_p.astype(BF16), p_o.astype(BF16), l))
    for h, sl in enumerate(heads):
        p_p, p_o, l = probs[h]
        vp = vp_ref[0, pl.ds(h, D_BAND_LEFT, stride=H_D), :].astype(BF16)
        out = _dot(p_p, vp) + _dot(p_o, vn_ref[0, :, sl].astype(BF16))
        o_ref[0, :, sl] = (out / l).astype(o_ref.dtype)


def _attn_d_step(proj3, cache_k, cache_v, bias):
    b, t, _ = proj3.shape
    w = H_D * HEAD_DIM
    return pl.pallas_call(
        functools.partial(_attn_d_step_kernel, t=t),
        grid=(b,),
        in_specs=[
            pl.BlockSpec((1, t, w), lambda i: (i, 0, OD_DQ // w)),
            pl.BlockSpec((1, D_BAND_LEFT * H_D, HEAD_DIM), lambda i: (i, 0, 0)),
            pl.BlockSpec((1, D_BAND_LEFT * H_D, HEAD_DIM), lambda i: (i, 0, 0)),
            pl.BlockSpec((1, t, w), lambda i: (i, 0, OD_DK // w)),
            pl.BlockSpec((1, t, w), lambda i: (i, 0, OD_DV // w)),
            pl.BlockSpec((H_D, t, D_BAND_LEFT + t), lambda i: (0, 0, 0)),
        ],
        out_specs=pl.BlockSpec((1, t, w), lambda i: (i, 0, 0)),
        out_shape=jax.ShapeDtypeStruct((b, t, w), BF16),
        compiler_params=_cparams("parallel"),
        name="attn_d_step",
    )(proj3, cache_k, cache_v, proj3, proj3, bias)


def _pad_rows(x, rows):
    return jnp.pad(x, ((0, 0), (0, rows - x.shape[1]), (0, 0)))


def _even_mixer(h2, b, t, past, w, tz_prompt, tz_step, tq_a, mm):
    proj = mm("in_even", h2, w["w_in"], mode="headnorm", gain=w["gain"], flag=w["flag"], n_norm_tiles=9,
              name="proj_even")
    proj3 = proj.reshape(b, t, EV_PAD)
    ak = proj3[:, :, EV_AK:EV_AK + KV_A * HEAD_DIM]
    av = proj3[:, :, EV_AV:EV_AV + KV_A * HEAD_DIM]
    ik = proj3[:, :, EV_MISC + MISC_IK:EV_MISC + MISC_IK + IDX_DIM]
    bk = proj3[:, :, EV_BK:EV_BK + H_B * HEAD_DIM]
    bv = proj3[:, :, EV_BV:EV_BV + H_B * HEAD_DIM]
    raw_h = proj3[:, :, EV_MISC + MISC_BF:EV_MISC + MISC_BF + H_B].transpose(0, 2, 1)
    iq_col = EV_IQ // (IDX_HEADS * IDX_DIM)
    misc_col = EV_MISC // LANE
    if past is None:
        logf_h, cum_h = _logf_cum(raw_h, w["f_bias"], None)
        mask = _select_mask(proj3, iq_col, proj3, misc_col, proj3, misc_col, t=t, tq=tq_a, lp=t, l_valid=t,
                            q_base=0, n_sel=min(TOPK_MAX, t // 4), tk_out=tq_a, causal_prefix=True)
        out_a = _attn_a_prompt(proj3, mask, tz_prompt, tq=tq_a)
        out_b = _attn_b_prompt(proj3, cum_h, tq=min(512, t))
    else:
        p_ak, p_av, p_ik, p_bk, p_bv, p_lf = past
        p = p_ak.shape[1]
        l_valid = p + t
        lp = p + LANE
        assert p % LANE == 0 and t <= LANE, (p, t)
        logf_h, cum_h = _logf_cum(raw_h, w["f_bias"], p_lf.transpose(0, 2, 1))
        ik_all = _pad_rows(jnp.concatenate([p_ik, ik], axis=1), lp)
        mask = _select_mask(proj3, iq_col, proj3, misc_col, ik_all, None, t=t, tq=t, lp=lp, l_valid=l_valid,
                            q_base=p, n_sel=min(TOPK_MAX, l_valid // 4), tk_out=lp, causal_prefix=False)
        out_a = _attn_a_step(proj3, p_ak.reshape(b, p * KV_A, HEAD_DIM), p_av.reshape(b, p * KV_A, HEAD_DIM),
                             mask, tz_step[:, :t])
        out_b = _attn_b_step(proj3, p_bk.reshape(b, p * H_B, HEAD_DIM), p_bv.reshape(b, p * H_B, HEAD_DIM), cum_h)
    mixed = [out_a.reshape(b * t, -1), out_b.reshape(b * t, -1)]
    state = (ak.reshape(b, t, KV_A, HEAD_DIM), av.reshape(b, t, KV_A, HEAD_DIM), ik,
             bk.reshape(b, t, H_B, HEAD_DIM), bv.reshape(b, t, H_B, HEAD_DIM), logf_h.transpose(0, 2, 1))
    return mixed, state


def _odd_mixer(h2, b, t, past, w, band, mm):
    proj = mm("in_odd", h2, w["w_in"], col_block=_odd_col_block, mode="headnorm", gain=w["gain"], flag=w["flag"],
              n_norm_tiles=4, name="proj_odd")
    proj3 = proj.reshape(b, t, OD_WIDTH)
    ck = proj3[:, :, OD_CK:OD_CK + H_C * HEAD_DIM]
    cv = proj3[:, :, OD_CV:OD_CV + H_C * HEAD_DIM]
    dk = proj3[:, :, OD_DK:OD_DK + H_D * HEAD_DIM]
    dv = proj3[:, :, OD_DV:OD_DV + H_D * HEAD_DIM]
    if past is None:
        out_c = _attn_c_prompt(proj3, tq=min(512, t))
        out_d = _attn_d(proj3, band)
        keep = min(D_BAND_LEFT, t)
        d_rows = (dk[:, t - keep:], dv[:, t - keep:])
    else:
        p_ck, p_cv, p_dk, p_dv = past
        p = p_ck.shape[1]
        out_c = _attn_c_step(proj3, p_ck.reshape(b, p * H_C, HEAD_DIM), p_cv.reshape(b, p * H_C, HEAD_DIM))
        wd = p_dk.shape[1]
        assert wd == D_BAND_LEFT and p % CHUNK == 0 and t <= CHUNK, (wd, p, t)
        out_d = _attn_d_step(proj3, p_dk.reshape(b, wd * H_D, HEAD_DIM), p_dv.reshape(b, wd * H_D, HEAD_DIM),
                             band[:, :t, :D_BAND_LEFT + t])
        d_rows = (dk, dv)
    mixed = [out_c.reshape(b * t, -1), out_d.reshape(b * t, -1)]
    state = (ck.reshape(b, t, H_C, HEAD_DIM), cv.reshape(b, t, H_C, HEAD_DIM),
             d_rows[0].reshape(b, -1, H_D, HEAD_DIM), d_rows[1].reshape(b, -1, H_D, HEAD_DIM))
    return mixed, state


def _odd_col_block(j):
    return jnp.where(j < 4, j + 6, jnp.where(j < 10, j - 4, j))


def _trunk(x, caches_even, caches_odd, wts, bf16_weights):
    def mm(key, a, w_f32, col_block=None, **kw):
        if key in bf16_weights:
            w_bf16 = bf16_weights[key]
            if kw.get("tn", 512) == 512 and w_bf16.shape[1] % 1024 == 0:
                kw = dict(kw, tn=1024, n_norm_tiles=(kw.get("n_norm_tiles", 0) + 1) // 2)
            return _matmul(a, w_bf16, **kw)
        out, bf16_weights[key] = _matmul_cast(a, w_f32, col_block=col_block, **kw)
        return out

    b, t, d = x.shape
    x2 = x.reshape(b * t, d)
    h2 = _rmsnorm(x2, wts["attn_norm"][0])
    mixed, st_even = _even_mixer(h2, b, t, caches_even, wts["even"], wts["tz_prompt"], wts["tz_step"],
                                 wts["tq_a"], mm)
    x2 = mm("out_even", mixed, wts["even"]["w_out"], mode="residual", residual=x2, name="out_even")
    h2 = _rmsnorm(x2, wts["mlp_norm"][0])
    u = mm("up0", h2, wts["w_up"][0], mode="relu2", out_dtype=BF16, tn=1024, name="mlp_up0")
    x2 = _matmul(u, wts["w_down"][0], mode="residual", residual=x2, tn=256, tk=u.shape[1], name="mlp_down0")
    h2 = _rmsnorm(x2, wts["attn_norm"][1])
    mixed, st_odd = _odd_mixer(h2, b, t, caches_odd, wts["odd"], wts["band"], mm)
    x2 = mm("out_odd", mixed, wts["odd"]["w_out"], mode="residual", residual=x2, name="out_odd")
    h2 = _rmsnorm(x2, wts["mlp_norm"][1])
    u = mm("up1", h2, wts["w_up"][1], mode="relu2", out_dtype=BF16, tn=1024, name="mlp_up1")
    x2 = _matmul(u, wts["w_down"][1], mode="residual", residual=x2, tn=256, tk=u.shape[1], name="mlp_down1")
    return x2.reshape(b, t, d), tuple(s[None] for s in st_even), tuple(s[None] for s in st_odd)


def _even_weights(w_in, w_out, a_qn, a_kn, b_qn, b_kn, f_bias):
    aq, ak, av, iq, ik, iw, bq, bk, bv, bf = jnp.split(
        w_in, [1024, 1280, 1536, 2048, 2112, 2120, 3144, 4168, 5192], axis=1)
    d = w_in.shape[0]
    pad = jnp.zeros((d, EV_PAD - EV_WIDTH + LANE - (IDX_DIM + IDX_HEADS + H_B)), w_in.dtype)
    w_r = jnp.concatenate([aq, bq, bk, bv, ak, av, iq, ik, iw, bf, pad], axis=1)
    ones = jnp.ones((EV_PAD,), F32)
    gain = ones.at[EV_AQ:EV_AQ + 1024].set(jnp.tile(a_qn, H_A))
    gain = gain.at[EV_BQ:EV_BQ + 1024].set(jnp.tile(b_qn, H_B))
    gain = gain.at[EV_BK:EV_BK + 1024].set(jnp.tile(b_kn, H_B))
    gain = gain.at[EV_AK:EV_AK + 256].set(jnp.tile(a_kn, KV_A))
    flag = jnp.zeros((EV_PAD,), F32).at[0:EV_BV].set(1.0).at[EV_AK:EV_AV].set(1.0)
    return {"w_in": w_r, "gain": gain.reshape(1, -1), "flag": flag.reshape(1, -1), "w_out": w_out,
            "f_bias": f_bias}


def _odd_weights(w_in, w_out, d_qn, d_kn):
    gain = jnp.ones((OD_WIDTH,), F32)
    gain = gain.at[OD_DQ:OD_DQ + 1024].set(jnp.tile(d_qn, H_D)).at[OD_DK:OD_DK + 1024].set(jnp.tile(d_kn, H_D))
    flag = jnp.zeros((OD_WIDTH,), F32).at[0:OD_CQ].set(1.0)
    return {"w_in": w_in, "gain": gain.reshape(1, -1), "flag": flag.reshape(1, -1), "w_out": w_out}


def kernel(x_prompt, x_sample, cache_a_k, cache_a_v, cache_a_kidx, cache_b_k, cache_b_v, cache_b_logf,
           cache_c_k, cache_c_v, cache_d_k, cache_d_v, attn_norm, mlp_norm, w_in_even, w_out_even,
           w_in_odd, w_out_odd, a_q_norm, a_k_norm, b_q_norm, b_k_norm, forget_bias, t5_bias,
           d_q_norm, d_k_norm, d_rel_bias, w_up, w_down):
    tq_a = min(256, x_prompt.shape[1])
    wts = {
        "attn_norm": attn_norm,
        "mlp_norm": mlp_norm,
        "even": _even_weights(w_in_even[0], w_out_even[0], a_q_norm[0], a_k_norm[0], b_q_norm[0], b_k_norm[0],
                              forget_bias[0]),
        "odd": _odd_weights(w_in_odd[0], w_out_odd[0], d_q_norm[0], d_k_norm[0]),
        "w_up": w_up,
        "w_down": w_down.astype(BF16),
        "tq_a": tq_a,
        "tz_prompt": _t5_strip(t5_bias, tq_a),
        "tz_step": _t5_strip(t5_bias, LANE),
        "band": _band_bias(d_rel_bias[0]),
    }
    bf16_weights = {}
    y_p, ev_p, od_p = _trunk(x_prompt, None, None, wts, bf16_weights)
    caches_even = (cache_a_k[0], cache_a_v[0], cache_a_kidx[0], cache_b_k[0], cache_b_v[0], cache_b_logf[0])
    caches_odd = (cache_c_k[0], cache_c_v[0], cache_d_k[0], cache_d_v[0])
    y_s, ev_s, od_s = _trunk(x_sample, caches_even, caches_odd, wts, bf16_weights)
    return (y_p, y_s) + ev_p + od_p + ev_s + od_s
```

```python
import functools

import numpy as np
import jax
import jax.numpy as jnp
from jax import lax
from jax.experimental import pallas as pl
from jax.experimental.pallas import tpu as pltpu

F32 = jnp.float32
BF16 = jnp.bfloat16
I32 = jnp.int32

HEAD_DIM = 128
CHUNK = 64
CHUNK_SHIFT = 6
H_A = 8
KV_A = 2
REP_A = H_A // KV_A
H_B = 8
H_C = 8
H_D = 8
IDX_HEADS = 8
IDX_DIM = 64
TOPK_MAX = 256
T5_BUCKETS = 32
D_LEFT_CHUNKS = 8
D_BAND_LEFT = D_LEFT_CHUNKS * CHUNK
D_REL_CLIP = 128
EPS = 1e-6
NEG_INF = -1e30
SCALE = HEAD_DIM ** -0.5
LANE = 128
MXU_WIDTH = 256
INT32_MIN = -(2 ** 31)
NEG_INF_KEY = int(np.float32(NEG_INF).view(np.int32)) ^ 0x7FFFFFFF

VMEM_LIMIT_BYTES = 56 * 1024 * 1024

EV_AQ, EV_BQ, EV_BK, EV_BV, EV_AK, EV_AV, EV_IQ, EV_MISC = 0, 1024, 2048, 3072, 4096, 4352, 4608, 5120
EV_WIDTH = 5248
EV_PAD = 5632
MISC_IK, MISC_IW, MISC_BF = 0, 64, 72
OD_DQ, OD_DK, OD_CQ, OD_CK, OD_CV, OD_DV = 0, 1024, 2048, 3072, 4096, 5120
OD_WIDTH = 6144


def _cparams(*sem):
    return pltpu.CompilerParams(dimension_semantics=sem, vmem_limit_bytes=VMEM_LIMIT_BYTES)


def _dot_nt(a, b):
    return lax.dot_general(a, b, (((1,), (1,)), ((), ())), preferred_element_type=F32)


def _dot(a, b):
    return jnp.dot(a, b, preferred_element_type=F32)


def _lane_blocks(x):
    return [x[..., j * LANE:(j + 1) * LANE] for j in range(x.shape[-1] // LANE)]


def _pad_keys(x, rows):
    return jnp.concatenate([x, jnp.zeros((rows - x.shape[0], x.shape[1]), x.dtype)], axis=0)


def _rmsnorm_kernel(x_ref, g_ref, o_ref):
    x = x_ref[...]
    ms = jnp.mean(x * x, axis=-1, keepdims=True)
    o_ref[...] = (x * lax.rsqrt(ms + EPS) * g_ref[...]).astype(o_ref.dtype)


def _rmsnorm(x2, gain):
    m, d = x2.shape
    tm = min(512, m)
    return pl.pallas_call(
        _rmsnorm_kernel,
        grid=(m // tm,),
        in_specs=[pl.BlockSpec((tm, d), lambda i: (i, 0)), pl.BlockSpec((1, d), lambda i: (0, 0))],
        out_specs=pl.BlockSpec((tm, d), lambda i: (i, 0)),
        out_shape=jax.ShapeDtypeStruct((m, d), BF16),
        compiler_params=_cparams("parallel"),
        name="rmsnorm",
    )(x2, gain.reshape(1, d))


def _mm_kernel(*refs, n_a, nk, mode, n_norm_tiles, cast_b):
    a_refs, b_ref = refs[:n_a], refs[n_a]
    refs = refs[n_a - 1:]
    if mode == "headnorm":
        gain_ref, flag_ref, o_ref = refs[2], refs[3], refs[4]
        rest = refs[5:]
    elif mode == "residual":
        res_ref, o_ref = refs[2], refs[3]
        rest = refs[4:]
    else:
        o_ref = refs[2]
        rest = refs[3:]
    n_axis = 0 if cast_b else 1
    if cast_b:
        wb_ref = rest[0]

        @pl.when(pl.program_id(1) == 0)
        def _():
            wb_ref[...] = b_ref[...].astype(BF16)

        b_ref = wb_ref

    tn = o_ref.shape[1]

    def product(c0=0, c1=tn):
        row, acc = 0, None
        for a_ref in a_refs:
            kp = a_ref.shape[1]
            part = _dot(a_ref[...], b_ref[row:row + kp, c0:c1])
            acc = part if acc is None else acc + part
            row += kp
        return acc

    def headnorm(acc, c0):
        for g in range(acc.shape[1] // HEAD_DIM):
            sl = slice(c0 + g * HEAD_DIM, c0 + (g + 1) * HEAD_DIM)
            blk = acc[:, g * HEAD_DIM:(g + 1) * HEAD_DIM]
            ms = jnp.mean(blk * blk, axis=-1, keepdims=True)
            normed = blk * lax.rsqrt(ms + EPS) * gain_ref[:, sl]
            o_ref[:, sl] = jnp.where(flag_ref[:, sl] > 0.0, normed, blk)

    def epilogue(acc, c0=0):
        sl = slice(c0, c0 + acc.shape[1])
        if mode == "headnorm":
            headnorm(acc, c0)
        elif mode == "relu2":
            r = jnp.maximum(acc, 0.0)
            o_ref[:, sl] = (r * r).astype(o_ref.dtype)
        elif mode == "residual":
            o_ref[:, sl] = res_ref[:, sl] + acc
        else:
            o_ref[:, sl] = acc.astype(o_ref.dtype)

    def chunked():
        starts = list(range(0, tn, MXU_WIDTH))
        accs = [product(c0, c0 + MXU_WIDTH) for c0 in starts]
        for c0, acc in zip(starts, accs):
            epilogue(acc, c0)

    if nk == 1 and mode == "headnorm":
        j = pl.program_id(n_axis)
        pl.when(j < n_norm_tiles)(chunked)

        @pl.when(j >= n_norm_tiles)
        def _():
            o_ref[...] = product()
    elif nk == 1:
        chunked()
    else:
        acc_ref = rest[0]
        k = pl.program_id(2)

        @pl.when(k == 0)
        def _():
            acc_ref[...] = jnp.zeros_like(acc_ref)

        acc_ref[...] += product()

        @pl.when(k == nk - 1)
        def _():
            epilogue(acc_ref[...])


def _matmul(a, b, *, mode="plain", out_dtype=F32, gain=None, flag=None, n_norm_tiles=0, residual=None,
            tm=1024, tn=512, tk=2048, name="matmul"):
    a_parts = list(a) if isinstance(a, (list, tuple)) else [a]
    m = a_parts[0].shape[0]
    kdim, n = b.shape
    tm = tm if m % tm == 0 else min(512, m)
    if len(a_parts) > 1:
        tk = kdim
        in_specs = [pl.BlockSpec((tm, part.shape[1]), lambda i, j, k: (i, 0)) for part in a_parts]
    else:
        in_specs = [pl.BlockSpec((tm, tk), lambda i, j, k: (i, k))]
    nk = kdim // tk
    in_specs.append(pl.BlockSpec((tk, tn), lambda i, j, k: (k, j)))
    args = a_parts + [b]
    if mode == "headnorm":
        in_specs += [pl.BlockSpec((1, tn), lambda i, j, k: (0, j)), pl.BlockSpec((1, tn), lambda i, j, k: (0, j))]
        args += [gain, flag]
    elif mode == "residual":
        in_specs += [pl.BlockSpec((tm, tn), lambda i, j, k: (i, j))]
        args += [residual]
    scratch = [pltpu.VMEM((tm, tn), F32)] if nk > 1 else []
    return pl.pallas_call(
        functools.partial(_mm_kernel, n_a=len(a_parts), nk=nk, mode=mode, n_norm_tiles=n_norm_tiles,
                          cast_b=False),
        grid=(m // tm, n // tn, nk),
        in_specs=in_specs,
        out_specs=pl.BlockSpec((tm, tn), lambda i, j, k: (i, j)),
        out_shape=jax.ShapeDtypeStruct((m, n), out_dtype),
        scratch_shapes=scratch,
        compiler_params=_cparams("parallel", "parallel", "arbitrary"),
        name=name,
    )(*args)


def _matmul_cast(a, b_f32, *, col_block=None, n=None, mode="plain", out_dtype=F32, gain=None, flag=None,
                 n_norm_tiles=0, residual=None, tm=1024, tn=512, name="matmul"):
    a_parts = list(a) if isinstance(a, (list, tuple)) else [a]
    m = a_parts[0].shape[0]
    kdim = b_f32.shape[0]
    n = b_f32.shape[1] if n is None else n
    tm = tm if m % tm == 0 else min(512, m)
    src = (lambda j: j) if col_block is None else col_block
    in_specs = [pl.BlockSpec((tm, part.shape[1]), lambda j, i: (i, 0)) for part in a_parts]
    in_specs.append(pl.BlockSpec((kdim, tn), lambda j, i: (0, src(j))))
    args = a_parts + [b_f32]
    if mode == "headnorm":
        in_specs += [pl.BlockSpec((1, tn), lambda j, i: (0, j)), pl.BlockSpec((1, tn), lambda j, i: (0, j))]
        args += [gain, flag]
    elif mode == "residual":
        in_specs += [pl.BlockSpec((tm, tn), lambda j, i: (i, j))]
        args += [residual]
    return pl.pallas_call(
        functools.partial(_mm_kernel, n_a=len(a_parts), nk=1, mode=mode, n_norm_tiles=n_norm_tiles, cast_b=True),
        grid=(n // tn, m // tm),
        in_specs=in_specs,
        out_specs=[pl.BlockSpec((tm, tn), lambda j, i: (i, j)), pl.BlockSpec((kdim, tn), lambda j, i: (0, j))],
        out_shape=[jax.ShapeDtypeStruct((m, n), out_dtype), jax.ShapeDtypeStruct((kdim, n), BF16)],
        compiler_params=_cparams("parallel", "arbitrary"),
        name=name,
    )(*args)


def _cumsum_block(x, carry):
    n = x.shape[1]
    r = lax.broadcasted_iota(I32, (n, n), 0)
    c = lax.broadcasted_iota(I32, (n, n), 1)
    tri = jnp.where(r <= c, 1.0, 0.0).astype(BF16)
    hi = x.astype(BF16)
    r1 = x - hi.astype(F32)
    mid = r1.astype(BF16)
    lo = (r1 - mid.astype(F32)).astype(BF16)
    cs = _dot(hi, tri) + _dot(mid, tri) + _dot(lo, tri) + carry
    return cs, cs[:, n - 1:n]


def _logf_cum_kernel(*refs, p_len, t_len, blk):
    if p_len:
        raw_ref, fb_ref, past_ref, lf_ref, cum_ref = refs
    else:
        raw_ref, fb_ref, lf_ref, cum_ref = refs
    x = raw_ref[0] + fb_ref[...]
    lf = jnp.minimum(x, 0.0) - jnp.log1p(jnp.exp(-jnp.abs(x)))
    lf_ref[0] = lf
    carry = jnp.zeros((H_B, 1), F32)
    for s in range(0, p_len, blk):
        cs, carry = _cumsum_block(past_ref[0, :, s:s + blk], carry)
        cum_ref[0, :, s:s + blk] = cs
    nb = min(blk, t_len)
    for s in range(0, t_len, nb):
        cs, carry = _cumsum_block(lf[:, s:s + nb], carry)
        cum_ref[0, :, p_len + s:p_len + s + nb] = cs


def _logf_cum(raw_h, fbias, past_h):
    b, h, t = raw_h.shape
    p = 0 if past_h is None else past_h.shape[2]
    in_specs = [pl.BlockSpec((1, h, t), lambda i: (i, 0, 0)), pl.BlockSpec((h, 1), lambda i: (0, 0))]
    args = [raw_h, fbias.reshape(h, 1)]
    if p:
        in_specs.append(pl.BlockSpec((1, h, p), lambda i: (i, 0, 0)))
        args.append(past_h)
    return pl.pallas_call(
        functools.partial(_logf_cum_kernel, p_len=p, t_len=t, blk=512),
        grid=(b,),
        in_specs=in_specs,
        out_specs=[pl.BlockSpec((1, h, t), lambda i: (i, 0, 0)), pl.BlockSpec((1, h, p + t), lambda i: (i, 0, 0))],
        out_shape=[jax.ShapeDtypeStruct((b, h, t), F32), jax.ShapeDtypeStruct((b, h, p + t), F32)],
        compiler_params=_cparams("parallel"),
        name="logf_cum",
    )(*args)


def _toeplitz(row_vals, rows, cols):
    w = row_vals.shape[1]
    full = jnp.broadcast_to(row_vals[0:1, :], (rows, w))
    return pltpu.roll(full, 0, 1, stride=1, stride_axis=0)[:, :cols]


def _t5_strip_kernel(tab_ref, o_ref, *, tile):
    h = pl.program_id(0)
    width = 4 * tile
    m = lax.broadcasted_iota(I32, (8, width), 1)
    d = jnp.where(m < width // 2, m, m - width)
    rel = d - tile
    n = jnp.abs(rel)
    large = jnp.full_like(n, 8)
    for thr in (12, 16, 23, 32, 46, 64, 91):
        large = large + jnp.where(n >= thr, 1, 0)
    bucket = jnp.where(rel > 0, T5_BUCKETS // 2, 0) + jnp.where(n < 8, n, large)
    vals = jnp.zeros((8, width), F32)
    for bkt in range(T5_BUCKETS):
        vals = jnp.where(bucket == bkt, tab_ref[bkt, h], vals)
    vals = vals - tab_ref[T5_BUCKETS // 2 - 1, h]
    o_ref[0] = _toeplitz(vals, tile, 2 * tile)


def _t5_strip(t5_table, tile):
    return pl.pallas_call(
        functools.partial(_t5_strip_kernel, tile=tile),
        grid=(H_A,),
        in_specs=[pl.BlockSpec(memory_space=pltpu.SMEM)],
        out_specs=pl.BlockSpec((1, tile, 2 * tile), lambda h: (h, 0, 0)),
        out_shape=jax.ShapeDtypeStruct((H_A, tile, 2 * tile), F32),
        compiler_params=_cparams("parallel"),
        name="t5_strip",
    )(t5_table)


def _band_bias_kernel(tab_ref, o_ref, *, rows, cols, width):
    h = pl.program_id(0)
    m = lax.broadcasted_iota(I32, (8, width), 1)
    d = jnp.where(m < width // 2, m, m - width)
    idx = jnp.clip(d - D_BAND_LEFT, -D_REL_CLIP, D_REL_CLIP) + D_REL_CLIP
    vals = jnp.zeros((8, width), F32)
    for r in range(2 * D_REL_CLIP + 1):
        vals = jnp.where(idx == r, tab_ref[r, h], vals)
    vals = vals - tab_ref[0, h]
    i_chunk = lax.broadcasted_iota(I32, (rows, cols), 0) >> CHUNK_SHIFT
    c_chunk = lax.broadcasted_iota(I32, (rows, cols), 1) >> CHUNK_SHIFT
    visible = (c_chunk >= i_chunk) & (c_chunk <= i_chunk + D_LEFT_CHUNKS)
    o_ref[0] = jnp.where(visible, _toeplitz(vals, rows, cols), NEG_INF)


def _band_bias(rel_table):
    rows, cols, width = D_BAND_LEFT, 2 * D_BAND_LEFT, 4 * D_BAND_LEFT
    return pl.pallas_call(
        functools.partial(_band_bias_kernel, rows=rows, cols=cols, width=width),
        grid=(H_D,),
        in_specs=[pl.BlockSpec(memory_space=pltpu.SMEM)],
        out_specs=pl.BlockSpec((1, rows, cols), lambda h: (h, 0, 0)),
        out_shape=jax.ShapeDtypeStruct((H_D, rows, cols), F32),
        compiler_params=_cparams("parallel"),
        name="band_bias",
    )(rel_table)


def _select_kernel(*refs, n_g, sb, tq, nq, lp, l_valid, q_base, n_sel, tk_out, causal_prefix):
    iq_refs, misc_refs = refs[:n_g], refs[n_g:2 * n_g]
    kk_ref, o_ref, key_ref = refs[2 * n_g:]
    step = pl.program_id(1)
    rows = sb * tq
    kf = float(n_sel)
    gs = range(n_g)

    def count(pred):
        return jnp.sum(jnp.where(pred, 1.0, 0.0), axis=-1, keepdims=True)

    def run(tiles):
        pfx = [p for _, p in tiles]
        adm = []
        for g, (q_tile, p) in enumerate(tiles):
            scores = []
            for s in range(sb):
                iq = iq_refs[g][s]
                iw = misc_refs[g][s][:, MISC_IW:MISC_IW + IDX_HEADS]
                ik = kk_ref[s, 0:p, 0:IDX_DIM].astype(BF16)
                score = jnp.zeros((tq, p), F32)
                for h in range(IDX_HEADS):
                    dots = _dot_nt(iq[:, h * IDX_DIM:(h + 1) * IDX_DIM].astype(BF16), ik)
                    score = score + iw[:, h:h + 1] * jnp.maximum(dots, 0.0)
                scores.append(score)
            score = jnp.concatenate(scores, axis=0) if sb > 1 else scores[0]
            row_in_tile = jnp.concatenate([lax.broadcasted_iota(I32, (tq, 1), 0)] * sb, axis=0)
            q_pos = q_base + q_tile * tq + row_in_tile
            kp = lax.broadcasted_iota(I32, (1, p), 1)
            ok = ((kp >> CHUNK_SHIFT) <= (q_pos >> CHUNK_SHIFT)) & (kp < l_valid)
            bits = lax.bitcast_convert_type(jnp.where(ok, score, NEG_INF), I32)
            key_ref[g, :, 0:p] = jnp.where(bits < 0, bits ^ jnp.int32(0x7FFFFFFF), bits)
            adm.append(ok)

        def keys(g):
            return key_ref[g, :, 0:pfx[g]]

        t0 = tuple(jnp.where(count(keys(g) >= 0) >= kf, jnp.int32(0), jnp.int32(INT32_MIN)) for g in gs)

        def thr_body(i, ts):
            bit = jnp.int32(1) << (30 - i)
            return tuple(jnp.where(count(keys(g) >= ts[g] + bit) >= kf, ts[g] + bit, ts[g]) for g in gs)

        thr = lax.fori_loop(0, 31, thr_body, t0)
        need = [kf - count(keys(g) > thr[g]) for g in gs]
        tied = [(count(keys(g) == thr[g]) != need[g]) & (thr[g] > NEG_INF_KEY) for g in gs]
        any_tied = sum(jnp.max(jnp.where(tied[g], 1.0, 0.0)) for g in gs) > 0.5

        def all_tied():
            return tuple(jnp.where(keys(g) == thr[g], 1.0, 0.0) for g in gs)

        def lowest_tied():
            r = lax.broadcasted_iota(I32, (LANE, 2 * LANE), 0)
            c = lax.broadcasted_iota(I32, (LANE, 2 * LANE), 1)
            count_mat = jnp.where((r <= c) | (c >= LANE), 1.0, 0.0).astype(BF16)
            out = []
            for g, tied_mask in enumerate(all_tied()):
                seen = jnp.zeros((rows, LANE), F32)
                kept = []
                for blk in _lane_blocks(tied_mask):
                    both = _dot(blk.astype(BF16), count_mat)
                    kept.append(jnp.where(both[:, :LANE] + seen <= need[g], blk, 0.0))
                    seen = seen + both[:, LANE:]
                out.append(jnp.concatenate(kept, axis=1) if len(kept) > 1 else kept[0])
            return tuple(out)

        take_tied = lax.cond(any_tied, lowest_tied, all_tied)
        for g in gs:
            sel = ((keys(g) > thr[g]) | (take_tied[g] > 0.5)) & adm[g]
            madd = jnp.where(sel, 0.0, NEG_INF)
            for s in range(sb):
                for kt in range(lp // tk_out):
                    if (kt + 1) * tk_out <= pfx[g]:
                        o_ref[s, g, 0, kt] = madd[s * tq:(s + 1) * tq, kt * tk_out:(kt + 1) * tk_out]
                    else:
                        o_ref[s, g, 0, kt] = jnp.full((tq, tk_out), NEG_INF, F32)

    if not causal_prefix:
        run([(step, lp)] * n_g)
    elif n_g == 2:
        for j in range(nq // 2):
            pl.when(step == j)(functools.partial(run, [(j, (j + 1) * tq), (nq - 1 - j, (nq - j) * tq)]))
    else:
        for j in range(nq):
            pl.when(step == j)(functools.partial(run, [(j, (j + 1) * tq)]))


def _select_mask(iq_arr, iq_col, misc_arr, misc_col, kk_arr, kk_col, *, t, tq, lp, l_valid, q_base, n_sel,
                 tk_out, causal_prefix):
    b = iq_arr.shape[0]
    nq = t // tq
    wk = kk_arr.shape[2] if kk_col is None else LANE
    kcol = 0 if kk_col is None else kk_col
    sb = 4 if (nq == 1 and tq <= 64 and b % 4 == 0) else 1
    n_g = 2 if (causal_prefix and nq % 2 == 0) else 1
    steps = nq // n_g
    tile_of = [lambda j: j, lambda j: nq - 1 - j]
    nkt = lp // tk_out
    in_specs = [pl.BlockSpec((sb, tq, IDX_HEADS * IDX_DIM), lambda i, j, g=g: (i, tile_of[g](j), iq_col))
                for g in range(n_g)]
    in_specs += [pl.BlockSpec((sb, tq, LANE), lambda i, j, g=g: (i, tile_of[g](j), misc_col)) for g in range(n_g)]
    in_specs.append(pl.BlockSpec((sb, lp, wk), lambda i, j: (i, 0, kcol)))
    return pl.pallas_call(
        functools.partial(_select_kernel, n_g=n_g, sb=sb, tq=tq, nq=nq, lp=lp, l_valid=l_valid, q_base=q_base,
                          n_sel=n_sel, tk_out=tk_out, causal_prefix=causal_prefix),
        grid=(b // sb, steps),
        in_specs=in_specs,
        out_specs=pl.BlockSpec((sb, n_g, 1, nkt, tq, tk_out), lambda i, j: (i, 0, j, 0, 0, 0)),
        out_shape=jax.ShapeDtypeStruct((b, n_g, steps, nkt, tq, tk_out), F32),
        scratch_shapes=[pltpu.VMEM((n_g, sb * tq, lp), I32)],
        compiler_params=_cparams("parallel", "parallel"),
        name="a_select",
    )(*([iq_arr] * n_g), *([misc_arr] * n_g), kk_arr)


def _mask_tile_index(q_tile, nq, n_g):
    if n_g == 1:
        return 0, q_tile
    upper = q_tile >= nq // 2
    return jnp.where(upper, 1, 0), jnp.where(upper, nq - 1 - q_tile, q_tile)


def _online_stats(s, m_s, l_s):
    blocks = _lane_blocks(s)
    mx = blocks[0]
    for blk in blocks[1:]:
        mx = jnp.maximum(mx, blk)
    m_prev = m_s[...]
    m_new = jnp.maximum(m_prev, jnp.max(mx, axis=-1, keepdims=True))
    alpha = jnp.exp(m_prev - m_new)
    ps = [jnp.exp(blk - m_new) for blk in blocks]
    psum = ps[0]
    for p in ps[1:]:
        psum = psum + p
    l_s[...] = alpha * l_s[...] + psum
    m_s[...] = m_new
    p_all = jnp.concatenate([p.astype(BF16) for p in ps], axis=1) if len(ps) > 1 else ps[0].astype(BF16)
    return alpha, p_all


def _online_init(m_s, l_s, acc_s):
    m_s[...] = jnp.full_like(m_s, NEG_INF)
    l_s[...] = jnp.zeros_like(l_s)
    acc_s[...] = jnp.zeros_like(acc_s)


def _online_result(l_s, acc_s):
    return acc_s[...] / jnp.sum(l_s[...], axis=-1, keepdims=True)


def _stack_heads(q):
    return jnp.concatenate([q[:, r * HEAD_DIM:(r + 1) * HEAD_DIM] for r in range(REP_A)], axis=0).astype(BF16)


def _attn_a_prompt_kernel(q_ref, k_ref, v_ref, mask_ref, tz_ref, o_ref, m_s, l_s, acc_s, *, tq):
    qi = pl.program_id(1)
    width = REP_A * HEAD_DIM
    qs = [_stack_heads(q_ref[0, :, g * width:(g + 1) * width] * SCALE) for g in range(KV_A)]
    _online_init(m_s, l_s, acc_s)

    def tile(kt, bias_cols):
        off = pl.multiple_of(kt * tq, tq)
        mask = mask_ref[0, 0, 0, kt][None]
        logits = []
        for g in range(KV_A):
            k = k_ref[0, pl.ds(off, tq), g * HEAD_DIM:(g + 1) * HEAD_DIM].astype(BF16)
            s = _dot_nt(qs[g], k).reshape(REP_A, tq, tq) + mask
            if bias_cols is not None:
                s = s + tz_ref[g * REP_A:(g + 1) * REP_A, :, bias_cols]
            logits.append(s.reshape(REP_A * tq, tq))
        stats = [_online_stats(logits[g], m_s.at[g], l_s.at[g]) for g in range(KV_A)]
        for g in range(KV_A):
            alpha, p_all = stats[g]
            v = v_ref[0, pl.ds(off, tq), g * HEAD_DIM:(g + 1) * HEAD_DIM].astype(BF16)
            acc_s[g] = alpha * acc_s[g] + _dot(p_all, v)

    def plain_body(kt, carry):
        tile(kt, None)
        return carry

    lax.fori_loop(0, jnp.maximum(qi - 1, 0), plain_body, 0)

    @pl.when(qi >= 1)
    def _():
        tile(qi - 1, slice(0, tq))

    tile(qi, slice(tq, 2 * tq))
    for g in range(KV_A):
        out = _online_result(l_s.at[g], acc_s.at[g])
        for r in range(REP_A):
            col = (g * REP_A + r) * HEAD_DIM
            o_ref[0, :, col:col + HEAD_DIM] = out[r * tq:(r + 1) * tq].astype(o_ref.dtype)


def _attn_a_prompt(proj3, mask, tz, *, tq):
    b, t, _ = proj3.shape
    nq = t // tq
    qw, kw = H_A * HEAD_DIM, KV_A * HEAD_DIM
    rows = REP_A * tq
    n_g = mask.shape[1]

    def mask_idx(i, j):
        grp, pos = _mask_tile_index(j, nq, n_g)
        return (i, grp, pos, 0, 0, 0)

    return pl.pallas_call(
        functools.partial(_attn_a_prompt_kernel, tq=tq),
        grid=(b, nq),
        in_specs=[
            pl.BlockSpec((1, tq, qw), lambda i, j: (i, j, EV_AQ // qw)),
            pl.BlockSpec((1, t, kw), lambda i, j: (i, 0, EV_AK // kw)),
            pl.BlockSpec((1, t, kw), lambda i, j: (i, 0, EV_AV // kw)),
            pl.BlockSpec((1, 1, 1, nq, tq, tq), mask_idx),
            pl.BlockSpec((H_A, tq, 2 * tq), lambda i, j: (0, 0, 0)),
        ],
        out_specs=pl.BlockSpec((1, tq, qw), lambda i, j: (i, j, 0)),
        out_shape=jax.ShapeDtypeStruct((b, t, qw), BF16),
        scratch_shapes=[pltpu.VMEM((KV_A, rows, LANE), F32), pltpu.VMEM((KV_A, rows, LANE), F32),
                        pltpu.VMEM((KV_A, rows, HEAD_DIM), F32)],
        compiler_params=_cparams("parallel", "parallel"),
        name="attn_a_prompt",
    )(proj3, proj3, proj3, mask, tz)


def _attn_a_step_kernel(q_ref, kc_ref, vc_ref, kn_ref, vn_ref, mask_ref, tz_ref, o_ref, *, t, p):
    width = REP_A * HEAD_DIM
    mask = mask_ref[0, 0, 0, 0]
    logits = []
    for g in range(KV_A):
        heads = slice(g * REP_A, (g + 1) * REP_A)
        qs = _stack_heads(q_ref[0, :, g * width:(g + 1) * width] * SCALE)
        kc = kc_ref[0, pl.ds(g, p, stride=KV_A), :].astype(BF16)
        kn = _pad_keys(kn_ref[0, :, g * HEAD_DIM:(g + 1) * HEAD_DIM], LANE).astype(BF16)
        s_c = _dot_nt(qs, kc).reshape(REP_A, t, p) + mask[None, :, 0:p]
        s_n = (_dot_nt(qs, kn).reshape(REP_A, t, LANE) + mask[None, :, p:p + LANE]
               + tz_ref[heads, :, LANE:2 * LANE])
        blocks = _lane_blocks(s_c)
        blocks[-1] = blocks[-1] + tz_ref[heads, :, 0:LANE]
        blocks.append(s_n)
        logits.append(blocks)
    probs = []
    for blocks in logits:
        mx = blocks[0]
        for blk in blocks[1:]:
            mx = jnp.maximum(mx, blk)
        m = jnp.max(mx, axis=-1, keepdims=True)
        ps = [jnp.exp(blk - m) for blk in blocks]
        psum = ps[0]
        for pb in ps[1:]:
            psum = psum + pb
        l = jnp.sum(psum, axis=-1, keepdims=True)
        p_c = jnp.concatenate([pb.astype(BF16) for pb in ps[:-1]], axis=-1).reshape(REP_A * t, p)
        probs.append((p_c, ps[-1].astype(BF16).reshape(REP_A * t, LANE), l))
    for g, (p_c, p_n, l) in enumerate(probs):
        vc = vc_ref[0, pl.ds(g, p, stride=KV_A), :].astype(BF16)
        vn = _pad_keys(vn_ref[0, :, g * HEAD_DIM:(g + 1) * HEAD_DIM], LANE).astype(BF16)
        out = (_dot(p_c, vc) + _dot(p_n, vn)).reshape(REP_A, t, HEAD_DIM) / l
        for r in range(REP_A):
            col = (g * REP_A + r) * HEAD_DIM
            o_ref[0, :, col:col + HEAD_DIM] = out[r].astype(o_ref.dtype)


def _attn_a_step(proj3, cache_k, cache_v, mask, tz):
    b, t, _ = proj3.shape
    p = cache_k.shape[1] // KV_A
    qw, kw = H_A * HEAD_DIM, KV_A * HEAD_DIM
    return pl.pallas_call(
        functools.partial(_attn_a_step_kernel, t=t, p=p),
        grid=(b,),
        in_specs=[
            pl.BlockSpec((1, t, qw), lambda i: (i, 0, EV_AQ // qw)),
            pl.BlockSpec((1, p * KV_A, HEAD_DIM), lambda i: (i, 0, 0)),
            pl.BlockSpec((1, p * KV_A, HEAD_DIM), lambda i: (i, 0, 0)),
            pl.BlockSpec((1, t, kw), lambda i: (i, 0, EV_AK // kw)),
            pl.BlockSpec((1, t, kw), lambda i: (i, 0, EV_AV // kw)),
            pl.BlockSpec((1, 1, 1, 1, t, p + LANE), lambda i: (i, 0, 0, 0, 0, 0)),
            pl.BlockSpec((H_A, t, 2 * LANE), lambda i: (0, 0, 0)),
        ],
        out_specs=pl.BlockSpec((1, t, qw), lambda i: (i, 0, 0)),
        out_shape=jax.ShapeDtypeStruct((b, t, qw), BF16),
        compiler_params=_cparams("parallel"),
        name="attn_a_step",
    )(proj3, cache_k, cache_v, proj3, proj3, mask, tz)


def _causal_mask(s, row0=0):
    r = row0 + lax.broadcasted_iota(I32, s.shape, 0)
    c = lax.broadcasted_iota(I32, s.shape, 1)
    return jnp.where(c <= r, s, NEG_INF)


B_HEADS_PER_STEP = 2


def _attn_b_prompt_kernel(q_ref, k_ref, v_ref, ck_ref, o_ref, m_s, l_s, acc_s, *, tq):
    qi = pl.program_id(2)
    heads = [slice(h * HEAD_DIM, (h + 1) * HEAD_DIM) for h in range(B_HEADS_PER_STEP)]
    qs = [(q_ref[0, :, sl] * SCALE).astype(BF16) for sl in heads]
    _online_init(m_s, l_s, acc_s)

    def tile(kt, diag):
        off = pl.multiple_of(kt * tq, tq)
        logits = []
        for h, sl in enumerate(heads):
            s = _dot_nt(qs[h], k_ref[0, pl.ds(off, tq), sl].astype(BF16)) - ck_ref[0, h, kt]
            logits.append(_causal_mask(s) if diag else s)
        stats = [_online_stats(logits[h], m_s.at[h], l_s.at[h]) for h in range(B_HEADS_PER_STEP)]
        for h, sl in enumerate(heads):
            alpha, p_all = stats[h]
            acc_s[h] = alpha * acc_s[h] + _dot(p_all, v_ref[0, pl.ds(off, tq), sl].astype(BF16))

    def body(kt, carry):
        tile(kt, False)
        return carry

    lax.fori_loop(0, qi, body, 0)
    tile(qi, True)
    for h, sl in enumerate(heads):
        o_ref[0, :, sl] = _online_result(l_s.at[h], acc_s.at[h]).astype(o_ref.dtype)


def _attn_b_prompt(proj3, cum_h, *, tq=512):
    b, t, _ = proj3.shape
    nq = t // tq
    ck = cum_h.reshape(b, H_B, nq, 1, tq)
    hp = B_HEADS_PER_STEP
    w = hp * HEAD_DIM
    return pl.pallas_call(
        functools.partial(_attn_b_prompt_kernel, tq=tq),
        grid=(b, H_B // hp, nq),
        in_specs=[
            pl.BlockSpec((1, tq, w), lambda i, h, j: (i, j, EV_BQ // w + h)),
            pl.BlockSpec((1, t, w), lambda i, h, j: (i, 0, EV_BK // w + h)),
            pl.BlockSpec((1, t, w), lambda i, h, j: (i, 0, EV_BV // w + h)),
            pl.BlockSpec((1, hp, nq, 1, tq), lambda i, h, j: (i, h, 0, 0, 0)),
        ],
        out_specs=pl.BlockSpec((1, tq, w), lambda i, h, j: (i, j, h)),
        out_shape=jax.ShapeDtypeStruct((b, t, H_B * HEAD_DIM), BF16),
        scratch_shapes=[pltpu.VMEM((hp, tq, LANE), F32), pltpu.VMEM((hp, tq, LANE), F32),
                        pltpu.VMEM((hp, tq, HEAD_DIM), F32)],
        compiler_params=_cparams("parallel", "parallel", "parallel"),
        name="attn_b_prompt",
    )(proj3, proj3, proj3, ck)


def _attn_b_step_kernel(q_ref, kc_ref, vc_ref, kn_ref, vn_ref, ckc_ref, ckn_ref, o_ref, m_s, l_s, acc_s,
                        *, nkc, t, tkc):
    kb = pl.program_id(1)

    @pl.when(kb == 0)
    def _():
        _online_init(m_s, l_s, acc_s)

    def q_head(h):
        return (q_ref[0, :, h * HEAD_DIM:(h + 1) * HEAD_DIM] * SCALE).astype(BF16)

    @pl.when(kb < nkc)
    def _():
        ck = ckc_ref[0, 0]
        logits = [_dot_nt(q_head(h), kc_ref[0, pl.ds(h, tkc, stride=H_B), :].astype(BF16)) - ck[h:h + 1, :]
                  for h in range(H_B)]
        stats = [_online_stats(logits[h], m_s.at[h], l_s.at[h]) for h in range(H_B)]
        for h in range(H_B):
            alpha, p_all = stats[h]
            v = vc_ref[0, pl.ds(h, tkc, stride=H_B), :].astype(BF16)
            acc_s[h] = alpha * acc_s[h] + _dot(p_all, v)

    @pl.when(kb == nkc)
    def _():
        ckn = ckn_ref[0]
        heads = [slice(h * HEAD_DIM, (h + 1) * HEAD_DIM) for h in range(H_B)]
        logits = [_causal_mask(_dot_nt(q_head(h), _pad_keys(kn_ref[0, :, heads[h]], LANE).astype(BF16))
                               - ckn[h:h + 1, :]) for h in range(H_B)]
        stats = [_online_stats(logits[h], m_s.at[h], l_s.at[h]) for h in range(H_B)]
        for h in range(H_B):
            alpha, p_all = stats[h]
            v = _pad_keys(vn_ref[0, :, heads[h]], LANE).astype(BF16)
            acc_s[h] = alpha * acc_s[h] + _dot(p_all, v)
            o_ref[0, :, heads[h]] = _online_result(l_s.at[h], acc_s.at[h]).astype(o_ref.dtype)


def _attn_b_step(proj3, cache_k, cache_v, cum_h, *, tkc=1024):
    b, t, _ = proj3.shape
    p = cache_k.shape[1] // H_B
    assert p % tkc == 0 and t <= LANE, (p, tkc, t)
    nkc = p // tkc
    w = H_B * HEAD_DIM
    ck_cache = cum_h[:, :, :p].reshape(b, H_B, nkc, tkc).transpose(0, 2, 1, 3)
    ck_new = jnp.pad(cum_h[:, :, p:], ((0, 0), (0, 0), (0, LANE - t)))
    last = nkc - 1
    return pl.pallas_call(
        functools.partial(_attn_b_step_kernel, nkc=nkc, t=t, tkc=tkc),
        grid=(b, nkc + 1),
        in_specs=[
            pl.BlockSpec((1, t, w), lambda i, j: (i, 0, EV_BQ // w)),
            pl.BlockSpec((1, tkc * H_B, HEAD_DIM), lambda i, j: (i, jnp.minimum(j, last), 0)),
            pl.BlockSpec((1, tkc * H_B, HEAD_DIM), lambda i, j: (i, jnp.minimum(j, last), 0)),
            pl.BlockSpec((1, t, w), lambda i, j: (i, 0, EV_BK // w)),
            pl.BlockSpec((1, t, w), lambda i, j: (i, 0, EV_BV // w)),
            pl.BlockSpec((1, 1, H_B, tkc), lambda i, j: (i, jnp.minimum(j, last), 0, 0)),
            pl.BlockSpec((1, H_B, LANE), lambda i, j: (i, 0, 0)),
        ],
        out_specs=pl.BlockSpec((1, t, w), lambda i, j: (i, 0, 0)),
        out_shape=jax.ShapeDtypeStruct((b, t, w), BF16),
        scratch_shapes=[pltpu.VMEM((H_B, t, LANE), F32), pltpu.VMEM((H_B, t, LANE), F32),
                        pltpu.VMEM((H_B, t, HEAD_DIM), F32)],
        compiler_params=_cparams("parallel", "arbitrary"),
        name="attn_b_step",
    )(proj3, cache_k, cache_v, proj3, proj3, ck_cache, ck_new)


def _tri_ones():
    r = lax.broadcasted_iota(I32, (2 * LANE, 2 * LANE), 0) & (LANE - 1)
    c = lax.broadcasted_iota(I32, (2 * LANE, 2 * LANE), 1)
    return jnp.where((r > c) | (c >= LANE), 1.0, 0.0).astype(BF16)


def _stick_scores(z, row_minus_col, tri_ones):
    rc_blocks = None if row_minus_col is None else _lane_blocks(row_minus_col)
    out = []
    for j, zb in enumerate(_lane_blocks(z)):
        tail = jnp.log(1.0 + jnp.exp(-jnp.abs(zb)))
        log_beta = jnp.minimum(zb, 0.0) - tail
        log_keep = -jnp.maximum(zb, 0.0) - tail
        strict = None
        if rc_blocks is not None:
            strict = rc_blocks[j] > 0
            log_keep = jnp.where(strict, log_keep, 0.0)
        hi = log_keep.astype(BF16)
        lo = (log_keep - hi.astype(F32)).astype(BF16)
        out.append((log_beta, strict, _dot(jnp.concatenate([hi, lo], axis=1), tri_ones)))
    return out


def _stick_weights(blocks, run):
    ws = [None] * len(blocks)
    for j in reversed(range(len(blocks))):
        log_beta, strict, sums = blocks[j]
        w = jnp.exp(log_beta + sums[:, :LANE] + run)
        if strict is not None:
            w = jnp.where(strict, w, 0.0)
        ws[j] = w.astype(BF16)
        run = run + sums[:, LANE:]
    return (jnp.concatenate(ws, axis=1) if len(ws) > 1 else ws[0]), run


C_HEADS_PER_STEP = 2


def _attn_c_prompt_kernel(q_ref, k_ref, v_ref, o_ref, run_s, acc_s, *, tq):
    qi = pl.program_id(2)
    heads = [slice(h * HEAD_DIM, (h + 1) * HEAD_DIM) for h in range(C_HEADS_PER_STEP)]
    qs = [(q_ref[0, :, sl] * SCALE).astype(BF16) for sl in heads]
    tri_ones = _tri_ones()

    def tile(kt, row_minus_col, first):
        off = pl.multiple_of(kt * tq, tq)
        zs = [_dot_nt(qs[h], k_ref[0, pl.ds(off, tq), sl].astype(BF16)) for h, sl in enumerate(heads)]
        scored = [_stick_scores(z, row_minus_col, tri_ones) for z in zs]
        for h, sl in enumerate(heads):
            run = jnp.zeros((tq, LANE), F32) if first else run_s[h]
            w_all, run = _stick_weights(scored[h], run)
            out = _dot(w_all, v_ref[0, pl.ds(off, tq), sl].astype(BF16))
            acc_s[h] = out if first else acc_s[h] + out
            run_s[h] = run

    rc = lax.broadcasted_iota(I32, (tq, tq), 0) - lax.broadcasted_iota(I32, (tq, tq), 1)
    tile(qi, rc, True)

    def body(i, carry):
        tile(qi - 1 - i, None, False)
        return carry

    lax.fori_loop(0, qi, body, 0)
    for h, sl in enumerate(heads):
        o_ref[0, :, sl] = acc_s[h].astype(o_ref.dtype)


def _attn_c_prompt(proj3, *, tq=512):
    b, t, _ = proj3.shape
    nq = t // tq
    hp = C_HEADS_PER_STEP
    w = hp * HEAD_DIM
    return pl.pallas_call(
        functools.partial(_attn_c_prompt_kernel, tq=tq),
        grid=(b, H_C // hp, nq),
        in_specs=[
            pl.BlockSpec((1, tq, w), lambda i, h, j: (i, j, OD_CQ // w + h)),
            pl.BlockSpec((1, t, w), lambda i, h, j: (i, 0, OD_CK // w + h)),
            pl.BlockSpec((1, t, w), lambda i, h, j: (i, 0, OD_CV // w + h)),
        ],
        out_specs=pl.BlockSpec((1, tq, w), lambda i, h, j: (i, j, h)),
        out_shape=jax.ShapeDtypeStruct((b, t, H_C * HEAD_DIM), BF16),
        scratch_shapes=[pltpu.VMEM((hp, tq, LANE), F32), pltpu.VMEM((hp, tq, HEAD_DIM), F32)],
        compiler_params=_cparams("parallel", "parallel", "parallel"),
        name="attn_c_prompt",
    )(proj3, proj3, proj3)


def _attn_c_step_kernel(q_ref, kc_ref, vc_ref, kn_ref, vn_ref, o_ref, run_s, acc_s, *, nkc, t, tkc, sub):
    kb = pl.program_id(1)
    tri_ones = _tri_ones()

    def q_head(h):
        return (q_ref[0, :, h * HEAD_DIM:(h + 1) * HEAD_DIM] * SCALE).astype(BF16)

    @pl.when(kb == 0)
    def _():
        rc = lax.broadcasted_iota(I32, (t, LANE), 0) - lax.broadcasted_iota(I32, (t, LANE), 1)
        heads = [slice(h * HEAD_DIM, (h + 1) * HEAD_DIM) for h in range(H_C)]
        zs = [_dot_nt(q_head(h), _pad_keys(kn_ref[0, :, heads[h]], LANE).astype(BF16)) for h in range(H_C)]
        scored = [_stick_scores(z, rc, tri_ones) for z in zs]
        for h in range(H_C):
            w_all, run = _stick_weights(scored[h], jnp.zeros((t, LANE), F32))
            acc_s[h] = _dot(w_all, _pad_keys(vn_ref[0, :, heads[h]], LANE).astype(BF16))
            run_s[h] = run

    @pl.when(kb > 0)
    def _():
        def body(i, carry):
            row0 = (tkc - sub - i * sub) * H_C
            zs = [_dot_nt(q_head(h), kc_ref[0, pl.ds(row0 + h, sub, stride=H_C), :].astype(BF16))
                  for h in range(H_C)]
            scored = [_stick_scores(z, None, tri_ones) for z in zs]
            for h in range(H_C):
                w_all, run = _stick_weights(scored[h], run_s[h])
                v = vc_ref[0, pl.ds(row0 + h, sub, stride=H_C), :].astype(BF16)
                acc_s[h] += _dot(w_all, v)
                run_s[h] = run
            return carry

        lax.fori_loop(0, tkc // sub, body, 0)

    @pl.when(kb == nkc)
    def _():
        for h in range(H_C):
            o_ref[0, :, h * HEAD_DIM:(h + 1) * HEAD_DIM] = acc_s[h].astype(o_ref.dtype)


def _attn_c_step(proj3, cache_k, cache_v, *, tkc=1024, sub=1024):
    b, t, _ = proj3.shape
    p = cache_k.shape[1] // H_C
    assert p % tkc == 0 and t <= LANE, (p, tkc, t)
    nkc = p // tkc
    w = H_C * HEAD_DIM

    def cache_idx(i, j):
        return (i, jnp.clip(nkc - j, 0, nkc - 1), 0)

    return pl.pallas_call(
        functools.partial(_attn_c_step_kernel, nkc=nkc, t=t, tkc=tkc, sub=sub),
        grid=(b, nkc + 1),
        in_specs=[
            pl.BlockSpec((1, t, w), lambda i, j: (i, 0, OD_CQ // w)),
            pl.BlockSpec((1, tkc * H_C, HEAD_DIM), cache_idx),
            pl.BlockSpec((1, tkc * H_C, HEAD_DIM), cache_idx),
            pl.BlockSpec((1, t, w), lambda i, j: (i, 0, OD_CK // w)),
            pl.BlockSpec((1, t, w), lambda i, j: (i, 0, OD_CV // w)),
        ],
        out_specs=pl.BlockSpec((1, t, w), lambda i, j: (i, 0, 0)),
        out_shape=jax.ShapeDtypeStruct((b, t, w), BF16),
        scratch_shapes=[pltpu.VMEM((H_C, t, LANE), F32), pltpu.VMEM((H_C, t, HEAD_DIM), F32)],
        compiler_params=_cparams("parallel", "arbitrary"),
        name="attn_c_step",
    )(proj3, cache_k, cache_v, proj3, proj3)


D_HEADS_PER_STEP = 4


def _attn_d_kernel(q_ref, kp_ref, vp_ref, ko_ref, vo_ref, bias_ref, o_ref, *, tq):
    qi = pl.program_id(2)
    heads = [slice(h * HEAD_DIM, (h + 1) * HEAD_DIM) for h in range(D_HEADS_PER_STEP)]
    qs = [(q_ref[0, :, sl] * SCALE).astype(BF16) for sl in heads]

    def own_logits(h):
        return _dot_nt(qs[h], ko_ref[0, :, heads[h]].astype(BF16)) + bias_ref[h, :, D_BAND_LEFT:D_BAND_LEFT + tq]

    def with_left():
        s_o = [own_logits(h) for h in range(D_HEADS_PER_STEP)]
        s_p = [_dot_nt(qs[h], kp_ref[0, :, heads[h]].astype(BF16)) + bias_ref[h, :, 0:D_BAND_LEFT]
               for h in range(D_HEADS_PER_STEP)]
        probs = []
        for h in range(D_HEADS_PER_STEP):
            m = jnp.maximum(jnp.max(s_p[h], axis=-1, keepdims=True), jnp.max(s_o[h], axis=-1, keepdims=True))
            p_p = jnp.exp(s_p[h] - m)
            p_o = jnp.exp(s_o[h] - m)
            l = jnp.sum(p_p, axis=-1, keepdims=True) + jnp.sum(p_o, axis=-1, keepdims=True)
            probs.append((p_p.astype(BF16), p_o.astype(BF16), l))
        for h, sl in enumerate(heads):
            p_p, p_o, l = probs[h]
            out = _dot(p_p, vp_ref[0, :, sl].astype(BF16)) + _dot(p_o, vo_ref[0, :, sl].astype(BF16))
            o_ref[0, :, sl] = (out / l).astype(o_ref.dtype)

    def own_only():
        s_o = [own_logits(h) for h in range(D_HEADS_PER_STEP)]
        for h, sl in enumerate(heads):
            p_o = jnp.exp(s_o[h] - jnp.max(s_o[h], axis=-1, keepdims=True))
            l = jnp.sum(p_o, axis=-1, keepdims=True)
            o_ref[0, :, sl] = (_dot(p_o.astype(BF16), vo_ref[0, :, sl].astype(BF16)) / l).astype(o_ref.dtype)

    pl.when(qi > 0)(with_left)
    pl.when(qi == 0)(own_only)


def _attn_d(proj3, bias):
    b, t, _ = proj3.shape
    tq = D_BAND_LEFT
    nq = t // tq
    hp = D_HEADS_PER_STEP
    w = hp * HEAD_DIM
    qc, kc, vc = OD_DQ // w, OD_DK // w, OD_DV // w
    return pl.pallas_call(
        functools.partial(_attn_d_kernel, tq=tq),
        grid=(b, H_D // hp, nq),
        in_specs=[
            pl.BlockSpec((1, tq, w), lambda i, h, j: (i, j, qc + h)),
            pl.BlockSpec((1, tq, w), lambda i, h, j: (i, jnp.maximum(j - 1, 0), kc + h)),
            pl.BlockSpec((1, tq, w), lambda i, h, j: (i, jnp.maximum(j - 1, 0), vc + h)),
            pl.BlockSpec((1, tq, w), lambda i, h, j: (i, j, kc + h)),
            pl.BlockSpec((1, tq, w), lambda i, h, j: (i, j, vc + h)),
            pl.BlockSpec((hp, tq, 2 * tq), lambda i, h, j: (h, 0, 0)),
        ],
        out_specs=pl.BlockSpec((1, tq, w), lambda i, h, j: (i, j, h)),
        out_shape=jax.ShapeDtypeStruct((b, t, H_D * HEAD_DIM), BF16),
        compiler_params=_cparams("parallel", "parallel", "parallel"),
        name="attn_d",
    )(proj3, proj3, proj3, proj3, proj3, bias)


def _attn_d_step_kernel(q_ref, kp_ref, vp_ref, kn_ref, vn_ref, bias_ref, o_ref, *, t):
    heads = [slice(h * HEAD_DIM, (h + 1) * HEAD_DIM) for h in range(H_D)]
    logits = []
    for h, sl in enumerate(heads):
        q = (q_ref[0, :, sl] * SCALE).astype(BF16)
        kp = kp_ref[0, pl.ds(h, D_BAND_LEFT, stride=H_D), :].astype(BF16)
        s_p = _dot_nt(q, kp) + bias_ref[h, :, 0:D_BAND_LEFT]
        s_o = _dot_nt(q, kn_ref[0, :, sl].astype(BF16)) + bias_ref[h, :, D_BAND_LEFT:D_BAND_LEFT + t]
        logits.append((s_p, s_o))
    probs = []
    for s_p, s_o in logits:
        m = jnp.maximum(jnp.max(s_p, axis=-1, keepdims=True), jnp.max(s_o, axis=-1, keepdims=True))
        p_p = jnp.exp(s_p - m)
        p_o = jnp.exp(s_o - m)
        l = jnp.sum(p_p, axis=-1, keepdims=True) + jnp.sum(p_o, axis=-1, keepdims=True)
        probs.append((p_p.astype(BF16), p_o.astype(BF16), l))
    for h, sl in enumerate(heads):
        p_p, p_o, l = probs[h]
        vp = vp_ref[0, pl.ds(h, D_BAND_LEFT, stride=H_D), :].astype(BF16)
        out = _dot(p_p, vp) + _dot(p_o, vn_ref[0, :, sl].astype(BF16))
        o_ref[0, :, sl] = (out / l).astype(o_ref.dtype)


def _attn_d_step(proj3, cache_k, cache_v, bias):
    b, t, _ = proj3.shape
    w = H_D * HEAD_DIM
    return pl.pallas_call(
        functools.partial(_attn_d_step_kernel, t=t),
        grid=(b,),
        in_specs=[
            pl.BlockSpec((1, t, w), lambda i: (i, 0, OD_DQ // w)),
            pl.BlockSpec((1, D_BAND_LEFT * H_D, HEAD_DIM), lambda i: (i, 0, 0)),
            pl.BlockSpec((1, D_BAND_LEFT * H_D, HEAD_DIM), lambda i: (i, 0, 0)),
            pl.BlockSpec((1, t, w), lambda i: (i, 0, OD_DK // w)),
            pl.BlockSpec((1, t, w), lambda i: (i, 0, OD_DV // w)),
            pl.BlockSpec((H_D, t, D_BAND_LEFT + t), lambda i: (0, 0, 0)),
        ],
        out_specs=pl.BlockSpec((1, t, w), lambda i: (i, 0, 0)),
        out_shape=jax.ShapeDtypeStruct((b, t, w), BF16),
        compiler_params=_cparams("parallel"),
        name="attn_d_step",
    )(proj3, cache_k, cache_v, proj3, proj3, bias)


def _head_major_kernel(x_ref, o_ref, *, n_heads, rows):
    for h in range(n_heads):
        o_ref[0, pl.ds(h, rows, stride=n_heads), :] = x_ref[0, :, h * HEAD_DIM:(h + 1) * HEAD_DIM]


def _head_major(x3, col0, n_heads, row0=0):
    b, t, _ = x3.shape
    n_rows = t - row0
    tq = min(512, n_rows)
    w = n_heads * HEAD_DIM
    assert col0 % w == 0 and row0 % tq == 0 and n_rows % tq == 0, (col0, row0, n_rows)
    flat = pl.pallas_call(
        functools.partial(_head_major_kernel, n_heads=n_heads, rows=tq),
        grid=(b, n_rows // tq),
        in_specs=[pl.BlockSpec((1, tq, w), lambda i, j: (i, row0 // tq + j, col0 // w))],
        out_specs=pl.BlockSpec((1, tq * n_heads, HEAD_DIM), lambda i, j: (i, j, 0)),
        out_shape=jax.ShapeDtypeStruct((b, n_rows * n_heads, HEAD_DIM), x3.dtype),
        compiler_params=_cparams("parallel", "parallel"),
        name="head_major",
    )(x3)
    return flat.reshape(b, n_rows, n_heads, HEAD_DIM)


def _pad_rows(x, rows):
    return jnp.pad(x, ((0, 0), (0, rows - x.shape[1]), (0, 0)))


def _even_mixer(h2, b, t, past, w, tz_prompt, tz_step, tq_a, mm):
    proj = mm("in_even", h2, w["w_in"], mode="headnorm", gain=w["gain"], flag=w["flag"], n_norm_tiles=9,
              name="proj_even")
    proj3 = proj.reshape(b, t, EV_PAD)
    ak = proj3[:, :, EV_AK:EV_AK + KV_A * HEAD_DIM]
    av = proj3[:, :, EV_AV:EV_AV + KV_A * HEAD_DIM]
    ik = proj3[:, :, EV_MISC + MISC_IK:EV_MISC + MISC_IK + IDX_DIM]
    bk = proj3[:, :, EV_BK:EV_BK + H_B * HEAD_DIM]
    bv = proj3[:, :, EV_BV:EV_BV + H_B * HEAD_DIM]
    raw_h = proj3[:, :, EV_MISC + MISC_BF:EV_MISC + MISC_BF + H_B].transpose(0, 2, 1)
    iq_col = EV_IQ // (IDX_HEADS * IDX_DIM)
    misc_col = EV_MISC // LANE
    if past is None:
        logf_h, cum_h = _logf_cum(raw_h, w["f_bias"], None)
        mask = _select_mask(proj3, iq_col, proj3, misc_col, proj3, misc_col, t=t, tq=tq_a, lp=t, l_valid=t,
                            q_base=0, n_sel=min(TOPK_MAX, t // 4), tk_out=tq_a, causal_prefix=True)
        out_a = _attn_a_prompt(proj3, mask, tz_prompt, tq=tq_a)
        out_b = _attn_b_prompt(proj3, cum_h, tq=min(512, t))
    else:
        p_ak, p_av, p_ik, p_bk, p_bv, p_lf = past
        p = p_ak.shape[1]
        l_valid = p + t
        lp = p + LANE
        assert p % LANE == 0 and t <= LANE, (p, t)
        logf_h, cum_h = _logf_cum(raw_h, w["f_bias"], p_lf.transpose(0, 2, 1))
        ik_all = _pad_rows(jnp.concatenate([p_ik, ik], axis=1), lp)
        mask = _select_mask(proj3, iq_col, proj3, misc_col, ik_all, None, t=t, tq=t, lp=lp, l_valid=l_valid,
                            q_base=p, n_sel=min(TOPK_MAX, l_valid // 4), tk_out=lp, causal_prefix=False)
        out_a = _attn_a_step(proj3, p_ak.reshape(b, p * KV_A, HEAD_DIM), p_av.reshape(b, p * KV_A, HEAD_DIM),
                             mask, tz_step[:, :t])
        out_b = _attn_b_step(proj3, p_bk.reshape(b, p * H_B, HEAD_DIM), p_bv.reshape(b, p * H_B, HEAD_DIM), cum_h)
    mixed = [out_a.reshape(b * t, -1), out_b.reshape(b * t, -1)]
    if past is None:
        state = (_head_major(proj3, EV_AK, KV_A), _head_major(proj3, EV_AV, KV_A), ik,
                 _head_major(proj3, EV_BK, H_B), _head_major(proj3, EV_BV, H_B), logf_h.transpose(0, 2, 1))
    else:
        state = (ak.reshape(b, t, KV_A, HEAD_DIM), av.reshape(b, t, KV_A, HEAD_DIM), ik,
                 bk.reshape(b, t, H_B, HEAD_DIM), bv.reshape(b, t, H_B, HEAD_DIM), logf_h.transpose(0, 2, 1))
    return mixed, state


def _odd_mixer(h2, b, t, past, w, band, mm):
    proj = mm("in_odd", h2, w["w_in"], col_block=_odd_col_block, mode="headnorm", gain=w["gain"], flag=w["flag"],
              n_norm_tiles=4, name="proj_odd")
    proj3 = proj.reshape(b, t, OD_WIDTH)
    if past is None:
        out_c = _attn_c_prompt(proj3, tq=min(512, t))
        out_d = _attn_d(proj3, band)
        keep = min(D_BAND_LEFT, t)
        state = (_head_major(proj3, OD_CK, H_C), _head_major(proj3, OD_CV, H_C),
                 _head_major(proj3, OD_DK, H_D, row0=t - keep), _head_major(proj3, OD_DV, H_D, row0=t - keep))
    else:
        p_ck, p_cv, p_dk, p_dv = past
        p = p_ck.shape[1]
        out_c = _attn_c_step(proj3, p_ck.reshape(b, p * H_C, HEAD_DIM), p_cv.reshape(b, p * H_C, HEAD_DIM))
        wd = p_dk.shape[1]
        assert wd == D_BAND_LEFT and p % CHUNK == 0 and t <= CHUNK, (wd, p, t)
        out_d = _attn_d_step(proj3, p_dk.reshape(b, wd * H_D, HEAD_DIM), p_dv.reshape(b, wd * H_D, HEAD_DIM),
                             band[:, :t, :D_BAND_LEFT + t])
        state = tuple(proj3[:, :, c0:c0 + H_C * HEAD_DIM].reshape(b, t, H_C, HEAD_DIM)
                      for c0 in (OD_CK, OD_CV, OD_DK, OD_DV))
    mixed = [out_c.reshape(b * t, -1), out_d.reshape(b * t, -1)]
    return mixed, state


def _odd_col_block(j):
    return jnp.where(j < 4, j + 6, jnp.where(j < 10, j - 4, j))


def _trunk(x, caches_even, caches_odd, wts, bf16_weights):
    def mm(key, a, w_f32, col_block=None, **kw):
        if key in bf16_weights:
            w_bf16 = bf16_weights[key]
            if kw.get("tn", 512) == 512 and w_bf16.shape[1] % 1024 == 0:
                kw = dict(kw, tn=1024, n_norm_tiles=(kw.get("n_norm_tiles", 0) + 1) // 2)
            return _matmul(a, w_bf16, **kw)
        out, bf16_weights[key] = _matmul_cast(a, w_f32, col_block=col_block, **kw)
        return out

    b, t, d = x.shape
    x2 = x.reshape(b * t, d)
    h2 = _rmsnorm(x2, wts["attn_norm"][0])
    mixed, st_even = _even_mixer(h2, b, t, caches_even, wts["even"], wts["tz_prompt"], wts["tz_step"],
                                 wts["tq_a"], mm)
    x2 = mm("out_even", mixed, wts["even"]["w_out"], mode="residual", residual=x2, name="out_even")
    h2 = _rmsnorm(x2, wts["mlp_norm"][0])
    u = mm("up0", h2, wts["w_up"][0], mode="relu2", out_dtype=BF16, tn=1024, name="mlp_up0")
    x2 = _matmul(u, wts["w_down"][0], mode="residual", residual=x2, tn=256, tk=u.shape[1], name="mlp_down0")
    h2 = _rmsnorm(x2, wts["attn_norm"][1])
    mixed, st_odd = _odd_mixer(h2, b, t, caches_odd, wts["odd"], wts["band"], mm)
    x2 = mm("out_odd", mixed, wts["odd"]["w_out"], mode="residual", residual=x2, name="out_odd")
    h2 = _rmsnorm(x2, wts["mlp_norm"][1])
    u = mm("up1", h2, wts["w_up"][1], mode="relu2", out_dtype=BF16, tn=1024, name="mlp_up1")
    x2 = _matmul(u, wts["w_down"][1], mode="residual", residual=x2, tn=256, tk=u.shape[1], name="mlp_down1")
    return x2.reshape(b, t, d), tuple(s[None] for s in st_even), tuple(s[None] for s in st_odd)


def _even_weights(w_in, w_out, a_qn, a_kn, b_qn, b_kn, f_bias):
    aq, ak, av, iq, ik, iw, bq, bk, bv, bf = jnp.split(
        w_in, [1024, 1280, 1536, 2048, 2112, 2120, 3144, 4168, 5192], axis=1)
    d = w_in.shape[0]
    pad = jnp.zeros((d, EV_PAD - EV_WIDTH + LANE - (IDX_DIM + IDX_HEADS + H_B)), w_in.dtype)
    w_r = jnp.concatenate([aq, bq, bk, bv, ak, av, iq, ik, iw, bf, pad], axis=1)
    ones = jnp.ones((EV_PAD,), F32)
    gain = ones.at[EV_AQ:EV_AQ + 1024].set(jnp.tile(a_qn, H_A))
    gain = gain.at[EV_BQ:EV_BQ + 1024].set(jnp.tile(b_qn, H_B))
    gain = gain.at[EV_BK:EV_BK + 1024].set(jnp.tile(b_kn, H_B))
    gain = gain.at[EV_AK:EV_AK + 256].set(jnp.tile(a_kn, KV_A))
    flag = jnp.zeros((EV_PAD,), F32).at[0:EV_BV].set(1.0).at[EV_AK:EV_AV].set(1.0)
    return {"w_in": w_r, "gain": gain.reshape(1, -1), "flag": flag.reshape(1, -1), "w_out": w_out,
            "f_bias": f_bias}


def _odd_weights(w_in, w_out, d_qn, d_kn):
    gain = jnp.ones((OD_WIDTH,), F32)
    gain = gain.at[OD_DQ:OD_DQ + 1024].set(jnp.tile(d_qn, H_D)).at[OD_DK:OD_DK + 1024].set(jnp.tile(d_kn, H_D))
    flag = jnp.zeros((OD_WIDTH,), F32).at[0:OD_CQ].set(1.0)
    return {"w_in": w_in, "gain": gain.reshape(1, -1), "flag": flag.reshape(1, -1), "w_out": w_out}


def kernel(x_prompt, x_sample, cache_a_k, cache_a_v, cache_a_kidx, cache_b_k, cache_b_v, cache_b_logf,
           cache_c_k, cache_c_v, cache_d_k, cache_d_v, attn_norm, mlp_norm, w_in_even, w_out_even,
           w_in_odd, w_out_odd, a_q_norm, a_k_norm, b_q_norm, b_k_norm, forget_bias, t5_bias,
           d_q_norm, d_k_norm, d_rel_bias, w_up, w_down):
    tq_a = min(256, x_prompt.shape[1])
    wts = {
        "attn_norm": attn_norm,
        "mlp_norm": mlp_norm,
        "even": _even_weights(w_in_even[0], w_out_even[0], a_q_norm[0], a_k_norm[0], b_q_norm[0], b_k_norm[0],
                              forget_bias[0]),
        "odd": _odd_weights(w_in_odd[0], w_out_odd[0], d_q_norm[0], d_k_norm[0]),
        "w_up": w_up,
        "w_down": w_down.astype(BF16),
        "tq_a": tq_a,
        "tz_prompt": _t5_strip(t5_bias, tq_a),
        "tz_step": _t5_strip(t5_bias, LANE),
        "band": _band_bias(d_rel_bias[0]),
    }
    bf16_weights = {}
    y_p, ev_p, od_p = _trunk(x_prompt, None, None, wts, bf16_weights)
    caches_even = (cache_a_k[0], cache_a_v[0], cache_a_kidx[0], cache_b_k[0], cache_b_v[0], cache_b_logf[0])
    caches_odd = (cache_c_k[0], cache_c_v[0], cache_d_k[0], cache_d_v[0])
    y_s, ev_s, od_s = _trunk(x_sample, caches_even, caches_odd, wts, bf16_weights)
    return (y_p, y_s) + ev_p + od_p + ev_s + od_s
```

```python
import functools

import numpy as np
import jax
import jax.numpy as jnp
from jax import lax
from jax.experimental import pallas as pl
from jax.experimental.pallas import tpu as pltpu

F32 = jnp.float32
BF16 = jnp.bfloat16
I32 = jnp.int32

HEAD_DIM = 128
CHUNK = 64
CHUNK_SHIFT = 6
H_A = 8
KV_A = 2
REP_A = H_A // KV_A
H_B = 8
H_C = 8
H_D = 8
IDX_HEADS = 8
IDX_DIM = 64
TOPK_MAX = 256
T5_BUCKETS = 32
D_LEFT_CHUNKS = 8
D_BAND_LEFT = D_LEFT_CHUNKS * CHUNK
D_REL_CLIP = 128
EPS = 1e-6
NEG_INF = -1e30
SCALE = HEAD_DIM ** -0.5
LANE = 128
MXU_WIDTH = 256
INT32_MIN = -(2 ** 31)
NEG_INF_KEY = int(np.float32(NEG_INF).view(np.int32)) ^ 0x7FFFFFFF

VMEM_LIMIT_BYTES = 56 * 1024 * 1024

EV_AQ, EV_BQ, EV_BK, EV_BV, EV_AK, EV_AV, EV_IQ, EV_MISC = 0, 1024, 2048, 3072, 4096, 4352, 4608, 5120
EV_WIDTH = 5248
EV_PAD = 5632
MISC_IK, MISC_IW, MISC_BF = 0, 64, 72
OD_DQ, OD_DK, OD_CQ, OD_CK, OD_CV, OD_DV = 0, 1024, 2048, 3072, 4096, 5120
OD_WIDTH = 6144


def _cparams(*sem):
    return pltpu.CompilerParams(dimension_semantics=sem, vmem_limit_bytes=VMEM_LIMIT_BYTES)


def _dot_nt(a, b):
    return lax.dot_general(a, b, (((1,), (1,)), ((), ())), preferred_element_type=F32)


def _dot(a, b):
    return jnp.dot(a, b, preferred_element_type=F32)


def _lane_blocks(x):
    return [x[..., j * LANE:(j + 1) * LANE] for j in range(x.shape[-1] // LANE)]


def _pad_keys(x, rows):
    return jnp.concatenate([x, jnp.zeros((rows - x.shape[0], x.shape[1]), x.dtype)], axis=0)


def _rmsnorm_kernel(x_ref, g_ref, o_ref):
    x = x_ref[...]
    ms = jnp.mean(x * x, axis=-1, keepdims=True)
    o_ref[...] = (x * lax.rsqrt(ms + EPS) * g_ref[...]).astype(o_ref.dtype)


def _rmsnorm(x2, gain):
    m, d = x2.shape
    tm = min(512, m)
    return pl.pallas_call(
        _rmsnorm_kernel,
        grid=(m // tm,),
        in_specs=[pl.BlockSpec((tm, d), lambda i: (i, 0)), pl.BlockSpec((1, d), lambda i: (0, 0))],
        out_specs=pl.BlockSpec((tm, d), lambda i: (i, 0)),
        out_shape=jax.ShapeDtypeStruct((m, d), BF16),
        compiler_params=_cparams("parallel"),
        name="rmsnorm",
    )(x2, gain.reshape(1, d))


def _mm_kernel(*refs, n_a, nk, mode, n_norm_tiles, cast_b):
    a_refs, b_ref = refs[:n_a], refs[n_a]
    refs = refs[n_a - 1:]
    if mode == "headnorm":
        gain_ref, flag_ref, o_ref = refs[2], refs[3], refs[4]
        rest = refs[5:]
    elif mode == "residual":
        res_ref, o_ref = refs[2], refs[3]
        rest = refs[4:]
    else:
        o_ref = refs[2]
        rest = refs[3:]
    n_axis = 0 if cast_b else 1
    if cast_b:
        wb_ref = rest[0]

        @pl.when(pl.program_id(1) == 0)
        def _():
            wb_ref[...] = b_ref[...].astype(BF16)

        b_ref = wb_ref

    tn = o_ref.shape[1]

    def product(c0=0, c1=tn):
        row, acc = 0, None
        for a_ref in a_refs:
            kp = a_ref.shape[1]
            part = _dot(a_ref[...], b_ref[row:row + kp, c0:c1])
            acc = part if acc is None else acc + part
            row += kp
        return acc

    def headnorm(acc, c0):
        for g in range(acc.shape[1] // HEAD_DIM):
            sl = slice(c0 + g * HEAD_DIM, c0 + (g + 1) * HEAD_DIM)
            blk = acc[:, g * HEAD_DIM:(g + 1) * HEAD_DIM]
            ms = jnp.mean(blk * blk, axis=-1, keepdims=True)
            normed = blk * lax.rsqrt(ms + EPS) * gain_ref[:, sl]
            o_ref[:, sl] = jnp.where(flag_ref[:, sl] > 0.0, normed, blk)

    def epilogue(acc, c0=0):
        sl = slice(c0, c0 + acc.shape[1])
        if mode == "headnorm":
            headnorm(acc, c0)
        elif mode == "relu2":
            r = jnp.maximum(acc, 0.0)
            o_ref[:, sl] = (r * r).astype(o_ref.dtype)
        elif mode == "residual":
            o_ref[:, sl] = res_ref[:, sl] + acc
        else:
            o_ref[:, sl] = acc.astype(o_ref.dtype)

    def chunked():
        starts = list(range(0, tn, MXU_WIDTH))
        accs = [product(c0, c0 + MXU_WIDTH) for c0 in starts]
        for c0, acc in zip(starts, accs):
            epilogue(acc, c0)

    if nk == 1 and mode == "headnorm":
        j = pl.program_id(n_axis)
        pl.when(j < n_norm_tiles)(chunked)

        @pl.when(j >= n_norm_tiles)
        def _():
            o_ref[...] = product()
    elif nk == 1:
        chunked()
    else:
        acc_ref = rest[0]
        k = pl.program_id(2)

        @pl.when(k == 0)
        def _():
            acc_ref[...] = jnp.zeros_like(acc_ref)

        acc_ref[...] += product()

        @pl.when(k == nk - 1)
        def _():
            epilogue(acc_ref[...])


def _matmul(a, b, *, mode="plain", out_dtype=F32, gain=None, flag=None, n_norm_tiles=0, residual=None,
            tm=1024, tn=512, tk=2048, layer=None, name="matmul"):
    a_parts = list(a) if isinstance(a, (list, tuple)) else [a]
    m = a_parts[0].shape[0]
    kdim, n = b.shape[-2:]
    tm = tm if m % tm == 0 else min(512, m)
    if len(a_parts) > 1:
        tk = kdim
        in_specs = [pl.BlockSpec((tm, part.shape[1]), lambda i, j, k: (i, 0)) for part in a_parts]
    else:
        in_specs = [pl.BlockSpec((tm, tk), lambda i, j, k: (i, k))]
    nk = kdim // tk
    if layer is None:
        in_specs.append(pl.BlockSpec((tk, tn), lambda i, j, k: (k, j)))
    else:
        in_specs.append(pl.BlockSpec((None, tk, tn), lambda i, j, k: (layer, k, j)))
    args = a_parts + [b]
    if mode == "headnorm":
        in_specs += [pl.BlockSpec((1, tn), lambda i, j, k: (0, j)), pl.BlockSpec((1, tn), lambda i, j, k: (0, j))]
        args += [gain, flag]
    elif mode == "residual":
        in_specs += [pl.BlockSpec((tm, tn), lambda i, j, k: (i, j))]
        args += [residual]
    scratch = [pltpu.VMEM((tm, tn), F32)] if nk > 1 else []
    return pl.pallas_call(
        functools.partial(_mm_kernel, n_a=len(a_parts), nk=nk, mode=mode, n_norm_tiles=n_norm_tiles,
                          cast_b=False),
        grid=(m // tm, n // tn, nk),
        in_specs=in_specs,
        out_specs=pl.BlockSpec((tm, tn), lambda i, j, k: (i, j)),
        out_shape=jax.ShapeDtypeStruct((m, n), out_dtype),
        scratch_shapes=scratch,
        compiler_params=_cparams("parallel", "parallel", "arbitrary"),
        name=name,
    )(*args)


def _matmul_cast(a, b_f32, *, col_block=None, n=None, mode="plain", out_dtype=F32, gain=None, flag=None,
                 n_norm_tiles=0, residual=None, tm=1024, tn=512, layer=None, name="matmul"):
    a_parts = list(a) if isinstance(a, (list, tuple)) else [a]
    m = a_parts[0].shape[0]
    kdim = b_f32.shape[-2]
    n = b_f32.shape[-1] if n is None else n
    tm = tm if m % tm == 0 else min(512, m)
    src = (lambda j: j) if col_block is None else col_block
    in_specs = [pl.BlockSpec((tm, part.shape[1]), lambda j, i: (i, 0)) for part in a_parts]
    if layer is None:
        in_specs.append(pl.BlockSpec((kdim, tn), lambda j, i: (0, src(j))))
    else:
        in_specs.append(pl.BlockSpec((None, kdim, tn), lambda j, i: (layer, 0, src(j))))
    args = a_parts + [b_f32]
    if mode == "headnorm":
        in_specs += [pl.BlockSpec((1, tn), lambda j, i: (0, j)), pl.BlockSpec((1, tn), lambda j, i: (0, j))]
        args += [gain, flag]
    elif mode == "residual":
        in_specs += [pl.BlockSpec((tm, tn), lambda j, i: (i, j))]
        args += [residual]
    return pl.pallas_call(
        functools.partial(_mm_kernel, n_a=len(a_parts), nk=1, mode=mode, n_norm_tiles=n_norm_tiles, cast_b=True),
        grid=(n // tn, m // tm),
        in_specs=in_specs,
        out_specs=[pl.BlockSpec((tm, tn), lambda j, i: (i, j)), pl.BlockSpec((kdim, tn), lambda j, i: (0, j))],
        out_shape=[jax.ShapeDtypeStruct((m, n), out_dtype), jax.ShapeDtypeStruct((kdim, n), BF16)],
        compiler_params=_cparams("parallel", "arbitrary"),
        name=name,
    )(*args)


def _cumsum_block(x, carry):
    n = x.shape[1]
    r = lax.broadcasted_iota(I32, (n, n), 0)
    c = lax.broadcasted_iota(I32, (n, n), 1)
    tri = jnp.where(r <= c, 1.0, 0.0).astype(BF16)
    hi = x.astype(BF16)
    r1 = x - hi.astype(F32)
    mid = r1.astype(BF16)
    lo = (r1 - mid.astype(F32)).astype(BF16)
    cs = _dot(hi, tri) + _dot(mid, tri) + _dot(lo, tri) + carry
    return cs, cs[:, n - 1:n]


def _logf_cum_kernel(*refs, p_len, t_len, blk):
    if p_len:
        raw_ref, fb_ref, past_ref, lf_ref, cum_ref = refs
    else:
        raw_ref, fb_ref, lf_ref, cum_ref = refs
    x = raw_ref[0] + fb_ref[...]
    lf = jnp.minimum(x, 0.0) - jnp.log1p(jnp.exp(-jnp.abs(x)))
    lf_ref[0] = lf
    carry = jnp.zeros((H_B, 1), F32)
    for s in range(0, p_len, blk):
        cs, carry = _cumsum_block(past_ref[0, :, s:s + blk], carry)
        cum_ref[0, :, s:s + blk] = cs
    nb = min(blk, t_len)
    for s in range(0, t_len, nb):
        cs, carry = _cumsum_block(lf[:, s:s + nb], carry)
        cum_ref[0, :, p_len + s:p_len + s + nb] = cs


def _logf_cum(raw_h, fbias, past_h):
    b, h, t = raw_h.shape
    p = 0 if past_h is None else past_h.shape[2]
    in_specs = [pl.BlockSpec((1, h, t), lambda i: (i, 0, 0)), pl.BlockSpec((h, 1), lambda i: (0, 0))]
    args = [raw_h, fbias.reshape(h, 1)]
    if p:
        in_specs.append(pl.BlockSpec((1, h, p), lambda i: (i, 0, 0)))
        args.append(past_h)
    return pl.pallas_call(
        functools.partial(_logf_cum_kernel, p_len=p, t_len=t, blk=512),
        grid=(b,),
        in_specs=in_specs,
        out_specs=[pl.BlockSpec((1, h, t), lambda i: (i, 0, 0)), pl.BlockSpec((1, h, p + t), lambda i: (i, 0, 0))],
        out_shape=[jax.ShapeDtypeStruct((b, h, t), F32), jax.ShapeDtypeStruct((b, h, p + t), F32)],
        compiler_params=_cparams("parallel"),
        name="logf_cum",
    )(*args)


def _toeplitz(row_vals, rows, cols):
    w = row_vals.shape[1]
    full = jnp.broadcast_to(row_vals[0:1, :], (rows, w))
    return pltpu.roll(full, 0, 1, stride=1, stride_axis=0)[:, :cols]


def _t5_strip_kernel(tab_ref, o_ref, *, tile):
    h = pl.program_id(0)
    width = 4 * tile
    m = lax.broadcasted_iota(I32, (8, width), 1)
    d = jnp.where(m < width // 2, m, m - width)
    rel = d - tile
    n = jnp.abs(rel)
    large = jnp.full_like(n, 8)
    for thr in (12, 16, 23, 32, 46, 64, 91):
        large = large + jnp.where(n >= thr, 1, 0)
    bucket = jnp.where(rel > 0, T5_BUCKETS // 2, 0) + jnp.where(n < 8, n, large)
    vals = jnp.zeros((8, width), F32)
    for bkt in range(T5_BUCKETS):
        vals = jnp.where(bucket == bkt, tab_ref[bkt, h], vals)
    vals = vals - tab_ref[T5_BUCKETS // 2 - 1, h]
    o_ref[0] = _toeplitz(vals, tile, 2 * tile)


def _t5_strip(t5_table, tile):
    return pl.pallas_call(
        functools.partial(_t5_strip_kernel, tile=tile),
        grid=(H_A,),
        in_specs=[pl.BlockSpec(memory_space=pltpu.SMEM)],
        out_specs=pl.BlockSpec((1, tile, 2 * tile), lambda h: (h, 0, 0)),
        out_shape=jax.ShapeDtypeStruct((H_A, tile, 2 * tile), F32),
        compiler_params=_cparams("parallel"),
        name="t5_strip",
    )(t5_table)


def _band_bias_kernel(tab_ref, o_ref, *, rows, cols, width):
    h = pl.program_id(0)
    m = lax.broadcasted_iota(I32, (8, width), 1)
    d = jnp.where(m < width // 2, m, m - width)
    idx = jnp.clip(d - D_BAND_LEFT, -D_REL_CLIP, D_REL_CLIP) + D_REL_CLIP
    vals = jnp.zeros((8, width), F32)
    for r in range(2 * D_REL_CLIP + 1):
        vals = jnp.where(idx == r, tab_ref[r, h], vals)
    vals = vals - tab_ref[0, h]
    i_chunk = lax.broadcasted_iota(I32, (rows, cols), 0) >> CHUNK_SHIFT
    c_chunk = lax.broadcasted_iota(I32, (rows, cols), 1) >> CHUNK_SHIFT
    visible = (c_chunk >= i_chunk) & (c_chunk <= i_chunk + D_LEFT_CHUNKS)
    o_ref[0] = jnp.where(visible, _toeplitz(vals, rows, cols), NEG_INF)


def _band_bias(rel_table):
    rows, cols, width = D_BAND_LEFT, 2 * D_BAND_LEFT, 4 * D_BAND_LEFT
    return pl.pallas_call(
        functools.partial(_band_bias_kernel, rows=rows, cols=cols, width=width),
        grid=(H_D,),
        in_specs=[pl.BlockSpec(memory_space=pltpu.SMEM)],
        out_specs=pl.BlockSpec((1, rows, cols), lambda h: (h, 0, 0)),
        out_shape=jax.ShapeDtypeStruct((H_D, rows, cols), F32),
        compiler_params=_cparams("parallel"),
        name="band_bias",
    )(rel_table)


def _select_kernel(*refs, n_g, sb, tq, nq, lp, l_valid, q_base, n_sel, tk_out, causal_prefix):
    iq_refs, misc_refs = refs[:n_g], refs[n_g:2 * n_g]
    kk_ref, o_ref, key_ref = refs[2 * n_g:]
    step = pl.program_id(1)
    rows = sb * tq
    kf = float(n_sel)
    gs = range(n_g)

    def count(pred):
        return jnp.sum(jnp.where(pred, 1.0, 0.0), axis=-1, keepdims=True)

    def run(tiles):
        pfx = [p for _, p in tiles]
        adm = []
        for g, (q_tile, p) in enumerate(tiles):
            scores = []
            for s in range(sb):
                iq = iq_refs[g][s]
                iw = misc_refs[g][s][:, MISC_IW:MISC_IW + IDX_HEADS]
                ik = kk_ref[s, 0:p, 0:IDX_DIM].astype(BF16)
                score = jnp.zeros((tq, p), F32)
                for h in range(IDX_HEADS):
                    dots = _dot_nt(iq[:, h * IDX_DIM:(h + 1) * IDX_DIM].astype(BF16), ik)
                    score = score + iw[:, h:h + 1] * jnp.maximum(dots, 0.0)
                scores.append(score)
            score = jnp.concatenate(scores, axis=0) if sb > 1 else scores[0]
            row_in_tile = jnp.concatenate([lax.broadcasted_iota(I32, (tq, 1), 0)] * sb, axis=0)
            q_pos = q_base + q_tile * tq + row_in_tile
            kp = lax.broadcasted_iota(I32, (1, p), 1)
            ok = ((kp >> CHUNK_SHIFT) <= (q_pos >> CHUNK_SHIFT)) & (kp < l_valid)
            bits = lax.bitcast_convert_type(jnp.where(ok, score, NEG_INF), I32)
            key_ref[g, :, 0:p] = jnp.where(bits < 0, bits ^ jnp.int32(0x7FFFFFFF), bits)
            adm.append(ok)

        def keys(g):
            return key_ref[g, :, 0:pfx[g]]

        t0 = tuple(jnp.where(count(keys(g) >= 0) >= kf, jnp.int32(0), jnp.int32(INT32_MIN)) for g in gs)

        def thr_body(i, ts):
            bit = jnp.int32(1) << (30 - i)
            return tuple(jnp.where(count(keys(g) >= ts[g] + bit) >= kf, ts[g] + bit, ts[g]) for g in gs)

        thr = lax.fori_loop(0, 31, thr_body, t0)
        need = [kf - count(keys(g) > thr[g]) for g in gs]
        tied = [(count(keys(g) == thr[g]) != need[g]) & (thr[g] > NEG_INF_KEY) for g in gs]
        any_tied = sum(jnp.max(jnp.where(tied[g], 1.0, 0.0)) for g in gs) > 0.5

        def all_tied():
            return tuple(jnp.where(keys(g) == thr[g], 1.0, 0.0) for g in gs)

        def lowest_tied():
            r = lax.broadcasted_iota(I32, (LANE, 2 * LANE), 0)
            c = lax.broadcasted_iota(I32, (LANE, 2 * LANE), 1)
            count_mat = jnp.where((r <= c) | (c >= LANE), 1.0, 0.0).astype(BF16)
            out = []
            for g, tied_mask in enumerate(all_tied()):
                seen = jnp.zeros((rows, LANE), F32)
                kept = []
                for blk in _lane_blocks(tied_mask):
                    both = _dot(blk.astype(BF16), count_mat)
                    kept.append(jnp.where(both[:, :LANE] + seen <= need[g], blk, 0.0))
                    seen = seen + both[:, LANE:]
                out.append(jnp.concatenate(kept, axis=1) if len(kept) > 1 else kept[0])
            return tuple(out)

        take_tied = lax.cond(any_tied, lowest_tied, all_tied)
        for g in gs:
            sel = ((keys(g) > thr[g]) | (take_tied[g] > 0.5)) & adm[g]
            madd = jnp.where(sel, 0.0, NEG_INF)
            for s in range(sb):
                for kt in range(lp // tk_out):
                    if (kt + 1) * tk_out <= pfx[g]:
                        o_ref[s, g, 0, kt] = madd[s * tq:(s + 1) * tq, kt * tk_out:(kt + 1) * tk_out]
                    else:
                        o_ref[s, g, 0, kt] = jnp.full((tq, tk_out), NEG_INF, F32)

    if not causal_prefix:
        run([(step, lp)] * n_g)
    elif n_g == 2:
        for j in range(nq // 2):
            pl.when(step == j)(functools.partial(run, [(j, (j + 1) * tq), (nq - 1 - j, (nq - j) * tq)]))
    else:
        for j in range(nq):
            pl.when(step == j)(functools.partial(run, [(j, (j + 1) * tq)]))


def _select_mask(iq_arr, iq_col, misc_arr, misc_col, kk_arr, kk_col, *, t, tq, lp, l_valid, q_base, n_sel,
                 tk_out, causal_prefix):
    b = iq_arr.shape[0]
    nq = t // tq
    wk = kk_arr.shape[2] if kk_col is None else LANE
    kcol = 0 if kk_col is None else kk_col
    sb = 4 if (nq == 1 and tq <= 64 and b % 4 == 0) else 1
    n_g = 2 if (causal_prefix and nq % 2 == 0) else 1
    steps = nq // n_g
    tile_of = [lambda j: j, lambda j: nq - 1 - j]
    nkt = lp // tk_out
    in_specs = [pl.BlockSpec((sb, tq, IDX_HEADS * IDX_DIM), lambda i, j, g=g: (i, tile_of[g](j), iq_col))
                for g in range(n_g)]
    in_specs += [pl.BlockSpec((sb, tq, LANE), lambda i, j, g=g: (i, tile_of[g](j), misc_col)) for g in range(n_g)]
    in_specs.append(pl.BlockSpec((sb, lp, wk), lambda i, j: (i, 0, kcol)))
    return pl.pallas_call(
        functools.partial(_select_kernel, n_g=n_g, sb=sb, tq=tq, nq=nq, lp=lp, l_valid=l_valid, q_base=q_base,
                          n_sel=n_sel, tk_out=tk_out, causal_prefix=causal_prefix),
        grid=(b // sb, steps),
        in_specs=in_specs,
        out_specs=pl.BlockSpec((sb, n_g, 1, nkt, tq, tk_out), lambda i, j: (i, 0, j, 0, 0, 0)),
        out_shape=jax.ShapeDtypeStruct((b, n_g, steps, nkt, tq, tk_out), F32),
        scratch_shapes=[pltpu.VMEM((n_g, sb * tq, lp), I32)],
        compiler_params=_cparams("parallel", "parallel"),
        name="a_select",
    )(*([iq_arr] * n_g), *([misc_arr] * n_g), kk_arr)


def _mask_tile_index(q_tile, nq, n_g):
    if n_g == 1:
        return 0, q_tile
    upper = q_tile >= nq // 2
    return jnp.where(upper, 1, 0), jnp.where(upper, nq - 1 - q_tile, q_tile)


def _online_stats(s, m_s, l_s):
    blocks = _lane_blocks(s)
    mx = blocks[0]
    for blk in blocks[1:]:
        mx = jnp.maximum(mx, blk)
    m_prev = m_s[...]
    m_new = jnp.maximum(m_prev, jnp.max(mx, axis=-1, keepdims=True))
    alpha = jnp.exp(m_prev - m_new)
    ps = [jnp.exp(blk - m_new) for blk in blocks]
    psum = ps[0]
    for p in ps[1:]:
        psum = psum + p
    l_s[...] = alpha * l_s[...] + psum
    m_s[...] = m_new
    p_all = jnp.concatenate([p.astype(BF16) for p in ps], axis=1) if len(ps) > 1 else ps[0].astype(BF16)
    return alpha, p_all


def _online_init(m_s, l_s, acc_s):
    m_s[...] = jnp.full_like(m_s, NEG_INF)
    l_s[...] = jnp.zeros_like(l_s)
    acc_s[...] = jnp.zeros_like(acc_s)


def _online_result(l_s, acc_s):
    return acc_s[...] / jnp.sum(l_s[...], axis=-1, keepdims=True)


def _stack_heads(q):
    return jnp.concatenate([q[:, r * HEAD_DIM:(r + 1) * HEAD_DIM] for r in range(REP_A)], axis=0).astype(BF16)


def _attn_a_prompt_kernel(q_ref, k_ref, v_ref, mask_ref, tz_ref, o_ref, m_s, l_s, acc_s, *, tq):
    qi = pl.program_id(1)
    width = REP_A * HEAD_DIM
    qs = [_stack_heads(q_ref[0, :, g * width:(g + 1) * width] * SCALE) for g in range(KV_A)]
    _online_init(m_s, l_s, acc_s)

    def tile(kt, bias_cols):
        off = pl.multiple_of(kt * tq, tq)
        mask = mask_ref[0, 0, 0, kt][None]
        logits = []
        for g in range(KV_A):
            k = k_ref[0, pl.ds(off, tq), g * HEAD_DIM:(g + 1) * HEAD_DIM].astype(BF16)
            s = _dot_nt(qs[g], k).reshape(REP_A, tq, tq) + mask
            if bias_cols is not None:
                s = s + tz_ref[g * REP_A:(g + 1) * REP_A, :, bias_cols]
            logits.append(s.reshape(REP_A * tq, tq))
        stats = [_online_stats(logits[g], m_s.at[g], l_s.at[g]) for g in range(KV_A)]
        for g in range(KV_A):
            alpha, p_all = stats[g]
            v = v_ref[0, pl.ds(off, tq), g * HEAD_DIM:(g + 1) * HEAD_DIM].astype(BF16)
            acc_s[g] = alpha * acc_s[g] + _dot(p_all, v)

    def plain_body(kt, carry):
        tile(kt, None)
        return carry

    lax.fori_loop(0, jnp.maximum(qi - 1, 0), plain_body, 0)

    @pl.when(qi >= 1)
    def _():
        tile(qi - 1, slice(0, tq))

    tile(qi, slice(tq, 2 * tq))
    for g in range(KV_A):
        out = _online_result(l_s.at[g], acc_s.at[g])
        for r in range(REP_A):
            col = (g * REP_A + r) * HEAD_DIM
            o_ref[0, :, col:col + HEAD_DIM] = out[r * tq:(r + 1) * tq].astype(o_ref.dtype)


def _attn_a_prompt(proj3, mask, tz, *, tq):
    b, t, _ = proj3.shape
    nq = t // tq
    qw, kw = H_A * HEAD_DIM, KV_A * HEAD_DIM
    rows = REP_A * tq
    n_g = mask.shape[1]

    def mask_idx(i, j):
        grp, pos = _mask_tile_index(j, nq, n_g)
        return (i, grp, pos, 0, 0, 0)

    return pl.pallas_call(
        functools.partial(_attn_a_prompt_kernel, tq=tq),
        grid=(b, nq),
        in_specs=[
            pl.BlockSpec((1, tq, qw), lambda i, j: (i, j, EV_AQ // qw)),
            pl.BlockSpec((1, t, kw), lambda i, j: (i, 0, EV_AK // kw)),
            pl.BlockSpec((1, t, kw), lambda i, j: (i, 0, EV_AV // kw)),
            pl.BlockSpec((1, 1, 1, nq, tq, tq), mask_idx),
            pl.BlockSpec((H_A, tq, 2 * tq), lambda i, j: (0, 0, 0)),
        ],
        out_specs=pl.BlockSpec((1, tq, qw), lambda i, j: (i, j, 0)),
        out_shape=jax.ShapeDtypeStruct((b, t, qw), BF16),
        scratch_shapes=[pltpu.VMEM((KV_A, rows, LANE), F32), pltpu.VMEM((KV_A, rows, LANE), F32),
                        pltpu.VMEM((KV_A, rows, HEAD_DIM), F32)],
        compiler_params=_cparams("parallel", "parallel"),
        name="attn_a_prompt",
    )(proj3, proj3, proj3, mask, tz)


def _attn_a_step_kernel(q_ref, kc_ref, vc_ref, kn_ref, vn_ref, mask_ref, tz_ref, o_ref, *, t, p):
    width = REP_A * HEAD_DIM
    mask = mask_ref[0, 0, 0, 0]
    logits = []
    for g in range(KV_A):
        heads = slice(g * REP_A, (g + 1) * REP_A)
        qs = _stack_heads(q_ref[0, :, g * width:(g + 1) * width] * SCALE)
        kc = kc_ref[0, pl.ds(g, p, stride=KV_A), :].astype(BF16)
        kn = _pad_keys(kn_ref[0, :, g * HEAD_DIM:(g + 1) * HEAD_DIM], LANE).astype(BF16)
        s_c = _dot_nt(qs, kc).reshape(REP_A, t, p) + mask[None, :, 0:p]
        s_n = (_dot_nt(qs, kn).reshape(REP_A, t, LANE) + mask[None, :, p:p + LANE]
               + tz_ref[heads, :, LANE:2 * LANE])
        blocks = _lane_blocks(s_c)
        blocks[-1] = blocks[-1] + tz_ref[heads, :, 0:LANE]
        blocks.append(s_n)
        logits.append(blocks)
    probs = []
    for blocks in logits:
        mx = blocks[0]
        for blk in blocks[1:]:
            mx = jnp.maximum(mx, blk)
        m = jnp.max(mx, axis=-1, keepdims=True)
        ps = [jnp.exp(blk - m) for blk in blocks]
        psum = ps[0]
        for pb in ps[1:]:
            psum = psum + pb
        l = jnp.sum(psum, axis=-1, keepdims=True)
        p_c = jnp.concatenate([pb.astype(BF16) for pb in ps[:-1]], axis=-1).reshape(REP_A * t, p)
        probs.append((p_c, ps[-1].astype(BF16).reshape(REP_A * t, LANE), l))
    for g, (p_c, p_n, l) in enumerate(probs):
        vc = vc_ref[0, pl.ds(g, p, stride=KV_A), :].astype(BF16)
        vn = _pad_keys(vn_ref[0, :, g * HEAD_DIM:(g + 1) * HEAD_DIM], LANE).astype(BF16)
        out = (_dot(p_c, vc) + _dot(p_n, vn)).reshape(REP_A, t, HEAD_DIM) / l
        for r in range(REP_A):
            col = (g * REP_A + r) * HEAD_DIM
            o_ref[0, :, col:col + HEAD_DIM] = out[r].astype(o_ref.dtype)


def _attn_a_step(proj3, cache_k, cache_v, mask, tz):
    b, t, _ = proj3.shape
    p = cache_k.shape[1] // KV_A
    qw, kw = H_A * HEAD_DIM, KV_A * HEAD_DIM
    return pl.pallas_call(
        functools.partial(_attn_a_step_kernel, t=t, p=p),
        grid=(b,),
        in_specs=[
            pl.BlockSpec((1, t, qw), lambda i: (i, 0, EV_AQ // qw)),
            pl.BlockSpec((1, p * KV_A, HEAD_DIM), lambda i: (i, 0, 0)),
            pl.BlockSpec((1, p * KV_A, HEAD_DIM), lambda i: (i, 0, 0)),
            pl.BlockSpec((1, t, kw), lambda i: (i, 0, EV_AK // kw)),
            pl.BlockSpec((1, t, kw), lambda i: (i, 0, EV_AV // kw)),
            pl.BlockSpec((1, 1, 1, 1, t, p + LANE), lambda i: (i, 0, 0, 0, 0, 0)),
            pl.BlockSpec((H_A, t, 2 * LANE), lambda i: (0, 0, 0)),
        ],
        out_specs=pl.BlockSpec((1, t, qw), lambda i: (i, 0, 0)),
        out_shape=jax.ShapeDtypeStruct((b, t, qw), BF16),
        compiler_params=_cparams("parallel"),
        name="attn_a_step",
    )(proj3, cache_k, cache_v, proj3, proj3, mask, tz)


def _causal_mask(s, row0=0):
    r = row0 + lax.broadcasted_iota(I32, s.shape, 0)
    c = lax.broadcasted_iota(I32, s.shape, 1)
    return jnp.where(c <= r, s, NEG_INF)


B_HEADS_PER_STEP = 2


def _attn_b_prompt_kernel(q_ref, k_ref, v_ref, ck_ref, o_ref, m_s, l_s, acc_s, *, tq):
    qi = pl.program_id(2)
    heads = [slice(h * HEAD_DIM, (h + 1) * HEAD_DIM) for h in range(B_HEADS_PER_STEP)]
    qs = [(q_ref[0, :, sl] * SCALE).astype(BF16) for sl in heads]
    _online_init(m_s, l_s, acc_s)

    def tile(kt, diag):
        off = pl.multiple_of(kt * tq, tq)
        logits = []
        for h, sl in enumerate(heads):
            s = _dot_nt(qs[h], k_ref[0, pl.ds(off, tq), sl].astype(BF16)) - ck_ref[0, h, kt]
            logits.append(_causal_mask(s) if diag else s)
        stats = [_online_stats(logits[h], m_s.at[h], l_s.at[h]) for h in range(B_HEADS_PER_STEP)]
        for h, sl in enumerate(heads):
            alpha, p_all = stats[h]
            acc_s[h] = alpha * acc_s[h] + _dot(p_all, v_ref[0, pl.ds(off, tq), sl].astype(BF16))

    def body(kt, carry):
        tile(kt, False)
        return carry

    lax.fori_loop(0, qi, body, 0)
    tile(qi, True)
    for h, sl in enumerate(heads):
        o_ref[0, :, sl] = _online_result(l_s.at[h], acc_s.at[h]).astype(o_ref.dtype)


def _attn_b_prompt(proj3, cum_h, *, tq=512):
    b, t, _ = proj3.shape
    nq = t // tq
    ck = cum_h.reshape(b, H_B, nq, 1, tq)
    hp = B_HEADS_PER_STEP
    w = hp * HEAD_DIM
    return pl.pallas_call(
        functools.partial(_attn_b_prompt_kernel, tq=tq),
        grid=(b, H_B // hp, nq),
        in_specs=[
            pl.BlockSpec((1, tq, w), lambda i, h, j: (i, j, EV_BQ // w + h)),
            pl.BlockSpec((1, t, w), lambda i, h, j: (i, 0, EV_BK // w + h)),
            pl.BlockSpec((1, t, w), lambda i, h, j: (i, 0, EV_BV // w + h)),
            pl.BlockSpec((1, hp, nq, 1, tq), lambda i, h, j: (i, h, 0, 0, 0)),
        ],
        out_specs=pl.BlockSpec((1, tq, w), lambda i, h, j: (i, j, h)),
        out_shape=jax.ShapeDtypeStruct((b, t, H_B * HEAD_DIM), BF16),
        scratch_shapes=[pltpu.VMEM((hp, tq, LANE), F32), pltpu.VMEM((hp, tq, LANE), F32),
                        pltpu.VMEM((hp, tq, HEAD_DIM), F32)],
        compiler_params=_cparams("parallel", "parallel", "parallel"),
        name="attn_b_prompt",
    )(proj3, proj3, proj3, ck)


def _attn_b_step_kernel(q_ref, kc_ref, vc_ref, kn_ref, vn_ref, ckc_ref, ckn_ref, o_ref, m_s, l_s, acc_s,
                        *, nkc, t, tkc):
    kb = pl.program_id(1)

    @pl.when(kb == 0)
    def _():
        _online_init(m_s, l_s, acc_s)

    def q_head(h):
        return (q_ref[0, :, h * HEAD_DIM:(h + 1) * HEAD_DIM] * SCALE).astype(BF16)

    @pl.when(kb < nkc)
    def _():
        ck = ckc_ref[0, 0]
        logits = [_dot_nt(q_head(h), kc_ref[0, pl.ds(h, tkc, stride=H_B), :].astype(BF16)) - ck[h:h + 1, :]
                  for h in range(H_B)]
        stats = [_online_stats(logits[h], m_s.at[h], l_s.at[h]) for h in range(H_B)]
        for h in range(H_B):
            alpha, p_all = stats[h]
            v = vc_ref[0, pl.ds(h, tkc, stride=H_B), :].astype(BF16)
            acc_s[h] = alpha * acc_s[h] + _dot(p_all, v)

    @pl.when(kb == nkc)
    def _():
        ckn = ckn_ref[0]
        heads = [slice(h * HEAD_DIM, (h + 1) * HEAD_DIM) for h in range(H_B)]
        logits = [_causal_mask(_dot_nt(q_head(h), _pad_keys(kn_ref[0, :, heads[h]], LANE).astype(BF16))
                               - ckn[h:h + 1, :]) for h in range(H_B)]
        stats = [_online_stats(logits[h], m_s.at[h], l_s.at[h]) for h in range(H_B)]
        for h in range(H_B):
            alpha, p_all = stats[h]
            v = _pad_keys(vn_ref[0, :, heads[h]], LANE).astype(BF16)
            acc_s[h] = alpha * acc_s[h] + _dot(p_all, v)
            o_ref[0, :, heads[h]] = _online_result(l_s.at[h], acc_s.at[h]).astype(o_ref.dtype)


def _attn_b_step(proj3, cache_k, cache_v, cum_h, *, tkc=1024):
    b, t, _ = proj3.shape
    p = cache_k.shape[1] // H_B
    assert p % tkc == 0 and t <= LANE, (p, tkc, t)
    nkc = p // tkc
    w = H_B * HEAD_DIM
    ck_cache = cum_h[:, :, :p].reshape(b, H_B, nkc, tkc).transpose(0, 2, 1, 3)
    ck_new = jnp.pad(cum_h[:, :, p:], ((0, 0), (0, 0), (0, LANE - t)))
    last = nkc - 1
    return pl.pallas_call(
        functools.partial(_attn_b_step_kernel, nkc=nkc, t=t, tkc=tkc),
        grid=(b, nkc + 1),
        in_specs=[
            pl.BlockSpec((1, t, w), lambda i, j: (i, 0, EV_BQ // w)),
            pl.BlockSpec((1, tkc * H_B, HEAD_DIM), lambda i, j: (i, jnp.minimum(j, last), 0)),
            pl.BlockSpec((1, tkc * H_B, HEAD_DIM), lambda i, j: (i, jnp.minimum(j, last), 0)),
            pl.BlockSpec((1, t, w), lambda i, j: (i, 0, EV_BK // w)),
            pl.BlockSpec((1, t, w), lambda i, j: (i, 0, EV_BV // w)),
            pl.BlockSpec((1, 1, H_B, tkc), lambda i, j: (i, jnp.minimum(j, last), 0, 0)),
            pl.BlockSpec((1, H_B, LANE), lambda i, j: (i, 0, 0)),
        ],
        out_specs=pl.BlockSpec((1, t, w), lambda i, j: (i, 0, 0)),
        out_shape=jax.ShapeDtypeStruct((b, t, w), BF16),
        scratch_shapes=[pltpu.VMEM((H_B, t, LANE), F32), pltpu.VMEM((H_B, t, LANE), F32),
                        pltpu.VMEM((H_B, t, HEAD_DIM), F32)],
        compiler_params=_cparams("parallel", "arbitrary"),
        name="attn_b_step",
    )(proj3, cache_k, cache_v, proj3, proj3, ck_cache, ck_new)


def _tri_ones():
    r = lax.broadcasted_iota(I32, (2 * LANE, 2 * LANE), 0) & (LANE - 1)
    c = lax.broadcasted_iota(I32, (2 * LANE, 2 * LANE), 1)
    return jnp.where((r > c) | (c >= LANE), 1.0, 0.0).astype(BF16)


def _stick_scores(z, row_minus_col, tri_ones):
    rc_blocks = None if row_minus_col is None else _lane_blocks(row_minus_col)
    out = []
    for j, zb in enumerate(_lane_blocks(z)):
        tail = jnp.log(1.0 + jnp.exp(-jnp.abs(zb)))
        log_beta = jnp.minimum(zb, 0.0) - tail
        log_keep = -jnp.maximum(zb, 0.0) - tail
        strict = None
        if rc_blocks is not None:
            strict = rc_blocks[j] > 0
            log_keep = jnp.where(strict, log_keep, 0.0)
        hi = log_keep.astype(BF16)
        lo = (log_keep - hi.astype(F32)).astype(BF16)
        out.append((log_beta, strict, _dot(jnp.concatenate([hi, lo], axis=1), tri_ones)))
    return out


def _stick_weights(blocks, run):
    ws = [None] * len(blocks)
    for j in reversed(range(len(blocks))):
        log_beta, strict, sums = blocks[j]
        w = jnp.exp(log_beta + sums[:, :LANE] + run)
        if strict is not None:
            w = jnp.where(strict, w, 0.0)
        ws[j] = w.astype(BF16)
        run = run + sums[:, LANE:]
    return (jnp.concatenate(ws, axis=1) if len(ws) > 1 else ws[0]), run


C_HEADS_PER_STEP = 2


def _attn_c_prompt_kernel(q_ref, k_ref, v_ref, o_ref, run_s, acc_s, *, tq):
    qi = pl.program_id(2)
    heads = [slice(h * HEAD_DIM, (h + 1) * HEAD_DIM) for h in range(C_HEADS_PER_STEP)]
    qs = [(q_ref[0, :, sl] * SCALE).astype(BF16) for sl in heads]
    tri_ones = _tri_ones()

    def tile(kt, row_minus_col, first):
        off = pl.multiple_of(kt * tq, tq)
        zs = [_dot_nt(qs[h], k_ref[0, pl.ds(off, tq), sl].astype(BF16)) for h, sl in enumerate(heads)]
        scored = [_stick_scores(z, row_minus_col, tri_ones) for z in zs]
        for h, sl in enumerate(heads):
            run = jnp.zeros((tq, LANE), F32) if first else run_s[h]
            w_all, run = _stick_weights(scored[h], run)
            out = _dot(w_all, v_ref[0, pl.ds(off, tq), sl].astype(BF16))
            acc_s[h] = out if first else acc_s[h] + out
            run_s[h] = run

    rc = lax.broadcasted_iota(I32, (tq, tq), 0) - lax.broadcasted_iota(I32, (tq, tq), 1)
    tile(qi, rc, True)

    def body(i, carry):
        tile(qi - 1 - i, None, False)
        return carry

    lax.fori_loop(0, qi, body, 0)
    for h, sl in enumerate(heads):
        o_ref[0, :, sl] = acc_s[h].astype(o_ref.dtype)


def _attn_c_prompt(proj3, *, tq=512):
    b, t, _ = proj3.shape
    nq = t // tq
    hp = C_HEADS_PER_STEP
    w = hp * HEAD_DIM
    return pl.pallas_call(
        functools.partial(_attn_c_prompt_kernel, tq=tq),
        grid=(b, H_C // hp, nq),
        in_specs=[
            pl.BlockSpec((1, tq, w), lambda i, h, j: (i, j, OD_CQ // w + h)),
            pl.BlockSpec((1, t, w), lambda i, h, j: (i, 0, OD_CK // w + h)),
            pl.BlockSpec((1, t, w), lambda i, h, j: (i, 0, OD_CV // w + h)),
        ],
        out_specs=pl.BlockSpec((1, tq, w), lambda i, h, j: (i, j, h)),
        out_shape=jax.ShapeDtypeStruct((b, t, H_C * HEAD_DIM), BF16),
        scratch_shapes=[pltpu.VMEM((hp, tq, LANE), F32), pltpu.VMEM((hp, tq, HEAD_DIM), F32)],
        compiler_params=_cparams("parallel", "parallel", "parallel"),
        name="attn_c_prompt",
    )(proj3, proj3, proj3)


def _attn_c_step_kernel(q_ref, kc_ref, vc_ref, kn_ref, vn_ref, o_ref, run_s, acc_s, *, nkc, t, tkc, sub):
    kb = pl.program_id(1)
    tri_ones = _tri_ones()

    def q_head(h):
        return (q_ref[0, :, h * HEAD_DIM:(h + 1) * HEAD_DIM] * SCALE).astype(BF16)

    @pl.when(kb == 0)
    def _():
        rc = lax.broadcasted_iota(I32, (t, LANE), 0) - lax.broadcasted_iota(I32, (t, LANE), 1)
        heads = [slice(h * HEAD_DIM, (h + 1) * HEAD_DIM) for h in range(H_C)]
        zs = [_dot_nt(q_head(h), _pad_keys(kn_ref[0, :, heads[h]], LANE).astype(BF16)) for h in range(H_C)]
        scored = [_stick_scores(z, rc, tri_ones) for z in zs]
        for h in range(H_C):
            w_all, run = _stick_weights(scored[h], jnp.zeros((t, LANE), F32))
            acc_s[h] = _dot(w_all, _pad_keys(vn_ref[0, :, heads[h]], LANE).astype(BF16))
            run_s[h] = run

    @pl.when(kb > 0)
    def _():
        def body(i, carry):
            row0 = (tkc - sub - i * sub) * H_C
            zs = [_dot_nt(q_head(h), kc_ref[0, pl.ds(row0 + h, sub, stride=H_C), :].astype(BF16))
                  for h in range(H_C)]
            scored = [_stick_scores(z, None, tri_ones) for z in zs]
            for h in range(H_C):
                w_all, run = _stick_weights(scored[h], run_s[h])
                v = vc_ref[0, pl.ds(row0 + h, sub, stride=H_C), :].astype(BF16)
                acc_s[h] += _dot(w_all, v)
                run_s[h] = run
            return carry

        lax.fori_loop(0, tkc // sub, body, 0)

    @pl.when(kb == nkc)
    def _():
        for h in range(H_C):
            o_ref[0, :, h * HEAD_DIM:(h + 1) * HEAD_DIM] = acc_s[h].astype(o_ref.dtype)


def _attn_c_step(proj3, cache_k, cache_v, *, tkc=1024, sub=1024):
    b, t, _ = proj3.shape
    p = cache_k.shape[1] // H_C
    assert p % tkc == 0 and t <= LANE, (p, tkc, t)
    nkc = p // tkc
    w = H_C * HEAD_DIM

    def cache_idx(i, j):
        return (i, jnp.clip(nkc - j, 0, nkc - 1), 0)

    return pl.pallas_call(
        functools.partial(_attn_c_step_kernel, nkc=nkc, t=t, tkc=tkc, sub=sub),
        grid=(b, nkc + 1),
        in_specs=[
            pl.BlockSpec((1, t, w), lambda i, j: (i, 0, OD_CQ // w)),
            pl.BlockSpec((1, tkc * H_C, HEAD_DIM), cache_idx),
            pl.BlockSpec((1, tkc * H_C, HEAD_DIM), cache_idx),
            pl.BlockSpec((1, t, w), lambda i, j: (i, 0, OD_CK // w)),
            pl.BlockSpec((1, t, w), lambda i, j: (i, 0, OD_CV // w)),
        ],
        out_specs=pl.BlockSpec((1, t, w), lambda i, j: (i, 0, 0)),
        out_shape=jax.ShapeDtypeStruct((b, t, w), BF16),
        scratch_shapes=[pltpu.VMEM((H_C, t, LANE), F32), pltpu.VMEM((H_C, t, HEAD_DIM), F32)],
        compiler_params=_cparams("parallel", "arbitrary"),
        name="attn_c_step",
    )(proj3, cache_k, cache_v, proj3, proj3)


D_HEADS_PER_STEP = 4


def _attn_d_kernel(q_ref, kp_ref, vp_ref, ko_ref, vo_ref, bias_ref, o_ref, *, tq):
    qi = pl.program_id(2)
    heads = [slice(h * HEAD_DIM, (h + 1) * HEAD_DIM) for h in range(D_HEADS_PER_STEP)]
    qs = [(q_ref[0, :, sl] * SCALE).astype(BF16) for sl in heads]

    def own_logits(h):
        return _dot_nt(qs[h], ko_ref[0, :, heads[h]].astype(BF16)) + bias_ref[h, :, D_BAND_LEFT:D_BAND_LEFT + tq]

    def with_left():
        s_o = [own_logits(h) for h in range(D_HEADS_PER_STEP)]
        s_p = [_dot_nt(qs[h], kp_ref[0, :, heads[h]].astype(BF16)) + bias_ref[h, :, 0:D_BAND_LEFT]
               for h in range(D_HEADS_PER_STEP)]
        probs = []
        for h in range(D_HEADS_PER_STEP):
            m = jnp.maximum(jnp.max(s_p[h], axis=-1, keepdims=True), jnp.max(s_o[h], axis=-1, keepdims=True))
            p_p = jnp.exp(s_p[h] - m)
            p_o = jnp.exp(s_o[h] - m)
            l = jnp.sum(p_p, axis=-1, keepdims=True) + jnp.sum(p_o, axis=-1, keepdims=True)
            probs.append((p_p.astype(BF16), p_o.astype(BF16), l))
        for h, sl in enumerate(heads):
            p_p, p_o, l = probs[h]
            out = _dot(p_p, vp_ref[0, :, sl].astype(BF16)) + _dot(p_o, vo_ref[0, :, sl].astype(BF16))
            o_ref[0, :, sl] = (out / l).astype(o_ref.dtype)

    def own_only():
        s_o = [own_logits(h) for h in range(D_HEADS_PER_STEP)]
        for h, sl in enumerate(heads):
            p_o = jnp.exp(s_o[h] - jnp.max(s_o[h], axis=-1, keepdims=True))
            l = jnp.sum(p_o, axis=-1, keepdims=True)
            o_ref[0, :, sl] = (_dot(p_o.astype(BF16), vo_ref[0, :, sl].astype(BF16)) / l).astype(o_ref.dtype)

    pl.when(qi > 0)(with_left)
    pl.when(qi == 0)(own_only)


def _attn_d(proj3, bias):
    b, t, _ = proj3.shape
    tq = D_BAND_LEFT
    nq = t // tq
    hp = D_HEADS_PER_STEP
    w = hp * HEAD_DIM
    qc, kc, vc = OD_DQ // w, OD_DK // w, OD_DV // w
    return pl.pallas_call(
        functools.partial(_attn_d_kernel, tq=tq),
        grid=(b, H_D // hp, nq),
        in_specs=[
            pl.BlockSpec((1, tq, w), lambda i, h, j: (i, j, qc + h)),
            pl.BlockSpec((1, tq, w), lambda i, h, j: (i, jnp.maximum(j - 1, 0), kc + h)),
            pl.BlockSpec((1, tq, w), lambda i, h, j: (i, jnp.maximum(j - 1, 0), vc + h)),
            pl.BlockSpec((1, tq, w), lambda i, h, j: (i, j, kc + h)),
            pl.BlockSpec((1, tq, w), lambda i, h, j: (i, j, vc + h)),
            pl.BlockSpec((hp, tq, 2 * tq), lambda i, h, j: (h, 0, 0)),
        ],
        out_specs=pl.BlockSpec((1, tq, w), lambda i, h, j: (i, j, h)),
        out_shape=jax.ShapeDtypeStruct((b, t, H_D * HEAD_DIM), BF16),
        compiler_params=_cparams("parallel", "parallel", "parallel"),
        name="attn_d",
    )(proj3, proj3, proj3, proj3, proj3, bias)


def _attn_d_step_kernel(q_ref, kp_ref, vp_ref, kn_ref, vn_ref, bias_ref, o_ref, *, t):
    heads = [slice(h * HEAD_DIM, (h + 1) * HEAD_DIM) for h in range(H_D)]
    logits = []
    for h, sl in enumerate(heads):
        q = (q_ref[0, :, sl] * SCALE).astype(BF16)
        kp = kp_ref[0, pl.ds(h, D_BAND_LEFT, stride=H_D), :].astype(BF16)
        s_p = _dot_nt(q, kp) + bias_ref[h, :, 0:D_BAND_LEFT]
        s_o = _dot_nt(q, kn_ref[0, :, sl].astype(BF16)) + bias_ref[h, :, D_BAND_LEFT:D_BAND_LEFT + t]
        logits.append((s_p, s_o))
    probs = []
    for s_p, s_o in logits:
        m = jnp.maximum(jnp.max(s_p, axis=-1, keepdims=True), jnp.max(s_o, axis=-1, keepdims=True))
        p_p = jnp.exp(s_p - m)
        p_o = jnp.exp(s_o - m)
        l = jnp.sum(p_p, axis=-1, keepdims=True) + jnp.sum(p_o, axis=-1, keepdims=True)
        probs.append((p_p.astype(BF16), p_o.astype(BF16), l))
    for h, sl in enumerate(heads):
        p_p, p_o, l = probs[h]
        vp = vp_ref[0, pl.ds(h, D_BAND_LEFT, stride=H_D), :].astype(BF16)
        out = _dot(p_p, vp) + _dot(p_o, vn_ref[0, :, sl].astype(BF16))
        o_ref[0, :, sl] = (out / l).astype(o_ref.dtype)


def _attn_d_step(proj3, cache_k, cache_v, bias):
    b, t, _ = proj3.shape
    w = H_D * HEAD_DIM
    return pl.pallas_call(
        functools.partial(_attn_d_step_kernel, t=t),
        grid=(b,),
        in_specs=[
            pl.BlockSpec((1, t, w), lambda i: (i, 0, OD_DQ // w)),
            pl.BlockSpec((1, D_BAND_LEFT * H_D, HEAD_DIM), lambda i: (i, 0, 0)),
            pl.BlockSpec((1, D_BAND_LEFT * H_D, HEAD_DIM), lambda i: (i, 0, 0)),
            pl.BlockSpec((1, t, w), lambda i: (i, 0, OD_DK // w)),
            pl.BlockSpec((1, t, w), lambda i: (i, 0, OD_DV // w)),
            pl.BlockSpec((H_D, t, D_BAND_LEFT + t), lambda i: (0, 0, 0)),
        ],
        out_specs=pl.BlockSpec((1, t, w), lambda i: (i, 0, 0)),
        out_shape=jax.ShapeDtypeStruct((b, t, w), BF16),
        compiler_params=_cparams("parallel"),
        name="attn_d_step",
    )(proj3, cache_k, cache_v, proj3, proj3, bias)


def _head_major_kernel(x_ref, o_ref, *, n_heads, rows):
    for h in range(n_heads):
        o_ref[0, pl.ds(h, rows, stride=n_heads), :] = x_ref[0, :, h * HEAD_DIM:(h + 1) * HEAD_DIM]


def _head_major(x3, col0, n_heads, row0=0):
    b, t, _ = x3.shape
    n_rows = t - row0
    tq = min(512, n_rows)
    w = n_heads * HEAD_DIM
    assert col0 % w == 0 and row0 % tq == 0 and n_rows % tq == 0, (col0, row0, n_rows)
    flat = pl.pallas_call(
        functools.partial(_head_major_kernel, n_heads=n_heads, rows=tq),
        grid=(b, n_rows // tq),
        in_specs=[pl.BlockSpec((1, tq, w), lambda i, j: (i, row0 // tq + j, col0 // w))],
        out_specs=pl.BlockSpec((1, tq * n_heads, HEAD_DIM), lambda i, j: (i, j, 0)),
        out_shape=jax.ShapeDtypeStruct((b, n_rows * n_heads, HEAD_DIM), x3.dtype),
        compiler_params=_cparams("parallel", "parallel"),
        name="head_major",
    )(x3)
    return flat.reshape(b, n_rows, n_heads, HEAD_DIM)


def _pad_rows(x, rows):
    return jnp.pad(x, ((0, 0), (0, rows - x.shape[1]), (0, 0)))


def _even_mixer(h2, b, t, past, w, tz_prompt, tz_step, tq_a, mm):
    proj = mm("in_even", h2, w["w_in"], mode="headnorm", gain=w["gain"], flag=w["flag"], n_norm_tiles=9,
              name="proj_even")
    proj3 = proj.reshape(b, t, EV_PAD)
    ak = proj3[:, :, EV_AK:EV_AK + KV_A * HEAD_DIM]
    av = proj3[:, :, EV_AV:EV_AV + KV_A * HEAD_DIM]
    ik = proj3[:, :, EV_MISC + MISC_IK:EV_MISC + MISC_IK + IDX_DIM]
    bk = proj3[:, :, EV_BK:EV_BK + H_B * HEAD_DIM]
    bv = proj3[:, :, EV_BV:EV_BV + H_B * HEAD_DIM]
    raw_h = proj3[:, :, EV_MISC + MISC_BF:EV_MISC + MISC_BF + H_B].transpose(0, 2, 1)
    iq_col = EV_IQ // (IDX_HEADS * IDX_DIM)
    misc_col = EV_MISC // LANE
    if past is None:
        logf_h, cum_h = _logf_cum(raw_h, w["f_bias"], None)
        mask = _select_mask(proj3, iq_col, proj3, misc_col, proj3, misc_col, t=t, tq=tq_a, lp=t, l_valid=t,
                            q_base=0, n_sel=min(TOPK_MAX, t // 4), tk_out=tq_a, causal_prefix=True)
        out_a = _attn_a_prompt(proj3, mask, tz_prompt, tq=tq_a)
        out_b = _attn_b_prompt(proj3, cum_h, tq=min(512, t))
    else:
        p_ak, p_av, p_ik, p_bk, p_bv, p_lf = past
        p = p_ak.shape[1]
        l_valid = p + t
        lp = p + LANE
        assert p % LANE == 0 and t <= LANE, (p, t)
        logf_h, cum_h = _logf_cum(raw_h, w["f_bias"], p_lf.transpose(0, 2, 1))
        ik_all = _pad_rows(jnp.concatenate([p_ik, ik], axis=1), lp)
        mask = _select_mask(proj3, iq_col, proj3, misc_col, ik_all, None, t=t, tq=t, lp=lp, l_valid=l_valid,
                            q_base=p, n_sel=min(TOPK_MAX, l_valid // 4), tk_out=lp, causal_prefix=False)
        out_a = _attn_a_step(proj3, p_ak.reshape(b, p * KV_A, HEAD_DIM), p_av.reshape(b, p * KV_A, HEAD_DIM),
                             mask, tz_step[:, :t])
        out_b = _attn_b_step(proj3, p_bk.reshape(b, p * H_B, HEAD_DIM), p_bv.reshape(b, p * H_B, HEAD_DIM), cum_h)
    mixed = [out_a.reshape(b * t, -1), out_b.reshape(b * t, -1)]
    if past is None:
        state = (_head_major(proj3, EV_AK, KV_A), _head_major(proj3, EV_AV, KV_A), ik,
                 _head_major(proj3, EV_BK, H_B), _head_major(proj3, EV_BV, H_B), logf_h.transpose(0, 2, 1))
    else:
        state = (ak.reshape(b, t, KV_A, HEAD_DIM), av.reshape(b, t, KV_A, HEAD_DIM), ik,
                 bk.reshape(b, t, H_B, HEAD_DIM), bv.reshape(b, t, H_B, HEAD_DIM), logf_h.transpose(0, 2, 1))
    return mixed, state


def _odd_mixer(h2, b, t, past, w, band, mm):
    proj = mm("in_odd", h2, w["w_in"], col_block=_odd_col_block, layer=0, mode="headnorm", gain=w["gain"],
              flag=w["flag"], n_norm_tiles=4, name="proj_odd")
    proj3 = proj.reshape(b, t, OD_WIDTH)
    if past is None:
        out_c = _attn_c_prompt(proj3, tq=min(512, t))
        out_d = _attn_d(proj3, band)
        keep = min(D_BAND_LEFT, t)
        state = (_head_major(proj3, OD_CK, H_C), _head_major(proj3, OD_CV, H_C),
                 _head_major(proj3, OD_DK, H_D, row0=t - keep), _head_major(proj3, OD_DV, H_D, row0=t - keep))
    else:
        p_ck, p_cv, p_dk, p_dv = past
        p = p_ck.shape[1]
        out_c = _attn_c_step(proj3, p_ck.reshape(b, p * H_C, HEAD_DIM), p_cv.reshape(b, p * H_C, HEAD_DIM))
        wd = p_dk.shape[1]
        assert wd == D_BAND_LEFT and p % CHUNK == 0 and t <= CHUNK, (wd, p, t)
        out_d = _attn_d_step(proj3, p_dk.reshape(b, wd * H_D, HEAD_DIM), p_dv.reshape(b, wd * H_D, HEAD_DIM),
                             band[:, :t, :D_BAND_LEFT + t])
        state = tuple(proj3[:, :, c0:c0 + H_C * HEAD_DIM].reshape(b, t, H_C, HEAD_DIM)
                      for c0 in (OD_CK, OD_CV, OD_DK, OD_DV))
    mixed = [out_c.reshape(b * t, -1), out_d.reshape(b * t, -1)]
    return mixed, state


def _odd_col_block(j):
    return jnp.where(j < 4, j + 6, jnp.where(j < 10, j - 4, j))


def _trunk(x, caches_even, caches_odd, wts, bf16_weights):
    def mm(key, a, w_f32, col_block=None, layer=None, **kw):
        if key in bf16_weights:
            w_bf16 = bf16_weights[key]
            if kw.get("tn", 512) == 512 and w_bf16.shape[1] % 1024 == 0:
                kw = dict(kw, tn=1024, n_norm_tiles=(kw.get("n_norm_tiles", 0) + 1) // 2)
            return _matmul(a, w_bf16, **kw)
        out, bf16_weights[key] = _matmul_cast(a, w_f32, col_block=col_block, layer=layer, **kw)
        return out

    b, t, d = x.shape
    x2 = x.reshape(b * t, d)
    h2 = _rmsnorm(x2, wts["attn_norm"][0])
    mixed, st_even = _even_mixer(h2, b, t, caches_even, wts["even"], wts["tz_prompt"], wts["tz_step"],
                                 wts["tq_a"], mm)
    x2 = mm("out_even", mixed, wts["even"]["w_out"], layer=0, mode="residual", residual=x2, name="out_even")
    h2 = _rmsnorm(x2, wts["mlp_norm"][0])
    u = mm("up0", h2, wts["w_up"], layer=0, mode="relu2", out_dtype=BF16, tn=1024, name="mlp_up0")
    x2 = _matmul(u, wts["w_down"], layer=0, mode="residual", residual=x2, tn=256, tk=u.shape[1], name="mlp_down0")
    h2 = _rmsnorm(x2, wts["attn_norm"][1])
    mixed, st_odd = _odd_mixer(h2, b, t, caches_odd, wts["odd"], wts["band"], mm)
    x2 = mm("out_odd", mixed, wts["odd"]["w_out"], layer=0, mode="residual", residual=x2, name="out_odd")
    h2 = _rmsnorm(x2, wts["mlp_norm"][1])
    u = mm("up1", h2, wts["w_up"], layer=1, mode="relu2", out_dtype=BF16, tn=1024, name="mlp_up1")
    x2 = _matmul(u, wts["w_down"], layer=1, mode="residual", residual=x2, tn=256, tk=u.shape[1], name="mlp_down1")
    return x2.reshape(b, t, d), tuple(s[None] for s in st_even), tuple(s[None] for s in st_odd)


def _even_weights(w_in, w_out, a_qn, a_kn, b_qn, b_kn, f_bias):
    aq, ak, av, iq, ik, iw, bq, bk, bv, bf = jnp.split(
        w_in, [1024, 1280, 1536, 2048, 2112, 2120, 3144, 4168, 5192], axis=1)
    d = w_in.shape[0]
    pad = jnp.zeros((d, EV_PAD - EV_WIDTH + LANE - (IDX_DIM + IDX_HEADS + H_B)), w_in.dtype)
    w_r = jnp.concatenate([aq, bq, bk, bv, ak, av, iq, ik, iw, bf, pad], axis=1)
    ones = jnp.ones((EV_PAD,), F32)
    gain = ones.at[EV_AQ:EV_AQ + 1024].set(jnp.tile(a_qn, H_A))
    gain = gain.at[EV_BQ:EV_BQ + 1024].set(jnp.tile(b_qn, H_B))
    gain = gain.at[EV_BK:EV_BK + 1024].set(jnp.tile(b_kn, H_B))
    gain = gain.at[EV_AK:EV_AK + 256].set(jnp.tile(a_kn, KV_A))
    flag = jnp.zeros((EV_PAD,), F32).at[0:EV_BV].set(1.0).at[EV_AK:EV_AV].set(1.0)
    return {"w_in": w_r, "gain": gain.reshape(1, -1), "flag": flag.reshape(1, -1), "w_out": w_out,
            "f_bias": f_bias}


def _odd_weights(w_in, w_out, d_qn, d_kn):
    gain = jnp.ones((OD_WIDTH,), F32)
    gain = gain.at[OD_DQ:OD_DQ + 1024].set(jnp.tile(d_qn, H_D)).at[OD_DK:OD_DK + 1024].set(jnp.tile(d_kn, H_D))
    flag = jnp.zeros((OD_WIDTH,), F32).at[0:OD_CQ].set(1.0)
    return {"w_in": w_in, "gain": gain.reshape(1, -1), "flag": flag.reshape(1, -1), "w_out": w_out}


def kernel(x_prompt, x_sample, cache_a_k, cache_a_v, cache_a_kidx, cache_b_k, cache_b_v, cache_b_logf,
           cache_c_k, cache_c_v, cache_d_k, cache_d_v, attn_norm, mlp_norm, w_in_even, w_out_even,
           w_in_odd, w_out_odd, a_q_norm, a_k_norm, b_q_norm, b_k_norm, forget_bias, t5_bias,
           d_q_norm, d_k_norm, d_rel_bias, w_up, w_down):
    tq_a = min(256, x_prompt.shape[1])
    wts = {
        "attn_norm": attn_norm,
        "mlp_norm": mlp_norm,
        "even": _even_weights(w_in_even[0], w_out_even, a_q_norm[0], a_k_norm[0], b_q_norm[0], b_k_norm[0],
                              forget_bias[0]),
        "odd": _odd_weights(w_in_odd, w_out_odd, d_q_norm[0], d_k_norm[0]),
        "w_up": w_up,
        "w_down": w_down.astype(BF16),
        "tq_a": tq_a,
        "tz_prompt": _t5_strip(t5_bias, tq_a),
        "tz_step": _t5_strip(t5_bias, LANE),
        "band": _band_bias(d_rel_bias[0]),
    }
    bf16_weights = {}
    y_p, ev_p, od_p = _trunk(x_prompt, None, None, wts, bf16_weights)
    caches_even = (cache_a_k[0], cache_a_v[0], cache_a_kidx[0], cache_b_k[0], cache_b_v[0], cache_b_logf[0])
    caches_odd = (cache_c_k[0], cache_c_v[0], cache_d_k[0], cache_d_v[0])
    y_s, ev_s, od_s = _trunk(x_sample, caches_even, caches_odd, wts, bf16_weights)
    return (y_p, y_s) + ev_p + od_p + ev_s + od_s
```

```python
import functools

import numpy as np
import jax
import jax.numpy as jnp
from jax import lax
from jax.experimental import pallas as pl
from jax.experimental.pallas import tpu as pltpu

F32 = jnp.float32
BF16 = jnp.bfloat16
I32 = jnp.int32

HEAD_DIM = 128
CHUNK = 64
CHUNK_SHIFT = 6
H_A = 8
KV_A = 2
REP_A = H_A // KV_A
H_B = 8
H_C = 8
H_D = 8
IDX_HEADS = 8
IDX_DIM = 64
TOPK_MAX = 256
T5_BUCKETS = 32
D_LEFT_CHUNKS = 8
D_BAND_LEFT = D_LEFT_CHUNKS * CHUNK
D_REL_CLIP = 128
EPS = 1e-6
NEG_INF = -1e30
SCALE = HEAD_DIM ** -0.5
LANE = 128
MXU_WIDTH = 256
INT32_MIN = -(2 ** 31)
NEG_INF_KEY = int(np.float32(NEG_INF).view(np.int32)) ^ 0x7FFFFFFF

VMEM_LIMIT_BYTES = 56 * 1024 * 1024

EV_AQ, EV_BQ, EV_BK, EV_BV, EV_AK, EV_AV, EV_IQ, EV_MISC = 0, 1024, 2048, 3072, 4096, 4352, 4608, 5120
EV_WIDTH = 5248
EV_PAD = 5632
MISC_IK, MISC_IW, MISC_BF = 0, 64, 72
OD_DQ, OD_DK, OD_CQ, OD_CK, OD_CV, OD_DV = 0, 1024, 2048, 3072, 4096, 5120
OD_WIDTH = 6144


def _cparams(*sem):
    return pltpu.CompilerParams(dimension_semantics=sem, vmem_limit_bytes=VMEM_LIMIT_BYTES)


def _dot_nt(a, b):
    return lax.dot_general(a, b, (((1,), (1,)), ((), ())), preferred_element_type=F32)


def _dot(a, b):
    return jnp.dot(a, b, preferred_element_type=F32)


def _lane_blocks(x):
    return [x[..., j * LANE:(j + 1) * LANE] for j in range(x.shape[-1] // LANE)]


def _pad_keys(x, rows):
    return jnp.concatenate([x, jnp.zeros((rows - x.shape[0], x.shape[1]), x.dtype)], axis=0)


def _rmsnorm_kernel(x_ref, g_ref, o_ref):
    x = x_ref[...]
    ms = jnp.mean(x * x, axis=-1, keepdims=True)
    o_ref[...] = (x * lax.rsqrt(ms + EPS) * g_ref[...]).astype(o_ref.dtype)


def _rmsnorm(x2, gain):
    m, d = x2.shape
    tm = min(512, m)
    return pl.pallas_call(
        _rmsnorm_kernel,
        grid=(m // tm,),
        in_specs=[pl.BlockSpec((tm, d), lambda i: (i, 0)), pl.BlockSpec((1, d), lambda i: (0, 0))],
        out_specs=pl.BlockSpec((tm, d), lambda i: (i, 0)),
        out_shape=jax.ShapeDtypeStruct((m, d), BF16),
        compiler_params=_cparams("parallel"),
        name="rmsnorm",
    )(x2, gain.reshape(1, d))


def _mm_kernel(*refs, n_a, nk, mode, n_norm_tiles, cast_b):
    a_refs, b_ref = refs[:n_a], refs[n_a]
    refs = refs[n_a - 1:]
    if mode == "headnorm":
        gain_ref, flag_ref, o_ref = refs[2], refs[3], refs[4]
        rest = refs[5:]
    elif mode == "residual":
        res_ref, o_ref = refs[2], refs[3]
        rest = refs[4:]
    else:
        o_ref = refs[2]
        rest = refs[3:]
    n_axis = 0 if cast_b else 1
    if cast_b:
        wb_ref = rest[0]

        @pl.when(pl.program_id(1) == 0)
        def _():
            wb_ref[...] = b_ref[...].astype(BF16)

        b_ref = wb_ref

    tn = o_ref.shape[1]

    def product(c0=0, c1=tn):
        row, acc = 0, None
        for a_ref in a_refs:
            kp = a_ref.shape[1]
            part = _dot(a_ref[...], b_ref[row:row + kp, c0:c1])
            acc = part if acc is None else acc + part
            row += kp
        return acc

    def headnorm(acc, c0):
        for g in range(acc.shape[1] // HEAD_DIM):
            sl = slice(c0 + g * HEAD_DIM, c0 + (g + 1) * HEAD_DIM)
            blk = acc[:, g * HEAD_DIM:(g + 1) * HEAD_DIM]
            ms = jnp.mean(blk * blk, axis=-1, keepdims=True)
            normed = blk * lax.rsqrt(ms + EPS) * gain_ref[:, sl]
            o_ref[:, sl] = jnp.where(flag_ref[:, sl] > 0.0, normed, blk)

    def epilogue(acc, c0=0):
        sl = slice(c0, c0 + acc.shape[1])
        if mode == "headnorm":
            headnorm(acc, c0)
        elif mode == "relu2":
            r = jnp.maximum(acc, 0.0)
            o_ref[:, sl] = (r * r).astype(o_ref.dtype)
        elif mode == "residual":
            o_ref[:, sl] = res_ref[:, sl] + acc
        else:
            o_ref[:, sl] = acc.astype(o_ref.dtype)

    def chunked():
        starts = list(range(0, tn, MXU_WIDTH))
        accs = [product(c0, c0 + MXU_WIDTH) for c0 in starts]
        for c0, acc in zip(starts, accs):
            epilogue(acc, c0)

    if nk == 1 and mode == "headnorm":
        j = pl.program_id(n_axis)
        pl.when(j < n_norm_tiles)(chunked)

        @pl.when(j >= n_norm_tiles)
        def _():
            o_ref[...] = product()
    elif nk == 1:
        chunked()
    else:
        acc_ref = rest[0]
        k = pl.program_id(2)

        @pl.when(k == 0)
        def _():
            acc_ref[...] = jnp.zeros_like(acc_ref)

        acc_ref[...] += product()

        @pl.when(k == nk - 1)
        def _():
            epilogue(acc_ref[...])


def _matmul(a, b, *, mode="plain", out_dtype=F32, gain=None, flag=None, n_norm_tiles=0, residual=None,
            tm=1024, tn=512, tk=2048, layer=None, name="matmul"):
    a_parts = list(a) if isinstance(a, (list, tuple)) else [a]
    m = a_parts[0].shape[0]
    kdim, n = b.shape[-2:]
    tm = tm if m % tm == 0 else min(512, m)
    if len(a_parts) > 1:
        tk = kdim
        in_specs = [pl.BlockSpec((tm, part.shape[1]), lambda i, j, k: (i, 0)) for part in a_parts]
    else:
        in_specs = [pl.BlockSpec((tm, tk), lambda i, j, k: (i, k))]
    nk = kdim // tk
    if layer is None:
        in_specs.append(pl.BlockSpec((tk, tn), lambda i, j, k: (k, j)))
    else:
        in_specs.append(pl.BlockSpec((None, tk, tn), lambda i, j, k: (layer, k, j)))
    args = a_parts + [b]
    if mode == "headnorm":
        in_specs += [pl.BlockSpec((1, tn), lambda i, j, k: (0, j)), pl.BlockSpec((1, tn), lambda i, j, k: (0, j))]
        args += [gain, flag]
    elif mode == "residual":
        in_specs += [pl.BlockSpec((tm, tn), lambda i, j, k: (i, j))]
        args += [residual]
    scratch = [pltpu.VMEM((tm, tn), F32)] if nk > 1 else []
    return pl.pallas_call(
        functools.partial(_mm_kernel, n_a=len(a_parts), nk=nk, mode=mode, n_norm_tiles=n_norm_tiles,
                          cast_b=False),
        grid=(m // tm, n // tn, nk),
        in_specs=in_specs,
        out_specs=pl.BlockSpec((tm, tn), lambda i, j, k: (i, j)),
        out_shape=jax.ShapeDtypeStruct((m, n), out_dtype),
        scratch_shapes=scratch,
        compiler_params=_cparams("parallel", "parallel", "arbitrary"),
        name=name,
    )(*args)


def _matmul_cast(a, b_f32, *, col_block=None, n=None, mode="plain", out_dtype=F32, gain=None, flag=None,
                 n_norm_tiles=0, residual=None, tm=1024, tn=512, layer=None, name="matmul"):
    a_parts = list(a) if isinstance(a, (list, tuple)) else [a]
    m = a_parts[0].shape[0]
    kdim = b_f32.shape[-2]
    n = b_f32.shape[-1] if n is None else n
    tm = tm if m % tm == 0 else min(512, m)
    src = (lambda j: j) if col_block is None else col_block
    in_specs = [pl.BlockSpec((tm, part.shape[1]), lambda j, i: (i, 0)) for part in a_parts]
    if layer is None:
        in_specs.append(pl.BlockSpec((kdim, tn), lambda j, i: (0, src(j))))
    else:
        in_specs.append(pl.BlockSpec((None, kdim, tn), lambda j, i: (layer, 0, src(j))))
    args = a_parts + [b_f32]
    if mode == "headnorm":
        in_specs += [pl.BlockSpec((1, tn), lambda j, i: (0, j)), pl.BlockSpec((1, tn), lambda j, i: (0, j))]
        args += [gain, flag]
    elif mode == "residual":
        in_specs += [pl.BlockSpec((tm, tn), lambda j, i: (i, j))]
        args += [residual]
    return pl.pallas_call(
        functools.partial(_mm_kernel, n_a=len(a_parts), nk=1, mode=mode, n_norm_tiles=n_norm_tiles, cast_b=True),
        grid=(n // tn, m // tm),
        in_specs=in_specs,
        out_specs=[pl.BlockSpec((tm, tn), lambda j, i: (i, j)), pl.BlockSpec((kdim, tn), lambda j, i: (0, j))],
        out_shape=[jax.ShapeDtypeStruct((m, n), out_dtype), jax.ShapeDtypeStruct((kdim, n), BF16)],
        compiler_params=_cparams("parallel", "arbitrary"),
        name=name,
    )(*args)


def _cumsum_block(x, carry):
    n = x.shape[1]
    r = lax.broadcasted_iota(I32, (n, n), 0)
    c = lax.broadcasted_iota(I32, (n, n), 1)
    tri = jnp.where(r <= c, 1.0, 0.0).astype(BF16)
    hi = x.astype(BF16)
    r1 = x - hi.astype(F32)
    mid = r1.astype(BF16)
    lo = (r1 - mid.astype(F32)).astype(BF16)
    cs = _dot(hi, tri) + _dot(mid, tri) + _dot(lo, tri) + carry
    return cs, cs[:, n - 1:n]


def _logf_cum_kernel(*refs, p_len, t_len, blk):
    if p_len:
        raw_ref, fb_ref, past_ref, lf_ref, cum_ref = refs
    else:
        raw_ref, fb_ref, lf_ref, cum_ref = refs
    x = raw_ref[0] + fb_ref[...]
    lf = jnp.minimum(x, 0.0) - jnp.log1p(jnp.exp(-jnp.abs(x)))
    lf_ref[0] = lf
    carry = jnp.zeros((H_B, 1), F32)
    for s in range(0, p_len, blk):
        cs, carry = _cumsum_block(past_ref[0, :, s:s + blk], carry)
        cum_ref[0, :, s:s + blk] = cs
    nb = min(blk, t_len)
    for s in range(0, t_len, nb):
        cs, carry = _cumsum_block(lf[:, s:s + nb], carry)
        cum_ref[0, :, p_len + s:p_len + s + nb] = cs


def _logf_cum(raw_h, fbias, past_h):
    b, h, t = raw_h.shape
    p = 0 if past_h is None else past_h.shape[2]
    in_specs = [pl.BlockSpec((1, h, t), lambda i: (i, 0, 0)), pl.BlockSpec((h, 1), lambda i: (0, 0))]
    args = [raw_h, fbias.reshape(h, 1)]
    if p:
        in_specs.append(pl.BlockSpec((1, h, p), lambda i: (i, 0, 0)))
        args.append(past_h)
    return pl.pallas_call(
        functools.partial(_logf_cum_kernel, p_len=p, t_len=t, blk=512),
        grid=(b,),
        in_specs=in_specs,
        out_specs=[pl.BlockSpec((1, h, t), lambda i: (i, 0, 0)), pl.BlockSpec((1, h, p + t), lambda i: (i, 0, 0))],
        out_shape=[jax.ShapeDtypeStruct((b, h, t), F32), jax.ShapeDtypeStruct((b, h, p + t), F32)],
        compiler_params=_cparams("parallel"),
        name="logf_cum",
    )(*args)


def _toeplitz(row_vals, rows, cols):
    w = row_vals.shape[1]
    full = jnp.broadcast_to(row_vals[0:1, :], (rows, w))
    return pltpu.roll(full, 0, 1, stride=1, stride_axis=0)[:, :cols]


def _t5_strip_kernel(tab_ref, o_ref, *, tile):
    h = pl.program_id(0)
    width = 4 * tile
    m = lax.broadcasted_iota(I32, (8, width), 1)
    d = jnp.where(m < width // 2, m, m - width)
    rel = d - tile
    n = jnp.abs(rel)
    large = jnp.full_like(n, 8)
    for thr in (12, 16, 23, 32, 46, 64, 91):
        large = large + jnp.where(n >= thr, 1, 0)
    bucket = jnp.where(rel > 0, T5_BUCKETS // 2, 0) + jnp.where(n < 8, n, large)
    vals = jnp.zeros((8, width), F32)
    for bkt in range(T5_BUCKETS):
        vals = jnp.where(bucket == bkt, tab_ref[bkt, h], vals)
    vals = vals - tab_ref[T5_BUCKETS // 2 - 1, h]
    o_ref[0] = _toeplitz(vals, tile, 2 * tile)


def _t5_strip(t5_table, tile):
    return pl.pallas_call(
        functools.partial(_t5_strip_kernel, tile=tile),
        grid=(H_A,),
        in_specs=[pl.BlockSpec(memory_space=pltpu.SMEM)],
        out_specs=pl.BlockSpec((1, tile, 2 * tile), lambda h: (h, 0, 0)),
        out_shape=jax.ShapeDtypeStruct((H_A, tile, 2 * tile), F32),
        compiler_params=_cparams("parallel"),
        name="t5_strip",
    )(t5_table)


def _band_bias_kernel(tab_ref, o_ref, *, rows, cols, width):
    h = pl.program_id(0)
    m = lax.broadcasted_iota(I32, (8, width), 1)
    d = jnp.where(m < width // 2, m, m - width)
    idx = jnp.clip(d - D_BAND_LEFT, -D_REL_CLIP, D_REL_CLIP) + D_REL_CLIP
    vals = jnp.zeros((8, width), F32)
    for r in range(2 * D_REL_CLIP + 1):
        vals = jnp.where(idx == r, tab_ref[r, h], vals)
    vals = vals - tab_ref[0, h]
    i_chunk = lax.broadcasted_iota(I32, (rows, cols), 0) >> CHUNK_SHIFT
    c_chunk = lax.broadcasted_iota(I32, (rows, cols), 1) >> CHUNK_SHIFT
    visible = (c_chunk >= i_chunk) & (c_chunk <= i_chunk + D_LEFT_CHUNKS)
    o_ref[0] = jnp.where(visible, _toeplitz(vals, rows, cols), NEG_INF)


def _band_bias(rel_table):
    rows, cols, width = D_BAND_LEFT, 2 * D_BAND_LEFT, 4 * D_BAND_LEFT
    return pl.pallas_call(
        functools.partial(_band_bias_kernel, rows=rows, cols=cols, width=width),
        grid=(H_D,),
        in_specs=[pl.BlockSpec(memory_space=pltpu.SMEM)],
        out_specs=pl.BlockSpec((1, rows, cols), lambda h: (h, 0, 0)),
        out_shape=jax.ShapeDtypeStruct((H_D, rows, cols), F32),
        compiler_params=_cparams("parallel"),
        name="band_bias",
    )(rel_table)


def _select_kernel(*refs, n_g, sb, tq, nq, lp, l_valid, q_base, n_sel, tk_out, causal_prefix):
    iq_refs, misc_refs = refs[:n_g], refs[n_g:2 * n_g]
    kk_ref, o_ref, key_ref = refs[2 * n_g:]
    step = pl.program_id(1)
    rows = sb * tq
    kf = float(n_sel)
    gs = range(n_g)

    def count(pred):
        return jnp.sum(jnp.where(pred, 1.0, 0.0), axis=-1, keepdims=True)

    def run(tiles):
        pfx = [p for _, p in tiles]
        adm = []
        for g, (q_tile, p) in enumerate(tiles):
            scores = []
            for s in range(sb):
                iq = iq_refs[g][s]
                iw = misc_refs[g][s][:, MISC_IW:MISC_IW + IDX_HEADS]
                ik = kk_ref[s, 0:p, 0:IDX_DIM].astype(BF16)
                score = jnp.zeros((tq, p), F32)
                for h in range(IDX_HEADS):
                    dots = _dot_nt(iq[:, h * IDX_DIM:(h + 1) * IDX_DIM].astype(BF16), ik)
                    score = score + iw[:, h:h + 1] * jnp.maximum(dots, 0.0)
                scores.append(score)
            score = jnp.concatenate(scores, axis=0) if sb > 1 else scores[0]
            row_in_tile = jnp.concatenate([lax.broadcasted_iota(I32, (tq, 1), 0)] * sb, axis=0)
            q_pos = q_base + q_tile * tq + row_in_tile
            kp = lax.broadcasted_iota(I32, (1, p), 1)
            ok = ((kp >> CHUNK_SHIFT) <= (q_pos >> CHUNK_SHIFT)) & (kp < l_valid)
            bits = lax.bitcast_convert_type(jnp.where(ok, score, NEG_INF), I32)
            key_ref[g, :, 0:p] = jnp.where(bits < 0, bits ^ jnp.int32(0x7FFFFFFF), bits)
            adm.append(ok)

        def keys(g):
            return key_ref[g, :, 0:pfx[g]]

        t0 = tuple(jnp.where(count(keys(g) >= 0) >= kf, jnp.int32(0), jnp.int32(INT32_MIN)) for g in gs)

        def thr_body(i, ts):
            bit = jnp.int32(1) << (30 - i)
            return tuple(jnp.where(count(keys(g) >= ts[g] + bit) >= kf, ts[g] + bit, ts[g]) for g in gs)

        thr = lax.fori_loop(0, 31, thr_body, t0)
        need = [kf - count(keys(g) > thr[g]) for g in gs]
        tied = [(count(keys(g) == thr[g]) != need[g]) & (thr[g] > NEG_INF_KEY) for g in gs]
        any_tied = sum(jnp.max(jnp.where(tied[g], 1.0, 0.0)) for g in gs) > 0.5

        def all_tied():
            return tuple(jnp.where(keys(g) == thr[g], 1.0, 0.0) for g in gs)

        def lowest_tied():
            r = lax.broadcasted_iota(I32, (LANE, 2 * LANE), 0)
            c = lax.broadcasted_iota(I32, (LANE, 2 * LANE), 1)
            count_mat = jnp.where((r <= c) | (c >= LANE), 1.0, 0.0).astype(BF16)
            out = []
            for g, tied_mask in enumerate(all_tied()):
                seen = jnp.zeros((rows, LANE), F32)
                kept = []
                for blk in _lane_blocks(tied_mask):
                    both = _dot(blk.astype(BF16), count_mat)
                    kept.append(jnp.where(both[:, :LANE] + seen <= need[g], blk, 0.0))
                    seen = seen + both[:, LANE:]
                out.append(jnp.concatenate(kept, axis=1) if len(kept) > 1 else kept[0])
            return tuple(out)

        take_tied = lax.cond(any_tied, lowest_tied, all_tied)
        for g in gs:
            sel = ((keys(g) > thr[g]) | (take_tied[g] > 0.5)) & adm[g]
            madd = jnp.where(sel, 0.0, NEG_INF)
            for s in range(sb):
                for kt in range(lp // tk_out):
                    if (kt + 1) * tk_out <= pfx[g]:
                        o_ref[s, g, 0, kt] = madd[s * tq:(s + 1) * tq, kt * tk_out:(kt + 1) * tk_out]
                    else:
                        o_ref[s, g, 0, kt] = jnp.full((tq, tk_out), NEG_INF, F32)

    if not causal_prefix:
        run([(step, lp)] * n_g)
    elif n_g == 2:
        for j in range(nq // 2):
            pl.when(step == j)(functools.partial(run, [(j, (j + 1) * tq), (nq - 1 - j, (nq - j) * tq)]))
    else:
        for j in range(nq):
            pl.when(step == j)(functools.partial(run, [(j, (j + 1) * tq)]))


def _select_mask(iq_arr, iq_col, misc_arr, misc_col, kk_arr, kk_col, *, t, tq, lp, l_valid, q_base, n_sel,
                 tk_out, causal_prefix):
    b = iq_arr.shape[0]
    nq = t // tq
    wk = kk_arr.shape[2] if kk_col is None else LANE
    kcol = 0 if kk_col is None else kk_col
    sb = 4 if (nq == 1 and tq <= 64 and b % 4 == 0) else 1
    n_g = 2 if (causal_prefix and nq % 2 == 0) else 1
    steps = nq // n_g
    tile_of = [lambda j: j, lambda j: nq - 1 - j]
    nkt = lp // tk_out
    in_specs = [pl.BlockSpec((sb, tq, IDX_HEADS * IDX_DIM), lambda i, j, g=g: (i, tile_of[g](j), iq_col))
                for g in range(n_g)]
    in_specs += [pl.BlockSpec((sb, tq, LANE), lambda i, j, g=g: (i, tile_of[g](j), misc_col)) for g in range(n_g)]
    in_specs.append(pl.BlockSpec((sb, lp, wk), lambda i, j: (i, 0, kcol)))
    return pl.pallas_call(
        functools.partial(_select_kernel, n_g=n_g, sb=sb, tq=tq, nq=nq, lp=lp, l_valid=l_valid, q_base=q_base,
                          n_sel=n_sel, tk_out=tk_out, causal_prefix=causal_prefix),
        grid=(b // sb, steps),
        in_specs=in_specs,
        out_specs=pl.BlockSpec((sb, n_g, 1, nkt, tq, tk_out), lambda i, j: (i, 0, j, 0, 0, 0)),
        out_shape=jax.ShapeDtypeStruct((b, n_g, steps, nkt, tq, tk_out), F32),
        scratch_shapes=[pltpu.VMEM((n_g, sb * tq, lp), I32)],
        compiler_params=_cparams("parallel", "parallel"),
        name="a_select",
    )(*([iq_arr] * n_g), *([misc_arr] * n_g), kk_arr)


def _mask_tile_index(q_tile, nq, n_g):
    if n_g == 1:
        return 0, q_tile
    upper = q_tile >= nq // 2
    return jnp.where(upper, 1, 0), jnp.where(upper, nq - 1 - q_tile, q_tile)


def _online_stats(s, m_s, l_s):
    blocks = _lane_blocks(s)
    mx = blocks[0]
    for blk in blocks[1:]:
        mx = jnp.maximum(mx, blk)
    m_prev = m_s[...]
    m_new = jnp.maximum(m_prev, jnp.max(mx, axis=-1, keepdims=True))
    alpha = jnp.exp(m_prev - m_new)
    ps = [jnp.exp(blk - m_new) for blk in blocks]
    psum = ps[0]
    for p in ps[1:]:
        psum = psum + p
    l_s[...] = alpha * l_s[...] + psum
    m_s[...] = m_new
    p_all = jnp.concatenate([p.astype(BF16) for p in ps], axis=1) if len(ps) > 1 else ps[0].astype(BF16)
    return alpha, p_all


def _online_init(m_s, l_s, acc_s):
    m_s[...] = jnp.full_like(m_s, NEG_INF)
    l_s[...] = jnp.zeros_like(l_s)
    acc_s[...] = jnp.zeros_like(acc_s)


def _online_result(l_s, acc_s):
    return acc_s[...] / jnp.sum(l_s[...], axis=-1, keepdims=True)


def _stack_heads(q):
    return jnp.concatenate([q[:, r * HEAD_DIM:(r + 1) * HEAD_DIM] for r in range(REP_A)], axis=0).astype(BF16)


def _attn_a_prompt_kernel(q_ref, k_ref, v_ref, mask_ref, tz_ref, o_ref, m_s, l_s, acc_s, *, tq):
    qi = pl.program_id(1)
    width = REP_A * HEAD_DIM
    qs = [_stack_heads(q_ref[0, :, g * width:(g + 1) * width] * SCALE) for g in range(KV_A)]
    _online_init(m_s, l_s, acc_s)

    def tile(kt, bias_cols):
        off = pl.multiple_of(kt * tq, tq)
        mask = mask_ref[0, 0, 0, kt][None]
        logits = []
        for g in range(KV_A):
            k = k_ref[0, pl.ds(off, tq), g * HEAD_DIM:(g + 1) * HEAD_DIM].astype(BF16)
            s = _dot_nt(qs[g], k).reshape(REP_A, tq, tq) + mask
            if bias_cols is not None:
                s = s + tz_ref[g * REP_A:(g + 1) * REP_A, :, bias_cols]
            logits.append(s.reshape(REP_A * tq, tq))
        stats = [_online_stats(logits[g], m_s.at[g], l_s.at[g]) for g in range(KV_A)]
        for g in range(KV_A):
            alpha, p_all = stats[g]
            v = v_ref[0, pl.ds(off, tq), g * HEAD_DIM:(g + 1) * HEAD_DIM].astype(BF16)
            acc_s[g] = alpha * acc_s[g] + _dot(p_all, v)

    def plain_body(kt, carry):
        tile(kt, None)
        return carry

    lax.fori_loop(0, jnp.maximum(qi - 1, 0), plain_body, 0)

    @pl.when(qi >= 1)
    def _():
        tile(qi - 1, slice(0, tq))

    tile(qi, slice(tq, 2 * tq))
    for g in range(KV_A):
        out = _online_result(l_s.at[g], acc_s.at[g])
        for r in range(REP_A):
            col = (g * REP_A + r) * HEAD_DIM
            o_ref[0, :, col:col + HEAD_DIM] = out[r * tq:(r + 1) * tq].astype(o_ref.dtype)


def _attn_a_prompt(proj3, mask, tz, *, tq):
    b, t, _ = proj3.shape
    nq = t // tq
    qw, kw = H_A * HEAD_DIM, KV_A * HEAD_DIM
    rows = REP_A * tq
    n_g = mask.shape[1]

    def mask_idx(i, j):
        grp, pos = _mask_tile_index(j, nq, n_g)
        return (i, grp, pos, 0, 0, 0)

    return pl.pallas_call(
        functools.partial(_attn_a_prompt_kernel, tq=tq),
        grid=(b, nq),
        in_specs=[
            pl.BlockSpec((1, tq, qw), lambda i, j: (i, j, EV_AQ // qw)),
            pl.BlockSpec((1, t, kw), lambda i, j: (i, 0, EV_AK // kw)),
            pl.BlockSpec((1, t, kw), lambda i, j: (i, 0, EV_AV // kw)),
            pl.BlockSpec((1, 1, 1, nq, tq, tq), mask_idx),
            pl.BlockSpec((H_A, tq, 2 * tq), lambda i, j: (0, 0, 0)),
        ],
        out_specs=pl.BlockSpec((1, tq, qw), lambda i, j: (i, j, 0)),
        out_shape=jax.ShapeDtypeStruct((b, t, qw), BF16),
        scratch_shapes=[pltpu.VMEM((KV_A, rows, LANE), F32), pltpu.VMEM((KV_A, rows, LANE), F32),
                        pltpu.VMEM((KV_A, rows, HEAD_DIM), F32)],
        compiler_params=_cparams("parallel", "parallel"),
        name="attn_a_prompt",
    )(proj3, proj3, proj3, mask, tz)


def _attn_a_step_kernel(q_ref, kc_ref, vc_ref, kn_ref, vn_ref, mask_ref, tz_ref, o_ref, *, t, p):
    width = REP_A * HEAD_DIM
    mask = mask_ref[0, 0, 0, 0]
    logits = []
    for g in range(KV_A):
        heads = slice(g * REP_A, (g + 1) * REP_A)
        qs = _stack_heads(q_ref[0, :, g * width:(g + 1) * width] * SCALE)
        kc = kc_ref[0, pl.ds(g, p, stride=KV_A), :].astype(BF16)
        kn = _pad_keys(kn_ref[0, :, g * HEAD_DIM:(g + 1) * HEAD_DIM], LANE).astype(BF16)
        s_c = _dot_nt(qs, kc).reshape(REP_A, t, p) + mask[None, :, 0:p]
        s_n = (_dot_nt(qs, kn).reshape(REP_A, t, LANE) + mask[None, :, p:p + LANE]
               + tz_ref[heads, :, LANE:2 * LANE])
        blocks = _lane_blocks(s_c)
        blocks[-1] = blocks[-1] + tz_ref[heads, :, 0:LANE]
        blocks.append(s_n)
        logits.append(blocks)
    probs = []
    for blocks in logits:
        mx = blocks[0]
        for blk in blocks[1:]:
            mx = jnp.maximum(mx, blk)
        m = jnp.max(mx, axis=-1, keepdims=True)
        ps = [jnp.exp(blk - m) for blk in blocks]
        psum = ps[0]
        for pb in ps[1:]:
            psum = psum + pb
        l = jnp.sum(psum, axis=-1, keepdims=True)
        p_c = jnp.concatenate([pb.astype(BF16) for pb in ps[:-1]], axis=-1).reshape(REP_A * t, p)
        probs.append((p_c, ps[-1].astype(BF16).reshape(REP_A * t, LANE), l))
    for g, (p_c, p_n, l) in enumerate(probs):
        vc = vc_ref[0, pl.ds(g, p, stride=KV_A), :].astype(BF16)
        vn = _pad_keys(vn_ref[0, :, g * HEAD_DIM:(g + 1) * HEAD_DIM], LANE).astype(BF16)
        out = (_dot(p_c, vc) + _dot(p_n, vn)).reshape(REP_A, t, HEAD_DIM) / l
        for r in range(REP_A):
            col = (g * REP_A + r) * HEAD_DIM
            o_ref[0, :, col:col + HEAD_DIM] = out[r].astype(o_ref.dtype)


def _attn_a_step(proj3, cache_k, cache_v, mask, tz):
    b, t, _ = proj3.shape
    p = cache_k.shape[1] // KV_A
    qw, kw = H_A * HEAD_DIM, KV_A * HEAD_DIM
    return pl.pallas_call(
        functools.partial(_attn_a_step_kernel, t=t, p=p),
        grid=(b,),
        in_specs=[
            pl.BlockSpec((1, t, qw), lambda i: (i, 0, EV_AQ // qw)),
            pl.BlockSpec((1, p * KV_A, HEAD_DIM), lambda i: (i, 0, 0)),
            pl.BlockSpec((1, p * KV_A, HEAD_DIM), lambda i: (i, 0, 0)),
            pl.BlockSpec((1, t, kw), lambda i: (i, 0, EV_AK // kw)),
            pl.BlockSpec((1, t, kw), lambda i: (i, 0, EV_AV // kw)),
            pl.BlockSpec((1, 1, 1, 1, t, p + LANE), lambda i: (i, 0, 0, 0, 0, 0)),
            pl.BlockSpec((H_A, t, 2 * LANE), lambda i: (0, 0, 0)),
        ],
        out_specs=pl.BlockSpec((1, t, qw), lambda i: (i, 0, 0)),
        out_shape=jax.ShapeDtypeStruct((b, t, qw), BF16),
        compiler_params=_cparams("parallel"),
        name="attn_a_step",
    )(proj3, cache_k, cache_v, proj3, proj3, mask, tz)


def _causal_mask(s, row0=0):
    r = row0 + lax.broadcasted_iota(I32, s.shape, 0)
    c = lax.broadcasted_iota(I32, s.shape, 1)
    return jnp.where(c <= r, s, NEG_INF)


B_HEADS_PER_STEP = 2


def _attn_b_prompt_kernel(q_ref, k_ref, v_ref, ck_ref, o_ref, m_s, l_s, acc_s, *, tq):
    qi = pl.program_id(2)
    heads = [slice(h * HEAD_DIM, (h + 1) * HEAD_DIM) for h in range(B_HEADS_PER_STEP)]
    qs = [(q_ref[0, :, sl] * SCALE).astype(BF16) for sl in heads]
    _online_init(m_s, l_s, acc_s)

    def tile(kt, diag):
        off = pl.multiple_of(kt * tq, tq)
        logits = []
        for h, sl in enumerate(heads):
            s = _dot_nt(qs[h], k_ref[0, pl.ds(off, tq), sl].astype(BF16)) - ck_ref[0, h, kt]
            logits.append(_causal_mask(s) if diag else s)
        stats = [_online_stats(logits[h], m_s.at[h], l_s.at[h]) for h in range(B_HEADS_PER_STEP)]
        for h, sl in enumerate(heads):
            alpha, p_all = stats[h]
            acc_s[h] = alpha * acc_s[h] + _dot(p_all, v_ref[0, pl.ds(off, tq), sl].astype(BF16))

    def body(kt, carry):
        tile(kt, False)
        return carry

    lax.fori_loop(0, qi, body, 0)
    tile(qi, True)
    for h, sl in enumerate(heads):
        o_ref[0, :, sl] = _online_result(l_s.at[h], acc_s.at[h]).astype(o_ref.dtype)


def _attn_b_prompt(proj3, cum_h, *, tq=512):
    b, t, _ = proj3.shape
    nq = t // tq
    ck = cum_h.reshape(b, H_B, nq, 1, tq)
    hp = B_HEADS_PER_STEP
    w = hp * HEAD_DIM
    return pl.pallas_call(
        functools.partial(_attn_b_prompt_kernel, tq=tq),
        grid=(b, H_B // hp, nq),
        in_specs=[
            pl.BlockSpec((1, tq, w), lambda i, h, j: (i, j, EV_BQ // w + h)),
            pl.BlockSpec((1, t, w), lambda i, h, j: (i, 0, EV_BK // w + h)),
            pl.BlockSpec((1, t, w), lambda i, h, j: (i, 0, EV_BV // w + h)),
            pl.BlockSpec((1, hp, nq, 1, tq), lambda i, h, j: (i, h, 0, 0, 0)),
        ],
        out_specs=pl.BlockSpec((1, tq, w), lambda i, h, j: (i, j, h)),
        out_shape=jax.ShapeDtypeStruct((b, t, H_B * HEAD_DIM), BF16),
        scratch_shapes=[pltpu.VMEM((hp, tq, LANE), F32), pltpu.VMEM((hp, tq, LANE), F32),
                        pltpu.VMEM((hp, tq, HEAD_DIM), F32)],
        compiler_params=_cparams("parallel", "parallel", "parallel"),
        name="attn_b_prompt",
    )(proj3, proj3, proj3, ck)


def _attn_b_step_kernel(q_ref, kc_ref, vc_ref, kn_ref, vn_ref, ckc_ref, ckn_ref, o_ref, m_s, l_s, acc_s,
                        *, nkc, t, tkc):
    kb = pl.program_id(1)

    @pl.when(kb == 0)
    def _():
        _online_init(m_s, l_s, acc_s)

    def q_head(h):
        return (q_ref[0, :, h * HEAD_DIM:(h + 1) * HEAD_DIM] * SCALE).astype(BF16)

    @pl.when(kb < nkc)
    def _():
        ck = ckc_ref[0, 0]
        logits = [_dot_nt(q_head(h), kc_ref[0, pl.ds(h, tkc, stride=H_B), :].astype(BF16)) - ck[h:h + 1, :]
                  for h in range(H_B)]
        stats = [_online_stats(logits[h], m_s.at[h], l_s.at[h]) for h in range(H_B)]
        for h in range(H_B):
            alpha, p_all = stats[h]
            v = vc_ref[0, pl.ds(h, tkc, stride=H_B), :].astype(BF16)
            acc_s[h] = alpha * acc_s[h] + _dot(p_all, v)

    @pl.when(kb == nkc)
    def _():
        ckn = ckn_ref[0]
        heads = [slice(h * HEAD_DIM, (h + 1) * HEAD_DIM) for h in range(H_B)]
        logits = [_causal_mask(_dot_nt(q_head(h), _pad_keys(kn_ref[0, :, heads[h]], LANE).astype(BF16))
                               - ckn[h:h + 1, :]) for h in range(H_B)]
        stats = [_online_stats(logits[h], m_s.at[h], l_s.at[h]) for h in range(H_B)]
        for h in range(H_B):
            alpha, p_all = stats[h]
            v = _pad_keys(vn_ref[0, :, heads[h]], LANE).astype(BF16)
            acc_s[h] = alpha * acc_s[h] + _dot(p_all, v)
            o_ref[0, :, heads[h]] = _online_result(l_s.at[h], acc_s.at[h]).astype(o_ref.dtype)


def _attn_b_step(proj3, cache_k, cache_v, cum_h, *, tkc=2048):
    b, t, _ = proj3.shape
    p = cache_k.shape[1] // H_B
    tkc = min(tkc, p)
    assert p % tkc == 0 and t <= LANE, (p, tkc, t)
    nkc = p // tkc
    w = H_B * HEAD_DIM
    ck_cache = cum_h[:, :, :p].reshape(b, H_B, nkc, tkc).transpose(0, 2, 1, 3)
    ck_new = jnp.pad(cum_h[:, :, p:], ((0, 0), (0, 0), (0, LANE - t)))
    last = nkc - 1
    return pl.pallas_call(
        functools.partial(_attn_b_step_kernel, nkc=nkc, t=t, tkc=tkc),
        grid=(b, nkc + 1),
        in_specs=[
            pl.BlockSpec((1, t, w), lambda i, j: (i, 0, EV_BQ // w)),
            pl.BlockSpec((1, tkc * H_B, HEAD_DIM), lambda i, j: (i, jnp.minimum(j, last), 0)),
            pl.BlockSpec((1, tkc * H_B, HEAD_DIM), lambda i, j: (i, jnp.minimum(j, last), 0)),
            pl.BlockSpec((1, t, w), lambda i, j: (i, 0, EV_BK // w)),
            pl.BlockSpec((1, t, w), lambda i, j: (i, 0, EV_BV // w)),
            pl.BlockSpec((1, 1, H_B, tkc), lambda i, j: (i, jnp.minimum(j, last), 0, 0)),
            pl.BlockSpec((1, H_B, LANE), lambda i, j: (i, 0, 0)),
        ],
        out_specs=pl.BlockSpec((1, t, w), lambda i, j: (i, 0, 0)),
        out_shape=jax.ShapeDtypeStruct((b, t, w), BF16),
        scratch_shapes=[pltpu.VMEM((H_B, t, LANE), F32), pltpu.VMEM((H_B, t, LANE), F32),
                        pltpu.VMEM((H_B, t, HEAD_DIM), F32)],
        compiler_params=_cparams("parallel", "arbitrary"),
        name="attn_b_step",
    )(proj3, cache_k, cache_v, proj3, proj3, ck_cache, ck_new)


def _tri_ones():
    r = lax.broadcasted_iota(I32, (2 * LANE, 2 * LANE), 0) & (LANE - 1)
    c = lax.broadcasted_iota(I32, (2 * LANE, 2 * LANE), 1)
    return jnp.where((r > c) | (c >= LANE), 1.0, 0.0).astype(BF16)


def _stick_scores(z, row_minus_col, tri_ones):
    rc_blocks = None if row_minus_col is None else _lane_blocks(row_minus_col)
    out = []
    for j, zb in enumerate(_lane_blocks(z)):
        tail = jnp.log(1.0 + jnp.exp(-jnp.abs(zb)))
        log_beta = jnp.minimum(zb, 0.0) - tail
        log_keep = -jnp.maximum(zb, 0.0) - tail
        strict = None
        if rc_blocks is not None:
            strict = rc_blocks[j] > 0
            log_keep = jnp.where(strict, log_keep, 0.0)
        hi = log_keep.astype(BF16)
        lo = (log_keep - hi.astype(F32)).astype(BF16)
        out.append((log_beta, strict, _dot(jnp.concatenate([hi, lo], axis=1), tri_ones)))
    return out


def _stick_weights(blocks, run):
    ws = [None] * len(blocks)
    for j in reversed(range(len(blocks))):
        log_beta, strict, sums = blocks[j]
        w = jnp.exp(log_beta + sums[:, :LANE] + run)
        if strict is not None:
            w = jnp.where(strict, w, 0.0)
        ws[j] = w.astype(BF16)
        run = run + sums[:, LANE:]
    return (jnp.concatenate(ws, axis=1) if len(ws) > 1 else ws[0]), run


C_HEADS_PER_STEP = 2


def _attn_c_prompt_kernel(q_ref, k_ref, v_ref, o_ref, run_s, acc_s, *, tq):
    qi = pl.program_id(2)
    heads = [slice(h * HEAD_DIM, (h + 1) * HEAD_DIM) for h in range(C_HEADS_PER_STEP)]
    qs = [(q_ref[0, :, sl] * SCALE).astype(BF16) for sl in heads]
    tri_ones = _tri_ones()

    def tile(kt, row_minus_col, first):
        off = pl.multiple_of(kt * tq, tq)
        zs = [_dot_nt(qs[h], k_ref[0, pl.ds(off, tq), sl].astype(BF16)) for h, sl in enumerate(heads)]
        scored = [_stick_scores(z, row_minus_col, tri_ones) for z in zs]
        for h, sl in enumerate(heads):
            run = jnp.zeros((tq, LANE), F32) if first else run_s[h]
            w_all, run = _stick_weights(scored[h], run)
            out = _dot(w_all, v_ref[0, pl.ds(off, tq), sl].astype(BF16))
            acc_s[h] = out if first else acc_s[h] + out
            run_s[h] = run

    rc = lax.broadcasted_iota(I32, (tq, tq), 0) - lax.broadcasted_iota(I32, (tq, tq), 1)
    tile(qi, rc, True)

    def body(i, carry):
        tile(qi - 1 - i, None, False)
        return carry

    lax.fori_loop(0, qi, body, 0)
    for h, sl in enumerate(heads):
        o_ref[0, :, sl] = acc_s[h].astype(o_ref.dtype)


def _attn_c_prompt(proj3, *, tq=512):
    b, t, _ = proj3.shape
    nq = t // tq
    hp = C_HEADS_PER_STEP
    w = hp * HEAD_DIM
    return pl.pallas_call(
        functools.partial(_attn_c_prompt_kernel, tq=tq),
        grid=(b, H_C // hp, nq),
        in_specs=[
            pl.BlockSpec((1, tq, w), lambda i, h, j: (i, j, OD_CQ // w + h)),
            pl.BlockSpec((1, t, w), lambda i, h, j: (i, 0, OD_CK // w + h)),
            pl.BlockSpec((1, t, w), lambda i, h, j: (i, 0, OD_CV // w + h)),
        ],
        out_specs=pl.BlockSpec((1, tq, w), lambda i, h, j: (i, j, h)),
        out_shape=jax.ShapeDtypeStruct((b, t, H_C * HEAD_DIM), BF16),
        scratch_shapes=[pltpu.VMEM((hp, tq, LANE), F32), pltpu.VMEM((hp, tq, HEAD_DIM), F32)],
        compiler_params=_cparams("parallel", "parallel", "parallel"),
        name="attn_c_prompt",
    )(proj3, proj3, proj3)


def _attn_c_step_kernel(q_ref, kc_ref, vc_ref, kn_ref, vn_ref, o_ref, run_s, acc_s, *, nkc, t, tkc, sub):
    kb = pl.program_id(1)
    tri_ones = _tri_ones()

    def q_head(h):
        return (q_ref[0, :, h * HEAD_DIM:(h + 1) * HEAD_DIM] * SCALE).astype(BF16)

    @pl.when(kb == 0)
    def _():
        rc = lax.broadcasted_iota(I32, (t, LANE), 0) - lax.broadcasted_iota(I32, (t, LANE), 1)
        heads = [slice(h * HEAD_DIM, (h + 1) * HEAD_DIM) for h in range(H_C)]
        zs = [_dot_nt(q_head(h), _pad_keys(kn_ref[0, :, heads[h]], LANE).astype(BF16)) for h in range(H_C)]
        scored = [_stick_scores(z, rc, tri_ones) for z in zs]
        for h in range(H_C):
            w_all, run = _stick_weights(scored[h], jnp.zeros((t, LANE), F32))
            acc_s[h] = _dot(w_all, _pad_keys(vn_ref[0, :, heads[h]], LANE).astype(BF16))
            run_s[h] = run

    @pl.when(kb > 0)
    def _():
        def body(i, carry):
            row0 = (tkc - sub - i * sub) * H_C
            zs = [_dot_nt(q_head(h), kc_ref[0, pl.ds(row0 + h, sub, stride=H_C), :].astype(BF16))
                  for h in range(H_C)]
            scored = [_stick_scores(z, None, tri_ones) for z in zs]
            for h in range(H_C):
                w_all, run = _stick_weights(scored[h], run_s[h])
                v = vc_ref[0, pl.ds(row0 + h, sub, stride=H_C), :].astype(BF16)
                acc_s[h] += _dot(w_all, v)
                run_s[h] = run
            return carry

        lax.fori_loop(0, tkc // sub, body, 0)

    @pl.when(kb == nkc)
    def _():
        for h in range(H_C):
            o_ref[0, :, h * HEAD_DIM:(h + 1) * HEAD_DIM] = acc_s[h].astype(o_ref.dtype)


def _attn_c_step(proj3, cache_k, cache_v, *, tkc=2048):
    b, t, _ = proj3.shape
    p = cache_k.shape[1] // H_C
    tkc = min(tkc, p)
    sub = tkc
    assert p % tkc == 0 and t <= LANE, (p, tkc, t)
    nkc = p // tkc
    w = H_C * HEAD_DIM

    def cache_idx(i, j):
        return (i, jnp.clip(nkc - j, 0, nkc - 1), 0)

    return pl.pallas_call(
        functools.partial(_attn_c_step_kernel, nkc=nkc, t=t, tkc=tkc, sub=sub),
        grid=(b, nkc + 1),
        in_specs=[
            pl.BlockSpec((1, t, w), lambda i, j: (i, 0, OD_CQ // w)),
            pl.BlockSpec((1, tkc * H_C, HEAD_DIM), cache_idx),
            pl.BlockSpec((1, tkc * H_C, HEAD_DIM), cache_idx),
            pl.BlockSpec((1, t, w), lambda i, j: (i, 0, OD_CK // w)),
            pl.BlockSpec((1, t, w), lambda i, j: (i, 0, OD_CV // w)),
        ],
        out_specs=pl.BlockSpec((1, t, w), lambda i, j: (i, 0, 0)),
        out_shape=jax.ShapeDtypeStruct((b, t, w), BF16),
        scratch_shapes=[pltpu.VMEM((H_C, t, LANE), F32), pltpu.VMEM((H_C, t, HEAD_DIM), F32)],
        compiler_params=_cparams("parallel", "arbitrary"),
        name="attn_c_step",
    )(proj3, cache_k, cache_v, proj3, proj3)


D_HEADS_PER_STEP = 4


def _attn_d_kernel(q_ref, kp_ref, vp_ref, ko_ref, vo_ref, bias_ref, o_ref, *, tq):
    qi = pl.program_id(2)
    heads = [slice(h * HEAD_DIM, (h + 1) * HEAD_DIM) for h in range(D_HEADS_PER_STEP)]
    qs = [(q_ref[0, :, sl] * SCALE).astype(BF16) for sl in heads]

    def own_logits(h):
        return _dot_nt(qs[h], ko_ref[0, :, heads[h]].astype(BF16)) + bias_ref[h, :, D_BAND_LEFT:D_BAND_LEFT + tq]

    def with_left():
        s_o = [own_logits(h) for h in range(D_HEADS_PER_STEP)]
        s_p = [_dot_nt(qs[h], kp_ref[0, :, heads[h]].astype(BF16)) + bias_ref[h, :, 0:D_BAND_LEFT]
               for h in range(D_HEADS_PER_STEP)]
        probs = []
        for h in range(D_HEADS_PER_STEP):
            m = jnp.maximum(jnp.max(s_p[h], axis=-1, keepdims=True), jnp.max(s_o[h], axis=-1, keepdims=True))
            p_p = jnp.exp(s_p[h] - m)
            p_o = jnp.exp(s_o[h] - m)
            l = jnp.sum(p_p, axis=-1, keepdims=True) + jnp.sum(p_o, axis=-1, keepdims=True)
            probs.append((p_p.astype(BF16), p_o.astype(BF16), l))
        for h, sl in enumerate(heads):
            p_p, p_o, l = probs[h]
            out = _dot(p_p, vp_ref[0, :, sl].astype(BF16)) + _dot(p_o, vo_ref[0, :, sl].astype(BF16))
            o_ref[0, :, sl] = (out / l).astype(o_ref.dtype)

    def own_only():
        s_o = [own_logits(h) for h in range(D_HEADS_PER_STEP)]
        for h, sl in enumerate(heads):
            p_o = jnp.exp(s_o[h] - jnp.max(s_o[h], axis=-1, keepdims=True))
            l = jnp.sum(p_o, axis=-1, keepdims=True)
            o_ref[0, :, sl] = (_dot(p_o.astype(BF16), vo_ref[0, :, sl].astype(BF16)) / l).astype(o_ref.dtype)

    pl.when(qi > 0)(with_left)
    pl.when(qi == 0)(own_only)


def _attn_d(proj3, bias):
    b, t, _ = proj3.shape
    tq = D_BAND_LEFT
    nq = t // tq
    hp = D_HEADS_PER_STEP
    w = hp * HEAD_DIM
    qc, kc, vc = OD_DQ // w, OD_DK // w, OD_DV // w
    return pl.pallas_call(
        functools.partial(_attn_d_kernel, tq=tq),
        grid=(b, H_D // hp, nq),
        in_specs=[
            pl.BlockSpec((1, tq, w), lambda i, h, j: (i, j, qc + h)),
            pl.BlockSpec((1, tq, w), lambda i, h, j: (i, jnp.maximum(j - 1, 0), kc + h)),
            pl.BlockSpec((1, tq, w), lambda i, h, j: (i, jnp.maximum(j - 1, 0), vc + h)),
            pl.BlockSpec((1, tq, w), lambda i, h, j: (i, j, kc + h)),
            pl.BlockSpec((1, tq, w), lambda i, h, j: (i, j, vc + h)),
            pl.BlockSpec((hp, tq, 2 * tq), lambda i, h, j: (h, 0, 0)),
        ],
        out_specs=pl.BlockSpec((1, tq, w), lambda i, h, j: (i, j, h)),
        out_shape=jax.ShapeDtypeStruct((b, t, H_D * HEAD_DIM), BF16),
        compiler_params=_cparams("parallel", "parallel", "parallel"),
        name="attn_d",
    )(proj3, proj3, proj3, proj3, proj3, bias)


def _attn_d_step_kernel(q_ref, kp_ref, vp_ref, kn_ref, vn_ref, bias_ref, o_ref, *, t):
    heads = [slice(h * HEAD_DIM, (h + 1) * HEAD_DIM) for h in range(H_D)]
    logits = []
    for h, sl in enumerate(heads):
        q = (q_ref[0, :, sl] * SCALE).astype(BF16)
        kp = kp_ref[0, pl.ds(h, D_BAND_LEFT, stride=H_D), :].astype(BF16)
        s_p = _dot_nt(q, kp) + bias_ref[h, :, 0:D_BAND_LEFT]
        s_o = _dot_nt(q, kn_ref[0, :, sl].astype(BF16)) + bias_ref[h, :, D_BAND_LEFT:D_BAND_LEFT + t]
        logits.append((s_p, s_o))
    probs = []
    for s_p, s_o in logits:
        m = jnp.maximum(jnp.max(s_p, axis=-1, keepdims=True), jnp.max(s_o, axis=-1, keepdims=True))
        p_p = jnp.exp(s_p - m)
        p_o = jnp.exp(s_o - m)
        l = jnp.sum(p_p, axis=-1, keepdims=True) + jnp.sum(p_o, axis=-1, keepdims=True)
        probs.append((p_p.astype(BF16), p_o.astype(BF16), l))
    for h, sl in enumerate(heads):
        p_p, p_o, l = probs[h]
        vp = vp_ref[0, pl.ds(h, D_BAND_LEFT, stride=H_D), :].astype(BF16)
        out = _dot(p_p, vp) + _dot(p_o, vn_ref[0, :, sl].astype(BF16))
        o_ref[0, :, sl] = (out / l).astype(o_ref.dtype)


def _attn_d_step(proj3, cache_k, cache_v, bias):
    b, t, _ = proj3.shape
    w = H_D * HEAD_DIM
    return pl.pallas_call(
        functools.partial(_attn_d_step_kernel, t=t),
        grid=(b,),
        in_specs=[
            pl.BlockSpec((1, t, w), lambda i: (i, 0, OD_DQ // w)),
            pl.BlockSpec((1, D_BAND_LEFT * H_D, HEAD_DIM), lambda i: (i, 0, 0)),
            pl.BlockSpec((1, D_BAND_LEFT * H_D, HEAD_DIM), lambda i: (i, 0, 0)),
            pl.BlockSpec((1, t, w), lambda i: (i, 0, OD_DK // w)),
            pl.BlockSpec((1, t, w), lambda i: (i, 0, OD_DV // w)),
            pl.BlockSpec((H_D, t, D_BAND_LEFT + t), lambda i: (0, 0, 0)),
        ],
        out_specs=pl.BlockSpec((1, t, w), lambda i: (i, 0, 0)),
        out_shape=jax.ShapeDtypeStruct((b, t, w), BF16),
        compiler_params=_cparams("parallel"),
        name="attn_d_step",
    )(proj3, cache_k, cache_v, proj3, proj3, bias)


def _head_major_kernel(x_ref, o_ref, *, n_heads, rows):
    for h in range(n_heads):
        o_ref[0, pl.ds(h, rows, stride=n_heads), :] = x_ref[0, :, h * HEAD_DIM:(h + 1) * HEAD_DIM]


def _head_major(x3, col0, n_heads, row0=0):
    b, t, _ = x3.shape
    n_rows = t - row0
    tq = min(512, n_rows)
    w = n_heads * HEAD_DIM
    assert col0 % w == 0 and row0 % tq == 0 and n_rows % tq == 0, (col0, row0, n_rows)
    flat = pl.pallas_call(
        functools.partial(_head_major_kernel, n_heads=n_heads, rows=tq),
        grid=(b, n_rows // tq),
        in_specs=[pl.BlockSpec((1, tq, w), lambda i, j: (i, row0 // tq + j, col0 // w))],
        out_specs=pl.BlockSpec((1, tq * n_heads, HEAD_DIM), lambda i, j: (i, j, 0)),
        out_shape=jax.ShapeDtypeStruct((b, n_rows * n_heads, HEAD_DIM), x3.dtype),
        compiler_params=_cparams("parallel", "parallel"),
        name="head_major",
    )(x3)
    return flat.reshape(b, n_rows, n_heads, HEAD_DIM)


def _pad_rows(x, rows):
    return jnp.pad(x, ((0, 0), (0, rows - x.shape[1]), (0, 0)))


def _even_mixer(h2, b, t, past, w, tz_prompt, tz_step, tq_a, mm):
    proj = mm("in_even", h2, w["w_in"], mode="headnorm", gain=w["gain"], flag=w["flag"], n_norm_tiles=9,
              name="proj_even")
    proj3 = proj.reshape(b, t, EV_PAD)
    ak = proj3[:, :, EV_AK:EV_AK + KV_A * HEAD_DIM]
    av = proj3[:, :, EV_AV:EV_AV + KV_A * HEAD_DIM]
    ik = proj3[:, :, EV_MISC + MISC_IK:EV_MISC + MISC_IK + IDX_DIM]
    bk = proj3[:, :, EV_BK:EV_BK + H_B * HEAD_DIM]
    bv = proj3[:, :, EV_BV:EV_BV + H_B * HEAD_DIM]
    raw_h = proj3[:, :, EV_MISC + MISC_BF:EV_MISC + MISC_BF + H_B].transpose(0, 2, 1)
    iq_col = EV_IQ // (IDX_HEADS * IDX_DIM)
    misc_col = EV_MISC // LANE
    if past is None:
        logf_h, cum_h = _logf_cum(raw_h, w["f_bias"], None)
        mask = _select_mask(proj3, iq_col, proj3, misc_col, proj3, misc_col, t=t, tq=tq_a, lp=t, l_valid=t,
                            q_base=0, n_sel=min(TOPK_MAX, t // 4), tk_out=tq_a, causal_prefix=True)
        out_a = _attn_a_prompt(proj3, mask, tz_prompt, tq=tq_a)
        out_b = _attn_b_prompt(proj3, cum_h, tq=min(512, t))
    else:
        p_ak, p_av, p_ik, p_bk, p_bv, p_lf = past
        p = p_ak.shape[1]
        l_valid = p + t
        lp = p + LANE
        assert p % LANE == 0 and t <= LANE, (p, t)
        logf_h, cum_h = _logf_cum(raw_h, w["f_bias"], p_lf.transpose(0, 2, 1))
        ik_all = _pad_rows(jnp.concatenate([p_ik, ik], axis=1), lp)
        mask = _select_mask(proj3, iq_col, proj3, misc_col, ik_all, None, t=t, tq=t, lp=lp, l_valid=l_valid,
                            q_base=p, n_sel=min(TOPK_MAX, l_valid // 4), tk_out=lp, causal_prefix=False)
        out_a = _attn_a_step(proj3, p_ak.reshape(b, p * KV_A, HEAD_DIM), p_av.reshape(b, p * KV_A, HEAD_DIM),
                             mask, tz_step[:, :t])
        out_b = _attn_b_step(proj3, p_bk.reshape(b, p * H_B, HEAD_DIM), p_bv.reshape(b, p * H_B, HEAD_DIM), cum_h)
    mixed = [out_a.reshape(b * t, -1), out_b.reshape(b * t, -1)]
    if past is None:
        state = (_head_major(proj3, EV_AK, KV_A), _head_major(proj3, EV_AV, KV_A), ik,
                 _head_major(proj3, EV_BK, H_B), _head_major(proj3, EV_BV, H_B), logf_h.transpose(0, 2, 1))
    else:
        state = (ak.reshape(b, t, KV_A, HEAD_DIM), av.reshape(b, t, KV_A, HEAD_DIM), ik,
                 bk.reshape(b, t, H_B, HEAD_DIM), bv.reshape(b, t, H_B, HEAD_DIM), logf_h.transpose(0, 2, 1))
    return mixed, state


def _odd_mixer(h2, b, t, past, w, band, mm):
    proj = mm("in_odd", h2, w["w_in"], col_block=_odd_col_block, layer=0, mode="headnorm", gain=w["gain"],
              flag=w["flag"], n_norm_tiles=4, name="proj_odd")
    proj3 = proj.reshape(b, t, OD_WIDTH)
    if past is None:
        out_c = _attn_c_prompt(proj3, tq=min(512, t))
        out_d = _attn_d(proj3, band)
        keep = min(D_BAND_LEFT, t)
        state = (_head_major(proj3, OD_CK, H_C), _head_major(proj3, OD_CV, H_C),
                 _head_major(proj3, OD_DK, H_D, row0=t - keep), _head_major(proj3, OD_DV, H_D, row0=t - keep))
    else:
        p_ck, p_cv, p_dk, p_dv = past
        p = p_ck.shape[1]
        out_c = _attn_c_step(proj3, p_ck.reshape(b, p * H_C, HEAD_DIM), p_cv.reshape(b, p * H_C, HEAD_DIM))
        wd = p_dk.shape[1]
        assert wd == D_BAND_LEFT and p % CHUNK == 0 and t <= CHUNK, (wd, p, t)
        out_d = _attn_d_step(proj3, p_dk.reshape(b, wd * H_D, HEAD_DIM), p_dv.reshape(b, wd * H_D, HEAD_DIM),
                             band[:, :t, :D_BAND_LEFT + t])
        state = tuple(proj3[:, :, c0:c0 + H_C * HEAD_DIM].reshape(b, t, H_C, HEAD_DIM)
                      for c0 in (OD_CK, OD_CV, OD_DK, OD_DV))
    mixed = [out_c.reshape(b * t, -1), out_d.reshape(b * t, -1)]
    return mixed, state


def _odd_col_block(j):
    return jnp.where(j < 4, j + 6, jnp.where(j < 10, j - 4, j))


def _trunk(x, caches_even, caches_odd, wts, bf16_weights):
    def mm(key, a, w_f32, col_block=None, layer=None, **kw):
        if key in bf16_weights:
            w_bf16 = bf16_weights[key]
            if kw.get("tn", 512) == 512 and w_bf16.shape[1] % 1024 == 0:
                kw = dict(kw, tn=1024, n_norm_tiles=(kw.get("n_norm_tiles", 0) + 1) // 2)
            return _matmul(a, w_bf16, **kw)
        out, bf16_weights[key] = _matmul_cast(a, w_f32, col_block=col_block, layer=layer, **kw)
        return out

    b, t, d = x.shape
    x2 = x.reshape(b * t, d)
    h2 = _rmsnorm(x2, wts["attn_norm"][0])
    mixed, st_even = _even_mixer(h2, b, t, caches_even, wts["even"], wts["tz_prompt"], wts["tz_step"],
                                 wts["tq_a"], mm)
    x2 = mm("out_even", mixed, wts["even"]["w_out"], layer=0, mode="residual", residual=x2, name="out_even")
    h2 = _rmsnorm(x2, wts["mlp_norm"][0])
    u = mm("up0", h2, wts["w_up"], layer=0, mode="relu2", out_dtype=BF16, tn=1024, name="mlp_up0")
    x2 = _matmul(u, wts["w_down"], layer=0, mode="residual", residual=x2, tn=256, tk=u.shape[1], name="mlp_down0")
    h2 = _rmsnorm(x2, wts["attn_norm"][1])
    mixed, st_odd = _odd_mixer(h2, b, t, caches_odd, wts["odd"], wts["band"], mm)
    x2 = mm("out_odd", mixed, wts["odd"]["w_out"], layer=0, mode="residual", residual=x2, name="out_odd")
    h2 = _rmsnorm(x2, wts["mlp_norm"][1])
    u = mm("up1", h2, wts["w_up"], layer=1, mode="relu2", out_dtype=BF16, tn=1024, name="mlp_up1")
    x2 = _matmul(u, wts["w_down"], layer=1, mode="residual", residual=x2, tn=256, tk=u.shape[1], name="mlp_down1")
    return x2.reshape(b, t, d), tuple(s[None] for s in st_even), tuple(s[None] for s in st_odd)


def _even_weights(w_in, w_out, a_qn, a_kn, b_qn, b_kn, f_bias):
    aq, ak, av, iq, ik, iw, bq, bk, bv, bf = jnp.split(
        w_in, [1024, 1280, 1536, 2048, 2112, 2120, 3144, 4168, 5192], axis=1)
    d = w_in.shape[0]
    pad = jnp.zeros((d, EV_PAD - EV_WIDTH + LANE - (IDX_DIM + IDX_HEADS + H_B)), w_in.dtype)
    w_r = jnp.concatenate([aq, bq, bk, bv, ak, av, iq, ik, iw, bf, pad], axis=1)
    ones = jnp.ones((EV_PAD,), F32)
    gain = ones.at[EV_AQ:EV_AQ + 1024].set(jnp.tile(a_qn, H_A))
    gain = gain.at[EV_BQ:EV_BQ + 1024].set(jnp.tile(b_qn, H_B))
    gain = gain.at[EV_BK:EV_BK + 1024].set(jnp.tile(b_kn, H_B))
    gain = gain.at[EV_AK:EV_AK + 256].set(jnp.tile(a_kn, KV_A))
    flag = jnp.zeros((EV_PAD,), F32).at[0:EV_BV].set(1.0).at[EV_AK:EV_AV].set(1.0)
    return {"w_in": w_r, "gain": gain.reshape(1, -1), "flag": flag.reshape(1, -1), "w_out": w_out,
            "f_bias": f_bias}


def _odd_weights(w_in, w_out, d_qn, d_kn):
    gain = jnp.ones((OD_WIDTH,), F32)
    gain = gain.at[OD_DQ:OD_DQ + 1024].set(jnp.tile(d_qn, H_D)).at[OD_DK:OD_DK + 1024].set(jnp.tile(d_kn, H_D))
    flag = jnp.zeros((OD_WIDTH,), F32).at[0:OD_CQ].set(1.0)
    return {"w_in": w_in, "gain": gain.reshape(1, -1), "flag": flag.reshape(1, -1), "w_out": w_out}


def kernel(x_prompt, x_sample, cache_a_k, cache_a_v, cache_a_kidx, cache_b_k, cache_b_v, cache_b_logf,
           cache_c_k, cache_c_v, cache_d_k, cache_d_v, attn_norm, mlp_norm, w_in_even, w_out_even,
           w_in_odd, w_out_odd, a_q_norm, a_k_norm, b_q_norm, b_k_norm, forget_bias, t5_bias,
           d_q_norm, d_k_norm, d_rel_bias, w_up, w_down):
    tq_a = min(256, x_prompt.shape[1])
    wts = {
        "attn_norm": attn_norm,
        "mlp_norm": mlp_norm,
        "even": _even_weights(w_in_even[0], w_out_even, a_q_norm[0], a_k_norm[0], b_q_norm[0], b_k_norm[0],
                              forget_bias[0]),
        "odd": _odd_weights(w_in_odd, w_out_odd, d_q_norm[0], d_k_norm[0]),
        "w_up": w_up,
        "w_down": w_down.astype(BF16),
        "tq_a": tq_a,
        "tz_prompt": _t5_strip(t5_bias, tq_a),
        "tz_step": _t5_strip(t5_bias, LANE),
        "band": _band_bias(d_rel_bias[0]),
    }
    bf16_weights = {}
    y_p, ev_p, od_p = _trunk(x_prompt, None, None, wts, bf16_weights)
    caches_even = (cache_a_k[0], cache_a_v[0], cache_a_kidx[0], cache_b_k[0], cache_b_v[0], cache_b_logf[0])
    caches_odd = (cache_c_k[0], cache_c_v[0], cache_d_k[0], cache_d_v[0])
    y_s, ev_s, od_s = _trunk(x_sample, caches_even, caches_odd, wts, bf16_weights)
    return (y_p, y_s) + ev_p + od_p + ev_s + od_s
```

```python
import functools

import numpy as np
import jax
import jax.numpy as jnp
from jax import lax
from jax.experimental import pallas as pl
from jax.experimental.pallas import tpu as pltpu

F32 = jnp.float32
BF16 = jnp.bfloat16
I32 = jnp.int32

HEAD_DIM = 128
CHUNK = 64
CHUNK_SHIFT = 6
H_A = 8
KV_A = 2
REP_A = H_A // KV_A
H_B = 8
H_C = 8
H_D = 8
IDX_HEADS = 8
IDX_DIM = 64
TOPK_MAX = 256
T5_BUCKETS = 32
D_LEFT_CHUNKS = 8
D_BAND_LEFT = D_LEFT_CHUNKS * CHUNK
D_REL_CLIP = 128
EPS = 1e-6
NEG_INF = -1e30
SCALE = HEAD_DIM ** -0.5
LANE = 128
MXU_WIDTH = 256
INT32_MIN = -(2 ** 31)
NEG_INF_KEY = int(np.float32(NEG_INF).view(np.int32)) ^ 0x7FFFFFFF

VMEM_LIMIT_BYTES = 56 * 1024 * 1024

EV_AQ, EV_BQ, EV_BK, EV_BV, EV_AK, EV_AV, EV_IQ, EV_MISC = 0, 1024, 2048, 3072, 4096, 4352, 4608, 5120
EV_WIDTH = 5248
EV_PAD = 5632
MISC_IK, MISC_IW, MISC_BF = 0, 64, 72
OD_DQ, OD_DK, OD_CQ, OD_CK, OD_CV, OD_DV = 0, 1024, 2048, 3072, 4096, 5120
OD_WIDTH = 6144


def _cparams(*sem):
    return pltpu.CompilerParams(dimension_semantics=sem, vmem_limit_bytes=VMEM_LIMIT_BYTES)


def _dot_nt(a, b):
    return lax.dot_general(a, b, (((1,), (1,)), ((), ())), preferred_element_type=F32)


def _dot(a, b):
    return jnp.dot(a, b, preferred_element_type=F32)


def _lane_blocks(x):
    return [x[..., j * LANE:(j + 1) * LANE] for j in range(x.shape[-1] // LANE)]


def _pad_keys(x, rows):
    return jnp.concatenate([x, jnp.zeros((rows - x.shape[0], x.shape[1]), x.dtype)], axis=0)


def _rmsnorm_kernel(x_ref, g_ref, o_ref):
    x = x_ref[...]
    ms = jnp.mean(x * x, axis=-1, keepdims=True)
    o_ref[...] = (x * lax.rsqrt(ms + EPS) * g_ref[...]).astype(o_ref.dtype)


def _rmsnorm(x2, gain):
    m, d = x2.shape
    tm = min(512, m)
    return pl.pallas_call(
        _rmsnorm_kernel,
        grid=(m // tm,),
        in_specs=[pl.BlockSpec((tm, d), lambda i: (i, 0)), pl.BlockSpec((1, d), lambda i: (0, 0))],
        out_specs=pl.BlockSpec((tm, d), lambda i: (i, 0)),
        out_shape=jax.ShapeDtypeStruct((m, d), BF16),
        compiler_params=_cparams("parallel"),
        name="rmsnorm",
    )(x2, gain.reshape(1, d))


def _mm_kernel(*refs, n_a, nk, mode, n_norm_tiles, cast_b):
    a_refs, b_ref = refs[:n_a], refs[n_a]
    refs = refs[n_a - 1:]
    if mode == "headnorm":
        gain_ref, flag_ref, o_ref = refs[2], refs[3], refs[4]
        rest = refs[5:]
    elif mode == "residual":
        res_ref, o_ref = refs[2], refs[3]
        rest = refs[4:]
    else:
        o_ref = refs[2]
        rest = refs[3:]
    n_axis = 0 if cast_b else 1
    if cast_b:
        wb_ref = rest[0]

        @pl.when(pl.program_id(1) == 0)
        def _():
            wb_ref[...] = b_ref[...].astype(BF16)

        b_ref = wb_ref

    tn = o_ref.shape[1]

    def product(c0=0, c1=tn):
        row, acc = 0, None
        for a_ref in a_refs:
            kp = a_ref.shape[1]
            part = _dot(a_ref[...], b_ref[row:row + kp, c0:c1])
            acc = part if acc is None else acc + part
            row += kp
        return acc

    def headnorm(acc, c0):
        for g in range(acc.shape[1] // HEAD_DIM):
            sl = slice(c0 + g * HEAD_DIM, c0 + (g + 1) * HEAD_DIM)
            blk = acc[:, g * HEAD_DIM:(g + 1) * HEAD_DIM]
            ms = jnp.mean(blk * blk, axis=-1, keepdims=True)
            normed = blk * lax.rsqrt(ms + EPS) * gain_ref[:, sl]
            o_ref[:, sl] = jnp.where(flag_ref[:, sl] > 0.0, normed, blk)

    def epilogue(acc, c0=0):
        sl = slice(c0, c0 + acc.shape[1])
        if mode == "headnorm":
            headnorm(acc, c0)
        elif mode == "relu2":
            r = jnp.maximum(acc, 0.0)
            o_ref[:, sl] = (r * r).astype(o_ref.dtype)
        elif mode == "residual":
            o_ref[:, sl] = res_ref[:, sl] + acc
        else:
            o_ref[:, sl] = acc.astype(o_ref.dtype)

    def chunked():
        starts = list(range(0, tn, MXU_WIDTH))
        accs = [product(c0, c0 + MXU_WIDTH) for c0 in starts]
        for c0, acc in zip(starts, accs):
            epilogue(acc, c0)

    if nk == 1 and mode == "headnorm":
        j = pl.program_id(n_axis)
        pl.when(j < n_norm_tiles)(chunked)

        @pl.when(j >= n_norm_tiles)
        def _():
            o_ref[...] = product()
    elif nk == 1:
        chunked()
    else:
        acc_ref = rest[0]
        k = pl.program_id(2)

        @pl.when(k == 0)
        def _():
            acc_ref[...] = jnp.zeros_like(acc_ref)

        acc_ref[...] += product()

        @pl.when(k == nk - 1)
        def _():
            epilogue(acc_ref[...])


def _matmul(a, b, *, mode="plain", out_dtype=F32, gain=None, flag=None, n_norm_tiles=0, residual=None,
            tm=1024, tn=512, tk=2048, layer=None, name="matmul"):
    a_parts = list(a) if isinstance(a, (list, tuple)) else [a]
    m = a_parts[0].shape[0]
    kdim, n = b.shape[-2:]
    tm = tm if m % tm == 0 else min(512, m)
    if len(a_parts) > 1:
        tk = kdim
        in_specs = [pl.BlockSpec((tm, part.shape[1]), lambda i, j, k: (i, 0)) for part in a_parts]
    else:
        in_specs = [pl.BlockSpec((tm, tk), lambda i, j, k: (i, k))]
    nk = kdim // tk
    if layer is None:
        in_specs.append(pl.BlockSpec((tk, tn), lambda i, j, k: (k, j)))
    else:
        in_specs.append(pl.BlockSpec((None, tk, tn), lambda i, j, k: (layer, k, j)))
    args = a_parts + [b]
    if mode == "headnorm":
        in_specs += [pl.BlockSpec((1, tn), lambda i, j, k: (0, j)), pl.BlockSpec((1, tn), lambda i, j, k: (0, j))]
        args += [gain, flag]
    elif mode == "residual":
        in_specs += [pl.BlockSpec((tm, tn), lambda i, j, k: (i, j))]
        args += [residual]
    scratch = [pltpu.VMEM((tm, tn), F32)] if nk > 1 else []
    return pl.pallas_call(
        functools.partial(_mm_kernel, n_a=len(a_parts), nk=nk, mode=mode, n_norm_tiles=n_norm_tiles,
                          cast_b=False),
        grid=(m // tm, n // tn, nk),
        in_specs=in_specs,
        out_specs=pl.BlockSpec((tm, tn), lambda i, j, k: (i, j)),
        out_shape=jax.ShapeDtypeStruct((m, n), out_dtype),
        scratch_shapes=scratch,
        compiler_params=_cparams("parallel", "parallel", "arbitrary"),
        name=name,
    )(*args)


def _matmul_cast(a, b_f32, *, col_block=None, n=None, mode="plain", out_dtype=F32, gain=None, flag=None,
                 n_norm_tiles=0, residual=None, tm=2048, tn=512, layer=None, name="matmul"):
    a_parts = list(a) if isinstance(a, (list, tuple)) else [a]
    m = a_parts[0].shape[0]
    kdim = b_f32.shape[-2]
    n = b_f32.shape[-1] if n is None else n
    tm = tm if m % tm == 0 else min(512, m)
    src = (lambda j: j) if col_block is None else col_block
    in_specs = [pl.BlockSpec((tm, part.shape[1]), lambda j, i: (i, 0)) for part in a_parts]
    if layer is None:
        in_specs.append(pl.BlockSpec((kdim, tn), lambda j, i: (0, src(j))))
    else:
        in_specs.append(pl.BlockSpec((None, kdim, tn), lambda j, i: (layer, 0, src(j))))
    args = a_parts + [b_f32]
    if mode == "headnorm":
        in_specs += [pl.BlockSpec((1, tn), lambda j, i: (0, j)), pl.BlockSpec((1, tn), lambda j, i: (0, j))]
        args += [gain, flag]
    elif mode == "residual":
        in_specs += [pl.BlockSpec((tm, tn), lambda j, i: (i, j))]
        args += [residual]
    return pl.pallas_call(
        functools.partial(_mm_kernel, n_a=len(a_parts), nk=1, mode=mode, n_norm_tiles=n_norm_tiles, cast_b=True),
        grid=(n // tn, m // tm),
        in_specs=in_specs,
        out_specs=[pl.BlockSpec((tm, tn), lambda j, i: (i, j)), pl.BlockSpec((kdim, tn), lambda j, i: (0, j))],
        out_shape=[jax.ShapeDtypeStruct((m, n), out_dtype), jax.ShapeDtypeStruct((kdim, n), BF16)],
        compiler_params=_cparams("parallel", "arbitrary"),
        name=name,
    )(*args)


def _cumsum_block(x, carry):
    n = x.shape[1]
    r = lax.broadcasted_iota(I32, (n, n), 0)
    c = lax.broadcasted_iota(I32, (n, n), 1)
    tri = jnp.where(r <= c, 1.0, 0.0).astype(BF16)
    hi = x.astype(BF16)
    r1 = x - hi.astype(F32)
    mid = r1.astype(BF16)
    lo = (r1 - mid.astype(F32)).astype(BF16)
    cs = _dot(hi, tri) + _dot(mid, tri) + _dot(lo, tri) + carry
    return cs, cs[:, n - 1:n]


def _logf_cum_kernel(*refs, p_len, t_len, blk):
    if p_len:
        raw_ref, fb_ref, past_ref, lf_ref, cum_ref = refs
    else:
        raw_ref, fb_ref, lf_ref, cum_ref = refs
    x = raw_ref[0] + fb_ref[...]
    lf = jnp.minimum(x, 0.0) - jnp.log1p(jnp.exp(-jnp.abs(x)))
    lf_ref[0] = lf
    carry = jnp.zeros((H_B, 1), F32)
    for s in range(0, p_len, blk):
        cs, carry = _cumsum_block(past_ref[0, :, s:s + blk], carry)
        cum_ref[0, :, s:s + blk] = cs
    nb = min(blk, t_len)
    for s in range(0, t_len, nb):
        cs, carry = _cumsum_block(lf[:, s:s + nb], carry)
        cum_ref[0, :, p_len + s:p_len + s + nb] = cs


def _logf_cum(raw_h, fbias, past_h):
    b, h, t = raw_h.shape
    p = 0 if past_h is None else past_h.shape[2]
    in_specs = [pl.BlockSpec((1, h, t), lambda i: (i, 0, 0)), pl.BlockSpec((h, 1), lambda i: (0, 0))]
    args = [raw_h, fbias.reshape(h, 1)]
    if p:
        in_specs.append(pl.BlockSpec((1, h, p), lambda i: (i, 0, 0)))
        args.append(past_h)
    return pl.pallas_call(
        functools.partial(_logf_cum_kernel, p_len=p, t_len=t, blk=512),
        grid=(b,),
        in_specs=in_specs,
        out_specs=[pl.BlockSpec((1, h, t), lambda i: (i, 0, 0)), pl.BlockSpec((1, h, p + t), lambda i: (i, 0, 0))],
        out_shape=[jax.ShapeDtypeStruct((b, h, t), F32), jax.ShapeDtypeStruct((b, h, p + t), F32)],
        compiler_params=_cparams("parallel"),
        name="logf_cum",
    )(*args)


def _toeplitz(row_vals, rows, cols):
    w = row_vals.shape[1]
    full = jnp.broadcast_to(row_vals[0:1, :], (rows, w))
    return pltpu.roll(full, 0, 1, stride=1, stride_axis=0)[:, :cols]


def _t5_strip_kernel(tab_ref, o_ref, *, tile):
    h = pl.program_id(0)
    width = 4 * tile
    m = lax.broadcasted_iota(I32, (8, width), 1)
    d = jnp.where(m < width // 2, m, m - width)
    rel = d - tile
    n = jnp.abs(rel)
    large = jnp.full_like(n, 8)
    for thr in (12, 16, 23, 32, 46, 64, 91):
        large = large + jnp.where(n >= thr, 1, 0)
    bucket = jnp.where(rel > 0, T5_BUCKETS // 2, 0) + jnp.where(n < 8, n, large)
    vals = jnp.zeros((8, width), F32)
    for bkt in range(T5_BUCKETS):
        vals = jnp.where(bucket == bkt, tab_ref[bkt, h], vals)
    vals = vals - tab_ref[T5_BUCKETS // 2 - 1, h]
    o_ref[0] = _toeplitz(vals, tile, 2 * tile)


def _t5_strip(t5_table, tile):
    return pl.pallas_call(
        functools.partial(_t5_strip_kernel, tile=tile),
        grid=(H_A,),
        in_specs=[pl.BlockSpec(memory_space=pltpu.SMEM)],
        out_specs=pl.BlockSpec((1, tile, 2 * tile), lambda h: (h, 0, 0)),
        out_shape=jax.ShapeDtypeStruct((H_A, tile, 2 * tile), F32),
        compiler_params=_cparams("parallel"),
        name="t5_strip",
    )(t5_table)


def _band_bias_kernel(tab_ref, o_ref, *, rows, cols, width):
    h = pl.program_id(0)
    m = lax.broadcasted_iota(I32, (8, width), 1)
    d = jnp.where(m < width // 2, m, m - width)
    idx = jnp.clip(d - D_BAND_LEFT, -D_REL_CLIP, D_REL_CLIP) + D_REL_CLIP
    vals = jnp.zeros((8, width), F32)
    for r in range(2 * D_REL_CLIP + 1):
        vals = jnp.where(idx == r, tab_ref[r, h], vals)
    vals = vals - tab_ref[0, h]
    i_chunk = lax.broadcasted_iota(I32, (rows, cols), 0) >> CHUNK_SHIFT
    c_chunk = lax.broadcasted_iota(I32, (rows, cols), 1) >> CHUNK_SHIFT
    visible = (c_chunk >= i_chunk) & (c_chunk <= i_chunk + D_LEFT_CHUNKS)
    o_ref[0] = jnp.where(visible, _toeplitz(vals, rows, cols), NEG_INF)


def _band_bias(rel_table):
    rows, cols, width = D_BAND_LEFT, 2 * D_BAND_LEFT, 4 * D_BAND_LEFT
    return pl.pallas_call(
        functools.partial(_band_bias_kernel, rows=rows, cols=cols, width=width),
        grid=(H_D,),
        in_specs=[pl.BlockSpec(memory_space=pltpu.SMEM)],
        out_specs=pl.BlockSpec((1, rows, cols), lambda h: (h, 0, 0)),
        out_shape=jax.ShapeDtypeStruct((H_D, rows, cols), F32),
        compiler_params=_cparams("parallel"),
        name="band_bias",
    )(rel_table)


def _select_kernel(*refs, n_g, sb, tq, nq, lp, l_valid, q_base, n_sel, tk_out, causal_prefix):
    iq_refs, misc_refs = refs[:n_g], refs[n_g:2 * n_g]
    kk_ref, o_ref, key_ref = refs[2 * n_g:]
    step = pl.program_id(1)
    rows = sb * tq
    kf = float(n_sel)
    gs = range(n_g)

    def count(pred):
        return jnp.sum(jnp.where(pred, 1.0, 0.0), axis=-1, keepdims=True)

    def run(tiles):
        pfx = [p for _, p in tiles]
        adm = []
        for g, (q_tile, p) in enumerate(tiles):
            scores = []
            for s in range(sb):
                iq = iq_refs[g][s]
                iw = misc_refs[g][s][:, MISC_IW:MISC_IW + IDX_HEADS]
                ik = kk_ref[s, 0:p, 0:IDX_DIM].astype(BF16)
                score = jnp.zeros((tq, p), F32)
                for h in range(IDX_HEADS):
                    dots = _dot_nt(iq[:, h * IDX_DIM:(h + 1) * IDX_DIM].astype(BF16), ik)
                    score = score + iw[:, h:h + 1] * jnp.maximum(dots, 0.0)
                scores.append(score)
            score = jnp.concatenate(scores, axis=0) if sb > 1 else scores[0]
            row_in_tile = jnp.concatenate([lax.broadcasted_iota(I32, (tq, 1), 0)] * sb, axis=0)
            q_pos = q_base + q_tile * tq + row_in_tile
            kp = lax.broadcasted_iota(I32, (1, p), 1)
            ok = ((kp >> CHUNK_SHIFT) <= (q_pos >> CHUNK_SHIFT)) & (kp < l_valid)
            bits = lax.bitcast_convert_type(jnp.where(ok, score, NEG_INF), I32)
            key_ref[g, :, 0:p] = jnp.where(bits < 0, bits ^ jnp.int32(0x7FFFFFFF), bits)
            adm.append(ok)

        def keys(g):
            return key_ref[g, :, 0:pfx[g]]

        t0 = tuple(jnp.where(count(keys(g) >= 0) >= kf, jnp.int32(0), jnp.int32(INT32_MIN)) for g in gs)

        def thr_body(i, ts):
            bit = jnp.int32(1) << (30 - i)
            return tuple(jnp.where(count(keys(g) >= ts[g] + bit) >= kf, ts[g] + bit, ts[g]) for g in gs)

        thr = lax.fori_loop(0, 31, thr_body, t0)
        need = [kf - count(keys(g) > thr[g]) for g in gs]
        tied = [(count(keys(g) == thr[g]) != need[g]) & (thr[g] > NEG_INF_KEY) for g in gs]
        any_tied = sum(jnp.max(jnp.where(tied[g], 1.0, 0.0)) for g in gs) > 0.5

        def all_tied():
            return tuple(jnp.where(keys(g) == thr[g], 1.0, 0.0) for g in gs)

        def lowest_tied():
            r = lax.broadcasted_iota(I32, (LANE, 2 * LANE), 0)
            c = lax.broadcasted_iota(I32, (LANE, 2 * LANE), 1)
            count_mat = jnp.where((r <= c) | (c >= LANE), 1.0, 0.0).astype(BF16)
            out = []
            for g, tied_mask in enumerate(all_tied()):
                seen = jnp.zeros((rows, LANE), F32)
                kept = []
                for blk in _lane_blocks(tied_mask):
                    both = _dot(blk.astype(BF16), count_mat)
                    kept.append(jnp.where(both[:, :LANE] + seen <= need[g], blk, 0.0))
                    seen = seen + both[:, LANE:]
                out.append(jnp.concatenate(kept, axis=1) if len(kept) > 1 else kept[0])
            return tuple(out)

        take_tied = lax.cond(any_tied, lowest_tied, all_tied)
        for g in gs:
            sel = ((keys(g) > thr[g]) | (take_tied[g] > 0.5)) & adm[g]
            madd = jnp.where(sel, 0.0, NEG_INF)
            for s in range(sb):
                for kt in range(lp // tk_out):
                    if (kt + 1) * tk_out <= pfx[g]:
                        o_ref[s, g, 0, kt] = madd[s * tq:(s + 1) * tq, kt * tk_out:(kt + 1) * tk_out]
                    else:
                        o_ref[s, g, 0, kt] = jnp.full((tq, tk_out), NEG_INF, F32)

    if not causal_prefix:
        run([(step, lp)] * n_g)
    elif n_g == 2:
        for j in range(nq // 2):
            pl.when(step == j)(functools.partial(run, [(j, (j + 1) * tq), (nq - 1 - j, (nq - j) * tq)]))
    else:
        for j in range(nq):
            pl.when(step == j)(functools.partial(run, [(j, (j + 1) * tq)]))


def _select_mask(iq_arr, iq_col, misc_arr, misc_col, kk_arr, kk_col, *, t, tq, lp, l_valid, q_base, n_sel,
                 tk_out, causal_prefix):
    b = iq_arr.shape[0]
    nq = t // tq
    wk = kk_arr.shape[2] if kk_col is None else LANE
    kcol = 0 if kk_col is None else kk_col
    sb = 4 if (nq == 1 and tq <= 64 and b % 4 == 0) else 1
    n_g = 2 if (causal_prefix and nq % 2 == 0) else 1
    steps = nq // n_g
    tile_of = [lambda j: j, lambda j: nq - 1 - j]
    nkt = lp // tk_out
    in_specs = [pl.BlockSpec((sb, tq, IDX_HEADS * IDX_DIM), lambda i, j, g=g: (i, tile_of[g](j), iq_col))
                for g in range(n_g)]
    in_specs += [pl.BlockSpec((sb, tq, LANE), lambda i, j, g=g: (i, tile_of[g](j), misc_col)) for g in range(n_g)]
    in_specs.append(pl.BlockSpec((sb, lp, wk), lambda i, j: (i, 0, kcol)))
    return pl.pallas_call(
        functools.partial(_select_kernel, n_g=n_g, sb=sb, tq=tq, nq=nq, lp=lp, l_valid=l_valid, q_base=q_base,
                          n_sel=n_sel, tk_out=tk_out, causal_prefix=causal_prefix),
        grid=(b // sb, steps),
        in_specs=in_specs,
        out_specs=pl.BlockSpec((sb, n_g, 1, nkt, tq, tk_out), lambda i, j: (i, 0, j, 0, 0, 0)),
        out_shape=jax.ShapeDtypeStruct((b, n_g, steps, nkt, tq, tk_out), F32),
        scratch_shapes=[pltpu.VMEM((n_g, sb * tq, lp), I32)],
        compiler_params=_cparams("parallel", "parallel"),
        name="a_select",
    )(*([iq_arr] * n_g), *([misc_arr] * n_g), kk_arr)


def _mask_tile_index(q_tile, nq, n_g):
    if n_g == 1:
        return 0, q_tile
    upper = q_tile >= nq // 2
    return jnp.where(upper, 1, 0), jnp.where(upper, nq - 1 - q_tile, q_tile)


def _online_stats(s, m_s, l_s):
    blocks = _lane_blocks(s)
    mx = blocks[0]
    for blk in blocks[1:]:
        mx = jnp.maximum(mx, blk)
    m_prev = m_s[...]
    m_new = jnp.maximum(m_prev, jnp.max(mx, axis=-1, keepdims=True))
    alpha = jnp.exp(m_prev - m_new)
    ps = [jnp.exp(blk - m_new) for blk in blocks]
    psum = ps[0]
    for p in ps[1:]:
        psum = psum + p
    l_s[...] = alpha * l_s[...] + psum
    m_s[...] = m_new
    p_all = jnp.concatenate([p.astype(BF16) for p in ps], axis=1) if len(ps) > 1 else ps[0].astype(BF16)
    return alpha, p_all


def _online_init(m_s, l_s, acc_s):
    m_s[...] = jnp.full_like(m_s, NEG_INF)
    l_s[...] = jnp.zeros_like(l_s)
    acc_s[...] = jnp.zeros_like(acc_s)


def _online_result(l_s, acc_s):
    return acc_s[...] / jnp.sum(l_s[...], axis=-1, keepdims=True)


def _stack_heads(q):
    return jnp.concatenate([q[:, r * HEAD_DIM:(r + 1) * HEAD_DIM] for r in range(REP_A)], axis=0).astype(BF16)


def _attn_a_prompt_kernel(q_ref, k_ref, v_ref, mask_ref, tz_ref, o_ref, m_s, l_s, acc_s, *, tq):
    qi = pl.program_id(1)
    width = REP_A * HEAD_DIM
    qs = [_stack_heads(q_ref[0, :, g * width:(g + 1) * width] * SCALE) for g in range(KV_A)]
    _online_init(m_s, l_s, acc_s)

    def tile(kt, bias_cols):
        off = pl.multiple_of(kt * tq, tq)
        mask = mask_ref[0, 0, 0, kt][None]
        logits = []
        for g in range(KV_A):
            k = k_ref[0, pl.ds(off, tq), g * HEAD_DIM:(g + 1) * HEAD_DIM].astype(BF16)
            s = _dot_nt(qs[g], k).reshape(REP_A, tq, tq) + mask
            if bias_cols is not None:
                s = s + tz_ref[g * REP_A:(g + 1) * REP_A, :, bias_cols]
            logits.append(s.reshape(REP_A * tq, tq))
        stats = [_online_stats(logits[g], m_s.at[g], l_s.at[g]) for g in range(KV_A)]
        for g in range(KV_A):
            alpha, p_all = stats[g]
            v = v_ref[0, pl.ds(off, tq), g * HEAD_DIM:(g + 1) * HEAD_DIM].astype(BF16)
            acc_s[g] = alpha * acc_s[g] + _dot(p_all, v)

    def plain_body(kt, carry):
        tile(kt, None)
        return carry

    lax.fori_loop(0, jnp.maximum(qi - 1, 0), plain_body, 0)

    @pl.when(qi >= 1)
    def _():
        tile(qi - 1, slice(0, tq))

    tile(qi, slice(tq, 2 * tq))
    for g in range(KV_A):
        out = _online_result(l_s.at[g], acc_s.at[g])
        for r in range(REP_A):
            col = (g * REP_A + r) * HEAD_DIM
            o_ref[0, :, col:col + HEAD_DIM] = out[r * tq:(r + 1) * tq].astype(o_ref.dtype)


def _attn_a_prompt(proj3, mask, tz, *, tq):
    b, t, _ = proj3.shape
    nq = t // tq
    qw, kw = H_A * HEAD_DIM, KV_A * HEAD_DIM
    rows = REP_A * tq
    n_g = mask.shape[1]

    def mask_idx(i, j):
        grp, pos = _mask_tile_index(j, nq, n_g)
        return (i, grp, pos, 0, 0, 0)

    return pl.pallas_call(
        functools.partial(_attn_a_prompt_kernel, tq=tq),
        grid=(b, nq),
        in_specs=[
            pl.BlockSpec((1, tq, qw), lambda i, j: (i, j, EV_AQ // qw)),
            pl.BlockSpec((1, t, kw), lambda i, j: (i, 0, EV_AK // kw)),
            pl.BlockSpec((1, t, kw), lambda i, j: (i, 0, EV_AV // kw)),
            pl.BlockSpec((1, 1, 1, nq, tq, tq), mask_idx),
            pl.BlockSpec((H_A, tq, 2 * tq), lambda i, j: (0, 0, 0)),
        ],
        out_specs=pl.BlockSpec((1, tq, qw), lambda i, j: (i, j, 0)),
        out_shape=jax.ShapeDtypeStruct((b, t, qw), BF16),
        scratch_shapes=[pltpu.VMEM((KV_A, rows, LANE), F32), pltpu.VMEM((KV_A, rows, LANE), F32),
                        pltpu.VMEM((KV_A, rows, HEAD_DIM), F32)],
        compiler_params=_cparams("parallel", "parallel"),
        name="attn_a_prompt",
    )(proj3, proj3, proj3, mask, tz)


def _attn_a_step_kernel(q_ref, kc_ref, vc_ref, kn_ref, vn_ref, mask_ref, tz_ref, o_ref, *, t, p):
    width = REP_A * HEAD_DIM
    mask = mask_ref[0, 0, 0, 0]
    logits = []
    for g in range(KV_A):
        heads = slice(g * REP_A, (g + 1) * REP_A)
        qs = _stack_heads(q_ref[0, :, g * width:(g + 1) * width] * SCALE)
        kc = kc_ref[0, pl.ds(g, p, stride=KV_A), :].astype(BF16)
        kn = _pad_keys(kn_ref[0, :, g * HEAD_DIM:(g + 1) * HEAD_DIM], LANE).astype(BF16)
        s_c = _dot_nt(qs, kc).reshape(REP_A, t, p) + mask[None, :, 0:p]
        s_n = (_dot_nt(qs, kn).reshape(REP_A, t, LANE) + mask[None, :, p:p + LANE]
               + tz_ref[heads, :, LANE:2 * LANE])
        blocks = _lane_blocks(s_c)
        blocks[-1] = blocks[-1] + tz_ref[heads, :, 0:LANE]
        blocks.append(s_n)
        logits.append(blocks)
    probs = []
    for blocks in logits:
        mx = blocks[0]
        for blk in blocks[1:]:
            mx = jnp.maximum(mx, blk)
        m = jnp.max(mx, axis=-1, keepdims=True)
        ps = [jnp.exp(blk - m) for blk in blocks]
        psum = ps[0]
        for pb in ps[1:]:
            psum = psum + pb
        l = jnp.sum(psum, axis=-1, keepdims=True)
        p_c = jnp.concatenate([pb.astype(BF16) for pb in ps[:-1]], axis=-1).reshape(REP_A * t, p)
        probs.append((p_c, ps[-1].astype(BF16).reshape(REP_A * t, LANE), l))
    for g, (p_c, p_n, l) in enumerate(probs):
        vc = vc_ref[0, pl.ds(g, p, stride=KV_A), :].astype(BF16)
        vn = _pad_keys(vn_ref[0, :, g * HEAD_DIM:(g + 1) * HEAD_DIM], LANE).astype(BF16)
        out = (_dot(p_c, vc) + _dot(p_n, vn)).reshape(REP_A, t, HEAD_DIM) / l
        for r in range(REP_A):
            col = (g * REP_A + r) * HEAD_DIM
            o_ref[0, :, col:col + HEAD_DIM] = out[r].astype(o_ref.dtype)


def _attn_a_step(proj3, cache_k, cache_v, mask, tz):
    b, t, _ = proj3.shape
    p = cache_k.shape[1] // KV_A
    qw, kw = H_A * HEAD_DIM, KV_A * HEAD_DIM
    return pl.pallas_call(
        functools.partial(_attn_a_step_kernel, t=t, p=p),
        grid=(b,),
        in_specs=[
            pl.BlockSpec((1, t, qw), lambda i: (i, 0, EV_AQ // qw)),
            pl.BlockSpec((1, p * KV_A, HEAD_DIM), lambda i: (i, 0, 0)),
            pl.BlockSpec((1, p * KV_A, HEAD_DIM), lambda i: (i, 0, 0)),
            pl.BlockSpec((1, t, kw), lambda i: (i, 0, EV_AK // kw)),
            pl.BlockSpec((1, t, kw), lambda i: (i, 0, EV_AV // kw)),
            pl.BlockSpec((1, 1, 1, 1, t, p + LANE), lambda i: (i, 0, 0, 0, 0, 0)),
            pl.BlockSpec((H_A, t, 2 * LANE), lambda i: (0, 0, 0)),
        ],
        out_specs=pl.BlockSpec((1, t, qw), lambda i: (i, 0, 0)),
        out_shape=jax.ShapeDtypeStruct((b, t, qw), BF16),
        compiler_params=_cparams("parallel"),
        name="attn_a_step",
    )(proj3, cache_k, cache_v, proj3, proj3, mask, tz)


def _causal_mask(s, row0=0):
    r = row0 + lax.broadcasted_iota(I32, s.shape, 0)
    c = lax.broadcasted_iota(I32, s.shape, 1)
    return jnp.where(c <= r, s, NEG_INF)


B_HEADS_PER_STEP = 2


def _attn_b_prompt_kernel(q_ref, k_ref, v_ref, ck_ref, o_ref, m_s, l_s, acc_s, *, tq):
    qi = pl.program_id(2)
    heads = [slice(h * HEAD_DIM, (h + 1) * HEAD_DIM) for h in range(B_HEADS_PER_STEP)]
    qs = [(q_ref[0, :, sl] * SCALE).astype(BF16) for sl in heads]
    _online_init(m_s, l_s, acc_s)

    def tile(kt, diag):
        off = pl.multiple_of(kt * tq, tq)
        logits = []
        for h, sl in enumerate(heads):
            s = _dot_nt(qs[h], k_ref[0, pl.ds(off, tq), sl].astype(BF16)) - ck_ref[0, h, kt]
            logits.append(_causal_mask(s) if diag else s)
        stats = [_online_stats(logits[h], m_s.at[h], l_s.at[h]) for h in range(B_HEADS_PER_STEP)]
        for h, sl in enumerate(heads):
            alpha, p_all = stats[h]
            acc_s[h] = alpha * acc_s[h] + _dot(p_all, v_ref[0, pl.ds(off, tq), sl].astype(BF16))

    def body(kt, carry):
        tile(kt, False)
        return carry

    lax.fori_loop(0, qi, body, 0)
    tile(qi, True)
    for h, sl in enumerate(heads):
        o_ref[0, :, sl] = _online_result(l_s.at[h], acc_s.at[h]).astype(o_ref.dtype)


def _attn_b_prompt(proj3, cum_h, *, tq=512):
    b, t, _ = proj3.shape
    nq = t // tq
    ck = cum_h.reshape(b, H_B, nq, 1, tq)
    hp = B_HEADS_PER_STEP
    w = hp * HEAD_DIM
    return pl.pallas_call(
        functools.partial(_attn_b_prompt_kernel, tq=tq),
        grid=(b, H_B // hp, nq),
        in_specs=[
            pl.BlockSpec((1, tq, w), lambda i, h, j: (i, j, EV_BQ // w + h)),
            pl.BlockSpec((1, t, w), lambda i, h, j: (i, 0, EV_BK // w + h)),
            pl.BlockSpec((1, t, w), lambda i, h, j: (i, 0, EV_BV // w + h)),
            pl.BlockSpec((1, hp, nq, 1, tq), lambda i, h, j: (i, h, 0, 0, 0)),
        ],
        out_specs=pl.BlockSpec((1, tq, w), lambda i, h, j: (i, j, h)),
        out_shape=jax.ShapeDtypeStruct((b, t, H_B * HEAD_DIM), BF16),
        scratch_shapes=[pltpu.VMEM((hp, tq, LANE), F32), pltpu.VMEM((hp, tq, LANE), F32),
                        pltpu.VMEM((hp, tq, HEAD_DIM), F32)],
        compiler_params=_cparams("parallel", "parallel", "parallel"),
        name="attn_b_prompt",
    )(proj3, proj3, proj3, ck)


def _attn_b_step_kernel(q_ref, kc_ref, vc_ref, kn_ref, vn_ref, ckc_ref, ckn_ref, o_ref, m_s, l_s, acc_s,
                        *, nkc, t, tkc):
    kb = pl.program_id(1)

    @pl.when(kb == 0)
    def _():
        _online_init(m_s, l_s, acc_s)

    def q_head(h):
        return (q_ref[0, :, h * HEAD_DIM:(h + 1) * HEAD_DIM] * SCALE).astype(BF16)

    @pl.when(kb < nkc)
    def _():
        ck = ckc_ref[0, 0]
        logits = [_dot_nt(q_head(h), kc_ref[0, pl.ds(h, tkc, stride=H_B), :].astype(BF16)) - ck[h:h + 1, :]
                  for h in range(H_B)]
        stats = [_online_stats(logits[h], m_s.at[h], l_s.at[h]) for h in range(H_B)]
        for h in range(H_B):
            alpha, p_all = stats[h]
            v = vc_ref[0, pl.ds(h, tkc, stride=H_B), :].astype(BF16)
            acc_s[h] = alpha * acc_s[h] + _dot(p_all, v)

    @pl.when(kb == nkc)
    def _():
        ckn = ckn_ref[0]
        heads = [slice(h * HEAD_DIM, (h + 1) * HEAD_DIM) for h in range(H_B)]
        logits = [_causal_mask(_dot_nt(q_head(h), _pad_keys(kn_ref[0, :, heads[h]], LANE).astype(BF16))
                               - ckn[h:h + 1, :]) for h in range(H_B)]
        stats = [_online_stats(logits[h], m_s.at[h], l_s.at[h]) for h in range(H_B)]
        for h in range(H_B):
            alpha, p_all = stats[h]
            v = _pad_keys(vn_ref[0, :, heads[h]], LANE).astype(BF16)
            acc_s[h] = alpha * acc_s[h] + _dot(p_all, v)
            o_ref[0, :, heads[h]] = _online_result(l_s.at[h], acc_s.at[h]).astype(o_ref.dtype)


def _attn_b_step(proj3, cache_k, cache_v, cum_h, *, tkc=1024):
    b, t, _ = proj3.shape
    p = cache_k.shape[1] // H_B
    assert p % tkc == 0 and t <= LANE, (p, tkc, t)
    nkc = p // tkc
    w = H_B * HEAD_DIM
    ck_cache = cum_h[:, :, :p].reshape(b, H_B, nkc, tkc).transpose(0, 2, 1, 3)
    ck_new = jnp.pad(cum_h[:, :, p:], ((0, 0), (0, 0), (0, LANE - t)))
    last = nkc - 1
    return pl.pallas_call(
        functools.partial(_attn_b_step_kernel, nkc=nkc, t=t, tkc=tkc),
        grid=(b, nkc + 1),
        in_specs=[
            pl.BlockSpec((1, t, w), lambda i, j: (i, 0, EV_BQ // w)),
            pl.BlockSpec((1, tkc * H_B, HEAD_DIM), lambda i, j: (i, jnp.minimum(j, last), 0)),
            pl.BlockSpec((1, tkc * H_B, HEAD_DIM), lambda i, j: (i, jnp.minimum(j, last), 0)),
            pl.BlockSpec((1, t, w), lambda i, j: (i, 0, EV_BK // w)),
            pl.BlockSpec((1, t, w), lambda i, j: (i, 0, EV_BV // w)),
            pl.BlockSpec((1, 1, H_B, tkc), lambda i, j: (i, jnp.minimum(j, last), 0, 0)),
            pl.BlockSpec((1, H_B, LANE), lambda i, j: (i, 0, 0)),
        ],
        out_specs=pl.BlockSpec((1, t, w), lambda i, j: (i, 0, 0)),
        out_shape=jax.ShapeDtypeStruct((b, t, w), BF16),
        scratch_shapes=[pltpu.VMEM((H_B, t, LANE), F32), pltpu.VMEM((H_B, t, LANE), F32),
                        pltpu.VMEM((H_B, t, HEAD_DIM), F32)],
        compiler_params=_cparams("parallel", "arbitrary"),
        name="attn_b_step",
    )(proj3, cache_k, cache_v, proj3, proj3, ck_cache, ck_new)


def _tri_ones():
    r = lax.broadcasted_iota(I32, (2 * LANE, 2 * LANE), 0) & (LANE - 1)
    c = lax.broadcasted_iota(I32, (2 * LANE, 2 * LANE), 1)
    return jnp.where((r > c) | (c >= LANE), 1.0, 0.0).astype(BF16)


def _stick_scores(z, row_minus_col, tri_ones):
    rc_blocks = None if row_minus_col is None else _lane_blocks(row_minus_col)
    out = []
    for j, zb in enumerate(_lane_blocks(z)):
        tail = jnp.log(1.0 + jnp.exp(-jnp.abs(zb)))
        log_beta = jnp.minimum(zb, 0.0) - tail
        log_keep = -jnp.maximum(zb, 0.0) - tail
        strict = None
        if rc_blocks is not None:
            strict = rc_blocks[j] > 0
            log_keep = jnp.where(strict, log_keep, 0.0)
        hi = log_keep.astype(BF16)
        lo = (log_keep - hi.astype(F32)).astype(BF16)
        out.append((log_beta, strict, _dot(jnp.concatenate([hi, lo], axis=1), tri_ones)))
    return out


def _stick_weights(blocks, run):
    ws = [None] * len(blocks)
    for j in reversed(range(len(blocks))):
        log_beta, strict, sums = blocks[j]
        w = jnp.exp(log_beta + sums[:, :LANE] + run)
        if strict is not None:
            w = jnp.where(strict, w, 0.0)
        ws[j] = w.astype(BF16)
        run = run + sums[:, LANE:]
    return (jnp.concatenate(ws, axis=1) if len(ws) > 1 else ws[0]), run


C_HEADS_PER_STEP = 2


def _attn_c_prompt_kernel(q_ref, k_ref, v_ref, o_ref, run_s, acc_s, *, tq):
    qi = pl.program_id(2)
    heads = [slice(h * HEAD_DIM, (h + 1) * HEAD_DIM) for h in range(C_HEADS_PER_STEP)]
    qs = [(q_ref[0, :, sl] * SCALE).astype(BF16) for sl in heads]
    tri_ones = _tri_ones()

    def tile(kt, row_minus_col, first):
        off = pl.multiple_of(kt * tq, tq)
        zs = [_dot_nt(qs[h], k_ref[0, pl.ds(off, tq), sl].astype(BF16)) for h, sl in enumerate(heads)]
        scored = [_stick_scores(z, row_minus_col, tri_ones) for z in zs]
        for h, sl in enumerate(heads):
            run = jnp.zeros((tq, LANE), F32) if first else run_s[h]
            w_all, run = _stick_weights(scored[h], run)
            out = _dot(w_all, v_ref[0, pl.ds(off, tq), sl].astype(BF16))
            acc_s[h] = out if first else acc_s[h] + out
            run_s[h] = run

    rc = lax.broadcasted_iota(I32, (tq, tq), 0) - lax.broadcasted_iota(I32, (tq, tq), 1)
    tile(qi, rc, True)

    def body(i, carry):
        tile(qi - 1 - i, None, False)
        return carry

    lax.fori_loop(0, qi, body, 0)
    for h, sl in enumerate(heads):
        o_ref[0, :, sl] = acc_s[h].astype(o_ref.dtype)


def _attn_c_prompt(proj3, *, tq=512):
    b, t, _ = proj3.shape
    nq = t // tq
    hp = C_HEADS_PER_STEP
    w = hp * HEAD_DIM
    return pl.pallas_call(
        functools.partial(_attn_c_prompt_kernel, tq=tq),
        grid=(b, H_C // hp, nq),
        in_specs=[
            pl.BlockSpec((1, tq, w), lambda i, h, j: (i, j, OD_CQ // w + h)),
            pl.BlockSpec((1, t, w), lambda i, h, j: (i, 0, OD_CK // w + h)),
            pl.BlockSpec((1, t, w), lambda i, h, j: (i, 0, OD_CV // w + h)),
        ],
        out_specs=pl.BlockSpec((1, tq, w), lambda i, h, j: (i, j, h)),
        out_shape=jax.ShapeDtypeStruct((b, t, H_C * HEAD_DIM), BF16),
        scratch_shapes=[pltpu.VMEM((hp, tq, LANE), F32), pltpu.VMEM((hp, tq, HEAD_DIM), F32)],
        compiler_params=_cparams("parallel", "parallel", "parallel"),
        name="attn_c_prompt",
    )(proj3, proj3, proj3)


def _attn_c_step_kernel(q_ref, kc_ref, vc_ref, kn_ref, vn_ref, o_ref, run_s, acc_s, *, nkc, t, tkc, sub):
    kb = pl.program_id(1)
    tri_ones = _tri_ones()

    def q_head(h):
        return (q_ref[0, :, h * HEAD_DIM:(h + 1) * HEAD_DIM] * SCALE).astype(BF16)

    @pl.when(kb == 0)
    def _():
        rc = lax.broadcasted_iota(I32, (t, LANE), 0) - lax.broadcasted_iota(I32, (t, LANE), 1)
        heads = [slice(h * HEAD_DIM, (h + 1) * HEAD_DIM) for h in range(H_C)]
        zs = [_dot_nt(q_head(h), _pad_keys(kn_ref[0, :, heads[h]], LANE).astype(BF16)) for h in range(H_C)]
        scored = [_stick_scores(z, rc, tri_ones) for z in zs]
        for h in range(H_C):
            w_all, run = _stick_weights(scored[h], jnp.zeros((t, LANE), F32))
            acc_s[h] = _dot(w_all, _pad_keys(vn_ref[0, :, heads[h]], LANE).astype(BF16))
            run_s[h] = run

    @pl.when(kb > 0)
    def _():
        def body(i, carry):
            row0 = (tkc - sub - i * sub) * H_C
            zs = [_dot_nt(q_head(h), kc_ref[0, pl.ds(row0 + h, sub, stride=H_C), :].astype(BF16))
                  for h in range(H_C)]
            scored = [_stick_scores(z, None, tri_ones) for z in zs]
            for h in range(H_C):
                w_all, run = _stick_weights(scored[h], run_s[h])
                v = vc_ref[0, pl.ds(row0 + h, sub, stride=H_C), :].astype(BF16)
                acc_s[h] += _dot(w_all, v)
                run_s[h] = run
            return carry

        lax.fori_loop(0, tkc // sub, body, 0)

    @pl.when(kb == nkc)
    def _():
        for h in range(H_C):
            o_ref[0, :, h * HEAD_DIM:(h + 1) * HEAD_DIM] = acc_s[h].astype(o_ref.dtype)


def _attn_c_step(proj3, cache_k, cache_v, *, tkc=1024, sub=1024):
    b, t, _ = proj3.shape
    p = cache_k.shape[1] // H_C
    assert p % tkc == 0 and t <= LANE, (p, tkc, t)
    nkc = p // tkc
    w = H_C * HEAD_DIM

    def cache_idx(i, j):
        return (i, jnp.clip(nkc - j, 0, nkc - 1), 0)

    return pl.pallas_call(
        functools.partial(_attn_c_step_kernel, nkc=nkc, t=t, tkc=tkc, sub=sub),
        grid=(b, nkc + 1),
        in_specs=[
            pl.BlockSpec((1, t, w), lambda i, j: (i, 0, OD_CQ // w)),
            pl.BlockSpec((1, tkc * H_C, HEAD_DIM), cache_idx),
            pl.BlockSpec((1, tkc * H_C, HEAD_DIM), cache_idx),
            pl.BlockSpec((1, t, w), lambda i, j: (i, 0, OD_CK // w)),
            pl.BlockSpec((1, t, w), lambda i, j: (i, 0, OD_CV // w)),
        ],
        out_specs=pl.BlockSpec((1, t, w), lambda i, j: (i, 0, 0)),
        out_shape=jax.ShapeDtypeStruct((b, t, w), BF16),
        scratch_shapes=[pltpu.VMEM((H_C, t, LANE), F32), pltpu.VMEM((H_C, t, HEAD_DIM), F32)],
        compiler_params=_cparams("parallel", "arbitrary"),
        name="attn_c_step",
    )(proj3, cache_k, cache_v, proj3, proj3)


D_HEADS_PER_STEP = 4


def _attn_d_kernel(q_ref, kp_ref, vp_ref, ko_ref, vo_ref, bias_ref, o_ref, *, tq):
    qi = pl.program_id(2)
    heads = [slice(h * HEAD_DIM, (h + 1) * HEAD_DIM) for h in range(D_HEADS_PER_STEP)]
    qs = [(q_ref[0, :, sl] * SCALE).astype(BF16) for sl in heads]

    def own_logits(h):
        return _dot_nt(qs[h], ko_ref[0, :, heads[h]].astype(BF16)) + bias_ref[h, :, D_BAND_LEFT:D_BAND_LEFT + tq]

    def with_left():
        s_o = [own_logits(h) for h in range(D_HEADS_PER_STEP)]
        s_p = [_dot_nt(qs[h], kp_ref[0, :, heads[h]].astype(BF16)) + bias_ref[h, :, 0:D_BAND_LEFT]
               for h in range(D_HEADS_PER_STEP)]
        probs = []
        for h in range(D_HEADS_PER_STEP):
            m = jnp.maximum(jnp.max(s_p[h], axis=-1, keepdims=True), jnp.max(s_o[h], axis=-1, keepdims=True))
            p_p = jnp.exp(s_p[h] - m)
            p_o = jnp.exp(s_o[h] - m)
            l = jnp.sum(p_p, axis=-1, keepdims=True) + jnp.sum(p_o, axis=-1, keepdims=True)
            probs.append((p_p.astype(BF16), p_o.astype(BF16), l))
        for h, sl in enumerate(heads):
            p_p, p_o, l = probs[h]
            out = _dot(p_p, vp_ref[0, :, sl].astype(BF16)) + _dot(p_o, vo_ref[0, :, sl].astype(BF16))
            o_ref[0, :, sl] = (out / l).astype(o_ref.dtype)

    def own_only():
        s_o = [own_logits(h) for h in range(D_HEADS_PER_STEP)]
        for h, sl in enumerate(heads):
            p_o = jnp.exp(s_o[h] - jnp.max(s_o[h], axis=-1, keepdims=True))
            l = jnp.sum(p_o, axis=-1, keepdims=True)
            o_ref[0, :, sl] = (_dot(p_o.astype(BF16), vo_ref[0, :, sl].astype(BF16)) / l).astype(o_ref.dtype)

    pl.when(qi > 0)(with_left)
    pl.when(qi == 0)(own_only)


def _attn_d(proj3, bias):
    b, t, _ = proj3.shape
    tq = D_BAND_LEFT
    nq = t // tq
    hp = D_HEADS_PER_STEP
    w = hp * HEAD_DIM
    qc, kc, vc = OD_DQ // w, OD_DK // w, OD_DV // w
    return pl.pallas_call(
        functools.partial(_attn_d_kernel, tq=tq),
        grid=(b, H_D // hp, nq),
        in_specs=[
            pl.BlockSpec((1, tq, w), lambda i, h, j: (i, j, qc + h)),
            pl.BlockSpec((1, tq, w), lambda i, h, j: (i, jnp.maximum(j - 1, 0), kc + h)),
            pl.BlockSpec((1, tq, w), lambda i, h, j: (i, jnp.maximum(j - 1, 0), vc + h)),
            pl.BlockSpec((1, tq, w), lambda i, h, j: (i, j, kc + h)),
            pl.BlockSpec((1, tq, w), lambda i, h, j: (i, j, vc + h)),
            pl.BlockSpec((hp, tq, 2 * tq), lambda i, h, j: (h, 0, 0)),
        ],
        out_specs=pl.BlockSpec((1, tq, w), lambda i, h, j: (i, j, h)),
        out_shape=jax.ShapeDtypeStruct((b, t, H_D * HEAD_DIM), BF16),
        compiler_params=_cparams("parallel", "parallel", "parallel"),
        name="attn_d",
    )(proj3, proj3, proj3, proj3, proj3, bias)


def _attn_d_step_kernel(q_ref, kp_ref, vp_ref, kn_ref, vn_ref, bias_ref, o_ref, *, t):
    heads = [slice(h * HEAD_DIM, (h + 1) * HEAD_DIM) for h in range(H_D)]
    logits = []
    for h, sl in enumerate(heads):
        q = (q_ref[0, :, sl] * SCALE).astype(BF16)
        kp = kp_ref[0, pl.ds(h, D_BAND_LEFT, stride=H_D), :].astype(BF16)
        s_p = _dot_nt(q, kp) + bias_ref[h, :, 0:D_BAND_LEFT]
        s_o = _dot_nt(q, kn_ref[0, :, sl].astype(BF16)) + bias_ref[h, :, D_BAND_LEFT:D_BAND_LEFT + t]
        logits.append((s_p, s_o))
    probs = []
    for s_p, s_o in logits:
        m = jnp.maximum(jnp.max(s_p, axis=-1, keepdims=True), jnp.max(s_o, axis=-1, keepdims=True))
        p_p = jnp.exp(s_p - m)
        p_o = jnp.exp(s_o - m)
        l = jnp.sum(p_p, axis=-1, keepdims=True) + jnp.sum(p_o, axis=-1, keepdims=True)
        probs.append((p_p.astype(BF16), p_o.astype(BF16), l))
    for h, sl in enumerate(heads):
        p_p, p_o, l = probs[h]
        vp = vp_ref[0, pl.ds(h, D_BAND_LEFT, stride=H_D), :].astype(BF16)
        out = _dot(p_p, vp) + _dot(p_o, vn_ref[0, :, sl].astype(BF16))
        o_ref[0, :, sl] = (out / l).astype(o_ref.dtype)


def _attn_d_step(proj3, cache_k, cache_v, bias):
    b, t, _ = proj3.shape
    w = H_D * HEAD_DIM
    return pl.pallas_call(
        functools.partial(_attn_d_step_kernel, t=t),
        grid=(b,),
        in_specs=[
            pl.BlockSpec((1, t, w), lambda i: (i, 0, OD_DQ // w)),
            pl.BlockSpec((1, D_BAND_LEFT * H_D, HEAD_DIM), lambda i: (i, 0, 0)),
            pl.BlockSpec((1, D_BAND_LEFT * H_D, HEAD_DIM), lambda i: (i, 0, 0)),
            pl.BlockSpec((1, t, w), lambda i: (i, 0, OD_DK // w)),
            pl.BlockSpec((1, t, w), lambda i: (i, 0, OD_DV // w)),
            pl.BlockSpec((H_D, t, D_BAND_LEFT + t), lambda i: (0, 0, 0)),
        ],
        out_specs=pl.BlockSpec((1, t, w), lambda i: (i, 0, 0)),
        out_shape=jax.ShapeDtypeStruct((b, t, w), BF16),
        compiler_params=_cparams("parallel"),
        name="attn_d_step",
    )(proj3, cache_k, cache_v, proj3, proj3, bias)


def _head_major_kernel(x_ref, o_ref, *, n_heads, rows):
    for h in range(n_heads):
        o_ref[0, pl.ds(h, rows, stride=n_heads), :] = x_ref[0, :, h * HEAD_DIM:(h + 1) * HEAD_DIM]


def _head_major(x3, col0, n_heads, row0=0):
    b, t, _ = x3.shape
    n_rows = t - row0
    tq = min(512, n_rows)
    w = n_heads * HEAD_DIM
    assert col0 % w == 0 and row0 % tq == 0 and n_rows % tq == 0, (col0, row0, n_rows)
    flat = pl.pallas_call(
        functools.partial(_head_major_kernel, n_heads=n_heads, rows=tq),
        grid=(b, n_rows // tq),
        in_specs=[pl.BlockSpec((1, tq, w), lambda i, j: (i, row0 // tq + j, col0 // w))],
        out_specs=pl.BlockSpec((1, tq * n_heads, HEAD_DIM), lambda i, j: (i, j, 0)),
        out_shape=jax.ShapeDtypeStruct((b, n_rows * n_heads, HEAD_DIM), x3.dtype),
        compiler_params=_cparams("parallel", "parallel"),
        name="head_major",
    )(x3)
    return flat.reshape(b, n_rows, n_heads, HEAD_DIM)


def _pad_rows(x, rows):
    return jnp.pad(x, ((0, 0), (0, rows - x.shape[1]), (0, 0)))


def _even_mixer(h2, b, t, past, w, tz_prompt, tz_step, tq_a, mm):
    proj = mm("in_even", h2, w["w_in"], mode="headnorm", gain=w["gain"], flag=w["flag"], n_norm_tiles=9,
              name="proj_even")
    proj3 = proj.reshape(b, t, EV_PAD)
    ak = proj3[:, :, EV_AK:EV_AK + KV_A * HEAD_DIM]
    av = proj3[:, :, EV_AV:EV_AV + KV_A * HEAD_DIM]
    ik = proj3[:, :, EV_MISC + MISC_IK:EV_MISC + MISC_IK + IDX_DIM]
    bk = proj3[:, :, EV_BK:EV_BK + H_B * HEAD_DIM]
    bv = proj3[:, :, EV_BV:EV_BV + H_B * HEAD_DIM]
    raw_h = proj3[:, :, EV_MISC + MISC_BF:EV_MISC + MISC_BF + H_B].transpose(0, 2, 1)
    iq_col = EV_IQ // (IDX_HEADS * IDX_DIM)
    misc_col = EV_MISC // LANE
    if past is None:
        logf_h, cum_h = _logf_cum(raw_h, w["f_bias"], None)
        mask = _select_mask(proj3, iq_col, proj3, misc_col, proj3, misc_col, t=t, tq=tq_a, lp=t, l_valid=t,
                            q_base=0, n_sel=min(TOPK_MAX, t // 4), tk_out=tq_a, causal_prefix=True)
        out_a = _attn_a_prompt(proj3, mask, tz_prompt, tq=tq_a)
        out_b = _attn_b_prompt(proj3, cum_h, tq=min(512, t))
    else:
        p_ak, p_av, p_ik, p_bk, p_bv, p_lf = past
        p = p_ak.shape[1]
        l_valid = p + t
        lp = p + LANE
        assert p % LANE == 0 and t <= LANE, (p, t)
        logf_h, cum_h = _logf_cum(raw_h, w["f_bias"], p_lf.transpose(0, 2, 1))
        ik_all = _pad_rows(jnp.concatenate([p_ik, ik], axis=1), lp)
        mask = _select_mask(proj3, iq_col, proj3, misc_col, ik_all, None, t=t, tq=t, lp=lp, l_valid=l_valid,
                            q_base=p, n_sel=min(TOPK_MAX, l_valid // 4), tk_out=lp, causal_prefix=False)
        out_a = _attn_a_step(proj3, p_ak.reshape(b, p * KV_A, HEAD_DIM), p_av.reshape(b, p * KV_A, HEAD_DIM),
                             mask, tz_step[:, :t])
        out_b = _attn_b_step(proj3, p_bk.reshape(b, p * H_B, HEAD_DIM), p_bv.reshape(b, p * H_B, HEAD_DIM), cum_h)
    mixed = [out_a.reshape(b * t, -1), out_b.reshape(b * t, -1)]
    if past is None:
        state = (_head_major(proj3, EV_AK, KV_A), _head_major(proj3, EV_AV, KV_A), ik,
                 _head_major(proj3, EV_BK, H_B), _head_major(proj3, EV_BV, H_B), logf_h.transpose(0, 2, 1))
    else:
        state = (ak.reshape(b, t, KV_A, HEAD_DIM), av.reshape(b, t, KV_A, HEAD_DIM), ik,
                 bk.reshape(b, t, H_B, HEAD_DIM), bv.reshape(b, t, H_B, HEAD_DIM), logf_h.transpose(0, 2, 1))
    return mixed, state


def _odd_mixer(h2, b, t, past, w, band, mm):
    proj = mm("in_odd", h2, w["w_in"], col_block=_odd_col_block, layer=0, mode="headnorm", gain=w["gain"],
              flag=w["flag"], n_norm_tiles=4, name="proj_odd")
    proj3 = proj.reshape(b, t, OD_WIDTH)
    if past is None:
        out_c = _attn_c_prompt(proj3, tq=min(512, t))
        out_d = _attn_d(proj3, band)
        keep = min(D_BAND_LEFT, t)
        state = (_head_major(proj3, OD_CK, H_C), _head_major(proj3, OD_CV, H_C),
                 _head_major(proj3, OD_DK, H_D, row0=t - keep), _head_major(proj3, OD_DV, H_D, row0=t - keep))
    else:
        p_ck, p_cv, p_dk, p_dv = past
        p = p_ck.shape[1]
        out_c = _attn_c_step(proj3, p_ck.reshape(b, p * H_C, HEAD_DIM), p_cv.reshape(b, p * H_C, HEAD_DIM))
        wd = p_dk.shape[1]
        assert wd == D_BAND_LEFT and p % CHUNK == 0 and t <= CHUNK, (wd, p, t)
        out_d = _attn_d_step(proj3, p_dk.reshape(b, wd * H_D, HEAD_DIM), p_dv.reshape(b, wd * H_D, HEAD_DIM),
                             band[:, :t, :D_BAND_LEFT + t])
        state = tuple(proj3[:, :, c0:c0 + H_C * HEAD_DIM].reshape(b, t, H_C, HEAD_DIM)
                      for c0 in (OD_CK, OD_CV, OD_DK, OD_DV))
    mixed = [out_c.reshape(b * t, -1), out_d.reshape(b * t, -1)]
    return mixed, state


def _odd_col_block(j):
    return jnp.where(j < 4, j + 6, jnp.where(j < 10, j - 4, j))


def _trunk(x, caches_even, caches_odd, wts, bf16_weights):
    def mm(key, a, w_f32, col_block=None, layer=None, **kw):
        if key in bf16_weights:
            w_bf16 = bf16_weights[key]
            if kw.get("tn", 512) == 512 and w_bf16.shape[1] % 1024 == 0:
                kw = dict(kw, tn=1024, n_norm_tiles=(kw.get("n_norm_tiles", 0) + 1) // 2)
            return _matmul(a, w_bf16, **kw)
        out, bf16_weights[key] = _matmul_cast(a, w_f32, col_block=col_block, layer=layer, **kw)
        return out

    b, t, d = x.shape
    x2 = x.reshape(b * t, d)
    h2 = _rmsnorm(x2, wts["attn_norm"][0])
    mixed, st_even = _even_mixer(h2, b, t, caches_even, wts["even"], wts["tz_prompt"], wts["tz_step"],
                                 wts["tq_a"], mm)
    x2 = mm("out_even", mixed, wts["even"]["w_out"], layer=0, mode="residual", residual=x2, name="out_even")
    h2 = _rmsnorm(x2, wts["mlp_norm"][0])
    u = mm("up0", h2, wts["w_up"], layer=0, mode="relu2", out_dtype=BF16, tm=1024, tn=1024, name="mlp_up0")
    x2 = _matmul(u, wts["w_down"], layer=0, mode="residual", residual=x2, tn=256, tk=u.shape[1], name="mlp_down0")
    h2 = _rmsnorm(x2, wts["attn_norm"][1])
    mixed, st_odd = _odd_mixer(h2, b, t, caches_odd, wts["odd"], wts["band"], mm)
    x2 = mm("out_odd", mixed, wts["odd"]["w_out"], layer=0, mode="residual", residual=x2, name="out_odd")
    h2 = _rmsnorm(x2, wts["mlp_norm"][1])
    u = mm("up1", h2, wts["w_up"], layer=1, mode="relu2", out_dtype=BF16, tm=1024, tn=1024, name="mlp_up1")
    x2 = _matmul(u, wts["w_down"], layer=1, mode="residual", residual=x2, tn=256, tk=u.shape[1], name="mlp_down1")
    return x2.reshape(b, t, d), tuple(s[None] for s in st_even), tuple(s[None] for s in st_odd)


def _even_weights(w_in, w_out, a_qn, a_kn, b_qn, b_kn, f_bias):
    aq, ak, av, iq, ik, iw, bq, bk, bv, bf = jnp.split(
        w_in, [1024, 1280, 1536, 2048, 2112, 2120, 3144, 4168, 5192], axis=1)
    d = w_in.shape[0]
    pad = jnp.zeros((d, EV_PAD - EV_WIDTH + LANE - (IDX_DIM + IDX_HEADS + H_B)), w_in.dtype)
    w_r = jnp.concatenate([aq, bq, bk, bv, ak, av, iq, ik, iw, bf, pad], axis=1)
    ones = jnp.ones((EV_PAD,), F32)
    gain = ones.at[EV_AQ:EV_AQ + 1024].set(jnp.tile(a_qn, H_A))
    gain = gain.at[EV_BQ:EV_BQ + 1024].set(jnp.tile(b_qn, H_B))
    gain = gain.at[EV_BK:EV_BK + 1024].set(jnp.tile(b_kn, H_B))
    gain = gain.at[EV_AK:EV_AK + 256].set(jnp.tile(a_kn, KV_A))
    flag = jnp.zeros((EV_PAD,), F32).at[0:EV_BV].set(1.0).at[EV_AK:EV_AV].set(1.0)
    return {"w_in": w_r, "gain": gain.reshape(1, -1), "flag": flag.reshape(1, -1), "w_out": w_out,
            "f_bias": f_bias}


def _odd_weights(w_in, w_out, d_qn, d_kn):
    gain = jnp.ones((OD_WIDTH,), F32)
    gain = gain.at[OD_DQ:OD_DQ + 1024].set(jnp.tile(d_qn, H_D)).at[OD_DK:OD_DK + 1024].set(jnp.tile(d_kn, H_D))
    flag = jnp.zeros((OD_WIDTH,), F32).at[0:OD_CQ].set(1.0)
    return {"w_in": w_in, "gain": gain.reshape(1, -1), "flag": flag.reshape(1, -1), "w_out": w_out}


def kernel(x_prompt, x_sample, cache_a_k, cache_a_v, cache_a_kidx, cache_b_k, cache_b_v, cache_b_logf,
           cache_c_k, cache_c_v, cache_d_k, cache_d_v, attn_norm, mlp_norm, w_in_even, w_out_even,
           w_in_odd, w_out_odd, a_q_norm, a_k_norm, b_q_norm, b_k_norm, forget_bias, t5_bias,
           d_q_norm, d_k_norm, d_rel_bias, w_up, w_down):
    tq_a = min(256, x_prompt.shape[1])
    wts = {
        "attn_norm": attn_norm,
        "mlp_norm": mlp_norm,
        "even": _even_weights(w_in_even[0], w_out_even, a_q_norm[0], a_k_norm[0], b_q_norm[0], b_k_norm[0],
                              forget_bias[0]),
        "odd": _odd_weights(w_in_odd, w_out_odd, d_q_norm[0], d_k_norm[0]),
        "w_up": w_up,
        "w_down": w_down.astype(BF16),
        "tq_a": tq_a,
        "tz_prompt": _t5_strip(t5_bias, tq_a),
        "tz_step": _t5_strip(t5_bias, LANE),
        "band": _band_bias(d_rel_bias[0]),
    }
    bf16_weights = {}
    y_p, ev_p, od_p = _trunk(x_prompt, None, None, wts, bf16_weights)
    caches_even = (cache_a_k[0], cache_a_v[0], cache_a_kidx[0], cache_b_k[0], cache_b_v[0], cache_b_logf[0])
    caches_odd = (cache_c_k[0], cache_c_v[0], cache_d_k[0], cache_d_v[0])
    y_s, ev_s, od_s = _trunk(x_sample, caches_even, caches_odd, wts, bf16_weights)
    return (y_p, y_s) + ev_p + od_p + ev_s + od_s
```
